```python
import jax
import jax.numpy as jnp
from jax import lax
import numpy as np

D_MODEL = 1024
BATCH = 16
SEQ = 2048
DEPTH = 1
DEC_BATCH = 2
DEC_SEQ = 16384
PAST_LEN = 128

EPS = 1e-6
HG_HEADS = 4
HG_DK = 128
HG_DV = 128
HG_WIDTH = HG_HEADS * HG_DV
HG_CHUNK = 64
ATT_PATTERNS = ((128, 1), (512, 4), (2048, 16))
ATT_GROUPS = 3
ATT_HEADS = 4
ATT_HEAD_DIM = 64
ATT_WIDTH = ATT_GROUPS * ATT_HEADS * ATT_HEAD_DIM
ATT_OUT = ATT_HEADS * ATT_HEAD_DIM
ROPE_DIM = ATT_HEAD_DIM // 4
ROPE_THETA = 500000.0
IN_WIDTH = 5 * HG_WIDTH + 3 * ATT_WIDTH + 2 * D_MODEL
N_EXPERTS = 32
TOP_K = 4
D_FF = D_MODEL
SWIGLU_LIMIT = 7.0
SWIGLU_ALPHA = 1.702
MOE_BLOCK = 128

kernel_name = 'hybrid_hgrn2_dilated_attn_moe_encoder'


def rms_norm(x, gain):
    xf = x.astype(jnp.float32)
    y = xf * lax.rsqrt(jnp.mean(xf * xf, axis=-1, keepdims=True) + EPS)
    return (y * gain.astype(jnp.float32)).astype(x.dtype)


def partial_rotary(x, pos):
    half = ROPE_DIM // 2
    inv_freq = ROPE_THETA ** (-jnp.arange(half, dtype=jnp.float32) / half)
    ang = pos[:, None] * inv_freq[None, :]
    cos = jnp.cos(ang)[None, :, None, :]
    sin = jnp.sin(ang)[None, :, None, :]
    xr = x[..., :ROPE_DIM].astype(jnp.float32)
    x1, x2 = xr[..., :half], xr[..., half:]
    rot = jnp.concatenate([x1 * cos - x2 * sin, x2 * cos + x1 * sin], axis=-1).astype(x.dtype)
    return jnp.concatenate([rot, x[..., ROPE_DIM:]], axis=-1)


def hgrn2_scan(q, k, v, log_f):
    B, S, H, DK = q.shape
    DV = v.shape[-1]
    C = HG_CHUNK
    nC = S // C

    def chunks(t):
        return t.astype(jnp.float32).reshape(B, nC, C, H, t.shape[-1]).transpose(1, 0, 3, 2, 4)

    lower = jnp.tril(jnp.ones((C, C), dtype=bool))[None, None, :, :, None]

    def step(state, inp):
        qc, kc, vc, gc = inp
        b = jnp.cumsum(gc, axis=2)
        b_last = b[:, :, -1:, :]
        inter = jnp.einsum('bhtk,bhkv->bhtv', qc * jnp.exp(b), state)
        rel = jnp.where(lower, b[:, :, :, None, :] - b[:, :, None, :, :], -jnp.inf)
        att = jnp.einsum('bhtsk,bhsk->bhts', qc[:, :, :, None, :] * jnp.exp(rel), kc)
        intra = jnp.einsum('bhts,bhsv->bhtv', att, vc)
        new_state = (jnp.exp(b_last[:, :, 0, :])[..., None] * state
                     + jnp.einsum('bhsk,bhsv->bhkv', kc * jnp.exp(b_last - b), vc))
        return new_state, inter + intra

    state0 = jnp.zeros((B, H, DK, DV), jnp.float32)
    _, o = lax.scan(step, state0, (chunks(q), chunks(k), chunks(v), chunks(log_f)))
    return o.transpose(1, 0, 3, 2, 4).reshape(B, S, H, DV)


def band_attention(q, k, v, half):
    N, L, H, hd = q.shape
    nb = -(-L // half)
    Lp = nb * half
    qb = jnp.pad(q, ((0, 0), (0, Lp - L), (0, 0), (0, 0))).reshape(N, nb, half, H, hd)
    kv_pad = ((0, 0), (half, Lp - L + half), (0, 0), (0, 0))
    kp = jnp.pad(k, kv_pad).reshape(N, nb + 2, half, H, hd)
    vp = jnp.pad(v, kv_pad).reshape(N, nb + 2, half, H, hd)
    kb = jnp.concatenate([kp[:, :-2], kp[:, 1:-1], kp[:, 2:]], axis=2)
    vb = jnp.concatenate([vp[:, :-2], vp[:, 1:-1], vp[:, 2:]], axis=2)
    qpos = jnp.arange(nb)[:, None] * half + jnp.arange(half)[None, :]
    kpos = jnp.arange(nb)[:, None] * half - half + jnp.arange(3 * half)[None, :]
    mask = ((jnp.abs(qpos[:, :, None] - kpos[:, None, :]) <= half)
            & (kpos[:, None, :] >= 0) & (kpos[:, None, :] < L))
    s = jnp.einsum('nbqhd,nbkhd->nbhqk', qb, kb).astype(jnp.float32) * (hd ** -0.5)
    s = jnp.where(mask[None, :, None, :, :], s, -1e30)
    lse = jax.nn.logsumexp(s, axis=-1)
    p = jnp.exp(s - lse[..., None]).astype(v.dtype)
    o = jnp.einsum('nbhqk,nbkhd->nbqhd', p, vb).reshape(N, Lp, H, hd)[:, :L]
    lse = lse.transpose(0, 1, 3, 2).reshape(N, Lp, H)[:, :L]
    return o, lse


def dilated_group(q, k, v, window, dil):
    B, S, H, hd = q.shape
    Ld = S // dil

    def split(t):
        return t.reshape(B, Ld, dil, H, hd).transpose(0, 2, 1, 3, 4).reshape(B * dil, Ld, H, hd)

    o, lse = band_attention(split(q), split(k), split(v), window // (2 * dil))
    o = o.reshape(B, dil, Ld, H, hd).transpose(0, 2, 1, 3, 4).reshape(B, S, H, hd)
    lse = lse.reshape(B, dil, Ld, H).transpose(0, 2, 1, 3).reshape(B, S, H)
    return o, lse


def token_mixer(h, pos, layer, w_in, lb_logits, hg_norm_gain, w_hg_out, w_att_out, w_out):
    B, S, _ = h.shape
    sizes = [HG_WIDTH] * 5 + [ATT_WIDTH] * 3 + [D_MODEL] * 2
    cuts = [int(c) for c in np.cumsum(sizes)[:-1]]
    proj = h @ w_in
    q_hg, f_fw, f_bw, i_hg, g_hg, q_at, k_at, v_at, gate_hg, gate_at = jnp.split(proj, cuts, axis=-1)

    lb = jnp.cumsum(jax.nn.softmax(lb_logits.astype(jnp.float32), axis=1), axis=1)[:, layer]

    def gates(f, lb_d):
        ff = f.astype(jnp.float32)
        log_f = jnp.log(lb_d + (1.0 - lb_d) * jax.nn.sigmoid(ff))
        k_in = (1.0 - lb_d) * jax.nn.sigmoid(-ff)
        return log_f, k_in

    def heads(t):
        return t.reshape(B, S, HG_HEADS, -1)

    logf_fw, k_fw = gates(f_fw, lb[0])
    logf_bw, k_bw = gates(f_bw, lb[1])
    qh, vh = heads(q_hg), heads(i_hg)
    flip = lambda t: jnp.flip(t, axis=1)
    o_fw = hgrn2_scan(qh, heads(k_fw), vh, heads(logf_fw))
    o_bw = flip(hgrn2_scan(flip(qh), flip(heads(k_bw)), flip(vh), flip(heads(logf_bw))))
    o = o_fw + o_bw
    o = o * lax.rsqrt(jnp.mean(o * o, axis=-1, keepdims=True) + EPS)
    o = o.reshape(B, S, HG_WIDTH) * hg_norm_gain.astype(jnp.float32) * jax.nn.silu(g_hg.astype(jnp.float32))
    branch_hg = o.astype(h.dtype) @ w_hg_out

    def att_heads(t):
        return t.reshape(B, S, ATT_GROUPS * ATT_HEADS, ATT_HEAD_DIM)

    qa = partial_rotary(att_heads(q_at), pos).reshape(B, S, ATT_GROUPS, ATT_HEADS, ATT_HEAD_DIM)
    ka = partial_rotary(att_heads(k_at), pos).reshape(B, S, ATT_GROUPS, ATT_HEADS, ATT_HEAD_DIM)
    va = v_at.reshape(B, S, ATT_GROUPS, ATT_HEADS, ATT_HEAD_DIM)
    outs, lses = [], []
    for g, (window, dil) in enumerate(ATT_PATTERNS):
        og, lg = dilated_group(qa[:, :, g], ka[:, :, g], va[:, :, g], window, dil)
        outs.append(og)
        lses.append(lg)
    alpha = jax.nn.softmax(jnp.stack(lses, axis=0), axis=0)
    oa = jnp.sum(alpha[..., None] * jnp.stack(outs, axis=0).astype(jnp.float32), axis=0)
    branch_at = oa.reshape(B, S, ATT_OUT).astype(h.dtype) @ w_att_out

    merged = jax.nn.sigmoid(gate_hg) * branch_hg + jax.nn.sigmoid(gate_at) * branch_at
    return merged @ w_out


def routed_ffn(h, w_router, b_router, w_up, b_up, w_down, b_down):
    B, S, D = h.shape
    T = B * S
    A = T * TOP_K
    xt = h.reshape(T, D)
    logits = (xt @ w_router + b_router).astype(jnp.float32)
    top_val, top_idx = lax.top_k(logits, TOP_K)
    top_w = jax.nn.softmax(top_val, axis=-1)
    e = top_idx.reshape(-1)
    tok = jnp.arange(A) // TOP_K
    order = jnp.argsort(e)
    e_s, tok_s, w_s = e[order], tok[order], top_w.reshape(-1)[order]
    counts = jnp.bincount(e, length=N_EXPERTS)
    padded = (counts + MOE_BLOCK - 1) // MOE_BLOCK * MOE_BLOCK
    pad_end = jnp.cumsum(padded)
    pad_start = pad_end - padded
    start = jnp.cumsum(counts) - counts
    dest = pad_start[e_s] + jnp.arange(A) - start[e_s]
    n_blocks = -(-A // MOE_BLOCK) + N_EXPERTS
    R = n_blocks * MOE_BLOCK
    row_tok = jnp.zeros((R,), jnp.int32).at[dest].set(tok_s.astype(jnp.int32))
    row_w = jnp.zeros((R,), jnp.float32).at[dest].set(w_s)
    block_expert = jnp.minimum(
        jnp.searchsorted(pad_end, jnp.arange(n_blocks) * MOE_BLOCK, side='right'), N_EXPERTS - 1)
    xb = xt[row_tok].reshape(n_blocks, MOE_BLOCK, D)

    def expert_rows(args):
        xr, ei = args
        gu = xr @ w_up[ei] + b_up[ei]
        gate, up = gu[:, 0::2], gu[:, 1::2]
        gate = jnp.minimum(gate, SWIGLU_LIMIT)
        up = jnp.clip(up, -SWIGLU_LIMIT, SWIGLU_LIMIT)
        act = (up + 1.0) * gate * jax.nn.sigmoid(SWIGLU_ALPHA * gate)
        return act @ w_down[ei] + b_down[ei]

    yb = lax.map(expert_rows, (xb, block_expert)).reshape(R, D)
    yw = (yb.astype(jnp.float32) * row_w[:, None])
    out = jax.ops.segment_sum(yw, row_tok, num_segments=T)
    return out.astype(h.dtype).reshape(B, S, D)


def encoder_trunk(x, c, w_ada, b_ada, norm_pre, norm_post, w_in, lb_logits, hg_norm_gain,
                  w_hg_out, w_att_out, w_out, w_router, b_router, w_up, b_up, w_down, b_down):
    S = x.shape[1]
    pos = jnp.arange(S, dtype=jnp.float32)
    for l in range(DEPTH):
        mod = jax.nn.silu(c) @ w_ada[l] + b_ada[l]
        sh1, sc1, gt1, sh2, sc2, gt2 = jnp.split(mod[:, None, :], 6, axis=-1)
        h = rms_norm(x, norm_pre[l, 0]) * (1.0 + sc1) + sh1
        y = token_mixer(h, pos, l, w_in[l], lb_logits, hg_norm_gain[l], w_hg_out[l], w_att_out[l], w_out[l])
        x = x + gt1 * rms_norm(y, norm_post[l, 0])
        h = rms_norm(x, norm_pre[l, 1]) * (1.0 + sc2) + sh2
        y = routed_ffn(h, w_router[l], b_router[l], w_up[l], b_up[l], w_down[l], b_down[l])
        x = x + gt2 * rms_norm(y, norm_post[l, 1])
    return x


def setup_inputs(seed: int = 0) -> dict:
    key = jax.random.key(seed)
    ks = jax.random.split(key, 24)
    f32 = jnp.float32
    nrm = lambda k, shape, scale: jax.random.normal(k, shape, f32) * scale
    D = D_MODEL
    return {
        'x_prompt': nrm(ks[0], (BATCH, SEQ, D), 1.0),
        'x_sample': nrm(ks[1], (DEC_BATCH, DEC_SEQ, D), 1.0),
        'c_prompt': nrm(ks[2], (BATCH, D), 1.0),
        'c_sample': nrm(ks[3], (DEC_BATCH, D), 1.0),
        'w_ada': nrm(ks[4], (DEPTH, D, 6 * D), D ** -0.5),
        'b_ada': nrm(ks[5], (DEPTH, 6 * D), 0.02),
        'norm_pre': 1.0 + nrm(ks[6], (DEPTH, 2, D), 0.02),
        'norm_post': 1.0 + nrm(ks[7], (DEPTH, 2, D), 0.02),
        'w_in': nrm(ks[8], (DEPTH, D, IN_WIDTH), D ** -0.5),
        'lb_logits': nrm(ks[9], (2, DEPTH + 1, HG_WIDTH), 0.5),
        'hg_norm_gain': 1.0 + nrm(ks[10], (DEPTH, HG_WIDTH), 0.02),
        'w_hg_out': nrm(ks[11], (DEPTH, HG_WIDTH, D), HG_WIDTH ** -0.5),
        'w_att_out': nrm(ks[12], (DEPTH, ATT_OUT, D), ATT_OUT ** -0.5),
        'w_out': nrm(ks[13], (DEPTH, D, D), D ** -0.5),
        'w_router': nrm(ks[14], (DEPTH, D, N_EXPERTS), D ** -0.5),
        'b_router': nrm(ks[15], (DEPTH, N_EXPERTS), 0.01),
        'w_up': nrm(ks[16], (DEPTH, N_EXPERTS, D, 2 * D_FF), D ** -0.5),
        'b_up': nrm(ks[17], (DEPTH, N_EXPERTS, 2 * D_FF), 0.02),
        'w_down': nrm(ks[18], (DEPTH, N_EXPERTS, D_FF, D), D_FF ** -0.5),
        'b_down': nrm(ks[19], (DEPTH, N_EXPERTS, D), 0.02),
    }


def reference(x_prompt, x_sample, c_prompt, c_sample, w_ada, b_ada, norm_pre, norm_post, w_in,
              lb_logits, hg_norm_gain, w_hg_out, w_att_out, w_out, w_router, b_router,
              w_up, b_up, w_down, b_down):
    y_prompt = encoder_trunk(x_prompt, c_prompt, w_ada, b_ada, norm_pre, norm_post, w_in, lb_logits,
                             hg_norm_gain, w_hg_out, w_att_out, w_out, w_router, b_router,
                             w_up, b_up, w_down, b_down)
    y_sample = encoder_trunk(x_sample, c_sample, w_ada, b_ada, norm_pre, norm_post, w_in, lb_logits,
                             hg_norm_gain, w_hg_out, w_att_out, w_out, w_router, b_router,
                             w_up, b_up, w_down, b_down)
    return (y_prompt, y_sample)
```

```python
import functools
import math

import numpy as np
import jax
import jax.numpy as jnp
from jax import lax
from jax.experimental import pallas as pl
from jax.experimental.pallas import tpu as pltpu

F32 = jnp.float32
BF16 = jnp.bfloat16
I32 = jnp.int32

D = 1024
EPS = 1e-6
HG_H = 4
HG_DK = 128
HG_W = HG_H * HG_DK
HG_C = 64
HG_LEVELS = 6
HG_SAFE_LOGDECAY = -80.0
AT_DILS = (1, 4, 16)
AT_HALF = 64
AT_H = 4
AT_HD = 64
AT_GW = AT_H * AT_HD
ROPE_DIM = 16
ROPE_THETA = 500000.0
N_EXP = 32
TOP_K = 4
SWIGLU_LIMIT = 7.0
SWIGLU_ALPHA = 1.702
LANES = 128
NEG_BIG = -1e30

VMEM_LIMIT = 56 * 1024 * 1024


def _cp(sem, vmem=VMEM_LIMIT):
    return pltpu.CompilerParams(dimension_semantics=sem, vmem_limit_bytes=vmem)


def _dot(a, b):
    return jnp.dot(a, b, preferred_element_type=F32)


def _dot_nt(a, b):
    return lax.dot_general(a, b, (((1,), (1,)), ((), ())), preferred_element_type=F32)


def _dot_tn(a, b):
    return lax.dot_general(a, b, (((0,), (0,)), ((), ())), preferred_element_type=F32)


def _split3(x):
    hi = x.astype(BF16)
    r = x - hi.astype(F32)
    mid = r.astype(BF16)
    lo = (r - mid.astype(F32)).astype(BF16)
    return hi, mid, lo


def _rms(x, gain):
    ms = jnp.mean(x * x, axis=-1, keepdims=True)
    return x * lax.rsqrt(ms + EPS) * gain


def _const_spec(shape):
    n = len(shape)
    return pl.BlockSpec(shape, lambda *_: (0,) * n)


def _ada_kernel(c_ref, w_ref, b_ref, o_ref):
    c = c_ref[...]
    a = c * jax.nn.sigmoid(c)
    w = w_ref[...]
    a_hi = a.astype(BF16)
    a_lo = (a - a_hi.astype(F32)).astype(BF16)
    w_hi = w.astype(BF16)
    w_lo = (w - w_hi.astype(F32)).astype(BF16)
    o_ref[...] = _dot(a_hi, w_hi) + _dot(a_lo, w_hi) + _dot(a_hi, w_lo) + b_ref[...]


def _ada(c, w, b):
    nb = c.shape[0]
    n = w.shape[1]
    tn = 1536
    return pl.pallas_call(
        _ada_kernel,
        grid=(n // tn,),
        in_specs=[pl.BlockSpec((nb, D), lambda j: (0, 0)),
                  pl.BlockSpec((D, tn), lambda j: (0, j)),
                  pl.BlockSpec((1, tn), lambda j: (0, j))],
        out_specs=pl.BlockSpec((nb, tn), lambda j: (0, j)),
        out_shape=jax.ShapeDtypeStruct((nb, n), F32),
        compiler_params=_cp(("parallel",)),
        name="ada",
    )(c, w, b.reshape(1, n))


def _inproj_kernel(x_ref, mod_ref, gain_ref, cos_ref, s1_ref, s2_ref,
                   w_hga_ref, w_hgf_ref, w_at_ref, w_gt_ref,
                   hga_ref, hgf_ref, at0_ref, at1_ref, at2_ref, gt_ref):
    x = x_ref[...]
    sh = mod_ref[0, 0:1, :]
    sc = mod_ref[0, 1:2, :]
    h = _rms(x, gain_ref[...]) * (1.0 + sc) + sh
    hb = h.astype(BF16)
    hga_ref[...] = _dot(hb, w_hga_ref[...]).astype(BF16)
    hgf_ref[...] = _dot(hb, w_hgf_ref[...])
    cos = cos_ref[...]
    s1 = s1_ref[...]
    s2 = s2_ref[...]
    for g, o_ref in enumerate((at0_ref, at1_ref, at2_ref)):
        acc = _dot(hb, w_at_ref[:, g * 3 * AT_GW:(g + 1) * 3 * AT_GW])
        parts = []
        for j in range(2 * AT_GW // LANES):
            a = acc[:, j * LANES:(j + 1) * LANES]
            parts.append(a * cos + pltpu.roll(a, ROPE_DIM // 2, 1) * s1
                         + pltpu.roll(a, LANES - ROPE_DIM // 2, 1) * s2)
        parts.append(acc[:, 2 * AT_GW:])
        o_ref[...] = jnp.concatenate(parts, axis=1).astype(BF16)
    gt_ref[...] = _dot(hb, w_gt_ref[...]).astype(BF16)


def _inproj(x2, mod3, gain, rope, w_hga, w_hgf, w_at, w_gt, seq, tm):
    t = x2.shape[0]
    n_pos_blk = seq // tm
    row = lambda w: pl.BlockSpec((tm, w), lambda i: (i, 0))
    pos = pl.BlockSpec((tm, LANES), lambda i: (i % n_pos_blk, 0))
    return pl.pallas_call(
        _inproj_kernel,
        grid=(t // tm,),
        in_specs=[row(D),
                  pl.BlockSpec((1, 6, D), lambda i: ((i * tm) // seq, 0, 0)),
                  _const_spec((1, D)), pos, pos, pos,
                  _const_spec(w_hga.shape), _const_spec(w_hgf.shape),
                  _const_spec(w_at.shape), _const_spec(w_gt.shape)],
        out_specs=[row(3 * HG_W), row(2 * HG_W), row(3 * AT_GW), row(3 * AT_GW), row(3 * AT_GW),
                   row(2 * D)],
        out_shape=[jax.ShapeDtypeStruct((t, 3 * HG_W), BF16),
                   jax.ShapeDtypeStruct((t, 2 * HG_W), F32),
                   jax.ShapeDtypeStruct((t, 3 * AT_GW), BF16),
                   jax.ShapeDtypeStruct((t, 3 * AT_GW), BF16),
                   jax.ShapeDtypeStruct((t, 3 * AT_GW), BF16),
                   jax.ShapeDtypeStruct((t, 2 * D), BF16)],
        compiler_params=_cp(("parallel",)),
        name="inproj",
    )(x2, mod3, gain, *rope, w_hga, w_hgf, w_at, w_gt)


def _rope_tables(seq):
    half = ROPE_DIM // 2
    inv_freq = ROPE_THETA ** (-np.arange(half, dtype=np.float32) / half)
    ang = jnp.arange(seq, dtype=F32)[:, None] * jnp.asarray(inv_freq)[None, :]
    cos, sin = jnp.cos(ang), jnp.sin(ang)
    ones = jnp.ones((seq, AT_HD - ROPE_DIM), F32)
    zeros = jnp.zeros((seq, AT_HD - ROPE_DIM), F32)
    zh = jnp.zeros((seq, half), F32)
    c_head = jnp.concatenate([cos, cos, ones], axis=1)
    s1_head = jnp.concatenate([zh, sin, zeros], axis=1)
    s2_head = jnp.concatenate([-sin, zh, zeros], axis=1)
    rep = LANES // AT_HD
    return tuple(jnp.tile(a, (1, rep)) for a in (c_head, s1_head, s2_head))


def _hgrn_consts(reverse):
    c = HG_C
    t = np.arange(c)[:, None]
    u = np.arange(c)[None, :]
    tri = (u >= t) if reverse else (u <= t)
    sels = []
    for lvl in range(HG_LEVELS):
        blk = 2 << lvl
        start = (np.arange(c) // blk) * blk
        piv = start + (1 << lvl) - (0 if reverse else 1)
        sels.append(np.arange(c)[None, :] == piv[:, None])
    sel = np.concatenate(sels, axis=0)
    three = lambda m: jnp.asarray(np.concatenate([m, m, m], axis=1), BF16)
    return three(tri.astype(np.float32)), three(sel.astype(np.float32))


def _hgrn_kernel(*refs, reverse, final, n_chunks):
    if final:
        (lbl_ref, tri_ref, sel_ref, q_ref, f_ref, v_ref, ofw_ref, gh_ref, gain_ref,
         o_ref, st_ref, k_s, b_s) = refs
    else:
        (lbl_ref, tri_ref, sel_ref, q_ref, f_ref, v_ref,
         o_ref, st_ref, k_s, b_s) = refs
    c = HG_C

    @pl.when(pl.program_id(1) == 0)
    def _():
        st_ref[...] = jnp.zeros_like(st_ref)

    l0 = lbl_ref[0:1, :]
    l1 = lbl_ref[1:2, :]
    lm = jnp.maximum(l0, l1)
    e0 = jnp.exp(l0 - lm)
    e1 = jnp.exp(l1 - lm)
    lb = e0 / (e0 + e1)
    tri = tri_ref[...]

    def gates(ci, bmin):
        r0 = pl.multiple_of(ci * c, c)
        ff = f_ref[pl.ds(r0, c), :]
        e = jnp.exp(-jnp.abs(ff))
        r = 1.0 / (1.0 + e)
        pos = ff >= 0.0
        sg = jnp.where(pos, r, e * r)
        sgn = jnp.where(pos, e * r, r)
        g = jnp.log(lb + (1.0 - lb) * sg)
        k_s[pl.ds(r0, c), :] = (1.0 - lb) * sgn
        b = _dot(tri, jnp.concatenate(_split3(g), axis=0))
        b_s[pl.ds(r0, c), :] = b
        return jnp.minimum(bmin, jnp.min(b, axis=0, keepdims=True))

    bmin = lax.fori_loop(0, n_chunks, gates, jnp.zeros((1, HG_W), F32))
    safe = jnp.min(bmin) >= HG_SAFE_LOGDECAY

    ti = lax.broadcasted_iota(I32, (c, c), 0)
    si = lax.broadcasted_iota(I32, (c, c), 1)
    causal = (si >= ti) if reverse else (si <= ti)
    row = lax.broadcasted_iota(I32, (c, 1), 0)

    def att_fast(qe, kin, b):
        ke = (kin * jnp.exp(-b)).astype(BF16)
        return jnp.where(causal, _dot_nt(qe, ke), 0.0)

    def att_robust(q, kin, b, piv):
        att = jnp.where(ti == si, _dot_nt(q.astype(BF16), kin.astype(BF16)), 0.0)
        for lvl in range(HG_LEVELS):
            p = piv[lvl * c:(lvl + 1) * c, :]
            bit = ((row >> lvl) & 1) == 1
            q_side = jnp.logical_not(bit) if reverse else bit
            qe = jnp.where(q_side, q * jnp.exp(jnp.minimum(b - p, 0.0)), 0.0).astype(BF16)
            ke = jnp.where(q_side, 0.0, kin * jnp.exp(jnp.minimum(p - b, 0.0))).astype(BF16)
            same = (ti >> (lvl + 1)) == (si >> (lvl + 1))
            att = att + jnp.where(same, _dot_nt(qe, ke), 0.0)
        return att

    def make_step(robust):
        def step(i, carry):
            ci = (n_chunks - 1 - i) if reverse else i
            r0 = pl.multiple_of(ci * c, c)
            q = q_ref[pl.ds(r0, c), :].astype(F32)
            v = v_ref[pl.ds(r0, c), :]
            kin = k_s[pl.ds(r0, c), :]
            b = b_s[pl.ds(r0, c), :]
            btot = b[0:1, :] if reverse else b[c - 1:c, :]
            qe_all = (q * jnp.exp(b)).astype(BF16)
            kd_all = (kin * jnp.exp(btot - b)).astype(BF16)
            if robust:
                piv = _dot(sel_ref[...], jnp.concatenate(_split3(b), axis=0))
            outs = []
            for h in range(HG_H):
                sl = slice(h * HG_DK, (h + 1) * HG_DK)
                qe = qe_all[:, sl]
                if robust:
                    att = att_robust(q[:, sl], kin[:, sl], b[:, sl], piv[:, sl])
                else:
                    att = att_fast(qe, kin[:, sl], b[:, sl])
                st = st_ref[h]
                o_h = _dot_nt(qe, st.astype(BF16)) + _dot(att.astype(BF16), v[:, sl])
                st_ref[h] = st * jnp.exp(btot[:, sl]) + _dot_tn(v[:, sl], kd_all[:, sl])
                outs.append(o_h)
            o = jnp.concatenate(outs, axis=1)
            if final:
                o = o + ofw_ref[pl.ds(r0, c), :]
                normed = []
                for h in range(HG_H):
                    o_h = o[:, h * HG_DK:(h + 1) * HG_DK]
                    ms = jnp.mean(o_h * o_h, axis=-1, keepdims=True)
                    normed.append(o_h * lax.rsqrt(ms + EPS))
                gh = gh_ref[pl.ds(r0, c), :].astype(F32)
                o = jnp.concatenate(normed, axis=1) * gain_ref[...] * (gh * jax.nn.sigmoid(gh))
                o_ref[pl.ds(r0, c), :] = o.astype(o_ref.dtype)
            else:
                o_ref[pl.ds(r0, c), :] = o
            return carry
        return step

    def run(robust):
        def f():
            lax.fori_loop(0, n_chunks, make_step(robust), 0)
        return f

    lax.cond(safe, run(False), run(True))


def _hgrn(lb_logits_d, hga, hgf, batch, seq, tc, reverse, o_fw=None, gain=None):
    final = o_fw is not None
    t = hga.shape[0]
    nblk = seq // tc
    tri, sel = _hgrn_consts(reverse)

    def rows(col):
        if reverse:
            return lambda b, j: (b * nblk + nblk - 1 - j, col)
        return lambda b, j: (b * nblk + j, col)

    blk = lambda col: pl.BlockSpec((tc, HG_W), rows(col))
    in_specs = [_const_spec((2, HG_W)), _const_spec(tri.shape), _const_spec(sel.shape),
                blk(0), blk(1 if reverse else 0), blk(1)]
    args = [lb_logits_d, tri, sel, hga, hgf, hga]
    if final:
        in_specs += [blk(0), blk(2), _const_spec((1, HG_W))]
        args += [o_fw, hga, gain]
    return pl.pallas_call(
        functools.partial(_hgrn_kernel, reverse=reverse, final=final, n_chunks=tc // HG_C),
        grid=(batch, nblk),
        in_specs=in_specs,
        out_specs=blk(0),
        out_shape=jax.ShapeDtypeStruct((t, HG_W), BF16 if final else F32),
        scratch_shapes=[pltpu.VMEM((HG_H, HG_DK, HG_DK), F32),
                        pltpu.VMEM((tc, HG_W), F32),
                        pltpu.VMEM((tc, HG_W), F32)],
        compiler_params=_cp(("parallel", "arbitrary")),
        name="hgrn_bwd" if reverse else "hgrn_fwd",
    )(*args)


def _attn_kernel(q_ref, kl_ref, km_ref, kr_ref, vl_ref, vm_ref, vr_ref, o_ref, l_ref,
                 k_s, v_s, *, tq, ld):
    hw = AT_HALF
    k_s[0:hw, :] = kl_ref[0]
    k_s[hw:hw + tq, :] = km_ref[0]
    k_s[hw + tq:, :] = kr_ref[0]
    v_s[0:hw, :] = vl_ref[0]
    v_s[hw:hw + tq, :] = vm_ref[0]
    v_s[hw + tq:, :] = vr_ref[0]
    qb = 2 * hw
    kb = 4 * hw
    base = pl.program_id(2) * tq
    scale = AT_HD ** -0.5
    for qs in range(0, tq, qb):
        q = q_ref[0, qs:qs + qb, :]
        kk = k_s[qs:qs + kb, :]
        vv = v_s[qs:qs + kb, :]
        qpos = base + qs + lax.broadcasted_iota(I32, (qb, kb), 0)
        kpos = base + qs - hw + lax.broadcasted_iota(I32, (qb, kb), 1)
        mask = (jnp.abs(qpos - kpos) <= hw) & (kpos >= 0) & (kpos < ld)
        outs, lses = [], []
        for h in range(AT_H):
            sl = slice(h * AT_HD, (h + 1) * AT_HD)
            s = _dot_nt(q[:, sl], kk[:, sl]) * scale
            s = jnp.where(mask, s, NEG_BIG)
            m = jnp.max(s, axis=-1, keepdims=True)
            p = jnp.exp(s - m)
            l = jnp.sum(p, axis=-1, keepdims=True)
            o = _dot(p.astype(BF16), vv[:, sl]) * (1.0 / l)
            outs.append(o)
            lses.append(jnp.broadcast_to(m + jnp.log(l), (qb, AT_HD)))
        o_ref[0, qs:qs + qb, :] = jnp.concatenate(outs, axis=1).astype(o_ref.dtype)
        l_ref[0, qs:qs + qb, :] = jnp.concatenate(lses, axis=1)


def _attn_group(qkv, batch, seq, dil):
    ld = seq // dil
    hw = AT_HALF
    tq = min(ld, 512)
    nq = ld // tq
    nh = ld // hw
    per = tq // hw
    view = qkv.reshape(batch, ld, dil * 3 * AT_GW)
    main = lambda part: pl.BlockSpec((1, tq, AT_GW), lambda b, r, j: (b, j, 3 * r + part))
    left = lambda part: pl.BlockSpec(
        (1, hw, AT_GW), lambda b, r, j: (b, jnp.maximum(j * per - 1, 0), 3 * r + part))
    right = lambda part: pl.BlockSpec(
        (1, hw, AT_GW), lambda b, r, j: (b, jnp.minimum((j + 1) * per, nh - 1), 3 * r + part))
    out_spec = pl.BlockSpec((1, tq, AT_GW), lambda b, r, j: (b, j, r))
    o, lse = pl.pallas_call(
        functools.partial(_attn_kernel, tq=tq, ld=ld),
        grid=(batch, dil, nq),
        in_specs=[main(0), left(1), main(1), right(1), left(2), main(2), right(2)],
        out_specs=[out_spec, out_spec],
        out_shape=[jax.ShapeDtypeStruct((batch, ld, dil * AT_GW), BF16),
                   jax.ShapeDtypeStruct((batch, ld, dil * AT_GW), F32)],
        scratch_shapes=[pltpu.VMEM((tq + 2 * hw, AT_GW), BF16),
                        pltpu.VMEM((tq + 2 * hw, AT_GW), BF16)],
        compiler_params=_cp(("parallel", "parallel", "parallel")),
        name=f"attn_d{dil}",
    )(view, view, view, view, view, view, view)
    return o.reshape(batch * seq, AT_GW), lse.reshape(batch * seq, AT_GW)


def _merge_kernel(ohg_ref, o0_ref, o1_ref, o2_ref, l0_ref, l1_ref, l2_ref, gt_ref, x_ref, mod_ref,
                  w_hg_ref, w_at_ref, w_out_ref, npost_ref, npre_ref, w_r_ref, b_r_ref,
                  x1_ref, h2_ref, ids_ref, wts_ref, cnt_ref):
    l0, l1, l2 = l0_ref[...], l1_ref[...], l2_ref[...]
    m = jnp.maximum(jnp.maximum(l0, l1), l2)
    e0, e1, e2 = jnp.exp(l0 - m), jnp.exp(l1 - m), jnp.exp(l2 - m)
    oa = (e0 * o0_ref[...].astype(F32) + e1 * o1_ref[...].astype(F32)
          + e2 * o2_ref[...].astype(F32)) / (e0 + e1 + e2)
    b_hg = _dot(ohg_ref[...], w_hg_ref[...])
    b_at = _dot(oa.astype(BF16), w_at_ref[...])
    g_hg = jax.nn.sigmoid(gt_ref[:, 0:D].astype(F32))
    g_at = jax.nn.sigmoid(gt_ref[:, D:2 * D].astype(F32))
    merged = g_hg * b_hg + g_at * b_at
    y = _dot(merged.astype(BF16), w_out_ref[...])
    gt1 = mod_ref[0, 2:3, :]
    sh2 = mod_ref[0, 3:4, :]
    sc2 = mod_ref[0, 4:5, :]
    x1 = x_ref[...] + gt1 * _rms(y, npost_ref[...])
    x1_ref[...] = x1
    h2 = _rms(x1, npre_ref[...]) * (1.0 + sc2) + sh2
    h2_ref[...] = h2.astype(h2_ref.dtype)

    w_r = w_r_ref[...]
    h_hi = h2.astype(BF16)
    h_lo = (h2 - h_hi.astype(F32)).astype(BF16)
    w_hi = w_r.astype(BF16)
    w_lo = (w_r - w_hi.astype(F32)).astype(BF16)
    logits = _dot(h_hi, w_hi) + _dot(h_lo, w_hi) + _dot(h_hi, w_lo) + b_r_ref[...]

    lane = lax.broadcasted_iota(I32, logits.shape, 1)
    work = logits
    vals, idxs = [], []
    sel = jnp.zeros(logits.shape, F32)
    for _ in range(TOP_K):
        mk = jnp.max(work, axis=-1, keepdims=True)
        ik = jnp.min(jnp.where(work == mk, lane, LANES), axis=-1, keepdims=True)
        hit = lane == ik
        sel = jnp.where(hit, 1.0, sel)
        work = jnp.where(hit, -jnp.inf, work)
        vals.append(mk)
        idxs.append(ik)
    es = [jnp.exp(v - vals[0]) for v in vals]
    den = es[0] + es[1] + es[2] + es[3]
    ids = jnp.zeros(logits.shape, I32)
    wts = jnp.zeros(logits.shape, F32)
    for k in range(TOP_K):
        ids = jnp.where(lane == k, idxs[k], ids)
        wts = jnp.where(lane == k, es[k] / den, wts)
    ids_ref[...] = ids
    wts_ref[...] = wts
    cnt_ref[0] = jnp.sum(sel, axis=0, keepdims=True)


def _merge(ohg, att_o, att_l, gates, x2, mod3, w_hg, w_at, w_out, npost, npre, w_r, b_r, seq, tm):
    t = x2.shape[0]
    nt = t // tm
    row = lambda w: pl.BlockSpec((tm, w), lambda i: (i, 0))
    slab_i = jax.ShapeDtypeStruct((t, LANES), I32)
    slab_f = jax.ShapeDtypeStruct((t, LANES), F32)
    return pl.pallas_call(
        _merge_kernel,
        grid=(nt,),
        in_specs=[row(HG_W)] + [row(AT_GW)] * 6 + [row(2 * D), row(D),
                  pl.BlockSpec((1, 6, D), lambda i: ((i * tm) // seq, 0, 0)),
                  _const_spec(w_hg.shape), _const_spec(w_at.shape), _const_spec(w_out.shape),
                  _const_spec((1, D)), _const_spec((1, D)),
                  _const_spec(w_r.shape), _const_spec(b_r.shape)],
        out_specs=[row(D), row(D), row(LANES), row(LANES),
                   pl.BlockSpec((1, 1, LANES), lambda i: (i, 0, 0))],
        out_shape=[jax.ShapeDtypeStruct((t, D), F32), jax.ShapeDtypeStruct((t, D), BF16),
                   slab_i, slab_f, jax.ShapeDtypeStruct((nt, 1, LANES), F32)],
        compiler_params=_cp(("parallel",)),
        name="merge",
    )(ohg, *att_o, *att_l, gates, x2, mod3, w_hg, w_at, w_out, npost, npre, w_r, b_r)


def _route_kernel(ids_ref, base_ref, tril_ref, dest_ref):
    ids = ids_ref[...]
    lane = lax.broadcasted_iota(I32, ids.shape, 1)
    hits = [lane == ids[:, k:k + 1] for k in range(TOP_K)]
    sel = jnp.zeros(ids.shape, F32)
    for hit in hits:
        sel = jnp.where(hit, 1.0, sel)
    before = _dot(tril_ref[...], sel.astype(BF16)) + base_ref[0]
    dest = jnp.zeros(ids.shape, I32)
    for k, hit in enumerate(hits):
        rk = jnp.sum(jnp.where(hit, before, 0.0), axis=-1, keepdims=True)
        dest = jnp.where(lane == k, rk.astype(I32), dest)
    dest_ref[...] = dest


def _route(ids, tile_base, tm):
    t = ids.shape[0]
    tril = jnp.asarray(np.tril(np.ones((tm, tm), np.float32), -1), BF16)
    return pl.pallas_call(
        _route_kernel,
        grid=(t // tm,),
        in_specs=[pl.BlockSpec((tm, LANES), lambda i: (i, 0)),
                  pl.BlockSpec((1, 1, LANES), lambda i: (i, 0, 0)),
                  _const_spec((tm, tm))],
        out_specs=pl.BlockSpec((tm, LANES), lambda i: (i, 0)),
        out_shape=jax.ShapeDtypeStruct((t, LANES), I32),
        compiler_params=_cp(("parallel",)),
        name="route",
    )(ids, tile_base, tril)


def _moe_kernel(be_ref, nu_ref, x_ref, wg_ref, wl_ref, bg_ref, bl_ref, wd_ref, bd_ref, o_ref):
    i = pl.program_id(0)

    @pl.when(i < nu_ref[0])
    def _():
        x = x_ref[...]
        gate = _dot(x, wg_ref[0]) + bg_ref[0]
        up = _dot(x, wl_ref[0]) + bl_ref[0]
        gate = jnp.minimum(gate, SWIGLU_LIMIT)
        up = jnp.clip(up, -SWIGLU_LIMIT, SWIGLU_LIMIT)
        act = (up + 1.0) * gate * jax.nn.sigmoid(SWIGLU_ALPHA * gate)
        y = _dot(act.astype(BF16), wd_ref[0]) + bd_ref[0]
        o_ref[...] = y.astype(o_ref.dtype)

    @pl.when(i >= nu_ref[0])
    def _():
        o_ref[...] = jnp.zeros_like(o_ref)


def _moe(block_expert, n_used, xb, wg, wl, bg, bl, wd, bd, bm):
    r = xb.shape[0]
    nblk = r // bm
    ew = lambda shape: pl.BlockSpec((1,) + shape, lambda i, be, nu: (be[i], 0, 0))
    return pl.pallas_call(
        _moe_kernel,
        grid_spec=pltpu.PrefetchScalarGridSpec(
            num_scalar_prefetch=2,
            grid=(nblk,),
            in_specs=[pl.BlockSpec((bm, D), lambda i, be, nu: (i, 0)),
                      ew((D, D)), ew((D, D)), ew((1, D)), ew((1, D)), ew((D, D)), ew((1, D))],
            out_specs=pl.BlockSpec((bm, D), lambda i, be, nu: (i, 0)),
        ),
        out_shape=jax.ShapeDtypeStruct((r, D), BF16),
        compiler_params=_cp(("arbitrary",)),
        name="moe",
    )(block_expert, n_used, xb, wg, wl, bg, bl, wd, bd)


def _final_kernel(y0_ref, y1_ref, y2_ref, y3_ref, wts_ref, x1_ref, mod_ref, npost_ref, o_ref):
    wts = wts_ref[...]
    y = jnp.zeros(x1_ref.shape, F32)
    for k, y_ref in enumerate((y0_ref, y1_ref, y2_ref, y3_ref)):
        y = y + wts[:, k:k + 1] * y_ref[0].astype(F32)
    gt2 = mod_ref[0, 5:6, :]
    o_ref[...] = x1_ref[...] + gt2 * _rms(y, npost_ref[...])


def _final(yg, wts, x1, mod3, npost, seq, tm):
    t = x1.shape[0]
    slot = lambda k: pl.BlockSpec((1, tm, D), lambda i: (k, i, 0))
    return pl.pallas_call(
        _final_kernel,
        grid=(t // tm,),
        in_specs=[slot(0), slot(1), slot(2), slot(3),
                  pl.BlockSpec((tm, LANES), lambda i: (i, 0)),
                  pl.BlockSpec((tm, D), lambda i: (i, 0)),
                  pl.BlockSpec((1, 6, D), lambda i: ((i * tm) // seq, 0, 0)),
                  _const_spec((1, D))],
        out_specs=pl.BlockSpec((tm, D), lambda i: (i, 0)),
        out_shape=jax.ShapeDtypeStruct((t, D), F32),
        compiler_params=_cp(("parallel",)),
        name="final",
    )(yg, yg, yg, yg, wts, x1, mod3, npost)


def _tiles(seq):
    return dict(tm_in=min(seq, 512), tc=min(seq, 512), tm_merge=min(seq, 512),
                bm=512, tm_final=min(seq, 512))


def _prep_weights(w_in, w_hg_out, w_att_out, w_out, w_router, b_router, w_up, b_up, w_down, b_down):
    w = w_in[0]
    hw = HG_W
    a0 = 5 * hw
    aw = AT_GW * len(AT_DILS)
    q_at, k_at, v_at = (w[:, a0 + i * aw:a0 + (i + 1) * aw] for i in range(3))
    grp = lambda m, g: m[:, g * AT_GW:(g + 1) * AT_GW]
    w_at = jnp.concatenate(
        [jnp.concatenate([grp(q_at, g), grp(k_at, g), grp(v_at, g)], axis=1)
         for g in range(len(AT_DILS))], axis=1)
    pad = LANES - N_EXP
    return dict(
        w_hga=jnp.concatenate([w[:, 0:hw], w[:, 3 * hw:4 * hw], w[:, 4 * hw:5 * hw]], 1).astype(BF16),
        w_hgf=w[:, hw:3 * hw].astype(BF16),
        w_at=w_at.astype(BF16),
        w_gt=w[:, a0 + 3 * aw:].astype(BF16),
        w_hg_out=w_hg_out[0].astype(BF16),
        w_att_out=w_att_out[0].astype(BF16),
        w_out=w_out[0].astype(BF16),
        w_r=jnp.pad(w_router[0], ((0, 0), (0, pad))),
        b_r=jnp.pad(b_router[0], (0, pad), constant_values=NEG_BIG).reshape(1, LANES),
        wg=w_up[0][:, :, 0::2].astype(BF16),
        wl=w_up[0][:, :, 1::2].astype(BF16),
        bg=b_up[0][:, 0::2].reshape(N_EXP, 1, D),
        bl=b_up[0][:, 1::2].reshape(N_EXP, 1, D),
        wd=w_down[0].astype(BF16),
        bd=b_down[0].reshape(N_EXP, 1, D),
    )


def _gather_rows(table, idx):
    return jnp.take(table, idx, axis=0)


def _trunk(x, mod, wts, norm_pre, norm_post, lb_logits, hg_gain):
    batch, seq, _ = x.shape
    t = batch * seq
    tl = _tiles(seq)
    x2 = x.reshape(t, D)
    mod3 = mod.reshape(batch, 6, D)
    rope = _rope_tables(seq)

    hga, hgf, at0, at1, at2, gates = _inproj(
        x2, mod3, norm_pre[0, 0].reshape(1, D), rope,
        wts["w_hga"], wts["w_hgf"], wts["w_at"], wts["w_gt"], seq, tl["tm_in"])

    o_fw = _hgrn(lb_logits[0], hga, hgf, batch, seq, tl["tc"], reverse=False)
    ohg = _hgrn(lb_logits[1], hga, hgf, batch, seq, tl["tc"], reverse=True,
                o_fw=o_fw, gain=hg_gain[0].reshape(1, HG_W))

    att = [_attn_group(a, batch, seq, d) for a, d in zip((at0, at1, at2), AT_DILS)]

    x1, h2, ids, rw, cnt = _merge(
        ohg, [a[0] for a in att], [a[1] for a in att], gates, x2, mod3,
        wts["w_hg_out"], wts["w_att_out"], wts["w_out"],
        norm_post[0, 0].reshape(1, D), norm_pre[0, 1].reshape(1, D),
        wts["w_r"], wts["b_r"], seq, tl["tm_merge"])

    bm = tl["bm"]
    tmr = tl["tm_merge"]
    cnt_tiles = cnt.reshape(t // tmr, LANES).astype(I32)
    total = jnp.sum(cnt_tiles, axis=0)
    padded = (total + bm - 1) // bm * bm
    pad_end = jnp.cumsum(padded)
    pad_start = pad_end - padded
    tile_base = pad_start[None, :] + jnp.cumsum(cnt_tiles, axis=0) - cnt_tiles
    dest = _route(ids, tile_base.astype(F32).reshape(-1, 1, LANES), tmr)[:, :TOP_K]

    n_rows = t * TOP_K + N_EXP * bm
    n_blocks = n_rows // bm
    blk_start = jnp.arange(n_blocks, dtype=I32) * bm
    block_expert = jnp.minimum(
        jnp.searchsorted(pad_end[:N_EXP], blk_start, side="right"), N_EXP - 1).astype(I32)
    n_used = (pad_end[N_EXP - 1] // bm).astype(I32).reshape(1)

    tok = jnp.broadcast_to(jnp.arange(t, dtype=I32)[:, None], (t, TOP_K))
    row_tok = jnp.zeros((n_rows,), I32).at[dest.reshape(-1)].set(tok.reshape(-1))
    xb = _gather_rows(h2, row_tok)
    yb = _moe(block_expert, n_used, xb, wts["wg"], wts["wl"], wts["bg"], wts["bl"],
              wts["wd"], wts["bd"], bm)
    yg = _gather_rows(yb, dest.T.reshape(-1)).reshape(TOP_K, t, D)

    out = _final(yg, rw, x1, mod3, norm_post[0, 1].reshape(1, D), seq, tl["tm_final"])
    return out.reshape(batch, seq, D)


def kernel(x_prompt, x_sample, c_prompt, c_sample, w_ada, b_ada, norm_pre, norm_post, w_in,
           lb_logits, hg_norm_gain, w_hg_out, w_att_out, w_out, w_router, b_router,
           w_up, b_up, w_down, b_down):
    wts = _prep_weights(w_in, w_hg_out, w_att_out, w_out, w_router, b_router,
                        w_up, b_up, w_down, b_down)
    nb = c_prompt.shape[0]
    mod = _ada(jnp.concatenate([c_prompt, c_sample], axis=0), w_ada[0], b_ada[0])
    lb_l = lb_logits.astype(F32)
    y_p = _trunk(x_prompt, mod[:nb], wts, norm_pre, norm_post, lb_l, hg_norm_gain)
    y_s = _trunk(x_sample, mod[nb:], wts, norm_pre, norm_post, lb_l, hg_norm_gain)
    return (y_p, y_s)
```

```python
import functools
import math

import numpy as np
import jax
import jax.numpy as jnp
from jax import lax
from jax.experimental import pallas as pl
from jax.experimental.pallas import tpu as pltpu
from jax.experimental.pallas import tpu_sc as plsc

F32 = jnp.float32
BF16 = jnp.bfloat16
I32 = jnp.int32
U32 = jnp.uint32

D = 1024
EPS = 1e-6
HG_H = 4
HG_DK = 128
HG_W = HG_H * HG_DK
HG_C = 64
HG_LEVELS = 6
HG_SAFE_LOGDECAY = -80.0
AT_DILS = (1, 4, 16)
AT_HALF = 64
AT_H = 4
AT_HD = 64
AT_GW = AT_H * AT_HD
ROPE_DIM = 16
ROPE_THETA = 500000.0
N_EXP = 32
TOP_K = 4
SWIGLU_LIMIT = 7.0
SWIGLU_ALPHA = 1.702
LANES = 128
NEG_BIG = -1e30
HALF_D = D // 2
SC_CORES = 2
SC_SUBCORES = 16
SC_WORKERS = SC_CORES * SC_SUBCORES
SC_CHUNK = 128

VMEM_LIMIT = 56 * 1024 * 1024


def _cp(sem, vmem=VMEM_LIMIT):
    return pltpu.CompilerParams(dimension_semantics=sem, vmem_limit_bytes=vmem)


def _dot(a, b):
    return jnp.dot(a, b, preferred_element_type=F32)


def _dot_nt(a, b):
    return lax.dot_general(a, b, (((1,), (1,)), ((), ())), preferred_element_type=F32)


def _dot_tn(a, b):
    return lax.dot_general(a, b, (((0,), (0,)), ((), ())), preferred_element_type=F32)


def _split3(x):
    hi = x.astype(BF16)
    r = x - hi.astype(F32)
    mid = r.astype(BF16)
    lo = (r - mid.astype(F32)).astype(BF16)
    return hi, mid, lo


def _rms(x, gain):
    ms = jnp.mean(x * x, axis=-1, keepdims=True)
    return x * lax.rsqrt(ms + EPS) * gain


def _const_spec(shape):
    n = len(shape)
    return pl.BlockSpec(shape, lambda *_: (0,) * n)


def _pack_rows(y):
    bits = lambda a: lax.bitcast_convert_type(a.astype(BF16).astype(F32), U32)
    return (bits(y[:, :HALF_D]) >> 16) | (bits(y[:, HALF_D:]) & jnp.uint32(0xFFFF0000))


def _unpack_rows(w):
    lo = lax.bitcast_convert_type(w << 16, F32)
    hi = lax.bitcast_convert_type(w & jnp.uint32(0xFFFF0000), F32)
    return lo, hi


def _deint_kernel(w_ref, p_ref, g_ref, l_ref):
    p = p_ref[...]
    n_grp = w_ref.shape[2] // (2 * LANES)
    for b in range(n_grp):
        blk = w_ref[0, :, b * 2 * LANES:(b + 1) * 2 * LANES].astype(BF16)
        r = _dot(blk, p)
        g_ref[0, :, b * LANES:(b + 1) * LANES] = r[:, :LANES].astype(BF16)
        l_ref[0, :, b * LANES:(b + 1) * LANES] = r[:, LANES:].astype(BF16)


def _deinterleave(w_up):
    n_e, d, n2 = w_up.shape
    perm = np.zeros((2 * LANES, 2 * LANES), np.float32)
    perm[2 * np.arange(LANES), np.arange(LANES)] = 1.0
    perm[2 * np.arange(LANES) + 1, LANES + np.arange(LANES)] = 1.0
    tr = 512
    out = jax.ShapeDtypeStruct((n_e, d, n2 // 2), BF16)
    return pl.pallas_call(
        _deint_kernel,
        grid=(n_e, d // tr),
        in_specs=[pl.BlockSpec((1, tr, n2), lambda e, i: (e, i, 0)),
                  _const_spec(perm.shape)],
        out_specs=[pl.BlockSpec((1, tr, n2 // 2), lambda e, i: (e, i, 0))] * 2,
        out_shape=[out, out],
        compiler_params=_cp(("parallel", "parallel")),
        name="deinterleave",
    )(w_up, jnp.asarray(perm, BF16))


def _sc_mesh():
    return plsc.VectorSubcoreMesh(core_axis_name="c", subcore_axis_name="s")


def _sc_worker():
    return lax.axis_index("s") * SC_CORES + lax.axis_index("c")


def _sc_scatter_rows(src, dest_kt, n_rows):
    t, w = src.shape
    n_slot = dest_kt.shape[0]
    nch = t // (SC_WORKERS * SC_CHUNK)
    idx = dest_kt.reshape(n_slot, SC_WORKERS, nch, SC_CHUNK).transpose(1, 0, 2, 3)

    @functools.partial(
        pl.kernel, mesh=_sc_mesh(),
        out_type=jax.ShapeDtypeStruct((n_rows, w), src.dtype),
        scratch_types=[pltpu.VMEM((n_slot, nch, SC_CHUNK), I32),
                       pltpu.VMEM((SC_CHUNK, w), src.dtype)],
        name="sc_scatter")
    def run(src_hbm, idx_hbm, out_hbm, idx_v, rows_v):
        wid = _sc_worker()
        pltpu.sync_copy(idx_hbm.at[wid], idx_v)

        @pl.loop(0, nch)
        def _(j):
            pltpu.sync_copy(src_hbm.at[pl.ds((wid * nch + j) * SC_CHUNK, SC_CHUNK)], rows_v)
            for k in range(n_slot):
                pltpu.sync_copy(rows_v, out_hbm.at[idx_v.at[k, j]])

    return run(src, idx)


def _sc_gather_rows(table, idx):
    n = idx.shape[0]
    w = table.shape[1]
    nch = n // (SC_WORKERS * SC_CHUNK)
    idx3 = idx.reshape(SC_WORKERS, nch, SC_CHUNK)

    @functools.partial(
        pl.kernel, mesh=_sc_mesh(),
        out_type=jax.ShapeDtypeStruct((n, w), table.dtype),
        scratch_types=[pltpu.VMEM((nch, SC_CHUNK), I32),
                       pltpu.VMEM((SC_CHUNK, w), table.dtype)],
        name="sc_gather")
    def run(table_hbm, idx_hbm, out_hbm, idx_v, rows_v):
        wid = _sc_worker()
        pltpu.sync_copy(idx_hbm.at[wid], idx_v)

        @pl.loop(0, nch)
        def _(j):
            pltpu.sync_copy(table_hbm.at[idx_v.at[j]], rows_v)
            pltpu.sync_copy(rows_v, out_hbm.at[pl.ds((wid * nch + j) * SC_CHUNK, SC_CHUNK)])

    return run(table, idx3)


def _ada_kernel(c_ref, w_ref, b_ref, o_ref):
    c = c_ref[...]
    a = c * jax.nn.sigmoid(c)
    w = w_ref[...]
    a_hi = a.astype(BF16)
    a_lo = (a - a_hi.astype(F32)).astype(BF16)
    w_hi = w.astype(BF16)
    w_lo = (w - w_hi.astype(F32)).astype(BF16)
    o_ref[...] = _dot(a_hi, w_hi) + _dot(a_lo, w_hi) + _dot(a_hi, w_lo) + b_ref[...]


def _ada(c, w, b):
    nb = c.shape[0]
    n = w.shape[1]
    tn = 1536
    return pl.pallas_call(
        _ada_kernel,
        grid=(n // tn,),
        in_specs=[pl.BlockSpec((nb, D), lambda j: (0, 0)),
                  pl.BlockSpec((D, tn), lambda j: (0, j)),
                  pl.BlockSpec((1, tn), lambda j: (0, j))],
        out_specs=pl.BlockSpec((nb, tn), lambda j: (0, j)),
        out_shape=jax.ShapeDtypeStruct((nb, n), F32),
        compiler_params=_cp(("parallel",)),
        name="ada",
    )(c, w, b.reshape(1, n))


def _inproj_kernel(x_ref, mod_ref, gain_ref, cos_ref, s1_ref, s2_ref,
                   w_hga_ref, w_hgf_ref, w_at_ref, w_gt_ref,
                   hga_ref, hgf_ref, at0_ref, at1_ref, at2_ref, gt_ref):
    x = x_ref[...]
    sh = mod_ref[0, 0:1, :]
    sc = mod_ref[0, 1:2, :]
    h = _rms(x, gain_ref[...]) * (1.0 + sc) + sh
    hb = h.astype(BF16)
    hga_ref[...] = _dot(hb, w_hga_ref[...]).astype(BF16)
    hgf_ref[...] = _dot(hb, w_hgf_ref[...])
    cos = cos_ref[...]
    s1 = s1_ref[...]
    s2 = s2_ref[...]
    for g, o_ref in enumerate((at0_ref, at1_ref, at2_ref)):
        acc = _dot(hb, w_at_ref[:, g * 3 * AT_GW:(g + 1) * 3 * AT_GW])
        parts = []
        for j in range(2 * AT_GW // LANES):
            a = acc[:, j * LANES:(j + 1) * LANES]
            parts.append(a * cos + pltpu.roll(a, ROPE_DIM // 2, 1) * s1
                         + pltpu.roll(a, LANES - ROPE_DIM // 2, 1) * s2)
        parts.append(acc[:, 2 * AT_GW:])
        o_ref[...] = jnp.concatenate(parts, axis=1).astype(BF16)
    gt_ref[...] = _dot(hb, w_gt_ref[...]).astype(BF16)


def _inproj(x2, mod3, gain, rope, w_hga, w_hgf, w_at, w_gt, seq, tm):
    t = x2.shape[0]
    n_pos_blk = seq // tm
    row = lambda w: pl.BlockSpec((tm, w), lambda i: (i, 0))
    pos = pl.BlockSpec((tm, LANES), lambda i: (i % n_pos_blk, 0))
    return pl.pallas_call(
        _inproj_kernel,
        grid=(t // tm,),
        in_specs=[row(D),
                  pl.BlockSpec((1, 6, D), lambda i: ((i * tm) // seq, 0, 0)),
                  _const_spec((1, D)), pos, pos, pos,
                  _const_spec(w_hga.shape), _const_spec(w_hgf.shape),
                  _const_spec(w_at.shape), _const_spec(w_gt.shape)],
        out_specs=[row(3 * HG_W), row(2 * HG_W), row(3 * AT_GW), row(3 * AT_GW), row(3 * AT_GW),
                   row(2 * D)],
        out_shape=[jax.ShapeDtypeStruct((t, 3 * HG_W), BF16),
                   jax.ShapeDtypeStruct((t, 2 * HG_W), F32),
                   jax.ShapeDtypeStruct((t, 3 * AT_GW), BF16),
                   jax.ShapeDtypeStruct((t, 3 * AT_GW), BF16),
                   jax.ShapeDtypeStruct((t, 3 * AT_GW), BF16),
                   jax.ShapeDtypeStruct((t, 2 * D), BF16)],
        compiler_params=_cp(("parallel",)),
        name="inproj",
    )(x2, mod3, gain, *rope, w_hga, w_hgf, w_at, w_gt)


def _rope_tables(seq):
    half = ROPE_DIM // 2
    inv_freq = ROPE_THETA ** (-np.arange(half, dtype=np.float32) / half)
    ang = jnp.arange(seq, dtype=F32)[:, None] * jnp.asarray(inv_freq)[None, :]
    cos, sin = jnp.cos(ang), jnp.sin(ang)
    ones = jnp.ones((seq, AT_HD - ROPE_DIM), F32)
    zeros = jnp.zeros((seq, AT_HD - ROPE_DIM), F32)
    zh = jnp.zeros((seq, half), F32)
    c_head = jnp.concatenate([cos, cos, ones], axis=1)
    s1_head = jnp.concatenate([zh, sin, zeros], axis=1)
    s2_head = jnp.concatenate([-sin, zh, zeros], axis=1)
    rep = LANES // AT_HD
    return tuple(jnp.tile(a, (1, rep)) for a in (c_head, s1_head, s2_head))


def _hgrn_consts(reverse):
    c = HG_C
    t = np.arange(c)[:, None]
    u = np.arange(c)[None, :]
    tri = (u >= t) if reverse else (u <= t)
    sels = []
    for lvl in range(HG_LEVELS):
        blk = 2 << lvl
        start = (np.arange(c) // blk) * blk
        piv = start + (1 << lvl) - (0 if reverse else 1)
        sels.append(np.arange(c)[None, :] == piv[:, None])
    sel = np.concatenate(sels, axis=0)
    three = lambda m: jnp.asarray(np.concatenate([m, m, m], axis=1), BF16)
    return three(tri.astype(np.float32)), three(sel.astype(np.float32))


def _hgrn_kernel(*refs, reverse, final, n_chunks):
    if final:
        (lbl_ref, tri_ref, sel_ref, q_ref, f_ref, v_ref, ofw_ref, gh_ref, gain_ref,
         o_ref, st_ref, k_s, b_s) = refs
    else:
        (lbl_ref, tri_ref, sel_ref, q_ref, f_ref, v_ref,
         o_ref, st_ref, k_s, b_s) = refs
    c = HG_C

    @pl.when(pl.program_id(1) == 0)
    def _():
        st_ref[...] = jnp.zeros_like(st_ref)

    l0 = lbl_ref[0:1, :]
    l1 = lbl_ref[1:2, :]
    lm = jnp.maximum(l0, l1)
    e0 = jnp.exp(l0 - lm)
    e1 = jnp.exp(l1 - lm)
    lb = e0 / (e0 + e1)
    tri = tri_ref[...]

    def gates(ci, bmin):
        r0 = pl.multiple_of(ci * c, c)
        ff = f_ref[pl.ds(r0, c), :]
        e = jnp.exp(-jnp.abs(ff))
        r = 1.0 / (1.0 + e)
        pos = ff >= 0.0
        sg = jnp.where(pos, r, e * r)
        sgn = jnp.where(pos, e * r, r)
        g = jnp.log(lb + (1.0 - lb) * sg)
        k_s[pl.ds(r0, c), :] = (1.0 - lb) * sgn
        b = _dot(tri, jnp.concatenate(_split3(g), axis=0))
        b_s[pl.ds(r0, c), :] = b
        return jnp.minimum(bmin, jnp.min(b, axis=0, keepdims=True))

    bmin = lax.fori_loop(0, n_chunks, gates, jnp.zeros((1, HG_W), F32))
    safe = jnp.min(bmin) >= HG_SAFE_LOGDECAY

    ti = lax.broadcasted_iota(I32, (c, c), 0)
    si = lax.broadcasted_iota(I32, (c, c), 1)
    causal = (si >= ti) if reverse else (si <= ti)
    row = lax.broadcasted_iota(I32, (c, 1), 0)

    def att_fast(qe, kin, b):
        ke = (kin * jnp.exp(-b)).astype(BF16)
        return jnp.where(causal, _dot_nt(qe, ke), 0.0)

    def att_robust(q, kin, b, piv):
        att = jnp.where(ti == si, _dot_nt(q.astype(BF16), kin.astype(BF16)), 0.0)
        for lvl in range(HG_LEVELS):
            p = piv[lvl * c:(lvl + 1) * c, :]
            bit = ((row >> lvl) & 1) == 1
            q_side = jnp.logical_not(bit) if reverse else bit
            qe = jnp.where(q_side, q * jnp.exp(jnp.minimum(b - p, 0.0)), 0.0).astype(BF16)
            ke = jnp.where(q_side, 0.0, kin * jnp.exp(jnp.minimum(p - b, 0.0))).astype(BF16)
            same = (ti >> (lvl + 1)) == (si >> (lvl + 1))
            att = att + jnp.where(same, _dot_nt(qe, ke), 0.0)
        return att

    def make_step(robust):
        def step(i, carry):
            ci = (n_chunks - 1 - i) if reverse else i
            r0 = pl.multiple_of(ci * c, c)
            q = q_ref[pl.ds(r0, c), :].astype(F32)
            v = v_ref[pl.ds(r0, c), :]
            kin = k_s[pl.ds(r0, c), :]
            b = b_s[pl.ds(r0, c), :]
            btot = b[0:1, :] if reverse else b[c - 1:c, :]
            qe_all = (q * jnp.exp(b)).astype(BF16)
            kd_all = (kin * jnp.exp(btot - b)).astype(BF16)
            if robust:
                piv = _dot(sel_ref[...], jnp.concatenate(_split3(b), axis=0))
            outs = []
            for h in range(HG_H):
                sl = slice(h * HG_DK, (h + 1) * HG_DK)
                qe = qe_all[:, sl]
                if robust:
                    att = att_robust(q[:, sl], kin[:, sl], b[:, sl], piv[:, sl])
                else:
                    att = att_fast(qe, kin[:, sl], b[:, sl])
                st = st_ref[h]
                o_h = _dot_nt(qe, st.astype(BF16)) + _dot(att.astype(BF16), v[:, sl])
                st_ref[h] = st * jnp.exp(btot[:, sl]) + _dot_tn(v[:, sl], kd_all[:, sl])
                outs.append(o_h)
            o = jnp.concatenate(outs, axis=1)
            if final:
                o = o + ofw_ref[pl.ds(r0, c), :]
                normed = []
                for h in range(HG_H):
                    o_h = o[:, h * HG_DK:(h + 1) * HG_DK]
                    ms = jnp.mean(o_h * o_h, axis=-1, keepdims=True)
                    normed.append(o_h * lax.rsqrt(ms + EPS))
                gh = gh_ref[pl.ds(r0, c), :].astype(F32)
                o = jnp.concatenate(normed, axis=1) * gain_ref[...] * (gh * jax.nn.sigmoid(gh))
                o_ref[pl.ds(r0, c), :] = o.astype(o_ref.dtype)
            else:
                o_ref[pl.ds(r0, c), :] = o
            return carry
        return step

    def run(robust):
        def f():
            lax.fori_loop(0, n_chunks, make_step(robust), 0)
        return f

    lax.cond(safe, run(False), run(True))


def _hgrn(lb_logits_d, hga, hgf, batch, seq, tc, reverse, o_fw=None, gain=None):
    final = o_fw is not None
    t = hga.shape[0]
    nblk = seq // tc
    tri, sel = _hgrn_consts(reverse)

    def rows(col):
        if reverse:
            return lambda b, j: (b * nblk + nblk - 1 - j, col)
        return lambda b, j: (b * nblk + j, col)

    blk = lambda col: pl.BlockSpec((tc, HG_W), rows(col))
    in_specs = [_const_spec((2, HG_W)), _const_spec(tri.shape), _const_spec(sel.shape),
                blk(0), blk(1 if reverse else 0), blk(1)]
    args = [lb_logits_d, tri, sel, hga, hgf, hga]
    if final:
        in_specs += [blk(0), blk(2), _const_spec((1, HG_W))]
        args += [o_fw, hga, gain]
    return pl.pallas_call(
        functools.partial(_hgrn_kernel, reverse=reverse, final=final, n_chunks=tc // HG_C),
        grid=(batch, nblk),
        in_specs=in_specs,
        out_specs=blk(0),
        out_shape=jax.ShapeDtypeStruct((t, HG_W), BF16 if final else F32),
        scratch_shapes=[pltpu.VMEM((HG_H, HG_DK, HG_DK), F32),
                        pltpu.VMEM((tc, HG_W), F32),
                        pltpu.VMEM((tc, HG_W), F32)],
        compiler_params=_cp(("parallel", "arbitrary")),
        name="hgrn_bwd" if reverse else "hgrn_fwd",
    )(*args)


def _attn_kernel(q_ref, kl_ref, km_ref, kr_ref, vl_ref, vm_ref, vr_ref, o_ref, l_ref,
                 k_s, v_s, *, tq, ld):
    hw = AT_HALF
    k_s[0:hw, :] = kl_ref[0]
    k_s[hw:hw + tq, :] = km_ref[0]
    k_s[hw + tq:, :] = kr_ref[0]
    v_s[0:hw, :] = vl_ref[0]
    v_s[hw:hw + tq, :] = vm_ref[0]
    v_s[hw + tq:, :] = vr_ref[0]
    qb = 2 * hw
    kb = 4 * hw
    base = pl.program_id(2) * tq
    scale = AT_HD ** -0.5
    for qs in range(0, tq, qb):
        q = q_ref[0, qs:qs + qb, :]
        kk = k_s[qs:qs + kb, :]
        vv = v_s[qs:qs + kb, :]
        qpos = base + qs + lax.broadcasted_iota(I32, (qb, kb), 0)
        kpos = base + qs - hw + lax.broadcasted_iota(I32, (qb, kb), 1)
        mask = (jnp.abs(qpos - kpos) <= hw) & (kpos >= 0) & (kpos < ld)
        outs, lses = [], []
        for h in range(AT_H):
            sl = slice(h * AT_HD, (h + 1) * AT_HD)
            s = _dot_nt(q[:, sl], kk[:, sl]) * scale
            s = jnp.where(mask, s, NEG_BIG)
            m = jnp.max(s, axis=-1, keepdims=True)
            p = jnp.exp(s - m)
            l = jnp.sum(p, axis=-1, keepdims=True)
            o = _dot(p.astype(BF16), vv[:, sl]) * (1.0 / l)
            outs.append(o)
            lses.append(jnp.broadcast_to(m + jnp.log(l), (qb, AT_HD)))
        o_ref[0, qs:qs + qb, :] = jnp.concatenate(outs, axis=1).astype(o_ref.dtype)
        l_ref[0, qs:qs + qb, :] = jnp.concatenate(lses, axis=1)


def _attn_group(qkv, batch, seq, dil):
    ld = seq // dil
    hw = AT_HALF
    tq = min(ld, 512)
    nq = ld // tq
    nh = ld // hw
    per = tq // hw
    view = qkv.reshape(batch, ld, dil * 3 * AT_GW)
    main = lambda part: pl.BlockSpec((1, tq, AT_GW), lambda b, r, j: (b, j, 3 * r + part))
    left = lambda part: pl.BlockSpec(
        (1, hw, AT_GW), lambda b, r, j: (b, jnp.maximum(j * per - 1, 0), 3 * r + part))
    right = lambda part: pl.BlockSpec(
        (1, hw, AT_GW), lambda b, r, j: (b, jnp.minimum((j + 1) * per, nh - 1), 3 * r + part))
    out_spec = pl.BlockSpec((1, tq, AT_GW), lambda b, r, j: (b, j, r))
    o, lse = pl.pallas_call(
        functools.partial(_attn_kernel, tq=tq, ld=ld),
        grid=(batch, dil, nq),
        in_specs=[main(0), left(1), main(1), right(1), left(2), main(2), right(2)],
        out_specs=[out_spec, out_spec],
        out_shape=[jax.ShapeDtypeStruct((batch, ld, dil * AT_GW), BF16),
                   jax.ShapeDtypeStruct((batch, ld, dil * AT_GW), F32)],
        scratch_shapes=[pltpu.VMEM((tq + 2 * hw, AT_GW), BF16),
                        pltpu.VMEM((tq + 2 * hw, AT_GW), BF16)],
        compiler_params=_cp(("parallel", "parallel", "parallel")),
        name=f"attn_d{dil}",
    )(view, view, view, view, view, view, view)
    return o.reshape(batch * seq, AT_GW), lse.reshape(batch * seq, AT_GW)


def _merge_kernel(ohg_ref, o0_ref, o1_ref, o2_ref, l0_ref, l1_ref, l2_ref, gt_ref, x_ref, mod_ref,
                  w_hg_ref, w_at_ref, w_out_ref, npost_ref, npre_ref, w_r_ref, b_r_ref,
                  x1_ref, h2_ref, ids_ref, wts_ref, cnt_ref):
    l0, l1, l2 = l0_ref[...], l1_ref[...], l2_ref[...]
    m = jnp.maximum(jnp.maximum(l0, l1), l2)
    e0, e1, e2 = jnp.exp(l0 - m), jnp.exp(l1 - m), jnp.exp(l2 - m)
    oa = (e0 * o0_ref[...].astype(F32) + e1 * o1_ref[...].astype(F32)
          + e2 * o2_ref[...].astype(F32)) / (e0 + e1 + e2)
    b_hg = _dot(ohg_ref[...], w_hg_ref[...])
    b_at = _dot(oa.astype(BF16), w_at_ref[...])
    g_hg = jax.nn.sigmoid(gt_ref[:, 0:D].astype(F32))
    g_at = jax.nn.sigmoid(gt_ref[:, D:2 * D].astype(F32))
    merged = g_hg * b_hg + g_at * b_at
    y = _dot(merged.astype(BF16), w_out_ref[...])
    gt1 = mod_ref[0, 2:3, :]
    sh2 = mod_ref[0, 3:4, :]
    sc2 = mod_ref[0, 4:5, :]
    x1 = x_ref[...] + gt1 * _rms(y, npost_ref[...])
    x1_ref[...] = x1
    h2 = _rms(x1, npre_ref[...]) * (1.0 + sc2) + sh2
    h2_ref[...] = _pack_rows(h2)

    w_r = w_r_ref[...]
    h_hi = h2.astype(BF16)
    h_lo = (h2 - h_hi.astype(F32)).astype(BF16)
    w_hi = w_r.astype(BF16)
    w_lo = (w_r - w_hi.astype(F32)).astype(BF16)
    logits = _dot(h_hi, w_hi) + _dot(h_lo, w_hi) + _dot(h_hi, w_lo) + b_r_ref[...]

    lane = lax.broadcasted_iota(I32, logits.shape, 1)
    work = logits
    vals, idxs = [], []
    sel = jnp.zeros(logits.shape, F32)
    for _ in range(TOP_K):
        mk = jnp.max(work, axis=-1, keepdims=True)
        ik = jnp.min(jnp.where(work == mk, lane, LANES), axis=-1, keepdims=True)
        hit = lane == ik
        sel = jnp.where(hit, 1.0, sel)
        work = jnp.where(hit, -jnp.inf, work)
        vals.append(mk)
        idxs.append(ik)
    es = [jnp.exp(v - vals[0]) for v in vals]
    den = es[0] + es[1] + es[2] + es[3]
    ids = jnp.zeros(logits.shape, I32)
    wts = jnp.zeros(logits.shape, F32)
    for k in range(TOP_K):
        ids = jnp.where(lane == k, idxs[k], ids)
        wts = jnp.where(lane == k, es[k] / den, wts)
    ids_ref[...] = ids
    wts_ref[...] = wts
    cnt_ref[0] = jnp.sum(sel, axis=0, keepdims=True)


def _merge(ohg, att_o, att_l, gates, x2, mod3, w_hg, w_at, w_out, npost, npre, w_r, b_r, seq, tm):
    t = x2.shape[0]
    nt = t // tm
    row = lambda w: pl.BlockSpec((tm, w), lambda i: (i, 0))
    slab_i = jax.ShapeDtypeStruct((t, LANES), I32)
    slab_f = jax.ShapeDtypeStruct((t, LANES), F32)
    return pl.pallas_call(
        _merge_kernel,
        grid=(nt,),
        in_specs=[row(HG_W)] + [row(AT_GW)] * 6 + [row(2 * D), row(D),
                  pl.BlockSpec((1, 6, D), lambda i: ((i * tm) // seq, 0, 0)),
                  _const_spec(w_hg.shape), _const_spec(w_at.shape), _const_spec(w_out.shape),
                  _const_spec((1, D)), _const_spec((1, D)),
                  _const_spec(w_r.shape), _const_spec(b_r.shape)],
        out_specs=[row(D), row(HALF_D), row(LANES), row(LANES),
                   pl.BlockSpec((1, 1, LANES), lambda i: (i, 0, 0))],
        out_shape=[jax.ShapeDtypeStruct((t, D), F32), jax.ShapeDtypeStruct((t, HALF_D), U32),
                   slab_i, slab_f, jax.ShapeDtypeStruct((nt, 1, LANES), F32)],
        compiler_params=_cp(("parallel",)),
        name="merge",
    )(ohg, *att_o, *att_l, gates, x2, mod3, w_hg, w_at, w_out, npost, npre, w_r, b_r)


def _route_kernel(ids_ref, base_ref, tril_ref, dest_ref):
    ids = ids_ref[...]
    lane = lax.broadcasted_iota(I32, ids.shape, 1)
    hits = [lane == ids[:, k:k + 1] for k in range(TOP_K)]
    sel = jnp.zeros(ids.shape, F32)
    for hit in hits:
        sel = jnp.where(hit, 1.0, sel)
    before = _dot(tril_ref[...], sel.astype(BF16)) + base_ref[0]
    dest = jnp.zeros(ids.shape, I32)
    for k, hit in enumerate(hits):
        rk = jnp.sum(jnp.where(hit, before, 0.0), axis=-1, keepdims=True)
        dest = jnp.where(lane == k, rk.astype(I32), dest)
    dest_ref[...] = dest


def _route(ids, tile_base, tm):
    t = ids.shape[0]
    tril = jnp.asarray(np.tril(np.ones((tm, tm), np.float32), -1), BF16)
    return pl.pallas_call(
        _route_kernel,
        grid=(t // tm,),
        in_specs=[pl.BlockSpec((tm, LANES), lambda i: (i, 0)),
                  pl.BlockSpec((1, 1, LANES), lambda i: (i, 0, 0)),
                  _const_spec((tm, tm))],
        out_specs=pl.BlockSpec((tm, LANES), lambda i: (i, 0)),
        out_shape=jax.ShapeDtypeStruct((t, LANES), I32),
        compiler_params=_cp(("parallel",)),
        name="route",
    )(ids, tile_base, tril)


def _moe_kernel(be_ref, nu_ref, x_ref, wg_ref, wl_ref, bg_ref, bl_ref, wd_ref, bd_ref, o_ref):
    i = pl.program_id(0)

    @pl.when(i < nu_ref[0])
    def _():
        x_lo, x_hi = (a.astype(BF16) for a in _unpack_rows(x_ref[...]))
        gate = (_dot(x_lo, wg_ref[0, :HALF_D, :]) + _dot(x_hi, wg_ref[0, HALF_D:, :])
                + bg_ref[0])
        up = (_dot(x_lo, wl_ref[0, :HALF_D, :]) + _dot(x_hi, wl_ref[0, HALF_D:, :])
              + bl_ref[0])
        gate = jnp.minimum(gate, SWIGLU_LIMIT)
        up = jnp.clip(up, -SWIGLU_LIMIT, SWIGLU_LIMIT)
        act = (up + 1.0) * gate * jax.nn.sigmoid(SWIGLU_ALPHA * gate)
        y = _dot(act.astype(BF16), wd_ref[0]) + bd_ref[0]
        o_ref[...] = _pack_rows(y)

    @pl.when(i >= nu_ref[0])
    def _():
        o_ref[...] = jnp.zeros_like(o_ref)


def _moe(block_expert, n_used, xb, wg, wl, bg, bl, wd, bd, bm):
    r = xb.shape[0]
    nblk = r // bm
    ew = lambda shape: pl.BlockSpec((1,) + shape, lambda i, be, nu: (be[i], 0, 0))
    return pl.pallas_call(
        _moe_kernel,
        grid_spec=pltpu.PrefetchScalarGridSpec(
            num_scalar_prefetch=2,
            grid=(nblk,),
            in_specs=[pl.BlockSpec((bm, HALF_D), lambda i, be, nu: (i, 0)),
                      ew((D, D)), ew((D, D)), ew((1, D)), ew((1, D)), ew((D, D)), ew((1, D))],
            out_specs=pl.BlockSpec((bm, HALF_D), lambda i, be, nu: (i, 0)),
        ),
        out_shape=jax.ShapeDtypeStruct((r, HALF_D), U32),
        compiler_params=_cp(("arbitrary",)),
        name="moe",
    )(block_expert, n_used, xb, wg, wl, bg, bl, wd, bd)


def _final_kernel(y0_ref, y1_ref, y2_ref, y3_ref, wts_ref, x1_ref, mod_ref, npost_ref, o_ref):
    wts = wts_ref[...]
    y = jnp.zeros(x1_ref.shape, F32)
    for k, y_ref in enumerate((y0_ref, y1_ref, y2_ref, y3_ref)):
        y = y + wts[:, k:k + 1] * jnp.concatenate(_unpack_rows(y_ref[0]), axis=1)
    gt2 = mod_ref[0, 5:6, :]
    o_ref[...] = x1_ref[...] + gt2 * _rms(y, npost_ref[...])


def _final(yg, wts, x1, mod3, npost, seq, tm):
    t = x1.shape[0]
    slot = lambda k: pl.BlockSpec((1, tm, HALF_D), lambda i: (k, i, 0))
    return pl.pallas_call(
        _final_kernel,
        grid=(t // tm,),
        in_specs=[slot(0), slot(1), slot(2), slot(3),
                  pl.BlockSpec((tm, LANES), lambda i: (i, 0)),
                  pl.BlockSpec((tm, D), lambda i: (i, 0)),
                  pl.BlockSpec((1, 6, D), lambda i: ((i * tm) // seq, 0, 0)),
                  _const_spec((1, D))],
        out_specs=pl.BlockSpec((tm, D), lambda i: (i, 0)),
        out_shape=jax.ShapeDtypeStruct((t, D), F32),
        compiler_params=_cp(("parallel",)),
        name="final",
    )(yg, yg, yg, yg, wts, x1, mod3, npost)


def _tiles(seq):
    return dict(tm_in=min(seq, 512), tc=min(seq, 512), tm_merge=min(seq, 512),
                bm=512, tm_final=min(seq, 512))


def _prep_weights(w_in, w_hg_out, w_att_out, w_out, w_router, b_router, w_up, b_up, w_down, b_down):
    w = w_in[0]
    hw = HG_W
    a0 = 5 * hw
    aw = AT_GW * len(AT_DILS)
    q_at, k_at, v_at = (w[:, a0 + i * aw:a0 + (i + 1) * aw] for i in range(3))
    grp = lambda m, g: m[:, g * AT_GW:(g + 1) * AT_GW]
    w_at = jnp.concatenate(
        [jnp.concatenate([grp(q_at, g), grp(k_at, g), grp(v_at, g)], axis=1)
         for g in range(len(AT_DILS))], axis=1)
    pad = LANES - N_EXP
    wg, wl = _deinterleave(w_up[0])
    return dict(
        wg=wg, wl=wl,
        w_hga=jnp.concatenate([w[:, 0:hw], w[:, 3 * hw:4 * hw], w[:, 4 * hw:5 * hw]], 1).astype(BF16),
        w_hgf=w[:, hw:3 * hw].astype(BF16),
        w_at=w_at.astype(BF16),
        w_gt=w[:, a0 + 3 * aw:].astype(BF16),
        w_hg_out=w_hg_out[0].astype(BF16),
        w_att_out=w_att_out[0].astype(BF16),
        w_out=w_out[0].astype(BF16),
        w_r=jnp.pad(w_router[0], ((0, 0), (0, pad))),
        b_r=jnp.pad(b_router[0], (0, pad), constant_values=NEG_BIG).reshape(1, LANES),
        bg=b_up[0][:, 0::2].reshape(N_EXP, 1, D),
        bl=b_up[0][:, 1::2].reshape(N_EXP, 1, D),
        wd=w_down[0].astype(BF16),
        bd=b_down[0].reshape(N_EXP, 1, D),
    )


def _trunk(x, mod, wts, norm_pre, norm_post, lb_logits, hg_gain):
    batch, seq, _ = x.shape
    t = batch * seq
    tl = _tiles(seq)
    x2 = x.reshape(t, D)
    mod3 = mod.reshape(batch, 6, D)
    rope = _rope_tables(seq)

    hga, hgf, at0, at1, at2, gates = _inproj(
        x2, mod3, norm_pre[0, 0].reshape(1, D), rope,
        wts["w_hga"], wts["w_hgf"], wts["w_at"], wts["w_gt"], seq, tl["tm_in"])

    o_fw = _hgrn(lb_logits[0], hga, hgf, batch, seq, tl["tc"], reverse=False)
    ohg = _hgrn(lb_logits[1], hga, hgf, batch, seq, tl["tc"], reverse=True,
                o_fw=o_fw, gain=hg_gain[0].reshape(1, HG_W))

    att = [_attn_group(a, batch, seq, d) for a, d in zip((at0, at1, at2), AT_DILS)]

    x1, h2, ids, rw, cnt = _merge(
        ohg, [a[0] for a in att], [a[1] for a in att], gates, x2, mod3,
        wts["w_hg_out"], wts["w_att_out"], wts["w_out"],
        norm_post[0, 0].reshape(1, D), norm_pre[0, 1].reshape(1, D),
        wts["w_r"], wts["b_r"], seq, tl["tm_merge"])

    bm = tl["bm"]
    tmr = tl["tm_merge"]
    cnt_tiles = cnt.reshape(t // tmr, LANES).astype(I32)
    total = jnp.sum(cnt_tiles, axis=0)
    padded = (total + bm - 1) // bm * bm
    pad_end = jnp.cumsum(padded)
    pad_start = pad_end - padded
    tile_base = pad_start[None, :] + jnp.cumsum(cnt_tiles, axis=0) - cnt_tiles
    dest = _route(ids, tile_base.astype(F32).reshape(-1, 1, LANES), tmr)[:, :TOP_K]

    n_rows = t * TOP_K + N_EXP * bm
    n_blocks = n_rows // bm
    blk_start = jnp.arange(n_blocks, dtype=I32) * bm
    block_expert = jnp.minimum(
        jnp.searchsorted(pad_end[:N_EXP], blk_start, side="right"), N_EXP - 1).astype(I32)
    n_used = (pad_end[N_EXP - 1] // bm).astype(I32).reshape(1)

    dest_kt = dest.T
    xb = _sc_scatter_rows(h2, dest_kt, n_rows)
    yb = _moe(block_expert, n_used, xb, wts["wg"], wts["wl"], wts["bg"], wts["bl"],
              wts["wd"], wts["bd"], bm)
    yg = _sc_gather_rows(yb, dest_kt.reshape(-1)).reshape(TOP_K, t, HALF_D)

    out = _final(yg, rw, x1, mod3, norm_post[0, 1].reshape(1, D), seq, tl["tm_final"])
    return out.reshape(batch, seq, D)


def kernel(x_prompt, x_sample, c_prompt, c_sample, w_ada, b_ada, norm_pre, norm_post, w_in,
           lb_logits, hg_norm_gain, w_hg_out, w_att_out, w_out, w_router, b_router,
           w_up, b_up, w_down, b_down):
    wts = _prep_weights(w_in, w_hg_out, w_att_out, w_out, w_router, b_router,
                        w_up, b_up, w_down, b_down)
    nb = c_prompt.shape[0]
    mod = _ada(jnp.concatenate([c_prompt, c_sample], axis=0), w_ada[0], b_ada[0])
    lb_l = lb_logits.astype(F32)
    y_p = _trunk(x_prompt, mod[:nb], wts, norm_pre, norm_post, lb_l, hg_norm_gain)
    y_s = _trunk(x_sample, mod[nb:], wts, norm_pre, norm_post, lb_l, hg_norm_gain)
    return (y_p, y_s)
```

```python
import functools

import numpy as np
import jax
import jax.numpy as jnp
from jax import lax
from jax.experimental import pallas as pl
from jax.experimental.pallas import tpu as pltpu
from jax.experimental.pallas import tpu_sc as plsc

F32 = jnp.float32
BF16 = jnp.bfloat16
I32 = jnp.int32
U32 = jnp.uint32

D = 1024
EPS = 1e-6
HG_H = 4
HG_DK = 128
HG_W = HG_H * HG_DK
HG_C = 64
HG_LEVELS = 6
HG_SAFE_LOGDECAY = -80.0
HG_UNROLL = 4
AT_DILS = (1, 4, 16)
AT_HALF = 64
AT_H = 4
AT_HD = 64
AT_GW = AT_H * AT_HD
ROPE_DIM = 16
ROPE_THETA = 500000.0
N_EXP = 32
TOP_K = 4
SWIGLU_LIMIT = 7.0
SWIGLU_ALPHA = 1.702
LANES = 128
NEG_BIG = -1e30
HALF_D = D // 2
SC_CORES = 2
SC_SUBCORES = 16
SC_WORKERS = SC_CORES * SC_SUBCORES
SC_CHUNK = 128

VMEM_LIMIT = 56 * 1024 * 1024


def _cp(sem, vmem=VMEM_LIMIT):
    return pltpu.CompilerParams(dimension_semantics=sem, vmem_limit_bytes=vmem)


def _dot(a, b):
    return jnp.dot(a, b, preferred_element_type=F32)


def _dot_nt(a, b):
    return lax.dot_general(a, b, (((1,), (1,)), ((), ())), preferred_element_type=F32)


def _dot_tn(a, b):
    return lax.dot_general(a, b, (((0,), (0,)), ((), ())), preferred_element_type=F32)


def _split3(x):
    hi = x.astype(BF16)
    r = x - hi.astype(F32)
    mid = r.astype(BF16)
    lo = (r - mid.astype(F32)).astype(BF16)
    return hi, mid, lo


def _rms(x, gain):
    ms = jnp.mean(x * x, axis=-1, keepdims=True)
    return x * lax.rsqrt(ms + EPS) * gain


def _const_spec(shape):
    n = len(shape)
    return pl.BlockSpec(shape, lambda *_: (0,) * n)


def _pack_rows(y):
    bits = lambda a: lax.bitcast_convert_type(a.astype(BF16).astype(F32), U32)
    return (bits(y[:, :HALF_D]) >> 16) | (bits(y[:, HALF_D:]) & jnp.uint32(0xFFFF0000))


def _unpack_rows(w):
    lo = lax.bitcast_convert_type(w << 16, F32)
    hi = lax.bitcast_convert_type(w & jnp.uint32(0xFFFF0000), F32)
    return lo, hi


def _deint_kernel(w_ref, p_ref, g_ref, l_ref):
    p = p_ref[...]
    n_grp = w_ref.shape[2] // (2 * LANES)
    for b in range(n_grp):
        blk = w_ref[0, :, b * 2 * LANES:(b + 1) * 2 * LANES].astype(BF16)
        r = _dot(blk, p)
        g_ref[0, :, b * LANES:(b + 1) * LANES] = r[:, :LANES].astype(BF16)
        l_ref[0, :, b * LANES:(b + 1) * LANES] = r[:, LANES:].astype(BF16)


def _deinterleave(w_up):
    n_e, d, n2 = w_up.shape
    perm = np.zeros((2 * LANES, 2 * LANES), np.float32)
    perm[2 * np.arange(LANES), np.arange(LANES)] = 1.0
    perm[2 * np.arange(LANES) + 1, LANES + np.arange(LANES)] = 1.0
    tr = 512
    out = jax.ShapeDtypeStruct((n_e, d, n2 // 2), BF16)
    return pl.pallas_call(
        _deint_kernel,
        grid=(n_e, d // tr),
        in_specs=[pl.BlockSpec((1, tr, n2), lambda e, i: (e, i, 0)),
                  _const_spec(perm.shape)],
        out_specs=[pl.BlockSpec((1, tr, n2 // 2), lambda e, i: (e, i, 0))] * 2,
        out_shape=[out, out],
        compiler_params=_cp(("parallel", "parallel")),
        name="deinterleave",
    )(w_up, jnp.asarray(perm, BF16))


def _sc_mesh():
    return plsc.VectorSubcoreMesh(core_axis_name="c", subcore_axis_name="s")


def _sc_worker():
    return lax.axis_index("s") * SC_CORES + lax.axis_index("c")


def _sc_scatter_rows(src, dest_kt, n_rows):
    t, w = src.shape
    n_slot = dest_kt.shape[0]
    nch = t // (SC_WORKERS * SC_CHUNK)
    idx = dest_kt.reshape(n_slot, SC_WORKERS, nch, SC_CHUNK).transpose(1, 0, 2, 3)

    @functools.partial(
        pl.kernel, mesh=_sc_mesh(),
        out_type=jax.ShapeDtypeStruct((n_rows, w), src.dtype),
        scratch_types=[pltpu.VMEM((n_slot, nch, SC_CHUNK), I32),
                       pltpu.VMEM((SC_CHUNK, w), src.dtype)],
        name="sc_scatter")
    def run(src_hbm, idx_hbm, out_hbm, idx_v, rows_v):
        wid = _sc_worker()
        pltpu.sync_copy(idx_hbm.at[wid], idx_v)

        @pl.loop(0, nch)
        def _(j):
            pltpu.sync_copy(src_hbm.at[pl.ds((wid * nch + j) * SC_CHUNK, SC_CHUNK)], rows_v)
            for k in range(n_slot):
                pltpu.sync_copy(rows_v, out_hbm.at[idx_v.at[k, j]])

    return run(src, idx)


def _sc_gather_rows(table, idx):
    n = idx.shape[0]
    w = table.shape[1]
    nch = n // (SC_WORKERS * SC_CHUNK)
    idx3 = idx.reshape(SC_WORKERS, nch, SC_CHUNK)

    @functools.partial(
        pl.kernel, mesh=_sc_mesh(),
        out_type=jax.ShapeDtypeStruct((n, w), table.dtype),
        scratch_types=[pltpu.VMEM((nch, SC_CHUNK), I32),
                       pltpu.VMEM((SC_CHUNK, w), table.dtype)],
        name="sc_gather")
    def run(table_hbm, idx_hbm, out_hbm, idx_v, rows_v):
        wid = _sc_worker()
        pltpu.sync_copy(idx_hbm.at[wid], idx_v)

        @pl.loop(0, nch)
        def _(j):
            pltpu.sync_copy(table_hbm.at[idx_v.at[j]], rows_v)
            pltpu.sync_copy(rows_v, out_hbm.at[pl.ds((wid * nch + j) * SC_CHUNK, SC_CHUNK)])

    return run(table, idx3)


def _ada_kernel(c_ref, w_ref, b_ref, o_ref):
    c = c_ref[...]
    a = c * jax.nn.sigmoid(c)
    w = w_ref[...]
    a_hi = a.astype(BF16)
    a_lo = (a - a_hi.astype(F32)).astype(BF16)
    w_hi = w.astype(BF16)
    w_lo = (w - w_hi.astype(F32)).astype(BF16)
    o_ref[...] = _dot(a_hi, w_hi) + _dot(a_lo, w_hi) + _dot(a_hi, w_lo) + b_ref[...]


def _ada(c, w, b):
    nb = c.shape[0]
    n = w.shape[1]
    tn = 1536
    return pl.pallas_call(
        _ada_kernel,
        grid=(n // tn,),
        in_specs=[pl.BlockSpec((nb, D), lambda j: (0, 0)),
                  pl.BlockSpec((D, tn), lambda j: (0, j)),
                  pl.BlockSpec((1, tn), lambda j: (0, j))],
        out_specs=pl.BlockSpec((nb, tn), lambda j: (0, j)),
        out_shape=jax.ShapeDtypeStruct((nb, n), F32),
        compiler_params=_cp(("parallel",)),
        name="ada",
    )(c, w, b.reshape(1, n))


def _inproj_kernel(x_ref, mod_ref, gain_ref, cos_ref, s1_ref, s2_ref,
                   w_hga_ref, w_hgf_ref, w_at_ref, w_gt_ref,
                   hga_ref, hgf_ref, at0_ref, at1_ref, at2_ref, gt_ref):
    x = x_ref[...]
    sh = mod_ref[0, 0:1, :]
    sc = mod_ref[0, 1:2, :]
    h = _rms(x, gain_ref[...]) * (1.0 + sc) + sh
    hb = h.astype(BF16)
    hga_ref[...] = _dot(hb, w_hga_ref[...]).astype(BF16)
    hgf_ref[...] = _dot(hb, w_hgf_ref[...])
    cos = cos_ref[...]
    s1 = s1_ref[...]
    s2 = s2_ref[...]
    for g, o_ref in enumerate((at0_ref, at1_ref, at2_ref)):
        acc = _dot(hb, w_at_ref[:, g * 3 * AT_GW:(g + 1) * 3 * AT_GW])
        parts = []
        for j in range(2 * AT_GW // LANES):
            a = acc[:, j * LANES:(j + 1) * LANES]
            parts.append(a * cos + pltpu.roll(a, ROPE_DIM // 2, 1) * s1
                         + pltpu.roll(a, LANES - ROPE_DIM // 2, 1) * s2)
        parts.append(acc[:, 2 * AT_GW:])
        o_ref[...] = jnp.concatenate(parts, axis=1).astype(BF16)
    gt_ref[...] = _dot(hb, w_gt_ref[...]).astype(BF16)


def _inproj(x2, mod3, gain, rope, w_hga, w_hgf, w_at, w_gt, seq, tm):
    t = x2.shape[0]
    n_pos_blk = seq // tm
    row = lambda w: pl.BlockSpec((tm, w), lambda i: (i, 0))
    pos = pl.BlockSpec((tm, LANES), lambda i: (i % n_pos_blk, 0))
    return pl.pallas_call(
        _inproj_kernel,
        grid=(t // tm,),
        in_specs=[row(D),
                  pl.BlockSpec((1, 6, D), lambda i: ((i * tm) // seq, 0, 0)),
                  _const_spec((1, D)), pos, pos, pos,
                  _const_spec(w_hga.shape), _const_spec(w_hgf.shape),
                  _const_spec(w_at.shape), _const_spec(w_gt.shape)],
        out_specs=[row(3 * HG_W), row(2 * HG_W), row(3 * AT_GW), row(3 * AT_GW), row(3 * AT_GW),
                   row(2 * D)],
        out_shape=[jax.ShapeDtypeStruct((t, 3 * HG_W), BF16),
                   jax.ShapeDtypeStruct((t, 2 * HG_W), F32),
                   jax.ShapeDtypeStruct((t, 3 * AT_GW), BF16),
                   jax.ShapeDtypeStruct((t, 3 * AT_GW), BF16),
                   jax.ShapeDtypeStruct((t, 3 * AT_GW), BF16),
                   jax.ShapeDtypeStruct((t, 2 * D), BF16)],
        compiler_params=_cp(("parallel",)),
        name="inproj",
    )(x2, mod3, gain, *rope, w_hga, w_hgf, w_at, w_gt)


def _rope_tables(seq):
    half = ROPE_DIM // 2
    inv_freq = ROPE_THETA ** (-np.arange(half, dtype=np.float32) / half)
    ang = jnp.arange(seq, dtype=F32)[:, None] * jnp.asarray(inv_freq)[None, :]
    cos, sin = jnp.cos(ang), jnp.sin(ang)
    ones = jnp.ones((seq, AT_HD - ROPE_DIM), F32)
    zeros = jnp.zeros((seq, AT_HD - ROPE_DIM), F32)
    zh = jnp.zeros((seq, half), F32)
    c_head = jnp.concatenate([cos, cos, ones], axis=1)
    s1_head = jnp.concatenate([zh, sin, zeros], axis=1)
    s2_head = jnp.concatenate([-sin, zh, zeros], axis=1)
    rep = LANES // AT_HD
    return tuple(jnp.tile(a, (1, rep)) for a in (c_head, s1_head, s2_head))


def _hgrn_consts(reverse):
    c = HG_C
    t = np.arange(c)[:, None]
    u = np.arange(c)[None, :]
    tri = (u >= t) if reverse else (u <= t)
    sels = []
    for lvl in range(HG_LEVELS):
        blk = 2 << lvl
        start = (np.arange(c) // blk) * blk
        piv = start + (1 << lvl) - (0 if reverse else 1)
        sels.append(np.arange(c)[None, :] == piv[:, None])
    sel = np.concatenate(sels, axis=0)
    three = lambda m: jnp.asarray(np.concatenate([m, m, m], axis=1), BF16)
    return three(tri.astype(np.float32)), three(sel.astype(np.float32))


def _hgrn_kernel(*refs, reverse, final, n_chunks):
    if final:
        (lbl_ref, tri_ref, sel_ref, q_ref, f_ref, v_ref, ofw_ref, gh_ref, gain_ref,
         o_ref, st_ref, k_s, b_s) = refs
    else:
        (lbl_ref, tri_ref, sel_ref, q_ref, f_ref, v_ref,
         o_ref, st_ref, k_s, b_s) = refs
    c = HG_C

    @pl.when(pl.program_id(1) == 0)
    def _():
        st_ref[...] = jnp.zeros_like(st_ref)

    l0 = lbl_ref[0:1, :]
    l1 = lbl_ref[1:2, :]
    lm = jnp.maximum(l0, l1)
    e0 = jnp.exp(l0 - lm)
    e1 = jnp.exp(l1 - lm)
    lb = e0 / (e0 + e1)
    tri = tri_ref[...]

    def gates(ci, bmin):
        r0 = pl.multiple_of(ci * c, c)
        ff = f_ref[pl.ds(r0, c), :]
        e = jnp.exp(-jnp.abs(ff))
        r = 1.0 / (1.0 + e)
        pos = ff >= 0.0
        sg = jnp.where(pos, r, e * r)
        sgn = jnp.where(pos, e * r, r)
        g = jnp.log(lb + (1.0 - lb) * sg)
        k_s[pl.ds(r0, c), :] = (1.0 - lb) * sgn
        b = _dot(tri, jnp.concatenate(_split3(g), axis=0))
        b_s[pl.ds(r0, c), :] = b
        return jnp.minimum(bmin, jnp.min(b, axis=0, keepdims=True))

    bmin = lax.fori_loop(0, n_chunks, gates, jnp.zeros((1, HG_W), F32), unroll=HG_UNROLL)
    safe = jnp.min(bmin) >= HG_SAFE_LOGDECAY

    ti = lax.broadcasted_iota(I32, (c, c), 0)
    si = lax.broadcasted_iota(I32, (c, c), 1)
    causal = (si >= ti) if reverse else (si <= ti)
    row = lax.broadcasted_iota(I32, (c, 1), 0)

    def att_robust(q, kin, b, piv):
        att = jnp.where(ti == si, _dot_nt(q.astype(BF16), kin.astype(BF16)), 0.0)
        for lvl in range(HG_LEVELS):
            p = piv[lvl * c:(lvl + 1) * c, :]
            bit = ((row >> lvl) & 1) == 1
            q_side = jnp.logical_not(bit) if reverse else bit
            qe = jnp.where(q_side, q * jnp.exp(jnp.minimum(b - p, 0.0)), 0.0).astype(BF16)
            ke = jnp.where(q_side, 0.0, kin * jnp.exp(jnp.minimum(p - b, 0.0))).astype(BF16)
            same = (ti >> (lvl + 1)) == (si >> (lvl + 1))
            att = att + jnp.where(same, _dot_nt(qe, ke), 0.0)
        return att

    heads = [slice(h * HG_DK, (h + 1) * HG_DK) for h in range(HG_H)]

    def load_chunk(r0):
        q = q_ref[pl.ds(r0, c), :].astype(F32)
        v = v_ref[pl.ds(r0, c), :]
        kin = k_s[pl.ds(r0, c), :]
        b = b_s[pl.ds(r0, c), :]
        btot = b[0:1, :] if reverse else b[c - 1:c, :]
        return q, v, kin, b, btot

    def emit(r0, outs):
        o = jnp.concatenate(outs, axis=1)
        if final:
            o = o + ofw_ref[pl.ds(r0, c), :]
            normed = []
            for sl in heads:
                ms = jnp.mean(o[:, sl] * o[:, sl], axis=-1, keepdims=True)
                normed.append(o[:, sl] * lax.rsqrt(ms + EPS))
            gh = gh_ref[pl.ds(r0, c), :].astype(F32)
            o = jnp.concatenate(normed, axis=1) * gain_ref[...] * (gh * jax.nn.sigmoid(gh))
        o_ref[pl.ds(r0, c), :] = o.astype(o_ref.dtype)

    def robust_step(i, carry):
        ci = (n_chunks - 1 - i) if reverse else i
        r0 = pl.multiple_of(ci * c, c)
        q, v, kin, b, btot = load_chunk(r0)
        qe_all = (q * jnp.exp(b)).astype(BF16)
        kd_all = (kin * jnp.exp(btot - b)).astype(BF16)
        piv = _dot(sel_ref[...], jnp.concatenate(_split3(b), axis=0))
        outs = []
        for h, sl in enumerate(heads):
            att = att_robust(q[:, sl], kin[:, sl], b[:, sl], piv[:, sl])
            st = st_ref[h]
            outs.append(_dot_nt(qe_all[:, sl], st.astype(BF16)) + _dot(att.astype(BF16), v[:, sl]))
            st_ref[h] = st * jnp.exp(btot[:, sl]) + _dot_tn(v[:, sl], kd_all[:, sl])
        emit(r0, outs)
        return carry

    grp = HG_UNROLL

    def fast_group(i, carry):
        gi = (n_chunks // grp - 1 - i) if reverse else i
        base = gi * (grp * c)
        order = range(grp - 1, -1, -1) if reverse else range(grp)
        chunks = []
        for j in order:
            r0 = pl.multiple_of(base + j * c, c)
            q, v, kin, b, btot = load_chunk(r0)
            chunks.append(dict(
                r0=r0, v=v, dec=jnp.exp(btot),
                qe=(q * jnp.exp(b)).astype(BF16),
                ke=(kin * jnp.exp(-b)).astype(BF16),
                kd=(kin * jnp.exp(btot - b)).astype(BF16)))
        for ch in chunks:
            ch["att"] = [jnp.where(causal, _dot_nt(ch["qe"][:, sl], ch["ke"][:, sl]), 0.0).astype(BF16)
                         for sl in heads]
        for ch in chunks:
            ch["intra"] = [_dot(ch["att"][h], ch["v"][:, sl]) for h, sl in enumerate(heads)]
            ch["upd"] = [_dot_tn(ch["v"][:, sl], ch["kd"][:, sl]) for sl in heads]
        st = [st_ref[h] for h in range(HG_H)]
        for ch in chunks:
            outs = []
            for h, sl in enumerate(heads):
                outs.append(_dot_nt(ch["qe"][:, sl], st[h].astype(BF16)) + ch["intra"][h])
                st[h] = st[h] * ch["dec"][:, sl] + ch["upd"][h]
            emit(ch["r0"], outs)
        for h in range(HG_H):
            st_ref[h] = st[h]
        return carry

    def run_fast():
        lax.fori_loop(0, n_chunks // grp, fast_group, 0)

    def run_robust():
        lax.fori_loop(0, n_chunks, robust_step, 0)

    lax.cond(safe, run_fast, run_robust)


def _hgrn(lb_logits_d, hga, hgf, batch, seq, tc, reverse, o_fw=None, gain=None):
    final = o_fw is not None
    t = hga.shape[0]
    nblk = seq // tc
    tri, sel = _hgrn_consts(reverse)

    def rows(col):
        if reverse:
            return lambda b, j: (b * nblk + nblk - 1 - j, col)
        return lambda b, j: (b * nblk + j, col)

    blk = lambda col: pl.BlockSpec((tc, HG_W), rows(col))
    in_specs = [_const_spec((2, HG_W)), _const_spec(tri.shape), _const_spec(sel.shape),
                blk(0), blk(1 if reverse else 0), blk(1)]
    args = [lb_logits_d, tri, sel, hga, hgf, hga]
    if final:
        in_specs += [blk(0), blk(2), _const_spec((1, HG_W))]
        args += [o_fw, hga, gain]
    return pl.pallas_call(
        functools.partial(_hgrn_kernel, reverse=reverse, final=final, n_chunks=tc // HG_C),
        grid=(batch, nblk),
        in_specs=in_specs,
        out_specs=blk(0),
        out_shape=jax.ShapeDtypeStruct((t, HG_W), BF16 if final else F32),
        scratch_shapes=[pltpu.VMEM((HG_H, HG_DK, HG_DK), F32),
                        pltpu.VMEM((tc, HG_W), F32),
                        pltpu.VMEM((tc, HG_W), F32)],
        compiler_params=_cp(("parallel", "arbitrary")),
        name="hgrn_bwd" if reverse else "hgrn_fwd",
    )(*args)


def _attn_kernel(q_ref, kl_ref, km_ref, kr_ref, vl_ref, vm_ref, vr_ref, o_ref, l_ref,
                 k_s, v_s, *, tq, ld):
    hw = AT_HALF
    k_s[0:hw, :] = kl_ref[0]
    k_s[hw:hw + tq, :] = km_ref[0]
    k_s[hw + tq:, :] = kr_ref[0]
    v_s[0:hw, :] = vl_ref[0]
    v_s[hw:hw + tq, :] = vm_ref[0]
    v_s[hw + tq:, :] = vr_ref[0]
    qb = 2 * hw
    kb = 4 * hw
    base = pl.program_id(2) * tq
    lane_head = lax.broadcasted_iota(I32, (1, AT_GW), 1) // AT_HD
    q_sel = [jnp.where(lane_head == h, AT_HD ** -0.5, 0.0).astype(BF16) for h in range(AT_H)]
    out_head = lax.broadcasted_iota(I32, (qb, AT_GW), 1) // AT_HD
    for qs in range(0, tq, qb):
        q = q_ref[0, qs:qs + qb, :]
        kk = k_s[qs:qs + kb, :]
        vv = v_s[qs:qs + kb, :]
        qpos = base + qs + lax.broadcasted_iota(I32, (qb, kb), 0)
        kpos = base + qs - hw + lax.broadcasted_iota(I32, (qb, kb), 1)
        mask = (jnp.abs(qpos - kpos) <= hw) & (kpos >= 0) & (kpos < ld)
        o_all = jnp.zeros((qb, AT_GW), F32)
        l_all = jnp.zeros((qb, AT_GW), F32)
        scores = [jnp.where(mask, _dot_nt(q * q_sel[h], kk), NEG_BIG) for h in range(AT_H)]
        maxes = [jnp.max(s, axis=-1, keepdims=True) for s in scores]
        probs = [jnp.exp(s - m) for s, m in zip(scores, maxes)]
        sums = [jnp.sum(p, axis=-1, keepdims=True) for p in probs]
        pvs = [_dot(p.astype(BF16), vv) for p in probs]
        for h in range(AT_H):
            o_all = jnp.where(out_head == h, pvs[h] * (1.0 / sums[h]), o_all)
            l_all = jnp.where(out_head == h, maxes[h] + jnp.log(sums[h]), l_all)
        o_ref[0, qs:qs + qb, :] = o_all.astype(o_ref.dtype)
        l_ref[0, qs:qs + qb, :] = l_all


def _attn_group(qkv, batch, seq, dil):
    ld = seq // dil
    hw = AT_HALF
    tq = min(ld, 512)
    nq = ld // tq
    nh = ld // hw
    per = tq // hw
    view = qkv.reshape(batch, ld, dil * 3 * AT_GW)
    main = lambda part: pl.BlockSpec((1, tq, AT_GW), lambda b, r, j: (b, j, 3 * r + part))
    left = lambda part: pl.BlockSpec(
        (1, hw, AT_GW), lambda b, r, j: (b, jnp.maximum(j * per - 1, 0), 3 * r + part))
    right = lambda part: pl.BlockSpec(
        (1, hw, AT_GW), lambda b, r, j: (b, jnp.minimum((j + 1) * per, nh - 1), 3 * r + part))
    out_spec = pl.BlockSpec((1, tq, AT_GW), lambda b, r, j: (b, j, r))
    o, lse = pl.pallas_call(
        functools.partial(_attn_kernel, tq=tq, ld=ld),
        grid=(batch, dil, nq),
        in_specs=[main(0), left(1), main(1), right(1), left(2), main(2), right(2)],
        out_specs=[out_spec, out_spec],
        out_shape=[jax.ShapeDtypeStruct((batch, ld, dil * AT_GW), BF16),
                   jax.ShapeDtypeStruct((batch, ld, dil * AT_GW), F32)],
        scratch_shapes=[pltpu.VMEM((tq + 2 * hw, AT_GW), BF16),
                        pltpu.VMEM((tq + 2 * hw, AT_GW), BF16)],
        compiler_params=_cp(("parallel", "parallel", "parallel")),
        name=f"attn_d{dil}",
    )(view, view, view, view, view, view, view)
    return o.reshape(batch * seq, AT_GW), lse.reshape(batch * seq, AT_GW)


def _merge_kernel(ohg_ref, o0_ref, o1_ref, o2_ref, l0_ref, l1_ref, l2_ref, gt_ref, x_ref, mod_ref,
                  w_hg_ref, w_at_ref, w_out_ref, npost_ref, npre_ref, w_r_ref, b_r_ref,
                  x1_ref, h2_ref, ids_ref, wts_ref, cnt_ref):
    l0, l1, l2 = l0_ref[...], l1_ref[...], l2_ref[...]
    m = jnp.maximum(jnp.maximum(l0, l1), l2)
    e0, e1, e2 = jnp.exp(l0 - m), jnp.exp(l1 - m), jnp.exp(l2 - m)
    oa = (e0 * o0_ref[...].astype(F32) + e1 * o1_ref[...].astype(F32)
          + e2 * o2_ref[...].astype(F32)) / (e0 + e1 + e2)
    b_hg = _dot(ohg_ref[...], w_hg_ref[...])
    b_at = _dot(oa.astype(BF16), w_at_ref[...])
    g_hg = jax.nn.sigmoid(gt_ref[:, 0:D])
    g_at = jax.nn.sigmoid(gt_ref[:, D:2 * D])
    merged = g_hg * b_hg.astype(BF16) + g_at * b_at.astype(BF16)
    y = _dot(merged, w_out_ref[...])
    gt1 = mod_ref[0, 2:3, :]
    sh2 = mod_ref[0, 3:4, :]
    sc2 = mod_ref[0, 4:5, :]
    x1 = x_ref[...] + gt1 * _rms(y, npost_ref[...])
    x1_ref[...] = x1
    h2 = _rms(x1, npre_ref[...]) * (1.0 + sc2) + sh2
    h2_ref[...] = _pack_rows(h2)

    w_r = w_r_ref[...]
    h_hi = h2.astype(BF16)
    h_lo = (h2 - h_hi.astype(F32)).astype(BF16)
    w_hi = w_r.astype(BF16)
    w_lo = (w_r - w_hi.astype(F32)).astype(BF16)
    logits = _dot(h_hi, w_hi) + _dot(h_lo, w_hi) + _dot(h_hi, w_lo) + b_r_ref[...]

    lane = lax.broadcasted_iota(I32, logits.shape, 1)
    lane_f = lane.astype(F32)
    work = logits
    vals, idxs = [], []
    sel = jnp.zeros(logits.shape, F32)
    for _ in range(TOP_K):
        mk = jnp.max(work, axis=-1, keepdims=True)
        ik = jnp.min(jnp.where(work == mk, lane_f, float(LANES)), axis=-1, keepdims=True)
        hit = lane_f == ik
        sel = jnp.where(hit, 1.0, sel)
        work = jnp.where(hit, -jnp.inf, work)
        vals.append(mk)
        idxs.append(ik)
    es = [jnp.exp(v - vals[0]) for v in vals]
    den = es[0] + es[1] + es[2] + es[3]
    ids = jnp.zeros(logits.shape, F32)
    wts = jnp.zeros(logits.shape, F32)
    for k in range(TOP_K):
        ids = jnp.where(lane == k, idxs[k], ids)
        wts = jnp.where(lane == k, es[k] / den, wts)
    ids_ref[...] = ids.astype(I32)
    wts_ref[...] = wts
    cnt_ref[0] = jnp.sum(sel, axis=0, keepdims=True)


def _merge(ohg, att_o, att_l, gates, x2, mod3, w_hg, w_at, w_out, npost, npre, w_r, b_r, seq, tm):
    t = x2.shape[0]
    nt = t // tm
    row = lambda w: pl.BlockSpec((tm, w), lambda i: (i, 0))
    slab_i = jax.ShapeDtypeStruct((t, LANES), I32)
    slab_f = jax.ShapeDtypeStruct((t, LANES), F32)
    return pl.pallas_call(
        _merge_kernel,
        grid=(nt,),
        in_specs=[row(HG_W)] + [row(AT_GW)] * 6 + [row(2 * D), row(D),
                  pl.BlockSpec((1, 6, D), lambda i: ((i * tm) // seq, 0, 0)),
                  _const_spec(w_hg.shape), _const_spec(w_at.shape), _const_spec(w_out.shape),
                  _const_spec((1, D)), _const_spec((1, D)),
                  _const_spec(w_r.shape), _const_spec(b_r.shape)],
        out_specs=[row(D), row(HALF_D), row(LANES), row(LANES),
                   pl.BlockSpec((1, 1, LANES), lambda i: (i, 0, 0))],
        out_shape=[jax.ShapeDtypeStruct((t, D), F32), jax.ShapeDtypeStruct((t, HALF_D), U32),
                   slab_i, slab_f, jax.ShapeDtypeStruct((nt, 1, LANES), F32)],
        compiler_params=_cp(("parallel",)),
        name="merge",
    )(ohg, *att_o, *att_l, gates, x2, mod3, w_hg, w_at, w_out, npost, npre, w_r, b_r)


def _route_kernel(ids_ref, base_ref, tril_ref, dest_ref):
    ids = ids_ref[...]
    lane = lax.broadcasted_iota(I32, ids.shape, 1)
    hits = [lane == ids[:, k:k + 1] for k in range(TOP_K)]
    sel = jnp.zeros(ids.shape, F32)
    for hit in hits:
        sel = jnp.where(hit, 1.0, sel)
    before = _dot(tril_ref[...], sel.astype(BF16)) + base_ref[0]
    dest = jnp.zeros(ids.shape, I32)
    for k, hit in enumerate(hits):
        rk = jnp.sum(jnp.where(hit, before, 0.0), axis=-1, keepdims=True)
        dest = jnp.where(lane == k, rk.astype(I32), dest)
    dest_ref[...] = dest


def _route(ids, tile_base, tm):
    t = ids.shape[0]
    tril = jnp.asarray(np.tril(np.ones((tm, tm), np.float32), -1), BF16)
    return pl.pallas_call(
        _route_kernel,
        grid=(t // tm,),
        in_specs=[pl.BlockSpec((tm, LANES), lambda i: (i, 0)),
                  pl.BlockSpec((1, 1, LANES), lambda i: (i, 0, 0)),
                  _const_spec((tm, tm))],
        out_specs=pl.BlockSpec((tm, LANES), lambda i: (i, 0)),
        out_shape=jax.ShapeDtypeStruct((t, LANES), I32),
        compiler_params=_cp(("parallel",)),
        name="route",
    )(ids, tile_base, tril)


def _moe_kernel(be_ref, nu_ref, x_ref, wg_ref, wl_ref, bg_ref, bl_ref, wd_ref, bd_ref, o_ref):
    i = pl.program_id(0)

    @pl.when(i < nu_ref[0])
    def _():
        x_lo, x_hi = (a.astype(BF16) for a in _unpack_rows(x_ref[...]))
        gate = (_dot(x_lo, wg_ref[0, :HALF_D, :]) + _dot(x_hi, wg_ref[0, HALF_D:, :])
                + bg_ref[0])
        up = (_dot(x_lo, wl_ref[0, :HALF_D, :]) + _dot(x_hi, wl_ref[0, HALF_D:, :])
              + bl_ref[0])
        gate = jnp.minimum(gate, SWIGLU_LIMIT)
        up = jnp.clip(up, -SWIGLU_LIMIT, SWIGLU_LIMIT)
        act = (up + 1.0) * gate * jax.nn.sigmoid(SWIGLU_ALPHA * gate)
        y = _dot(act.astype(BF16), wd_ref[0]) + bd_ref[0]
        o_ref[...] = _pack_rows(y)

    @pl.when(i >= nu_ref[0])
    def _():
        o_ref[...] = jnp.zeros_like(o_ref)


def _moe(block_expert, n_used, xb, wg, wl, bg, bl, wd, bd, bm):
    r = xb.shape[0]
    nblk = r // bm
    ew = lambda shape: pl.BlockSpec((1,) + shape, lambda i, be, nu: (be[i], 0, 0))
    return pl.pallas_call(
        _moe_kernel,
        grid_spec=pltpu.PrefetchScalarGridSpec(
            num_scalar_prefetch=2,
            grid=(nblk,),
            in_specs=[pl.BlockSpec((bm, HALF_D), lambda i, be, nu: (i, 0)),
                      ew((D, D)), ew((D, D)), ew((1, D)), ew((1, D)), ew((D, D)), ew((1, D))],
            out_specs=pl.BlockSpec((bm, HALF_D), lambda i, be, nu: (i, 0)),
        ),
        out_shape=jax.ShapeDtypeStruct((r, HALF_D), U32),
        compiler_params=_cp(("arbitrary",)),
        name="moe",
    )(block_expert, n_used, xb, wg, wl, bg, bl, wd, bd)


def _final_kernel(y0_ref, y1_ref, y2_ref, y3_ref, wts_ref, x1_ref, mod_ref, npost_ref, o_ref):
    wts = wts_ref[...]
    y = jnp.zeros(x1_ref.shape, F32)
    for k, y_ref in enumerate((y0_ref, y1_ref, y2_ref, y3_ref)):
        y = y + wts[:, k:k + 1] * jnp.concatenate(_unpack_rows(y_ref[0]), axis=1)
    gt2 = mod_ref[0, 5:6, :]
    o_ref[...] = x1_ref[...] + gt2 * _rms(y, npost_ref[...])


def _final(yg, wts, x1, mod3, npost, seq, tm):
    t = x1.shape[0]
    slot = lambda k: pl.BlockSpec((1, tm, HALF_D), lambda i: (k, i, 0))
    return pl.pallas_call(
        _final_kernel,
        grid=(t // tm,),
        in_specs=[slot(0), slot(1), slot(2), slot(3),
                  pl.BlockSpec((tm, LANES), lambda i: (i, 0)),
                  pl.BlockSpec((tm, D), lambda i: (i, 0)),
                  pl.BlockSpec((1, 6, D), lambda i: ((i * tm) // seq, 0, 0)),
                  _const_spec((1, D))],
        out_specs=pl.BlockSpec((tm, D), lambda i: (i, 0)),
        out_shape=jax.ShapeDtypeStruct((t, D), F32),
        compiler_params=_cp(("parallel",)),
        name="final",
    )(yg, yg, yg, yg, wts, x1, mod3, npost)


def _tiles(seq):
    return dict(tm_in=min(seq, 512), tc=min(seq, 512), tm_merge=min(seq, 512),
                bm=512, tm_final=min(seq, 512))


def _prep_weights(w_in, w_hg_out, w_att_out, w_out, w_router, b_router, w_up, b_up, w_down, b_down):
    w = w_in[0]
    hw = HG_W
    a0 = 5 * hw
    aw = AT_GW * len(AT_DILS)
    q_at, k_at, v_at = (w[:, a0 + i * aw:a0 + (i + 1) * aw] for i in range(3))
    grp = lambda m, g: m[:, g * AT_GW:(g + 1) * AT_GW]
    w_at = jnp.concatenate(
        [jnp.concatenate([grp(q_at, g), grp(k_at, g), grp(v_at, g)], axis=1)
         for g in range(len(AT_DILS))], axis=1)
    pad = LANES - N_EXP
    wg, wl = _deinterleave(w_up[0])
    return dict(
        wg=wg, wl=wl,
        w_hga=jnp.concatenate([w[:, 0:hw], w[:, 3 * hw:4 * hw], w[:, 4 * hw:5 * hw]], 1).astype(BF16),
        w_hgf=w[:, hw:3 * hw].astype(BF16),
        w_at=w_at.astype(BF16),
        w_gt=w[:, a0 + 3 * aw:].astype(BF16),
        w_hg_out=w_hg_out[0].astype(BF16),
        w_att_out=w_att_out[0].astype(BF16),
        w_out=w_out[0].astype(BF16),
        w_r=jnp.pad(w_router[0], ((0, 0), (0, pad))),
        b_r=jnp.pad(b_router[0], (0, pad), constant_values=NEG_BIG).reshape(1, LANES),
        bg=b_up[0][:, 0::2].reshape(N_EXP, 1, D),
        bl=b_up[0][:, 1::2].reshape(N_EXP, 1, D),
        wd=w_down[0].astype(BF16),
        bd=b_down[0].reshape(N_EXP, 1, D),
    )


def _trunk(x, mod, wts, norm_pre, norm_post, lb_logits, hg_gain):
    batch, seq, _ = x.shape
    t = batch * seq
    tl = _tiles(seq)
    x2 = x.reshape(t, D)
    mod3 = mod.reshape(batch, 6, D)
    rope = _rope_tables(seq)

    hga, hgf, at0, at1, at2, gates = _inproj(
        x2, mod3, norm_pre[0, 0].reshape(1, D), rope,
        wts["w_hga"], wts["w_hgf"], wts["w_at"], wts["w_gt"], seq, tl["tm_in"])

    o_fw = _hgrn(lb_logits[0], hga, hgf, batch, seq, tl["tc"], reverse=False)
    ohg = _hgrn(lb_logits[1], hga, hgf, batch, seq, tl["tc"], reverse=True,
                o_fw=o_fw, gain=hg_gain[0].reshape(1, HG_W))

    att = [_attn_group(a, batch, seq, d) for a, d in zip((at0, at1, at2), AT_DILS)]

    x1, h2, ids, rw, cnt = _merge(
        ohg, [a[0] for a in att], [a[1] for a in att], gates, x2, mod3,
        wts["w_hg_out"], wts["w_att_out"], wts["w_out"],
        norm_post[0, 0].reshape(1, D), norm_pre[0, 1].reshape(1, D),
        wts["w_r"], wts["b_r"], seq, tl["tm_merge"])

    bm = tl["bm"]
    tmr = tl["tm_merge"]
    cnt_tiles = cnt.reshape(t // tmr, LANES).astype(I32)
    total = jnp.sum(cnt_tiles, axis=0)
    padded = (total + bm - 1) // bm * bm
    pad_end = jnp.cumsum(padded)
    pad_start = pad_end - padded
    tile_base = pad_start[None, :] + jnp.cumsum(cnt_tiles, axis=0) - cnt_tiles
    dest = _route(ids, tile_base.astype(F32).reshape(-1, 1, LANES), tmr)[:, :TOP_K]

    n_rows = t * TOP_K + N_EXP * bm
    n_blocks = n_rows // bm
    blk_start = jnp.arange(n_blocks, dtype=I32) * bm
    block_expert = jnp.minimum(
        jnp.sum(pad_end[None, :N_EXP] <= blk_start[:, None], axis=1), N_EXP - 1).astype(I32)
    n_used = (pad_end[N_EXP - 1] // bm).astype(I32).reshape(1)

    dest_kt = dest.T
    xb = _sc_scatter_rows(h2, dest_kt, n_rows)
    yb = _moe(block_expert, n_used, xb, wts["wg"], wts["wl"], wts["bg"], wts["bl"],
              wts["wd"], wts["bd"], bm)
    yg = _sc_gather_rows(yb, dest_kt.reshape(-1)).reshape(TOP_K, t, HALF_D)

    out = _final(yg, rw, x1, mod3, norm_post[0, 1].reshape(1, D), seq, tl["tm_final"])
    return out.reshape(batch, seq, D)


def kernel(x_prompt, x_sample, c_prompt, c_sample, w_ada, b_ada, norm_pre, norm_post, w_in,
           lb_logits, hg_norm_gain, w_hg_out, w_att_out, w_out, w_router, b_router,
           w_up, b_up, w_down, b_down):
    wts = _prep_weights(w_in, w_hg_out, w_att_out, w_out, w_router, b_router,
                        w_up, b_up, w_down, b_down)
    nb = c_prompt.shape[0]
    mod = _ada(jnp.concatenate([c_prompt, c_sample], axis=0), w_ada[0], b_ada[0])
    lb_l = lb_logits.astype(F32)
    y_p = _trunk(x_prompt, mod[:nb], wts, norm_pre, norm_post, lb_l, hg_norm_gain)
    y_s = _trunk(x_sample, mod[nb:], wts, norm_pre, norm_post, lb_l, hg_norm_gain)
    return (y_p, y_s)
```

```python
import functools

import numpy as np
import jax
import jax.numpy as jnp
from jax import lax
from jax.experimental import pallas as pl
from jax.experimental.pallas import tpu as pltpu
from jax.experimental.pallas import tpu_sc as plsc

F32 = jnp.float32
BF16 = jnp.bfloat16
I32 = jnp.int32
U32 = jnp.uint32

D = 1024
EPS = 1e-6
HG_H = 4
HG_DK = 128
HG_W = HG_H * HG_DK
HG_C = 64
HG_LEVELS = 6
HG_SAFE_LOGDECAY = -80.0
HG_UNROLL = 4
AT_DILS = (1, 4, 16)
AT_HALF = 64
AT_H = 4
AT_HD = 64
AT_GW = AT_H * AT_HD
ROPE_DIM = 16
ROPE_THETA = 500000.0
N_EXP = 32
TOP_K = 4
SWIGLU_LIMIT = 7.0
SWIGLU_ALPHA = 1.702
MERGE_SUBTILES = 4
LANES = 128
NEG_BIG = -1e30
HALF_D = D // 2
SC_CORES = 2
SC_SUBCORES = 16
SC_WORKERS = SC_CORES * SC_SUBCORES
SC_CHUNK = 128

VMEM_LIMIT = 56 * 1024 * 1024


def _cp(sem, vmem=VMEM_LIMIT):
    return pltpu.CompilerParams(dimension_semantics=sem, vmem_limit_bytes=vmem)


def _dot(a, b):
    return jnp.dot(a, b, preferred_element_type=F32)


def _dot_nt(a, b):
    return lax.dot_general(a, b, (((1,), (1,)), ((), ())), preferred_element_type=F32)


def _dot_tn(a, b):
    return lax.dot_general(a, b, (((0,), (0,)), ((), ())), preferred_element_type=F32)


def _split3(x):
    hi = x.astype(BF16)
    r = x - hi.astype(F32)
    mid = r.astype(BF16)
    lo = (r - mid.astype(F32)).astype(BF16)
    return hi, mid, lo


def _rms(x, gain):
    ms = jnp.mean(x * x, axis=-1, keepdims=True)
    return x * lax.rsqrt(ms + EPS) * gain


def _const_spec(shape):
    n = len(shape)
    return pl.BlockSpec(shape, lambda *_: (0,) * n)


def _pack_rows(y):
    bits = lambda a: lax.bitcast_convert_type(a.astype(BF16).astype(F32), U32)
    return (bits(y[:, :HALF_D]) >> 16) | (bits(y[:, HALF_D:]) & jnp.uint32(0xFFFF0000))


def _unpack_rows(w):
    lo = lax.bitcast_convert_type(w << 16, F32)
    hi = lax.bitcast_convert_type(w & jnp.uint32(0xFFFF0000), F32)
    return lo, hi


def _deint_kernel(w_ref, p_ref, g_ref, l_ref):
    p = p_ref[...]
    n_grp = w_ref.shape[2] // (2 * LANES)
    for b in range(n_grp):
        blk = w_ref[0, :, b * 2 * LANES:(b + 1) * 2 * LANES].astype(BF16)
        r = _dot(blk, p)
        g_ref[0, :, b * LANES:(b + 1) * LANES] = r[:, :LANES].astype(BF16)
        l_ref[0, :, b * LANES:(b + 1) * LANES] = r[:, LANES:].astype(BF16)


def _deinterleave(w_up):
    n_e, d, n2 = w_up.shape
    perm = np.zeros((2 * LANES, 2 * LANES), np.float32)
    perm[2 * np.arange(LANES), np.arange(LANES)] = 1.0
    perm[2 * np.arange(LANES) + 1, LANES + np.arange(LANES)] = 1.0
    tr = 512
    out = jax.ShapeDtypeStruct((n_e, d, n2 // 2), BF16)
    return pl.pallas_call(
        _deint_kernel,
        grid=(n_e, d // tr),
        in_specs=[pl.BlockSpec((1, tr, n2), lambda e, i: (e, i, 0)),
                  _const_spec(perm.shape)],
        out_specs=[pl.BlockSpec((1, tr, n2 // 2), lambda e, i: (e, i, 0))] * 2,
        out_shape=[out, out],
        compiler_params=_cp(("parallel", "parallel")),
        name="deinterleave",
    )(w_up, jnp.asarray(perm, BF16))


def _sc_mesh():
    return plsc.VectorSubcoreMesh(core_axis_name="c", subcore_axis_name="s")


def _sc_worker():
    return lax.axis_index("s") * SC_CORES + lax.axis_index("c")


def _sc_scatter_rows(src, dest_kt, n_rows):
    t, w = src.shape
    n_slot = dest_kt.shape[0]
    nch = t // (SC_WORKERS * SC_CHUNK)
    idx = dest_kt.reshape(n_slot, SC_WORKERS, nch, SC_CHUNK).transpose(1, 0, 2, 3)

    @functools.partial(
        pl.kernel, mesh=_sc_mesh(),
        out_type=jax.ShapeDtypeStruct((n_rows, w), src.dtype),
        scratch_types=[pltpu.VMEM((n_slot, nch, SC_CHUNK), I32),
                       pltpu.VMEM((SC_CHUNK, w), src.dtype)],
        name="sc_scatter")
    def run(src_hbm, idx_hbm, out_hbm, idx_v, rows_v):
        wid = _sc_worker()
        pltpu.sync_copy(idx_hbm.at[wid], idx_v)

        @pl.loop(0, nch)
        def _(j):
            pltpu.sync_copy(src_hbm.at[pl.ds((wid * nch + j) * SC_CHUNK, SC_CHUNK)], rows_v)
            for k in range(n_slot):
                pltpu.sync_copy(rows_v, out_hbm.at[idx_v.at[k, j]])

    return run(src, idx)


def _sc_gather_rows(table, idx):
    n = idx.shape[0]
    w = table.shape[1]
    nch = n // (SC_WORKERS * SC_CHUNK)
    idx3 = idx.reshape(SC_WORKERS, nch, SC_CHUNK)

    @functools.partial(
        pl.kernel, mesh=_sc_mesh(),
        out_type=jax.ShapeDtypeStruct((n, w), table.dtype),
        scratch_types=[pltpu.VMEM((nch, SC_CHUNK), I32),
                       pltpu.VMEM((SC_CHUNK, w), table.dtype)],
        name="sc_gather")
    def run(table_hbm, idx_hbm, out_hbm, idx_v, rows_v):
        wid = _sc_worker()
        pltpu.sync_copy(idx_hbm.at[wid], idx_v)

        @pl.loop(0, nch)
        def _(j):
            pltpu.sync_copy(table_hbm.at[idx_v.at[j]], rows_v)
            pltpu.sync_copy(rows_v, out_hbm.at[pl.ds((wid * nch + j) * SC_CHUNK, SC_CHUNK)])

    return run(table, idx3)


def _ada_kernel(c_ref, w_ref, b_ref, o_ref):
    c = c_ref[...]
    a = c * jax.nn.sigmoid(c)
    w = w_ref[...]
    a_hi = a.astype(BF16)
    a_lo = (a - a_hi.astype(F32)).astype(BF16)
    w_hi = w.astype(BF16)
    w_lo = (w - w_hi.astype(F32)).astype(BF16)
    o_ref[...] = _dot(a_hi, w_hi) + _dot(a_lo, w_hi) + _dot(a_hi, w_lo) + b_ref[...]


def _ada(c, w, b):
    nb = c.shape[0]
    n = w.shape[1]
    tn = 1536
    return pl.pallas_call(
        _ada_kernel,
        grid=(n // tn,),
        in_specs=[pl.BlockSpec((nb, D), lambda j: (0, 0)),
                  pl.BlockSpec((D, tn), lambda j: (0, j)),
                  pl.BlockSpec((1, tn), lambda j: (0, j))],
        out_specs=pl.BlockSpec((nb, tn), lambda j: (0, j)),
        out_shape=jax.ShapeDtypeStruct((nb, n), F32),
        compiler_params=_cp(("parallel",)),
        name="ada",
    )(c, w, b.reshape(1, n))


def _inproj_kernel(x_ref, mod_ref, gain_ref, cos_ref, s1_ref, s2_ref,
                   w_hga_ref, w_hgf_ref, w_at_ref, w_gt_ref,
                   hga_ref, hgf_ref, at0_ref, at1_ref, at2_ref, gt_ref, stage_ref):
    x = x_ref[...]
    tm = x.shape[0]
    sh = mod_ref[0, 0:1, :]
    sc = mod_ref[0, 1:2, :]
    h = _rms(x, gain_ref[...]) * (1.0 + sc) + sh
    hb = h.astype(BF16)
    hga_ref[...] = _dot(hb, w_hga_ref[...]).astype(BF16)
    hgf_ref[...] = _dot(hb, w_hgf_ref[...])
    cos = cos_ref[...]
    s1 = s1_ref[...]
    s2 = s2_ref[...]
    for g, o_ref in enumerate((at0_ref, at1_ref, at2_ref)):
        acc = _dot(hb, w_at_ref[:, g * 3 * AT_GW:(g + 1) * 3 * AT_GW])
        parts = []
        for j in range(2 * AT_GW // LANES):
            a = acc[:, j * LANES:(j + 1) * LANES]
            parts.append(a * cos + pltpu.roll(a, ROPE_DIM // 2, 1) * s1
                         + pltpu.roll(a, LANES - ROPE_DIM // 2, 1) * s2)
        parts.append(acc[:, 2 * AT_GW:])
        vals = jnp.concatenate(parts, axis=1)
        dil = AT_DILS[g]
        if dil == 1:
            o_ref[0, 0] = vals.astype(BF16)
        else:
            n_lt = 3 * AT_GW // LANES
            for c in range(n_lt):
                stage_ref[c] = vals[:, c * LANES:(c + 1) * LANES]
            for r in range(dil):
                o_ref[0, r] = jnp.concatenate(
                    [stage_ref[c, pl.ds(r, tm // dil, stride=dil), :] for c in range(n_lt)],
                    axis=1).astype(BF16)
    gt_ref[...] = _dot(hb, w_gt_ref[...]).astype(BF16)


def _inproj(x2, mod3, gain, rope, w_hga, w_hgf, w_at, w_gt, seq, tm):
    t = x2.shape[0]
    n_pos_blk = seq // tm
    row = lambda w: pl.BlockSpec((tm, w), lambda i: (i, 0))
    pos = pl.BlockSpec((tm, LANES), lambda i: (i % n_pos_blk, 0))
    batch = t // seq
    dilated = lambda d: pl.BlockSpec((1, d, tm // d, 3 * AT_GW),
                                     lambda i: ((i * tm) // seq, 0, i % n_pos_blk, 0))
    at_shape = lambda d: jax.ShapeDtypeStruct((batch, d, seq // d, 3 * AT_GW), BF16)
    return pl.pallas_call(
        _inproj_kernel,
        grid=(t // tm,),
        in_specs=[row(D),
                  pl.BlockSpec((1, 6, D), lambda i: ((i * tm) // seq, 0, 0)),
                  _const_spec((1, D)), pos, pos, pos,
                  _const_spec(w_hga.shape), _const_spec(w_hgf.shape),
                  _const_spec(w_at.shape), _const_spec(w_gt.shape)],
        out_specs=[row(3 * HG_W), row(2 * HG_W)] + [dilated(d) for d in AT_DILS] + [row(2 * D)],
        out_shape=[jax.ShapeDtypeStruct((t, 3 * HG_W), BF16),
                   jax.ShapeDtypeStruct((t, 2 * HG_W), F32)]
                  + [at_shape(d) for d in AT_DILS]
                  + [jax.ShapeDtypeStruct((t, 2 * D), BF16)],
        scratch_shapes=[pltpu.VMEM((3 * AT_GW // LANES, tm, LANES), F32)],
        compiler_params=_cp(("parallel",)),
        name="inproj",
    )(x2, mod3, gain, *rope, w_hga, w_hgf, w_at, w_gt)


def _rope_tables(seq):
    half = ROPE_DIM // 2
    inv_freq = ROPE_THETA ** (-np.arange(half, dtype=np.float32) / half)
    ang = jnp.arange(seq, dtype=F32)[:, None] * jnp.asarray(inv_freq)[None, :]
    cos, sin = jnp.cos(ang), jnp.sin(ang)
    ones = jnp.ones((seq, AT_HD - ROPE_DIM), F32)
    zeros = jnp.zeros((seq, AT_HD - ROPE_DIM), F32)
    zh = jnp.zeros((seq, half), F32)
    c_head = jnp.concatenate([cos, cos, ones], axis=1)
    s1_head = jnp.concatenate([zh, sin, zeros], axis=1)
    s2_head = jnp.concatenate([-sin, zh, zeros], axis=1)
    rep = LANES // AT_HD
    return tuple(jnp.tile(a, (1, rep)) for a in (c_head, s1_head, s2_head))


def _hgrn_consts(reverse):
    c = HG_C
    t = np.arange(c)[:, None]
    u = np.arange(c)[None, :]
    tri = (u >= t) if reverse else (u <= t)
    sels = []
    for lvl in range(HG_LEVELS):
        blk = 2 << lvl
        start = (np.arange(c) // blk) * blk
        piv = start + (1 << lvl) - (0 if reverse else 1)
        sels.append(np.arange(c)[None, :] == piv[:, None])
    sel = np.concatenate(sels, axis=0)
    three = lambda m: jnp.asarray(np.concatenate([m, m, m], axis=1), BF16)
    return three(tri.astype(np.float32)), three(sel.astype(np.float32))


def _hgrn_kernel(*refs, reverse, final, n_chunks):
    if final:
        (lbl_ref, tri_ref, sel_ref, q_ref, f_ref, v_ref, ofw_ref, gh_ref, gain_ref,
         o_ref, st_ref, k_s, b_s) = refs
    else:
        (lbl_ref, tri_ref, sel_ref, q_ref, f_ref, v_ref,
         o_ref, st_ref, k_s, b_s) = refs
    c = HG_C

    @pl.when(pl.program_id(1) == 0)
    def _():
        st_ref[...] = jnp.zeros_like(st_ref)

    l0 = lbl_ref[0:1, :]
    l1 = lbl_ref[1:2, :]
    lm = jnp.maximum(l0, l1)
    e0 = jnp.exp(l0 - lm)
    e1 = jnp.exp(l1 - lm)
    lb = e0 / (e0 + e1)
    tri = tri_ref[...]

    def gates(ci, bmin):
        r0 = pl.multiple_of(ci * c, c)
        ff = f_ref[pl.ds(r0, c), :]
        e = jnp.exp(-jnp.abs(ff))
        r = 1.0 / (1.0 + e)
        pos = ff >= 0.0
        sg = jnp.where(pos, r, e * r)
        sgn = jnp.where(pos, e * r, r)
        g = jnp.log(lb + (1.0 - lb) * sg)
        k_s[pl.ds(r0, c), :] = (1.0 - lb) * sgn
        b = _dot(tri, jnp.concatenate(_split3(g), axis=0))
        b_s[pl.ds(r0, c), :] = b
        return jnp.minimum(bmin, jnp.min(b, axis=0, keepdims=True))

    bmin = lax.fori_loop(0, n_chunks, gates, jnp.zeros((1, HG_W), F32), unroll=HG_UNROLL)
    safe = jnp.min(bmin) >= HG_SAFE_LOGDECAY

    ti = lax.broadcasted_iota(I32, (c, c), 0)
    si = lax.broadcasted_iota(I32, (c, c), 1)
    causal = (si >= ti) if reverse else (si <= ti)
    row = lax.broadcasted_iota(I32, (c, 1), 0)

    def att_robust(q, kin, b, piv):
        att = jnp.where(ti == si, _dot_nt(q.astype(BF16), kin.astype(BF16)), 0.0)
        for lvl in range(HG_LEVELS):
            p = piv[lvl * c:(lvl + 1) * c, :]
            bit = ((row >> lvl) & 1) == 1
            q_side = jnp.logical_not(bit) if reverse else bit
            qe = jnp.where(q_side, q * jnp.exp(jnp.minimum(b - p, 0.0)), 0.0).astype(BF16)
            ke = jnp.where(q_side, 0.0, kin * jnp.exp(jnp.minimum(p - b, 0.0))).astype(BF16)
            same = (ti >> (lvl + 1)) == (si >> (lvl + 1))
            att = att + jnp.where(same, _dot_nt(qe, ke), 0.0)
        return att

    heads = [slice(h * HG_DK, (h + 1) * HG_DK) for h in range(HG_H)]

    def load_chunk(r0):
        q = q_ref[pl.ds(r0, c), :].astype(F32)
        v = v_ref[pl.ds(r0, c), :]
        kin = k_s[pl.ds(r0, c), :]
        b = b_s[pl.ds(r0, c), :]
        btot = b[0:1, :] if reverse else b[c - 1:c, :]
        return q, v, kin, b, btot

    def emit(r0, outs):
        o = jnp.concatenate(outs, axis=1)
        if final:
            o = o + ofw_ref[pl.ds(r0, c), :]
            normed = []
            for sl in heads:
                ms = jnp.mean(o[:, sl] * o[:, sl], axis=-1, keepdims=True)
                normed.append(o[:, sl] * lax.rsqrt(ms + EPS))
            gh = gh_ref[pl.ds(r0, c), :].astype(F32)
            o = jnp.concatenate(normed, axis=1) * gain_ref[...] * (gh * jax.nn.sigmoid(gh))
        o_ref[pl.ds(r0, c), :] = o.astype(o_ref.dtype)

    def robust_step(i, carry):
        ci = (n_chunks - 1 - i) if reverse else i
        r0 = pl.multiple_of(ci * c, c)
        q, v, kin, b, btot = load_chunk(r0)
        qe_all = (q * jnp.exp(b)).astype(BF16)
        kd_all = (kin * jnp.exp(btot - b)).astype(BF16)
        piv = _dot(sel_ref[...], jnp.concatenate(_split3(b), axis=0))
        outs = []
        for h, sl in enumerate(heads):
            att = att_robust(q[:, sl], kin[:, sl], b[:, sl], piv[:, sl])
            st = st_ref[h]
            outs.append(_dot_nt(qe_all[:, sl], st.astype(BF16)) + _dot(att.astype(BF16), v[:, sl]))
            st_ref[h] = st * jnp.exp(btot[:, sl]) + _dot_tn(v[:, sl], kd_all[:, sl])
        emit(r0, outs)
        return carry

    grp = HG_UNROLL

    def fast_group(i, carry):
        gi = (n_chunks // grp - 1 - i) if reverse else i
        base = gi * (grp * c)
        order = range(grp - 1, -1, -1) if reverse else range(grp)
        chunks = []
        for j in order:
            r0 = pl.multiple_of(base + j * c, c)
            q, v, kin, b, btot = load_chunk(r0)
            chunks.append(dict(
                r0=r0, v=v, dec=jnp.exp(btot),
                qe=(q * jnp.exp(b)).astype(BF16),
                ke=(kin * jnp.exp(-b)).astype(BF16),
                kd=(kin * jnp.exp(btot - b)).astype(BF16)))
        for ch in chunks:
            ch["att"] = [jnp.where(causal, _dot_nt(ch["qe"][:, sl], ch["ke"][:, sl]), 0.0).astype(BF16)
                         for sl in heads]
        for ch in chunks:
            ch["intra"] = [_dot(ch["att"][h], ch["v"][:, sl]) for h, sl in enumerate(heads)]
            ch["upd"] = [_dot_tn(ch["v"][:, sl], ch["kd"][:, sl]) for sl in heads]
        st = [st_ref[h] for h in range(HG_H)]
        for ch in chunks:
            outs = []
            for h, sl in enumerate(heads):
                outs.append(_dot_nt(ch["qe"][:, sl], st[h].astype(BF16)) + ch["intra"][h])
                st[h] = st[h] * ch["dec"][:, sl] + ch["upd"][h]
            emit(ch["r0"], outs)
        for h in range(HG_H):
            st_ref[h] = st[h]
        return carry

    def run_fast():
        lax.fori_loop(0, n_chunks // grp, fast_group, 0)

    def run_robust():
        lax.fori_loop(0, n_chunks, robust_step, 0)

    lax.cond(safe, run_fast, run_robust)


def _hgrn(lb_logits_d, hga, hgf, batch, seq, tc, reverse, o_fw=None, gain=None):
    final = o_fw is not None
    t = hga.shape[0]
    nblk = seq // tc
    tri, sel = _hgrn_consts(reverse)

    def rows(col):
        if reverse:
            return lambda b, j: (b * nblk + nblk - 1 - j, col)
        return lambda b, j: (b * nblk + j, col)

    blk = lambda col: pl.BlockSpec((tc, HG_W), rows(col))
    in_specs = [_const_spec((2, HG_W)), _const_spec(tri.shape), _const_spec(sel.shape),
                blk(0), blk(1 if reverse else 0), blk(1)]
    args = [lb_logits_d, tri, sel, hga, hgf, hga]
    if final:
        in_specs += [blk(0), blk(2), _const_spec((1, HG_W))]
        args += [o_fw, hga, gain]
    return pl.pallas_call(
        functools.partial(_hgrn_kernel, reverse=reverse, final=final, n_chunks=tc // HG_C),
        grid=(batch, nblk),
        in_specs=in_specs,
        out_specs=blk(0),
        out_shape=jax.ShapeDtypeStruct((t, HG_W), BF16 if final else F32),
        scratch_shapes=[pltpu.VMEM((HG_H, HG_DK, HG_DK), F32),
                        pltpu.VMEM((tc, HG_W), F32),
                        pltpu.VMEM((tc, HG_W), F32)],
        compiler_params=_cp(("parallel", "arbitrary")),
        name="hgrn_bwd" if reverse else "hgrn_fwd",
    )(*args)


def _attn_kernel(q_ref, kl_ref, km_ref, kr_ref, vl_ref, vm_ref, vr_ref, o_ref, l_ref,
                 k_s, v_s, *, tq, ld):
    hw = AT_HALF
    k_s[0:hw, :] = kl_ref[0]
    k_s[hw:hw + tq, :] = km_ref[0]
    k_s[hw + tq:, :] = kr_ref[0]
    v_s[0:hw, :] = vl_ref[0]
    v_s[hw:hw + tq, :] = vm_ref[0]
    v_s[hw + tq:, :] = vr_ref[0]
    qb = 2 * hw
    kb = 4 * hw
    base = pl.program_id(2) * tq
    lane_head = lax.broadcasted_iota(I32, (1, AT_GW), 1) // AT_HD
    q_sel = [jnp.where(lane_head == h, AT_HD ** -0.5, 0.0).astype(BF16) for h in range(AT_H)]
    out_head = lax.broadcasted_iota(I32, (qb, AT_GW), 1) // AT_HD
    for qs in range(0, tq, qb):
        q = q_ref[0, qs:qs + qb, :]
        kk = k_s[qs:qs + kb, :]
        vv = v_s[qs:qs + kb, :]
        qpos = base + qs + lax.broadcasted_iota(I32, (qb, kb), 0)
        kpos = base + qs - hw + lax.broadcasted_iota(I32, (qb, kb), 1)
        mask = (jnp.abs(qpos - kpos) <= hw) & (kpos >= 0) & (kpos < ld)
        o_all = jnp.zeros((qb, AT_GW), F32)
        l_all = jnp.zeros((qb, AT_GW), F32)
        scores = [jnp.where(mask, _dot_nt(q * q_sel[h], kk), NEG_BIG) for h in range(AT_H)]
        maxes = [jnp.max(s, axis=-1, keepdims=True) for s in scores]
        probs = [jnp.exp(s - m) for s, m in zip(scores, maxes)]
        sums = [jnp.sum(p, axis=-1, keepdims=True) for p in probs]
        pvs = [_dot(p.astype(BF16), vv) for p in probs]
        for h in range(AT_H):
            o_all = jnp.where(out_head == h, pvs[h] * (1.0 / sums[h]), o_all)
            l_all = jnp.where(out_head == h, maxes[h] + jnp.log(sums[h]), l_all)
        o_ref[0, qs:qs + qb, :] = o_all.astype(o_ref.dtype)
        l_ref[0, qs:qs + qb, :] = l_all


def _attn_group(qkv):
    batch, dil, ld, _ = qkv.shape
    hw = AT_HALF
    tq = min(ld, 512)
    nq = ld // tq
    nh = ld // hw
    per = tq // hw
    main = lambda part: pl.BlockSpec((None, 1, tq, AT_GW), lambda b, r, j: (b, r, j, part))
    left = lambda part: pl.BlockSpec(
        (None, 1, hw, AT_GW), lambda b, r, j: (b, r, jnp.maximum(j * per - 1, 0), part))
    right = lambda part: pl.BlockSpec(
        (None, 1, hw, AT_GW), lambda b, r, j: (b, r, jnp.minimum((j + 1) * per, nh - 1), part))
    out_spec = pl.BlockSpec((None, 1, tq, AT_GW), lambda b, r, j: (b, r, j, 0))
    return pl.pallas_call(
        functools.partial(_attn_kernel, tq=tq, ld=ld),
        grid=(batch, dil, nq),
        in_specs=[main(0), left(1), main(1), right(1), left(2), main(2), right(2)],
        out_specs=[out_spec, out_spec],
        out_shape=[jax.ShapeDtypeStruct((batch, dil, ld, AT_GW), BF16),
                   jax.ShapeDtypeStruct((batch, dil, ld, AT_GW), F32)],
        scratch_shapes=[pltpu.VMEM((tq + 2 * hw, AT_GW), BF16),
                        pltpu.VMEM((tq + 2 * hw, AT_GW), BF16)],
        compiler_params=_cp(("parallel", "parallel", "parallel")),
        name=f"attn_d{dil}",
    )(qkv, qkv, qkv, qkv, qkv, qkv, qkv)


def _merge_kernel(ohg_ref, o0_ref, o1_ref, o2_ref, l0_ref, l1_ref, l2_ref, gt_ref, x_ref, mod_ref,
                  w_hg_ref, w_at_ref, w_out_ref, npost_ref, npre_ref, w_r_ref, b_r_ref,
                  x1_ref, h2_ref, ids_ref, wts_ref, cnt_ref,
                  so1_ref, sl1_ref, so2_ref, sl2_ref):
    tm = x_ref.shape[0]
    sub = tm // MERGE_SUBTILES
    rows = [slice(i * sub, (i + 1) * sub) for i in range(MERGE_SUBTILES)]
    gt1 = mod_ref[0, 2:3, :]
    sh2 = mod_ref[0, 3:4, :]
    sc2 = mod_ref[0, 4:5, :]

    for src_ref, dst_ref in ((o1_ref, so1_ref), (l1_ref, sl1_ref), (o2_ref, so2_ref), (l2_ref, sl2_ref)):
        dil = src_ref.shape[0]
        for res in range(dil):
            vals = src_ref[res].astype(F32)
            for c in range(AT_GW // LANES):
                dst_ref[c, pl.ds(res, tm // dil, stride=dil), :] = vals[:, c * LANES:(c + 1) * LANES]

    def staged(ref, r):
        return jnp.concatenate([ref[c, r, :] for c in range(AT_GW // LANES)], axis=1)

    def branches(r):
        o0, l0 = o0_ref[0, r, :].astype(F32), l0_ref[0, r, :]
        o1, l1 = staged(so1_ref, r), staged(sl1_ref, r)
        o2, l2 = staged(so2_ref, r), staged(sl2_ref, r)
        m = jnp.maximum(jnp.maximum(l0, l1), l2)
        e0, e1, e2 = jnp.exp(l0 - m), jnp.exp(l1 - m), jnp.exp(l2 - m)
        oa = (e0 * o0 + e1 * o1 + e2 * o2) / (e0 + e1 + e2)
        return _dot(ohg_ref[r, :], w_hg_ref[...]), _dot(oa.astype(BF16), w_at_ref[...])

    def out_proj(r, b_hg, b_at):
        g_hg = jax.nn.sigmoid(gt_ref[r, 0:D])
        g_at = jax.nn.sigmoid(gt_ref[r, D:2 * D])
        merged = g_hg * b_hg.astype(BF16) + g_at * b_at.astype(BF16)
        return _dot(merged, w_out_ref[...])

    def norms(r, y):
        x1 = x_ref[r, :] + gt1 * _rms(y, npost_ref[...])
        x1_ref[r, :] = x1
        h2 = _rms(x1, npre_ref[...]) * (1.0 + sc2) + sh2
        h2_ref[r, :] = _pack_rows(h2)
        return _dot(h2.astype(BF16), w_r_ref[...]) + b_r_ref[...]

    def top_k(r, logits):
        lane = lax.broadcasted_iota(I32, logits.shape, 1)
        lane_f = lane.astype(F32)
        work = logits
        vals, idxs = [], []
        sel = jnp.zeros(logits.shape, F32)
        for _ in range(TOP_K):
            mk = jnp.max(work, axis=-1, keepdims=True)
            ik = jnp.min(jnp.where(work == mk, lane_f, float(LANES)), axis=-1, keepdims=True)
            hit = lane_f == ik
            sel = jnp.where(hit, 1.0, sel)
            work = jnp.where(hit, -jnp.inf, work)
            vals.append(mk)
            idxs.append(ik)
        es = [jnp.exp(v - vals[0]) for v in vals]
        den = es[0] + es[1] + es[2] + es[3]
        ids = jnp.zeros(logits.shape, F32)
        wts = jnp.zeros(logits.shape, F32)
        for k in range(TOP_K):
            ids = jnp.where(lane == k, idxs[k], ids)
            wts = jnp.where(lane == k, es[k] / den, wts)
        ids_ref[r, :] = ids.astype(I32)
        wts_ref[r, :] = wts
        return jnp.sum(sel, axis=0, keepdims=True)

    br = [branches(r) for r in rows]
    ys = [out_proj(r, *b) for r, b in zip(rows, br)]
    logits = [norms(r, y) for r, y in zip(rows, ys)]
    counts = [top_k(r, lg) for r, lg in zip(rows, logits)]
    cnt_ref[0] = functools.reduce(lambda a, b: a + b, counts)


def _merge(ohg, att_o, att_l, gates, x2, mod3, w_hg, w_at, w_out, npost, npre, w_r, b_r, seq, tm):
    t = x2.shape[0]
    nt = t // tm
    row = lambda w: pl.BlockSpec((tm, w), lambda i: (i, 0))
    slab_i = jax.ShapeDtypeStruct((t, LANES), I32)
    slab_f = jax.ShapeDtypeStruct((t, LANES), F32)
    nblk = seq // tm
    grouped = [pl.BlockSpec((None, d, tm // d, AT_GW),
                            lambda i: ((i * tm) // seq, 0, i % nblk, 0)) for d in AT_DILS]
    return pl.pallas_call(
        _merge_kernel,
        grid=(nt,),
        in_specs=[row(HG_W)] + grouped + grouped + [row(2 * D), row(D),
                  pl.BlockSpec((1, 6, D), lambda i: ((i * tm) // seq, 0, 0)),
                  _const_spec(w_hg.shape), _const_spec(w_at.shape), _const_spec(w_out.shape),
                  _const_spec((1, D)), _const_spec((1, D)),
                  _const_spec(w_r.shape), _const_spec(b_r.shape)],
        out_specs=[row(D), row(HALF_D), row(LANES), row(LANES),
                   pl.BlockSpec((1, 1, LANES), lambda i: (i, 0, 0))],
        out_shape=[jax.ShapeDtypeStruct((t, D), F32), jax.ShapeDtypeStruct((t, HALF_D), U32),
                   slab_i, slab_f, jax.ShapeDtypeStruct((nt, 1, LANES), F32)],
        scratch_shapes=[pltpu.VMEM((AT_GW // LANES, tm, LANES), F32)] * 4,
        compiler_params=_cp(("parallel",)),
        name="merge",
    )(ohg, *att_o, *att_l, gates, x2, mod3, w_hg, w_at, w_out, npost, npre, w_r, b_r)


def _route_kernel(ids_ref, base_ref, tril_ref, dest_ref):
    ids = ids_ref[...]
    lane = lax.broadcasted_iota(I32, ids.shape, 1)
    hits = [lane == ids[:, k:k + 1] for k in range(TOP_K)]
    sel = jnp.zeros(ids.shape, F32)
    for hit in hits:
        sel = jnp.where(hit, 1.0, sel)
    before = _dot(tril_ref[...], sel.astype(BF16)) + base_ref[0]
    dest = jnp.zeros(ids.shape, I32)
    for k, hit in enumerate(hits):
        rk = jnp.sum(jnp.where(hit, before, 0.0), axis=-1, keepdims=True)
        dest = jnp.where(lane == k, rk.astype(I32), dest)
    dest_ref[...] = dest


def _route(ids, tile_base, tm):
    t = ids.shape[0]
    tril = jnp.asarray(np.tril(np.ones((tm, tm), np.float32), -1), BF16)
    return pl.pallas_call(
        _route_kernel,
        grid=(t // tm,),
        in_specs=[pl.BlockSpec((tm, LANES), lambda i: (i, 0)),
                  pl.BlockSpec((1, 1, LANES), lambda i: (i, 0, 0)),
                  _const_spec((tm, tm))],
        out_specs=pl.BlockSpec((tm, LANES), lambda i: (i, 0)),
        out_shape=jax.ShapeDtypeStruct((t, LANES), I32),
        compiler_params=_cp(("parallel",)),
        name="route",
    )(ids, tile_base, tril)


def _moe_kernel(be_ref, nu_ref, x_ref, wg_ref, wl_ref, bg_ref, bl_ref, wd_ref, bd_ref, o_ref):
    i = pl.program_id(0)

    @pl.when(i < nu_ref[0])
    def _():
        x_lo, x_hi = (a.astype(BF16) for a in _unpack_rows(x_ref[...]))
        gate = (_dot(x_lo, wg_ref[0, :HALF_D, :]) + _dot(x_hi, wg_ref[0, HALF_D:, :])
                + bg_ref[0])
        up = (_dot(x_lo, wl_ref[0, :HALF_D, :]) + _dot(x_hi, wl_ref[0, HALF_D:, :])
              + bl_ref[0])
        gate = jnp.minimum(gate, SWIGLU_LIMIT)
        up = jnp.clip(up, -SWIGLU_LIMIT, SWIGLU_LIMIT)
        act = (up + 1.0) * gate * jax.nn.sigmoid(SWIGLU_ALPHA * gate)
        y = _dot(act.astype(BF16), wd_ref[0]) + bd_ref[0]
        o_ref[...] = _pack_rows(y)

    @pl.when(i >= nu_ref[0])
    def _():
        o_ref[...] = jnp.zeros_like(o_ref)


def _moe(block_expert, n_used, xb, wg, wl, bg, bl, wd, bd, bm):
    r = xb.shape[0]
    nblk = r // bm
    ew = lambda shape: pl.BlockSpec((1,) + shape, lambda i, be, nu: (be[i], 0, 0))
    return pl.pallas_call(
        _moe_kernel,
        grid_spec=pltpu.PrefetchScalarGridSpec(
            num_scalar_prefetch=2,
            grid=(nblk,),
            in_specs=[pl.BlockSpec((bm, HALF_D), lambda i, be, nu: (i, 0)),
                      ew((D, D)), ew((D, D)), ew((1, D)), ew((1, D)), ew((D, D)), ew((1, D))],
            out_specs=pl.BlockSpec((bm, HALF_D), lambda i, be, nu: (i, 0)),
        ),
        out_shape=jax.ShapeDtypeStruct((r, HALF_D), U32),
        compiler_params=_cp(("arbitrary",)),
        name="moe",
    )(block_expert, n_used, xb, wg, wl, bg, bl, wd, bd)


def _final_kernel(y0_ref, y1_ref, y2_ref, y3_ref, wts_ref, x1_ref, mod_ref, npost_ref, o_ref):
    wts = wts_ref[...]
    y = jnp.zeros(x1_ref.shape, F32)
    for k, y_ref in enumerate((y0_ref, y1_ref, y2_ref, y3_ref)):
        y = y + wts[:, k:k + 1] * jnp.concatenate(_unpack_rows(y_ref[0]), axis=1)
    gt2 = mod_ref[0, 5:6, :]
    o_ref[...] = x1_ref[...] + gt2 * _rms(y, npost_ref[...])


def _final(yg, wts, x1, mod3, npost, seq, tm):
    t = x1.shape[0]
    slot = lambda k: pl.BlockSpec((1, tm, HALF_D), lambda i: (k, i, 0))
    return pl.pallas_call(
        _final_kernel,
        grid=(t // tm,),
        in_specs=[slot(0), slot(1), slot(2), slot(3),
                  pl.BlockSpec((tm, LANES), lambda i: (i, 0)),
                  pl.BlockSpec((tm, D), lambda i: (i, 0)),
                  pl.BlockSpec((1, 6, D), lambda i: ((i * tm) // seq, 0, 0)),
                  _const_spec((1, D))],
        out_specs=pl.BlockSpec((tm, D), lambda i: (i, 0)),
        out_shape=jax.ShapeDtypeStruct((t, D), F32),
        compiler_params=_cp(("parallel",)),
        name="final",
    )(yg, yg, yg, yg, wts, x1, mod3, npost)


def _tiles(seq):
    return dict(tm_in=min(seq, 512), tc=min(seq, 512), tm_merge=min(seq, 512),
                bm=512, tm_final=min(seq, 512))


def _prep_weights(w_in, w_hg_out, w_att_out, w_out, w_router, b_router, w_up, b_up, w_down, b_down):
    w = w_in[0]
    hw = HG_W
    a0 = 5 * hw
    aw = AT_GW * len(AT_DILS)
    q_at, k_at, v_at = (w[:, a0 + i * aw:a0 + (i + 1) * aw] for i in range(3))
    grp = lambda m, g: m[:, g * AT_GW:(g + 1) * AT_GW]
    w_at = jnp.concatenate(
        [jnp.concatenate([grp(q_at, g), grp(k_at, g), grp(v_at, g)], axis=1)
         for g in range(len(AT_DILS))], axis=1)
    pad = LANES - N_EXP
    wg, wl = _deinterleave(w_up[0])
    return dict(
        wg=wg, wl=wl,
        w_hga=jnp.concatenate([w[:, 0:hw], w[:, 3 * hw:4 * hw], w[:, 4 * hw:5 * hw]], 1).astype(BF16),
        w_hgf=w[:, hw:3 * hw].astype(BF16),
        w_at=w_at.astype(BF16),
        w_gt=w[:, a0 + 3 * aw:].astype(BF16),
        w_hg_out=w_hg_out[0].astype(BF16),
        w_att_out=w_att_out[0].astype(BF16),
        w_out=w_out[0].astype(BF16),
        w_r=jnp.pad(w_router[0], ((0, 0), (0, pad))).astype(BF16),
        b_r=jnp.pad(b_router[0], (0, pad), constant_values=NEG_BIG).reshape(1, LANES),
        bg=b_up[0][:, 0::2].reshape(N_EXP, 1, D),
        bl=b_up[0][:, 1::2].reshape(N_EXP, 1, D),
        wd=w_down[0].astype(BF16),
        bd=b_down[0].reshape(N_EXP, 1, D),
    )


def _trunk(x, mod, wts, norm_pre, norm_post, lb_logits, hg_gain):
    batch, seq, _ = x.shape
    t = batch * seq
    tl = _tiles(seq)
    x2 = x.reshape(t, D)
    mod3 = mod.reshape(batch, 6, D)
    rope = _rope_tables(seq)

    hga, hgf, at0, at1, at2, gates = _inproj(
        x2, mod3, norm_pre[0, 0].reshape(1, D), rope,
        wts["w_hga"], wts["w_hgf"], wts["w_at"], wts["w_gt"], seq, tl["tm_in"])

    o_fw = _hgrn(lb_logits[0], hga, hgf, batch, seq, tl["tc"], reverse=False)
    ohg = _hgrn(lb_logits[1], hga, hgf, batch, seq, tl["tc"], reverse=True,
                o_fw=o_fw, gain=hg_gain[0].reshape(1, HG_W))

    att = [_attn_group(a) for a in (at0, at1, at2)]

    x1, h2, ids, rw, cnt = _merge(
        ohg, [a[0] for a in att], [a[1] for a in att], gates, x2, mod3,
        wts["w_hg_out"], wts["w_att_out"], wts["w_out"],
        norm_post[0, 0].reshape(1, D), norm_pre[0, 1].reshape(1, D),
        wts["w_r"], wts["b_r"], seq, tl["tm_merge"])

    bm = tl["bm"]
    tmr = tl["tm_merge"]
    cnt_tiles = cnt.reshape(t // tmr, LANES).astype(I32)
    total = jnp.sum(cnt_tiles, axis=0)
    padded = (total + bm - 1) // bm * bm
    pad_end = jnp.cumsum(padded)
    pad_start = pad_end - padded
    tile_base = pad_start[None, :] + jnp.cumsum(cnt_tiles, axis=0) - cnt_tiles
    dest = _route(ids, tile_base.astype(F32).reshape(-1, 1, LANES), tmr)[:, :TOP_K]

    n_rows = t * TOP_K + N_EXP * bm
    n_blocks = n_rows // bm
    blk_start = jnp.arange(n_blocks, dtype=I32) * bm
    block_expert = jnp.minimum(
        jnp.sum(pad_end[None, :N_EXP] <= blk_start[:, None], axis=1), N_EXP - 1).astype(I32)
    n_used = (pad_end[N_EXP - 1] // bm).astype(I32).reshape(1)

    dest_kt = dest.T
    xb = _sc_scatter_rows(h2, dest_kt, n_rows)
    yb = _moe(block_expert, n_used, xb, wts["wg"], wts["wl"], wts["bg"], wts["bl"],
              wts["wd"], wts["bd"], bm)
    yg = _sc_gather_rows(yb, dest_kt.reshape(-1)).reshape(TOP_K, t, HALF_D)

    out = _final(yg, rw, x1, mod3, norm_post[0, 1].reshape(1, D), seq, tl["tm_final"])
    return out.reshape(batch, seq, D)


def kernel(x_prompt, x_sample, c_prompt, c_sample, w_ada, b_ada, norm_pre, norm_post, w_in,
           lb_logits, hg_norm_gain, w_hg_out, w_att_out, w_out, w_router, b_router,
           w_up, b_up, w_down, b_down):
    wts = _prep_weights(w_in, w_hg_out, w_att_out, w_out, w_router, b_router,
                        w_up, b_up, w_down, b_down)
    nb = c_prompt.shape[0]
    mod = _ada(jnp.concatenate([c_prompt, c_sample], axis=0), w_ada[0], b_ada[0])
    lb_l = lb_logits.astype(F32)
    y_p = _trunk(x_prompt, mod[:nb], wts, norm_pre, norm_post, lb_l, hg_norm_gain)
    y_s = _trunk(x_sample, mod[nb:], wts, norm_pre, norm_post, lb_l, hg_norm_gain)
    return (y_p, y_s)
```

```python
import functools

import numpy as np
import jax
import jax.numpy as jnp
from jax import lax
from jax.experimental import pallas as pl
from jax.experimental.pallas import tpu as pltpu
from jax.experimental.pallas import tpu_sc as plsc

F32 = jnp.float32
BF16 = jnp.bfloat16
I32 = jnp.int32
U32 = jnp.uint32

D = 1024
EPS = 1e-6
HG_H = 4
HG_DK = 128
HG_W = HG_H * HG_DK
HG_C = 64
HG_LEVELS = 6
HG_SAFE_LOGDECAY = -80.0
HG_UNROLL = 4
AT_DILS = (1, 4, 16)
AT_HALF = 64
AT_H = 4
AT_HD = 64
AT_GW = AT_H * AT_HD
ROPE_DIM = 16
ROPE_THETA = 500000.0
N_EXP = 32
TOP_K = 4
SWIGLU_LIMIT = 7.0
SWIGLU_ALPHA = 1.702
MERGE_SUBTILES = 4
LANES = 128
NEG_BIG = -1e30
HALF_D = D // 2
SC_CORES = 2
SC_SUBCORES = 16
SC_WORKERS = SC_CORES * SC_SUBCORES
SC_CHUNK = 128

VMEM_LIMIT = 56 * 1024 * 1024


def _cp(sem, vmem=VMEM_LIMIT):
    return pltpu.CompilerParams(dimension_semantics=sem, vmem_limit_bytes=vmem)


def _dot(a, b):
    return jnp.dot(a, b, preferred_element_type=F32)


def _dot_nt(a, b):
    return lax.dot_general(a, b, (((1,), (1,)), ((), ())), preferred_element_type=F32)


def _dot_tn(a, b):
    return lax.dot_general(a, b, (((0,), (0,)), ((), ())), preferred_element_type=F32)


def _split3(x):
    hi = x.astype(BF16)
    r = x - hi.astype(F32)
    mid = r.astype(BF16)
    lo = (r - mid.astype(F32)).astype(BF16)
    return hi, mid, lo


def _rms(x, gain):
    ms = jnp.mean(x * x, axis=-1, keepdims=True)
    return x * lax.rsqrt(ms + EPS) * gain


def _const_spec(shape):
    n = len(shape)
    return pl.BlockSpec(shape, lambda *_: (0,) * n)


def _pack_rows(y):
    bits = lambda a: lax.bitcast_convert_type(a.astype(BF16).astype(F32), U32)
    return (bits(y[:, :HALF_D]) >> 16) | (bits(y[:, HALF_D:]) & jnp.uint32(0xFFFF0000))


def _unpack_rows(w):
    lo = lax.bitcast_convert_type(w << 16, F32)
    hi = lax.bitcast_convert_type(w & jnp.uint32(0xFFFF0000), F32)
    return lo, hi


def _deint_kernel(w_ref, p_ref, g_ref, l_ref):
    p = p_ref[...]
    n_grp = w_ref.shape[2] // (2 * LANES)
    for b in range(n_grp):
        blk = w_ref[0, :, b * 2 * LANES:(b + 1) * 2 * LANES].astype(BF16)
        r = _dot(blk, p)
        g_ref[0, :, b * LANES:(b + 1) * LANES] = r[:, :LANES].astype(BF16)
        l_ref[0, :, b * LANES:(b + 1) * LANES] = r[:, LANES:].astype(BF16)


def _deinterleave(w_up):
    n_e, d, n2 = w_up.shape
    perm = np.zeros((2 * LANES, 2 * LANES), np.float32)
    perm[2 * np.arange(LANES), np.arange(LANES)] = 1.0
    perm[2 * np.arange(LANES) + 1, LANES + np.arange(LANES)] = 1.0
    tr = 512
    out = jax.ShapeDtypeStruct((n_e, d, n2 // 2), BF16)
    return pl.pallas_call(
        _deint_kernel,
        grid=(n_e, d // tr),
        in_specs=[pl.BlockSpec((1, tr, n2), lambda e, i: (e, i, 0)),
                  _const_spec(perm.shape)],
        out_specs=[pl.BlockSpec((1, tr, n2 // 2), lambda e, i: (e, i, 0))] * 2,
        out_shape=[out, out],
        compiler_params=_cp(("parallel", "parallel")),
        name="deinterleave",
    )(w_up, jnp.asarray(perm, BF16))


def _sc_mesh():
    return plsc.VectorSubcoreMesh(core_axis_name="c", subcore_axis_name="s")


def _sc_worker():
    return lax.axis_index("s") * SC_CORES + lax.axis_index("c")


def _sc_scatter_rows(src, dest_kt, n_rows):
    t, w = src.shape
    n_slot = dest_kt.shape[0]
    nch = t // (SC_WORKERS * SC_CHUNK)
    idx = dest_kt.reshape(n_slot, SC_WORKERS, nch, SC_CHUNK).transpose(1, 0, 2, 3)

    @functools.partial(
        pl.kernel, mesh=_sc_mesh(),
        out_type=jax.ShapeDtypeStruct((n_rows, w), src.dtype),
        scratch_types=[pltpu.VMEM((n_slot, nch, SC_CHUNK), I32),
                       pltpu.VMEM((SC_CHUNK, w), src.dtype)],
        name="sc_scatter")
    def run(src_hbm, idx_hbm, out_hbm, idx_v, rows_v):
        wid = _sc_worker()
        pltpu.sync_copy(idx_hbm.at[wid], idx_v)

        @pl.loop(0, nch)
        def _(j):
            pltpu.sync_copy(src_hbm.at[pl.ds((wid * nch + j) * SC_CHUNK, SC_CHUNK)], rows_v)
            for k in range(n_slot):
                pltpu.sync_copy(rows_v, out_hbm.at[idx_v.at[k, j]])

    return run(src, idx)


def _sc_gather_rows(table, idx):
    n = idx.shape[0]
    w = table.shape[1]
    nch = n // (SC_WORKERS * SC_CHUNK)
    idx3 = idx.reshape(SC_WORKERS, nch, SC_CHUNK)

    @functools.partial(
        pl.kernel, mesh=_sc_mesh(),
        out_type=jax.ShapeDtypeStruct((n, w), table.dtype),
        scratch_types=[pltpu.VMEM((nch, SC_CHUNK), I32),
                       pltpu.VMEM((SC_CHUNK, w), table.dtype)],
        name="sc_gather")
    def run(table_hbm, idx_hbm, out_hbm, idx_v, rows_v):
        wid = _sc_worker()
        pltpu.sync_copy(idx_hbm.at[wid], idx_v)

        @pl.loop(0, nch)
        def _(j):
            pltpu.sync_copy(table_hbm.at[idx_v.at[j]], rows_v)
            pltpu.sync_copy(rows_v, out_hbm.at[pl.ds((wid * nch + j) * SC_CHUNK, SC_CHUNK)])

    return run(table, idx3)


def _ada_kernel(c_ref, w_ref, b_ref, o_ref):
    c = c_ref[...]
    a = c * jax.nn.sigmoid(c)
    w = w_ref[...]
    a_hi = a.astype(BF16)
    a_lo = (a - a_hi.astype(F32)).astype(BF16)
    w_hi = w.astype(BF16)
    w_lo = (w - w_hi.astype(F32)).astype(BF16)
    o_ref[...] = _dot(a_hi, w_hi) + _dot(a_lo, w_hi) + _dot(a_hi, w_lo) + b_ref[...]


def _ada(c, w, b):
    nb = c.shape[0]
    n = w.shape[1]
    tn = 1536
    return pl.pallas_call(
        _ada_kernel,
        grid=(n // tn,),
        in_specs=[pl.BlockSpec((nb, D), lambda j: (0, 0)),
                  pl.BlockSpec((D, tn), lambda j: (0, j)),
                  pl.BlockSpec((1, tn), lambda j: (0, j))],
        out_specs=pl.BlockSpec((nb, tn), lambda j: (0, j)),
        out_shape=jax.ShapeDtypeStruct((nb, n), F32),
        compiler_params=_cp(("parallel",)),
        name="ada",
    )(c, w, b.reshape(1, n))


def _inproj_kernel(x_ref, mod_ref, gain_ref, cos_ref, s1_ref, s2_ref,
                   w_hga_ref, w_hgf_ref, w_at_ref, w_gt_ref,
                   hga_ref, hgf_ref, at0_ref, at1_ref, at2_ref, gt_ref, stage_ref):
    x = x_ref[...]
    tm = x.shape[0]
    sh = mod_ref[0, 0:1, :]
    sc = mod_ref[0, 1:2, :]
    h = _rms(x, gain_ref[...]) * (1.0 + sc) + sh
    hb = h.astype(BF16)
    hga_ref[...] = _dot(hb, w_hga_ref[...]).astype(BF16)
    hgf_ref[...] = _dot(hb, w_hgf_ref[...])
    cos = cos_ref[...]
    s1 = s1_ref[...]
    s2 = s2_ref[...]
    for g, o_ref in enumerate((at0_ref, at1_ref, at2_ref)):
        acc = _dot(hb, w_at_ref[:, g * 3 * AT_GW:(g + 1) * 3 * AT_GW])
        parts = []
        for j in range(2 * AT_GW // LANES):
            a = acc[:, j * LANES:(j + 1) * LANES]
            parts.append(a * cos + pltpu.roll(a, ROPE_DIM // 2, 1) * s1
                         + pltpu.roll(a, LANES - ROPE_DIM // 2, 1) * s2)
        parts.append(acc[:, 2 * AT_GW:])
        vals = jnp.concatenate(parts, axis=1)
        dil = AT_DILS[g]
        if dil == 1:
            o_ref[0, 0] = vals.astype(BF16)
        else:
            n_lt = 3 * AT_GW // LANES
            for c in range(n_lt):
                stage_ref[c] = vals[:, c * LANES:(c + 1) * LANES]
            for r in range(dil):
                o_ref[0, r] = jnp.concatenate(
                    [stage_ref[c, pl.ds(r, tm // dil, stride=dil), :] for c in range(n_lt)],
                    axis=1).astype(BF16)
    gt_ref[...] = _dot(hb, w_gt_ref[...]).astype(BF16)


def _inproj(x2, mod3, gain, rope, w_hga, w_hgf, w_at, w_gt, seq, tm):
    t = x2.shape[0]
    n_pos_blk = seq // tm
    row = lambda w: pl.BlockSpec((tm, w), lambda i: (i, 0))
    pos = pl.BlockSpec((tm, LANES), lambda i: (i % n_pos_blk, 0))
    batch = t // seq
    dilated = lambda d: pl.BlockSpec((1, d, tm // d, 3 * AT_GW),
                                     lambda i: ((i * tm) // seq, 0, i % n_pos_blk, 0))
    at_shape = lambda d: jax.ShapeDtypeStruct((batch, d, seq // d, 3 * AT_GW), BF16)
    return pl.pallas_call(
        _inproj_kernel,
        grid=(t // tm,),
        in_specs=[row(D),
                  pl.BlockSpec((1, 6, D), lambda i: ((i * tm) // seq, 0, 0)),
                  _const_spec((1, D)), pos, pos, pos,
                  _const_spec(w_hga.shape), _const_spec(w_hgf.shape),
                  _const_spec(w_at.shape), _const_spec(w_gt.shape)],
        out_specs=[row(3 * HG_W), row(2 * HG_W)] + [dilated(d) for d in AT_DILS] + [row(2 * D)],
        out_shape=[jax.ShapeDtypeStruct((t, 3 * HG_W), BF16),
                   jax.ShapeDtypeStruct((t, 2 * HG_W), F32)]
                  + [at_shape(d) for d in AT_DILS]
                  + [jax.ShapeDtypeStruct((t, 2 * D), BF16)],
        scratch_shapes=[pltpu.VMEM((3 * AT_GW // LANES, tm, LANES), F32)],
        compiler_params=_cp(("parallel",)),
        name="inproj",
    )(x2, mod3, gain, *rope, w_hga, w_hgf, w_at, w_gt)


def _rope_tables(seq):
    half = ROPE_DIM // 2
    inv_freq = ROPE_THETA ** (-np.arange(half, dtype=np.float32) / half)
    ang = jnp.arange(seq, dtype=F32)[:, None] * jnp.asarray(inv_freq)[None, :]
    cos, sin = jnp.cos(ang), jnp.sin(ang)
    ones = jnp.ones((seq, AT_HD - ROPE_DIM), F32)
    zeros = jnp.zeros((seq, AT_HD - ROPE_DIM), F32)
    zh = jnp.zeros((seq, half), F32)
    c_head = jnp.concatenate([cos, cos, ones], axis=1)
    s1_head = jnp.concatenate([zh, sin, zeros], axis=1)
    s2_head = jnp.concatenate([-sin, zh, zeros], axis=1)
    rep = LANES // AT_HD
    return tuple(jnp.tile(a, (1, rep)) for a in (c_head, s1_head, s2_head))


def _hgrn_consts(reverse):
    c = HG_C
    t = np.arange(c)[:, None]
    u = np.arange(c)[None, :]
    tri = (u >= t) if reverse else (u <= t)
    sels = []
    for lvl in range(HG_LEVELS):
        blk = 2 << lvl
        start = (np.arange(c) // blk) * blk
        piv = start + (1 << lvl) - (0 if reverse else 1)
        sels.append(np.arange(c)[None, :] == piv[:, None])
    sel = np.concatenate(sels, axis=0)
    rep = lambda m, n: jnp.asarray(np.concatenate([m.astype(np.float32)] * n, axis=1), BF16)
    return rep(tri, 2), rep(sel, 3)


def _hgrn_kernel(*refs, reverse, final, n_chunks):
    if final:
        (lbl_ref, tri_ref, sel_ref, q_ref, f_ref, v_ref, ofw_ref, gh_ref, gain_ref,
         o_ref, st_ref, k_s, b_s) = refs
    else:
        (lbl_ref, tri_ref, sel_ref, q_ref, f_ref, v_ref,
         o_ref, st_ref, k_s, b_s) = refs
    c = HG_C

    @pl.when(pl.program_id(1) == 0)
    def _():
        st_ref[...] = jnp.zeros_like(st_ref)

    l0 = lbl_ref[0:1, :]
    l1 = lbl_ref[1:2, :]
    lm = jnp.maximum(l0, l1)
    e0 = jnp.exp(l0 - lm)
    e1 = jnp.exp(l1 - lm)
    lb = e0 / (e0 + e1)
    tri = tri_ref[...]
    piv_row = c // 2 if reverse else c // 2 - 1

    def gates(ci, bmin):
        r0 = pl.multiple_of(ci * c, c)
        ff = f_ref[pl.ds(r0, c), :]
        e = jnp.exp(-jnp.abs(ff))
        r = 1.0 / (1.0 + e)
        pos = ff >= 0.0
        sg = jnp.where(pos, r, e * r)
        g = jnp.log(lb + (1.0 - lb) * sg)
        k_s[pl.ds(r0, c), :] = (1.0 - lb) * (1.0 - sg)
        g_hi = g.astype(BF16)
        g_lo = (g - g_hi.astype(F32)).astype(BF16)
        b = _dot(tri, jnp.concatenate([g_hi, g_lo], axis=0))
        b_s[pl.ds(r0, c), :] = b
        btot = b[0:1, :] if reverse else b[c - 1:c, :]
        half = b[piv_row:piv_row + 1, :]
        return jnp.minimum(bmin, jnp.minimum(half, btot - half))

    bmin = lax.fori_loop(0, n_chunks, gates, jnp.zeros((1, HG_W), F32), unroll=HG_UNROLL)
    safe = jnp.min(bmin) >= HG_SAFE_LOGDECAY

    ti = lax.broadcasted_iota(I32, (c, c), 0)
    si = lax.broadcasted_iota(I32, (c, c), 1)
    causal = (si >= ti) if reverse else (si <= ti)
    row = lax.broadcasted_iota(I32, (c, 1), 0)

    def att_robust(q, kin, b, piv):
        att = jnp.where(ti == si, _dot_nt(q.astype(BF16), kin.astype(BF16)), 0.0)
        for lvl in range(HG_LEVELS):
            p = piv[lvl * c:(lvl + 1) * c, :]
            bit = ((row >> lvl) & 1) == 1
            q_side = jnp.logical_not(bit) if reverse else bit
            qe = jnp.where(q_side, q * jnp.exp(jnp.minimum(b - p, 0.0)), 0.0).astype(BF16)
            ke = jnp.where(q_side, 0.0, kin * jnp.exp(jnp.minimum(p - b, 0.0))).astype(BF16)
            same = (ti >> (lvl + 1)) == (si >> (lvl + 1))
            att = att + jnp.where(same, _dot_nt(qe, ke), 0.0)
        return att

    heads = [slice(h * HG_DK, (h + 1) * HG_DK) for h in range(HG_H)]

    def load_chunk(r0):
        q = q_ref[pl.ds(r0, c), :].astype(F32)
        v = v_ref[pl.ds(r0, c), :]
        kin = k_s[pl.ds(r0, c), :]
        b = b_s[pl.ds(r0, c), :]
        btot = b[0:1, :] if reverse else b[c - 1:c, :]
        return q, v, kin, b, btot

    def emit(r0, outs):
        o = jnp.concatenate(outs, axis=1)
        if final:
            o = o + ofw_ref[pl.ds(r0, c), :]
            normed = []
            for sl in heads:
                ms = jnp.mean(o[:, sl] * o[:, sl], axis=-1, keepdims=True)
                normed.append(o[:, sl] * lax.rsqrt(ms + EPS))
            gh = gh_ref[pl.ds(r0, c), :].astype(F32)
            o = jnp.concatenate(normed, axis=1) * gain_ref[...] * (gh * jax.nn.sigmoid(gh))
        o_ref[pl.ds(r0, c), :] = o.astype(o_ref.dtype)

    def robust_step(i, carry):
        ci = (n_chunks - 1 - i) if reverse else i
        r0 = pl.multiple_of(ci * c, c)
        q, v, kin, b, btot = load_chunk(r0)
        qe_all = (q * jnp.exp(b)).astype(BF16)
        kd_all = (kin * jnp.exp(btot - b)).astype(BF16)
        piv = _dot(sel_ref[...], jnp.concatenate(_split3(b), axis=0))
        outs = []
        for h, sl in enumerate(heads):
            att = att_robust(q[:, sl], kin[:, sl], b[:, sl], piv[:, sl])
            st = st_ref[h]
            outs.append(_dot_nt(qe_all[:, sl], st.astype(BF16)) + _dot(att.astype(BF16), v[:, sl]))
            st_ref[h] = st * jnp.exp(btot[:, sl]) + _dot_tn(v[:, sl], kd_all[:, sl])
        emit(r0, outs)
        return carry

    grp = HG_UNROLL

    def fast_group(i, carry):
        gi = (n_chunks // grp - 1 - i) if reverse else i
        base = gi * (grp * c)
        order = range(grp - 1, -1, -1) if reverse else range(grp)
        chunks = []
        for j in order:
            r0 = pl.multiple_of(base + j * c, c)
            q, v, kin, b, btot = load_chunk(r0)
            bp = b[piv_row:piv_row + 1, :]
            chunks.append(dict(
                r0=r0, v=v, dec=jnp.exp(btot), pdec=jnp.exp(bp),
                qe=(q * jnp.exp(b - bp)).astype(BF16),
                ke=(kin * jnp.exp(bp - b)).astype(BF16),
                kd=(kin * jnp.exp(btot - b)).astype(BF16)))
        for ch in chunks:
            ch["att"] = [jnp.where(causal, _dot_nt(ch["qe"][:, sl], ch["ke"][:, sl]), 0.0).astype(BF16)
                         for sl in heads]
        for ch in chunks:
            ch["intra"] = [_dot(ch["att"][h], ch["v"][:, sl]) for h, sl in enumerate(heads)]
            ch["upd"] = [_dot_tn(ch["v"][:, sl], ch["kd"][:, sl]) for sl in heads]
        st = [st_ref[h] for h in range(HG_H)]
        for ch in chunks:
            outs = []
            for h, sl in enumerate(heads):
                st_piv = (st[h] * ch["pdec"][:, sl]).astype(BF16)
                outs.append(_dot_nt(ch["qe"][:, sl], st_piv) + ch["intra"][h])
                st[h] = st[h] * ch["dec"][:, sl] + ch["upd"][h]
            emit(ch["r0"], outs)
        for h in range(HG_H):
            st_ref[h] = st[h]
        return carry

    def run_fast():
        lax.fori_loop(0, n_chunks // grp, fast_group, 0)

    def run_robust():
        lax.fori_loop(0, n_chunks, robust_step, 0)

    lax.cond(safe, run_fast, run_robust)


def _hgrn(lb_logits_d, hga, hgf, batch, seq, tc, reverse, o_fw=None, gain=None):
    final = o_fw is not None
    t = hga.shape[0]
    nblk = seq // tc
    tri, sel = _hgrn_consts(reverse)

    def rows(col):
        if reverse:
            return lambda b, j: (b * nblk + nblk - 1 - j, col)
        return lambda b, j: (b * nblk + j, col)

    blk = lambda col: pl.BlockSpec((tc, HG_W), rows(col))
    in_specs = [_const_spec((2, HG_W)), _const_spec(tri.shape), _const_spec(sel.shape),
                blk(0), blk(1 if reverse else 0), blk(1)]
    args = [lb_logits_d, tri, sel, hga, hgf, hga]
    if final:
        in_specs += [blk(0), blk(2), _const_spec((1, HG_W))]
        args += [o_fw, hga, gain]
    return pl.pallas_call(
        functools.partial(_hgrn_kernel, reverse=reverse, final=final, n_chunks=tc // HG_C),
        grid=(batch, nblk),
        in_specs=in_specs,
        out_specs=blk(0),
        out_shape=jax.ShapeDtypeStruct((t, HG_W), BF16 if final else F32),
        scratch_shapes=[pltpu.VMEM((HG_H, HG_DK, HG_DK), F32),
                        pltpu.VMEM((tc, HG_W), F32),
                        pltpu.VMEM((tc, HG_W), F32)],
        compiler_params=_cp(("parallel", "arbitrary")),
        name="hgrn_bwd" if reverse else "hgrn_fwd",
    )(*args)


def _attn_kernel(q_ref, kl_ref, km_ref, kr_ref, vl_ref, vm_ref, vr_ref, o_ref, l_ref,
                 k_s, v_s, *, tq, ld):
    hw = AT_HALF
    n_res = q_ref.shape[0]
    for res in range(n_res):
        k_s[res, 0:hw, :] = kl_ref[res]
        k_s[res, hw:hw + tq, :] = km_ref[res]
        k_s[res, hw + tq:, :] = kr_ref[res]
        v_s[res, 0:hw, :] = vl_ref[res]
        v_s[res, hw:hw + tq, :] = vm_ref[res]
        v_s[res, hw + tq:, :] = vr_ref[res]
    qb = 2 * hw
    kb = 4 * hw
    base = pl.program_id(2) * tq
    lane_head = lax.broadcasted_iota(I32, (1, AT_GW), 1) // AT_HD
    q_sel = [jnp.where(lane_head == h, AT_HD ** -0.5, 0.0).astype(BF16) for h in range(AT_H)]
    out_head = lax.broadcasted_iota(I32, (qb, AT_GW), 1) // AT_HD
    for res, qs in [(res, qs) for res in range(n_res) for qs in range(0, tq, qb)]:
        q = q_ref[res, qs:qs + qb, :]
        kk = k_s[res, qs:qs + kb, :]
        vv = v_s[res, qs:qs + kb, :]
        qpos = base + qs + lax.broadcasted_iota(I32, (qb, kb), 0)
        kpos = base + qs - hw + lax.broadcasted_iota(I32, (qb, kb), 1)
        mask = (jnp.abs(qpos - kpos) <= hw) & (kpos >= 0) & (kpos < ld)
        o_all = jnp.zeros((qb, AT_GW), F32)
        l_all = jnp.zeros((qb, AT_GW), F32)
        scores = [jnp.where(mask, _dot_nt(q * q_sel[h], kk), NEG_BIG) for h in range(AT_H)]
        maxes = [jnp.max(s, axis=-1, keepdims=True) for s in scores]
        probs = [jnp.exp(s - m) for s, m in zip(scores, maxes)]
        sums = [jnp.sum(p, axis=-1, keepdims=True) for p in probs]
        pvs = [_dot(p.astype(BF16), vv) for p in probs]
        for h in range(AT_H):
            o_all = jnp.where(out_head == h, pvs[h] * (1.0 / sums[h]), o_all)
            l_all = jnp.where(out_head == h, maxes[h] + jnp.log(sums[h]), l_all)
        o_ref[res, qs:qs + qb, :] = o_all.astype(o_ref.dtype)
        l_ref[res, qs:qs + qb, :] = l_all


def _attn_group(qkv):
    batch, dil, ld, _ = qkv.shape
    hw = AT_HALF
    tq = min(ld, 512)
    nq = ld // tq
    nh = ld // hw
    per = tq // hw
    n_res = min(dil, max(1, 512 // tq))
    main = lambda part: pl.BlockSpec((None, n_res, tq, AT_GW), lambda b, r, j: (b, r, j, part))
    left = lambda part: pl.BlockSpec(
        (None, n_res, hw, AT_GW), lambda b, r, j: (b, r, jnp.maximum(j * per - 1, 0), part))
    right = lambda part: pl.BlockSpec(
        (None, n_res, hw, AT_GW), lambda b, r, j: (b, r, jnp.minimum((j + 1) * per, nh - 1), part))
    out_spec = pl.BlockSpec((None, n_res, tq, AT_GW), lambda b, r, j: (b, r, j, 0))
    return pl.pallas_call(
        functools.partial(_attn_kernel, tq=tq, ld=ld),
        grid=(batch, dil // n_res, nq),
        in_specs=[main(0), left(1), main(1), right(1), left(2), main(2), right(2)],
        out_specs=[out_spec, out_spec],
        out_shape=[jax.ShapeDtypeStruct((batch, dil, ld, AT_GW), BF16),
                   jax.ShapeDtypeStruct((batch, dil, ld, AT_GW), F32)],
        scratch_shapes=[pltpu.VMEM((n_res, tq + 2 * hw, AT_GW), BF16),
                        pltpu.VMEM((n_res, tq + 2 * hw, AT_GW), BF16)],
        compiler_params=_cp(("parallel", "parallel", "parallel")),
        name=f"attn_d{dil}",
    )(qkv, qkv, qkv, qkv, qkv, qkv, qkv)


def _merge_kernel(ohg_ref, o0_ref, o1_ref, o2_ref, l0_ref, l1_ref, l2_ref, gt_ref, x_ref, mod_ref,
                  w_hg_ref, w_at_ref, w_out_ref, npost_ref, npre_ref, w_r_ref, b_r_ref,
                  x1_ref, h2_ref, ids_ref, wts_ref, cnt_ref,
                  so1_ref, sl1_ref, so2_ref, sl2_ref):
    tm = x_ref.shape[0]
    sub = tm // MERGE_SUBTILES
    rows = [slice(i * sub, (i + 1) * sub) for i in range(MERGE_SUBTILES)]
    gt1 = mod_ref[0, 2:3, :]
    sh2 = mod_ref[0, 3:4, :]
    sc2 = mod_ref[0, 4:5, :]

    for src_ref, dst_ref in ((o1_ref, so1_ref), (l1_ref, sl1_ref), (o2_ref, so2_ref), (l2_ref, sl2_ref)):
        dil = src_ref.shape[0]
        for res in range(dil):
            vals = src_ref[res].astype(F32)
            for c in range(AT_GW // LANES):
                dst_ref[c, pl.ds(res, tm // dil, stride=dil), :] = vals[:, c * LANES:(c + 1) * LANES]

    def staged(ref, r):
        return jnp.concatenate([ref[c, r, :] for c in range(AT_GW // LANES)], axis=1)

    def branches(r):
        o0, l0 = o0_ref[0, r, :].astype(F32), l0_ref[0, r, :]
        o1, l1 = staged(so1_ref, r), staged(sl1_ref, r)
        o2, l2 = staged(so2_ref, r), staged(sl2_ref, r)
        m = jnp.maximum(jnp.maximum(l0, l1), l2)
        e0, e1, e2 = jnp.exp(l0 - m), jnp.exp(l1 - m), jnp.exp(l2 - m)
        oa = (e0 * o0 + e1 * o1 + e2 * o2) / (e0 + e1 + e2)
        return _dot(ohg_ref[r, :], w_hg_ref[...]), _dot(oa.astype(BF16), w_at_ref[...])

    def out_proj(r, b_hg, b_at):
        g_hg = jax.nn.sigmoid(gt_ref[r, 0:D])
        g_at = jax.nn.sigmoid(gt_ref[r, D:2 * D])
        merged = g_hg * b_hg.astype(BF16) + g_at * b_at.astype(BF16)
        return _dot(merged, w_out_ref[...])

    def norms(r, y):
        x1 = x_ref[r, :] + gt1 * _rms(y, npost_ref[...])
        x1_ref[r, :] = x1
        h2 = _rms(x1, npre_ref[...]) * (1.0 + sc2) + sh2
        h2_ref[r, :] = _pack_rows(h2)
        return _dot(h2.astype(BF16), w_r_ref[...]) + b_r_ref[...]

    def top_k(r, logits):
        lane = lax.broadcasted_iota(I32, logits.shape, 1)
        lane_f = lane.astype(F32)
        work = logits
        vals, idxs = [], []
        sel = jnp.zeros(logits.shape, F32)
        for _ in range(TOP_K):
            mk = jnp.max(work, axis=-1, keepdims=True)
            ik = jnp.min(jnp.where(work == mk, lane_f, float(LANES)), axis=-1, keepdims=True)
            hit = lane_f == ik
            sel = jnp.where(hit, 1.0, sel)
            work = jnp.where(hit, -jnp.inf, work)
            vals.append(mk)
            idxs.append(ik)
        es = [jnp.exp(v - vals[0]) for v in vals]
        den = es[0] + es[1] + es[2] + es[3]
        ids = jnp.zeros(logits.shape, F32)
        wts = jnp.zeros(logits.shape, F32)
        for k in range(TOP_K):
            ids = jnp.where(lane == k, idxs[k], ids)
            wts = jnp.where(lane == k, es[k] / den, wts)
        ids_ref[r, :] = ids.astype(I32)
        wts_ref[r, :] = wts
        return jnp.sum(sel, axis=0, keepdims=True)

    br = [branches(r) for r in rows]
    ys = [out_proj(r, *b) for r, b in zip(rows, br)]
    logits = [norms(r, y) for r, y in zip(rows, ys)]
    counts = [top_k(r, lg) for r, lg in zip(rows, logits)]
    cnt_ref[0] = functools.reduce(lambda a, b: a + b, counts)


def _merge(ohg, att_o, att_l, gates, x2, mod3, w_hg, w_at, w_out, npost, npre, w_r, b_r, seq, tm):
    t = x2.shape[0]
    nt = t // tm
    row = lambda w: pl.BlockSpec((tm, w), lambda i: (i, 0))
    slab_i = jax.ShapeDtypeStruct((t, LANES), I32)
    slab_f = jax.ShapeDtypeStruct((t, LANES), F32)
    nblk = seq // tm
    grouped = [pl.BlockSpec((None, d, tm // d, AT_GW),
                            lambda i: ((i * tm) // seq, 0, i % nblk, 0)) for d in AT_DILS]
    return pl.pallas_call(
        _merge_kernel,
        grid=(nt,),
        in_specs=[row(HG_W)] + grouped + grouped + [row(2 * D), row(D),
                  pl.BlockSpec((1, 6, D), lambda i: ((i * tm) // seq, 0, 0)),
                  _const_spec(w_hg.shape), _const_spec(w_at.shape), _const_spec(w_out.shape),
                  _const_spec((1, D)), _const_spec((1, D)),
                  _const_spec(w_r.shape), _const_spec(b_r.shape)],
        out_specs=[row(D), row(HALF_D), row(LANES), row(LANES),
                   pl.BlockSpec((1, 1, LANES), lambda i: (i, 0, 0))],
        out_shape=[jax.ShapeDtypeStruct((t, D), F32), jax.ShapeDtypeStruct((t, HALF_D), U32),
                   slab_i, slab_f, jax.ShapeDtypeStruct((nt, 1, LANES), F32)],
        scratch_shapes=[pltpu.VMEM((AT_GW // LANES, tm, LANES), F32)] * 4,
        compiler_params=_cp(("parallel",)),
        name="merge",
    )(ohg, *att_o, *att_l, gates, x2, mod3, w_hg, w_at, w_out, npost, npre, w_r, b_r)


def _route_kernel(ids_ref, base_ref, tril_ref, dest_ref):
    ids = ids_ref[...]
    lane = lax.broadcasted_iota(I32, ids.shape, 1)
    hits = [lane == ids[:, k:k + 1] for k in range(TOP_K)]
    sel = jnp.zeros(ids.shape, F32)
    for hit in hits:
        sel = jnp.where(hit, 1.0, sel)
    before = _dot(tril_ref[...], sel.astype(BF16)) + base_ref[0]
    dest = jnp.zeros(ids.shape, I32)
    for k, hit in enumerate(hits):
        rk = jnp.sum(jnp.where(hit, before, 0.0), axis=-1, keepdims=True)
        dest = jnp.where(lane == k, rk.astype(I32), dest)
    dest_ref[...] = dest


def _route(ids, tile_base, tm):
    t = ids.shape[0]
    tril = jnp.asarray(np.tril(np.ones((tm, tm), np.float32), -1), BF16)
    return pl.pallas_call(
        _route_kernel,
        grid=(t // tm,),
        in_specs=[pl.BlockSpec((tm, LANES), lambda i: (i, 0)),
                  pl.BlockSpec((1, 1, LANES), lambda i: (i, 0, 0)),
                  _const_spec((tm, tm))],
        out_specs=pl.BlockSpec((tm, LANES), lambda i: (i, 0)),
        out_shape=jax.ShapeDtypeStruct((t, LANES), I32),
        compiler_params=_cp(("parallel",)),
        name="route",
    )(ids, tile_base, tril)


def _moe_kernel(be_ref, nu_ref, x_ref, wg_ref, wl_ref, bg_ref, bl_ref, wd_ref, bd_ref, o_ref):
    i = pl.program_id(0)

    @pl.when(i < nu_ref[0])
    def _():
        x_lo, x_hi = (a.astype(BF16) for a in _unpack_rows(x_ref[...]))
        gate = (_dot(x_lo, wg_ref[0, :HALF_D, :]) + _dot(x_hi, wg_ref[0, HALF_D:, :])
                + bg_ref[0])
        up = (_dot(x_lo, wl_ref[0, :HALF_D, :]) + _dot(x_hi, wl_ref[0, HALF_D:, :])
              + bl_ref[0])
        gate = jnp.minimum(gate, SWIGLU_LIMIT)
        up = jnp.clip(up, -SWIGLU_LIMIT, SWIGLU_LIMIT)
        act = (up + 1.0) * gate * jax.nn.sigmoid(SWIGLU_ALPHA * gate)
        y = _dot(act.astype(BF16), wd_ref[0]) + bd_ref[0]
        o_ref[...] = _pack_rows(y)

    @pl.when(i >= nu_ref[0])
    def _():
        o_ref[...] = jnp.zeros_like(o_ref)


def _moe(block_expert, n_used, xb, wg, wl, bg, bl, wd, bd, bm):
    r = xb.shape[0]
    nblk = r // bm
    ew = lambda shape: pl.BlockSpec((1,) + shape, lambda i, be, nu: (be[i], 0, 0))
    return pl.pallas_call(
        _moe_kernel,
        grid_spec=pltpu.PrefetchScalarGridSpec(
            num_scalar_prefetch=2,
            grid=(nblk,),
            in_specs=[pl.BlockSpec((bm, HALF_D), lambda i, be, nu: (i, 0)),
                      ew((D, D)), ew((D, D)), ew((1, D)), ew((1, D)), ew((D, D)), ew((1, D))],
            out_specs=pl.BlockSpec((bm, HALF_D), lambda i, be, nu: (i, 0)),
        ),
        out_shape=jax.ShapeDtypeStruct((r, HALF_D), U32),
        compiler_params=_cp(("arbitrary",)),
        name="moe",
    )(block_expert, n_used, xb, wg, wl, bg, bl, wd, bd)


def _final_kernel(y0_ref, y1_ref, y2_ref, y3_ref, wts_ref, x1_ref, mod_ref, npost_ref, o_ref):
    wts = wts_ref[...]
    y = jnp.zeros(x1_ref.shape, F32)
    for k, y_ref in enumerate((y0_ref, y1_ref, y2_ref, y3_ref)):
        y = y + wts[:, k:k + 1] * jnp.concatenate(_unpack_rows(y_ref[0]), axis=1)
    gt2 = mod_ref[0, 5:6, :]
    o_ref[...] = x1_ref[...] + gt2 * _rms(y, npost_ref[...])


def _final(yg, wts, x1, mod3, npost, seq, tm):
    t = x1.shape[0]
    slot = lambda k: pl.BlockSpec((1, tm, HALF_D), lambda i: (k, i, 0))
    return pl.pallas_call(
        _final_kernel,
        grid=(t // tm,),
        in_specs=[slot(0), slot(1), slot(2), slot(3),
                  pl.BlockSpec((tm, LANES), lambda i: (i, 0)),
                  pl.BlockSpec((tm, D), lambda i: (i, 0)),
                  pl.BlockSpec((1, 6, D), lambda i: ((i * tm) // seq, 0, 0)),
                  _const_spec((1, D))],
        out_specs=pl.BlockSpec((tm, D), lambda i: (i, 0)),
        out_shape=jax.ShapeDtypeStruct((t, D), F32),
        compiler_params=_cp(("parallel",)),
        name="final",
    )(yg, yg, yg, yg, wts, x1, mod3, npost)


def _tiles(seq):
    return dict(tm_in=min(seq, 512), tc=min(seq, 512), tm_merge=min(seq, 512),
                bm=512, tm_final=min(seq, 512))


def _prep_weights(w_in, w_hg_out, w_att_out, w_out, w_router, b_router, w_up, b_up, w_down, b_down):
    w = w_in[0]
    hw = HG_W
    a0 = 5 * hw
    aw = AT_GW * len(AT_DILS)
    q_at, k_at, v_at = (w[:, a0 + i * aw:a0 + (i + 1) * aw] for i in range(3))
    grp = lambda m, g: m[:, g * AT_GW:(g + 1) * AT_GW]
    w_at = jnp.concatenate(
        [jnp.concatenate([grp(q_at, g), grp(k_at, g), grp(v_at, g)], axis=1)
         for g in range(len(AT_DILS))], axis=1)
    pad = LANES - N_EXP
    wg, wl = _deinterleave(w_up[0])
    return dict(
        wg=wg, wl=wl,
        w_hga=jnp.concatenate([w[:, 0:hw], w[:, 3 * hw:4 * hw], w[:, 4 * hw:5 * hw]], 1).astype(BF16),
        w_hgf=w[:, hw:3 * hw].astype(BF16),
        w_at=w_at.astype(BF16),
        w_gt=w[:, a0 + 3 * aw:].astype(BF16),
        w_hg_out=w_hg_out[0].astype(BF16),
        w_att_out=w_att_out[0].astype(BF16),
        w_out=w_out[0].astype(BF16),
        w_r=jnp.pad(w_router[0], ((0, 0), (0, pad))).astype(BF16),
        b_r=jnp.pad(b_router[0], (0, pad), constant_values=NEG_BIG).reshape(1, LANES),
        bg=b_up[0][:, 0::2].reshape(N_EXP, 1, D),
        bl=b_up[0][:, 1::2].reshape(N_EXP, 1, D),
        wd=w_down[0].astype(BF16),
        bd=b_down[0].reshape(N_EXP, 1, D),
    )


def _trunk(x, mod, wts, norm_pre, norm_post, lb_logits, hg_gain):
    batch, seq, _ = x.shape
    t = batch * seq
    tl = _tiles(seq)
    x2 = x.reshape(t, D)
    mod3 = mod.reshape(batch, 6, D)
    rope = _rope_tables(seq)

    hga, hgf, at0, at1, at2, gates = _inproj(
        x2, mod3, norm_pre[0, 0].reshape(1, D), rope,
        wts["w_hga"], wts["w_hgf"], wts["w_at"], wts["w_gt"], seq, tl["tm_in"])

    o_fw = _hgrn(lb_logits[0], hga, hgf, batch, seq, tl["tc"], reverse=False)
    ohg = _hgrn(lb_logits[1], hga, hgf, batch, seq, tl["tc"], reverse=True,
                o_fw=o_fw, gain=hg_gain[0].reshape(1, HG_W))

    att = [_attn_group(a) for a in (at0, at1, at2)]

    x1, h2, ids, rw, cnt = _merge(
        ohg, [a[0] for a in att], [a[1] for a in att], gates, x2, mod3,
        wts["w_hg_out"], wts["w_att_out"], wts["w_out"],
        norm_post[0, 0].reshape(1, D), norm_pre[0, 1].reshape(1, D),
        wts["w_r"], wts["b_r"], seq, tl["tm_merge"])

    bm = tl["bm"]
    tmr = tl["tm_merge"]
    cnt_tiles = cnt.reshape(t // tmr, LANES).astype(I32)
    total = jnp.sum(cnt_tiles, axis=0)
    padded = (total + bm - 1) // bm * bm
    pad_end = jnp.cumsum(padded)
    pad_start = pad_end - padded
    tile_base = pad_start[None, :] + jnp.cumsum(cnt_tiles, axis=0) - cnt_tiles
    dest = _route(ids, tile_base.astype(F32).reshape(-1, 1, LANES), tmr)[:, :TOP_K]

    n_rows = t * TOP_K + N_EXP * bm
    n_blocks = n_rows // bm
    blk_start = jnp.arange(n_blocks, dtype=I32) * bm
    block_expert = jnp.minimum(
        jnp.sum(pad_end[None, :N_EXP] <= blk_start[:, None], axis=1), N_EXP - 1).astype(I32)
    n_used = (pad_end[N_EXP - 1] // bm).astype(I32).reshape(1)

    dest_kt = dest.T
    xb = _sc_scatter_rows(h2, dest_kt, n_rows)
    yb = _moe(block_expert, n_used, xb, wts["wg"], wts["wl"], wts["bg"], wts["bl"],
              wts["wd"], wts["bd"], bm)
    yg = _sc_gather_rows(yb, dest_kt.reshape(-1)).reshape(TOP_K, t, HALF_D)

    out = _final(yg, rw, x1, mod3, norm_post[0, 1].reshape(1, D), seq, tl["tm_final"])
    return out.reshape(batch, seq, D)


def kernel(x_prompt, x_sample, c_prompt, c_sample, w_ada, b_ada, norm_pre, norm_post, w_in,
           lb_logits, hg_norm_gain, w_hg_out, w_att_out, w_out, w_router, b_router,
           w_up, b_up, w_down, b_down):
    wts = _prep_weights(w_in, w_hg_out, w_att_out, w_out, w_router, b_router,
                        w_up, b_up, w_down, b_down)
    nb = c_prompt.shape[0]
    mod = _ada(jnp.concatenate([c_prompt, c_sample], axis=0), w_ada[0], b_ada[0])
    lb_l = lb_logits.astype(F32)
    y_p = _trunk(x_prompt, mod[:nb], wts, norm_pre, norm_post, lb_l, hg_norm_gain)
    y_s = _trunk(x_sample, mod[nb:], wts, norm_pre, norm_post, lb_l, hg_norm_gain)
    return (y_p, y_s)
```

```python
import functools

import numpy as np
import jax
import jax.numpy as jnp
from jax import lax
from jax.experimental import pallas as pl
from jax.experimental.pallas import tpu as pltpu
from jax.experimental.pallas import tpu_sc as plsc

F32 = jnp.float32
BF16 = jnp.bfloat16
I32 = jnp.int32
U32 = jnp.uint32

D = 1024
EPS = 1e-6
HG_H = 4
HG_DK = 128
HG_W = HG_H * HG_DK
HG_C = 64
HG_LEVELS = 6
HG_SAFE_LOGDECAY = -80.0
HG_UNROLL = 4
AT_DILS = (1, 4, 16)
AT_HALF = 64
AT_H = 4
AT_HD = 64
AT_GW = AT_H * AT_HD
ROPE_DIM = 16
ROPE_THETA = 500000.0
N_EXP = 32
TOP_K = 4
SWIGLU_LIMIT = 7.0
SWIGLU_ALPHA = 1.702
MERGE_SUBTILES = 4
LANES = 128
NEG_BIG = -1e30
HALF_D = D // 2
SC_CORES = 2
SC_SUBCORES = 16
SC_WORKERS = SC_CORES * SC_SUBCORES
SC_CHUNK = 128

VMEM_LIMIT = 56 * 1024 * 1024


def _cp(sem, vmem=VMEM_LIMIT):
    return pltpu.CompilerParams(dimension_semantics=sem, vmem_limit_bytes=vmem)


def _dot(a, b):
    return jnp.dot(a, b, preferred_element_type=F32)


def _dot_nt(a, b):
    return lax.dot_general(a, b, (((1,), (1,)), ((), ())), preferred_element_type=F32)


def _dot_tn(a, b):
    return lax.dot_general(a, b, (((0,), (0,)), ((), ())), preferred_element_type=F32)


def _split3(x):
    hi = x.astype(BF16)
    r = x - hi.astype(F32)
    mid = r.astype(BF16)
    lo = (r - mid.astype(F32)).astype(BF16)
    return hi, mid, lo


def _rms(x, gain):
    ms = jnp.mean(x * x, axis=-1, keepdims=True)
    return x * lax.rsqrt(ms + EPS) * gain


def _const_spec(shape):
    n = len(shape)
    return pl.BlockSpec(shape, lambda *_: (0,) * n)


def _pack_rows(y):
    bits = lambda a: lax.bitcast_convert_type(a.astype(BF16).astype(F32), U32)
    return (bits(y[:, :HALF_D]) >> 16) | (bits(y[:, HALF_D:]) & jnp.uint32(0xFFFF0000))


def _unpack_rows(w):
    lo = lax.bitcast_convert_type(w << 16, F32)
    hi = lax.bitcast_convert_type(w & jnp.uint32(0xFFFF0000), F32)
    return lo, hi


def _deint_kernel(w_ref, p_ref, g_ref, l_ref):
    p = p_ref[...]
    n_grp = w_ref.shape[2] // (2 * LANES)
    for b in range(n_grp):
        blk = w_ref[0, :, b * 2 * LANES:(b + 1) * 2 * LANES].astype(BF16)
        r = _dot(blk, p)
        g_ref[0, :, b * LANES:(b + 1) * LANES] = r[:, :LANES].astype(BF16)
        l_ref[0, :, b * LANES:(b + 1) * LANES] = r[:, LANES:].astype(BF16)


def _deinterleave(w_up):
    n_e, d, n2 = w_up.shape
    perm = np.zeros((2 * LANES, 2 * LANES), np.float32)
    perm[2 * np.arange(LANES), np.arange(LANES)] = 1.0
    perm[2 * np.arange(LANES) + 1, LANES + np.arange(LANES)] = 1.0
    tr = 512
    out = jax.ShapeDtypeStruct((n_e, d, n2 // 2), BF16)
    return pl.pallas_call(
        _deint_kernel,
        grid=(n_e, d // tr),
        in_specs=[pl.BlockSpec((1, tr, n2), lambda e, i: (e, i, 0)),
                  _const_spec(perm.shape)],
        out_specs=[pl.BlockSpec((1, tr, n2 // 2), lambda e, i: (e, i, 0))] * 2,
        out_shape=[out, out],
        compiler_params=_cp(("parallel", "parallel")),
        name="deinterleave",
    )(w_up, jnp.asarray(perm, BF16))


def _sc_mesh():
    return plsc.VectorSubcoreMesh(core_axis_name="c", subcore_axis_name="s")


def _sc_worker():
    return lax.axis_index("s") * SC_CORES + lax.axis_index("c")


def _sc_scatter_rows(src, dest_kt, n_rows):
    t, w = src.shape
    n_slot = dest_kt.shape[0]
    nch = t // (SC_WORKERS * SC_CHUNK)
    idx = dest_kt.reshape(n_slot, SC_WORKERS, nch, SC_CHUNK).transpose(1, 0, 2, 3)

    @functools.partial(
        pl.kernel, mesh=_sc_mesh(),
        out_type=jax.ShapeDtypeStruct((n_rows, w), src.dtype),
        scratch_types=[pltpu.VMEM((n_slot, nch, SC_CHUNK), I32),
                       pltpu.VMEM((SC_CHUNK, w), src.dtype)],
        name="sc_scatter")
    def run(src_hbm, idx_hbm, out_hbm, idx_v, rows_v):
        wid = _sc_worker()
        pltpu.sync_copy(idx_hbm.at[wid], idx_v)

        @pl.loop(0, nch)
        def _(j):
            pltpu.sync_copy(src_hbm.at[pl.ds((wid * nch + j) * SC_CHUNK, SC_CHUNK)], rows_v)
            for k in range(n_slot):
                pltpu.sync_copy(rows_v, out_hbm.at[idx_v.at[k, j]])

    return run(src, idx)


def _sc_gather_rows(table, idx):
    n = idx.shape[0]
    w = table.shape[1]
    nch = n // (SC_WORKERS * SC_CHUNK)
    idx3 = idx.reshape(SC_WORKERS, nch, SC_CHUNK)

    @functools.partial(
        pl.kernel, mesh=_sc_mesh(),
        out_type=jax.ShapeDtypeStruct((n, w), table.dtype),
        scratch_types=[pltpu.VMEM((nch, SC_CHUNK), I32),
                       pltpu.VMEM((SC_CHUNK, w), table.dtype)],
        name="sc_gather")
    def run(table_hbm, idx_hbm, out_hbm, idx_v, rows_v):
        wid = _sc_worker()
        pltpu.sync_copy(idx_hbm.at[wid], idx_v)

        @pl.loop(0, nch)
        def _(j):
            pltpu.sync_copy(table_hbm.at[idx_v.at[j]], rows_v)
            pltpu.sync_copy(rows_v, out_hbm.at[pl.ds((wid * nch + j) * SC_CHUNK, SC_CHUNK)])

    return run(table, idx3)


def _ada_kernel(c_ref, w_ref, b_ref, o_ref):
    c = c_ref[...]
    a = c * jax.nn.sigmoid(c)
    w = w_ref[...]
    a_hi = a.astype(BF16)
    a_lo = (a - a_hi.astype(F32)).astype(BF16)
    w_hi = w.astype(BF16)
    w_lo = (w - w_hi.astype(F32)).astype(BF16)
    o_ref[...] = _dot(a_hi, w_hi) + _dot(a_lo, w_hi) + _dot(a_hi, w_lo) + b_ref[...]


def _ada(c, w, b):
    nb = c.shape[0]
    n = w.shape[1]
    tn = 1536
    return pl.pallas_call(
        _ada_kernel,
        grid=(n // tn,),
        in_specs=[pl.BlockSpec((nb, D), lambda j: (0, 0)),
                  pl.BlockSpec((D, tn), lambda j: (0, j)),
                  pl.BlockSpec((1, tn), lambda j: (0, j))],
        out_specs=pl.BlockSpec((nb, tn), lambda j: (0, j)),
        out_shape=jax.ShapeDtypeStruct((nb, n), F32),
        compiler_params=_cp(("parallel",)),
        name="ada",
    )(c, w, b.reshape(1, n))


def _inproj_kernel(x_ref, mod_ref, gain_ref, cos_ref, s1_ref, s2_ref,
                   w_hga_ref, w_hgf_ref, w_at_ref, w_gt_ref,
                   hga_ref, hgf_ref, at0_ref, at1_ref, at2_ref, gt_ref, stage_ref):
    x = x_ref[...]
    tm = x.shape[0]
    sh = mod_ref[0, 0:1, :]
    sc = mod_ref[0, 1:2, :]
    h = _rms(x, gain_ref[...]) * (1.0 + sc) + sh
    hb = h.astype(BF16)
    hga_ref[...] = _dot(hb, w_hga_ref[...]).astype(BF16)
    hgf_ref[...] = _dot(hb, w_hgf_ref[...])
    cos = cos_ref[...]
    s1 = s1_ref[...]
    s2 = s2_ref[...]
    for g, o_ref in enumerate((at0_ref, at1_ref, at2_ref)):
        acc = _dot(hb, w_at_ref[:, g * 3 * AT_GW:(g + 1) * 3 * AT_GW])
        parts = []
        for j in range(2 * AT_GW // LANES):
            a = acc[:, j * LANES:(j + 1) * LANES]
            parts.append(a * cos + pltpu.roll(a, ROPE_DIM // 2, 1) * s1
                         + pltpu.roll(a, LANES - ROPE_DIM // 2, 1) * s2)
        parts.append(acc[:, 2 * AT_GW:])
        vals = jnp.concatenate(parts, axis=1)
        dil = AT_DILS[g]
        if dil == 1:
            o_ref[0, 0] = vals.astype(BF16)
        else:
            n_lt = 3 * AT_GW // LANES
            for c in range(n_lt):
                stage_ref[c] = vals[:, c * LANES:(c + 1) * LANES]
            for r in range(dil):
                o_ref[0, r] = jnp.concatenate(
                    [stage_ref[c, pl.ds(r, tm // dil, stride=dil), :] for c in range(n_lt)],
                    axis=1).astype(BF16)
    gt_ref[...] = _dot(hb, w_gt_ref[...]).astype(BF16)


def _inproj(x2, mod3, gain, rope, w_hga, w_hgf, w_at, w_gt, seq, tm):
    t = x2.shape[0]
    n_pos_blk = seq // tm
    row = lambda w: pl.BlockSpec((tm, w), lambda i: (i, 0))
    pos = pl.BlockSpec((tm, LANES), lambda i: (i % n_pos_blk, 0))
    batch = t // seq
    dilated = lambda d: pl.BlockSpec((1, d, tm // d, 3 * AT_GW),
                                     lambda i: ((i * tm) // seq, 0, i % n_pos_blk, 0))
    at_shape = lambda d: jax.ShapeDtypeStruct((batch, d, seq // d, 3 * AT_GW), BF16)
    return pl.pallas_call(
        _inproj_kernel,
        grid=(t // tm,),
        in_specs=[row(D),
                  pl.BlockSpec((1, 6, D), lambda i: ((i * tm) // seq, 0, 0)),
                  _const_spec((1, D)), pos, pos, pos,
                  _const_spec(w_hga.shape), _const_spec(w_hgf.shape),
                  _const_spec(w_at.shape), _const_spec(w_gt.shape)],
        out_specs=[row(3 * HG_W), row(2 * HG_W)] + [dilated(d) for d in AT_DILS] + [row(2 * D)],
        out_shape=[jax.ShapeDtypeStruct((t, 3 * HG_W), BF16),
                   jax.ShapeDtypeStruct((t, 2 * HG_W), F32)]
                  + [at_shape(d) for d in AT_DILS]
                  + [jax.ShapeDtypeStruct((t, 2 * D), BF16)],
        scratch_shapes=[pltpu.VMEM((3 * AT_GW // LANES, tm, LANES), F32)],
        compiler_params=_cp(("parallel",)),
        name="inproj",
    )(x2, mod3, gain, *rope, w_hga, w_hgf, w_at, w_gt)


def _rope_tables(seq):
    half = ROPE_DIM // 2
    inv_freq = ROPE_THETA ** (-np.arange(half, dtype=np.float32) / half)
    ang = jnp.arange(seq, dtype=F32)[:, None] * jnp.asarray(inv_freq)[None, :]
    cos, sin = jnp.cos(ang), jnp.sin(ang)
    ones = jnp.ones((seq, AT_HD - ROPE_DIM), F32)
    zeros = jnp.zeros((seq, AT_HD - ROPE_DIM), F32)
    zh = jnp.zeros((seq, half), F32)
    c_head = jnp.concatenate([cos, cos, ones], axis=1)
    s1_head = jnp.concatenate([zh, sin, zeros], axis=1)
    s2_head = jnp.concatenate([-sin, zh, zeros], axis=1)
    rep = LANES // AT_HD
    return tuple(jnp.tile(a, (1, rep)) for a in (c_head, s1_head, s2_head))


def _hgrn_consts(reverse):
    c = HG_C
    t = np.arange(c)[:, None]
    u = np.arange(c)[None, :]
    tri = (u >= t) if reverse else (u <= t)
    sels = []
    for lvl in range(HG_LEVELS):
        blk = 2 << lvl
        start = (np.arange(c) // blk) * blk
        piv = start + (1 << lvl) - (0 if reverse else 1)
        sels.append(np.arange(c)[None, :] == piv[:, None])
    sel = np.concatenate(sels, axis=0)
    rep = lambda m, n: jnp.asarray(np.concatenate([m.astype(np.float32)] * n, axis=1), BF16)
    return rep(tri, 2), rep(sel, 3)


def _hgrn_kernel(*refs, reverse, final, n_chunks):
    if final:
        (lbl_ref, tri_ref, sel_ref, q_ref, f_ref, v_ref, ofw_ref, gh_ref, gain_ref,
         o_ref, st_ref, st0_ref) = refs
    else:
        (lbl_ref, tri_ref, sel_ref, q_ref, f_ref, v_ref,
         o_ref, st_ref, st0_ref) = refs
    c = HG_C

    @pl.when(pl.program_id(1) == 0)
    def _():
        st_ref[...] = jnp.zeros_like(st_ref)

    l0 = lbl_ref[0:1, :]
    l1 = lbl_ref[1:2, :]
    lm = jnp.maximum(l0, l1)
    e0 = jnp.exp(l0 - lm)
    e1 = jnp.exp(l1 - lm)
    lb = e0 / (e0 + e1)
    tri = tri_ref[...]
    piv_row = c // 2 if reverse else c // 2 - 1

    def gate_math(r0):
        ff = f_ref[pl.ds(r0, c), :]
        e = jnp.exp(-jnp.abs(ff))
        r = 1.0 / (1.0 + e)
        sg = jnp.where(ff >= 0.0, r, e * r)
        g = jnp.log(lb + (1.0 - lb) * sg)
        kin = (1.0 - lb) * (1.0 - sg)
        g_hi = g.astype(BF16)
        g_lo = (g - g_hi.astype(F32)).astype(BF16)
        b = _dot(tri, jnp.concatenate([g_hi, g_lo], axis=0))
        btot = b[0:1, :] if reverse else b[c - 1:c, :]
        return kin, b, btot

    ti = lax.broadcasted_iota(I32, (c, c), 0)
    si = lax.broadcasted_iota(I32, (c, c), 1)
    causal = (si >= ti) if reverse else (si <= ti)
    row = lax.broadcasted_iota(I32, (c, 1), 0)

    def att_robust(q, kin, b, piv):
        att = jnp.where(ti == si, _dot_nt(q.astype(BF16), kin.astype(BF16)), 0.0)
        for lvl in range(HG_LEVELS):
            p = piv[lvl * c:(lvl + 1) * c, :]
            bit = ((row >> lvl) & 1) == 1
            q_side = jnp.logical_not(bit) if reverse else bit
            qe = jnp.where(q_side, q * jnp.exp(jnp.minimum(b - p, 0.0)), 0.0).astype(BF16)
            ke = jnp.where(q_side, 0.0, kin * jnp.exp(jnp.minimum(p - b, 0.0))).astype(BF16)
            same = (ti >> (lvl + 1)) == (si >> (lvl + 1))
            att = att + jnp.where(same, _dot_nt(qe, ke), 0.0)
        return att

    heads = [slice(h * HG_DK, (h + 1) * HG_DK) for h in range(HG_H)]


    def emit(r0, outs):
        o = jnp.concatenate(outs, axis=1)
        if final:
            o = o + ofw_ref[pl.ds(r0, c), :]
            normed = []
            for sl in heads:
                ms = jnp.mean(o[:, sl] * o[:, sl], axis=-1, keepdims=True)
                normed.append(o[:, sl] * lax.rsqrt(ms + EPS))
            gh = gh_ref[pl.ds(r0, c), :].astype(F32)
            o = jnp.concatenate(normed, axis=1) * gain_ref[...] * (gh * jax.nn.sigmoid(gh))
        o_ref[pl.ds(r0, c), :] = o.astype(o_ref.dtype)

    def robust_step(i, carry):
        ci = (n_chunks - 1 - i) if reverse else i
        r0 = pl.multiple_of(ci * c, c)
        q = q_ref[pl.ds(r0, c), :].astype(F32)
        v = v_ref[pl.ds(r0, c), :]
        kin, b, btot = gate_math(r0)
        qe_all = (q * jnp.exp(b)).astype(BF16)
        kd_all = (kin * jnp.exp(btot - b)).astype(BF16)
        piv = _dot(sel_ref[...], jnp.concatenate(_split3(b), axis=0))
        outs = []
        for h, sl in enumerate(heads):
            att = att_robust(q[:, sl], kin[:, sl], b[:, sl], piv[:, sl])
            st = st_ref[h]
            outs.append(_dot_nt(qe_all[:, sl], st.astype(BF16)) + _dot(att.astype(BF16), v[:, sl]))
            st_ref[h] = st * jnp.exp(btot[:, sl]) + _dot_tn(v[:, sl], kd_all[:, sl])
        emit(r0, outs)
        return carry

    grp = HG_UNROLL

    def fast_group(i, bmin):
        gi = (n_chunks // grp - 1 - i) if reverse else i
        base = gi * (grp * c)
        order = range(grp - 1, -1, -1) if reverse else range(grp)
        chunks = []
        for j in order:
            r0 = pl.multiple_of(base + j * c, c)
            q = q_ref[pl.ds(r0, c), :].astype(F32)
            kin, b, btot = gate_math(r0)
            bp = b[piv_row:piv_row + 1, :]
            bmin = jnp.minimum(bmin, jnp.minimum(bp, btot - bp))
            chunks.append(dict(
                r0=r0, v=v_ref[pl.ds(r0, c), :], dec=jnp.exp(btot), pdec=jnp.exp(bp),
                qe=(q * jnp.exp(b - bp)).astype(BF16),
                ke=(kin * jnp.exp(bp - b)).astype(BF16),
                kd=(kin * jnp.exp(btot - b)).astype(BF16)))
        for ch in chunks:
            ch["att"] = [jnp.where(causal, _dot_nt(ch["qe"][:, sl], ch["ke"][:, sl]), 0.0).astype(BF16)
                         for sl in heads]
        for ch in chunks:
            ch["intra"] = [_dot(ch["att"][h], ch["v"][:, sl]) for h, sl in enumerate(heads)]
            ch["upd"] = [_dot_tn(ch["v"][:, sl], ch["kd"][:, sl]) for sl in heads]
        st = [st_ref[h] for h in range(HG_H)]
        for ch in chunks:
            outs = []
            for h, sl in enumerate(heads):
                st_piv = (st[h] * ch["pdec"][:, sl]).astype(BF16)
                outs.append(_dot_nt(ch["qe"][:, sl], st_piv) + ch["intra"][h])
                st[h] = st[h] * ch["dec"][:, sl] + ch["upd"][h]
            emit(ch["r0"], outs)
        for h in range(HG_H):
            st_ref[h] = st[h]
        return bmin

    st0_ref[...] = st_ref[...]
    bmin = lax.fori_loop(0, n_chunks // grp, fast_group, jnp.zeros((1, HG_W), F32))

    @pl.when(jnp.min(bmin) < HG_SAFE_LOGDECAY)
    def _():
        st_ref[...] = st0_ref[...]
        lax.fori_loop(0, n_chunks, robust_step, 0)


def _hgrn(lb_logits_d, hga, hgf, batch, seq, tc, reverse, o_fw=None, gain=None):
    final = o_fw is not None
    t = hga.shape[0]
    nblk = seq // tc
    tri, sel = _hgrn_consts(reverse)

    def rows(col):
        if reverse:
            return lambda b, j: (b * nblk + nblk - 1 - j, col)
        return lambda b, j: (b * nblk + j, col)

    blk = lambda col: pl.BlockSpec((tc, HG_W), rows(col))
    in_specs = [_const_spec((2, HG_W)), _const_spec(tri.shape), _const_spec(sel.shape),
                blk(0), blk(1 if reverse else 0), blk(1)]
    args = [lb_logits_d, tri, sel, hga, hgf, hga]
    if final:
        in_specs += [blk(0), blk(2), _const_spec((1, HG_W))]
        args += [o_fw, hga, gain]
    return pl.pallas_call(
        functools.partial(_hgrn_kernel, reverse=reverse, final=final, n_chunks=tc // HG_C),
        grid=(batch, nblk),
        in_specs=in_specs,
        out_specs=blk(0),
        out_shape=jax.ShapeDtypeStruct((t, HG_W), BF16 if final else F32),
        scratch_shapes=[pltpu.VMEM((HG_H, HG_DK, HG_DK), F32),
                        pltpu.VMEM((HG_H, HG_DK, HG_DK), F32)],
        compiler_params=_cp(("parallel", "arbitrary")),
        name="hgrn_bwd" if reverse else "hgrn_fwd",
    )(*args)


def _attn_kernel(q_ref, kl_ref, km_ref, kr_ref, vl_ref, vm_ref, vr_ref, o_ref, l_ref,
                 k_s, v_s, *, tq, ld):
    hw = AT_HALF
    n_res = q_ref.shape[0]
    for res in range(n_res):
        k_s[res, 0:hw, :] = kl_ref[res]
        k_s[res, hw:hw + tq, :] = km_ref[res]
        k_s[res, hw + tq:, :] = kr_ref[res]
        v_s[res, 0:hw, :] = vl_ref[res]
        v_s[res, hw:hw + tq, :] = vm_ref[res]
        v_s[res, hw + tq:, :] = vr_ref[res]
    qb = 2 * hw
    kb = 4 * hw
    base = pl.program_id(2) * tq
    lane_head = lax.broadcasted_iota(I32, (1, AT_GW), 1) // AT_HD
    q_sel = [jnp.where(lane_head == h, AT_HD ** -0.5, 0.0).astype(BF16) for h in range(AT_H)]
    out_head = lax.broadcasted_iota(I32, (qb, AT_GW), 1) // AT_HD
    rel = (lax.broadcasted_iota(I32, (AT_H * qb, kb), 0) % qb
           - lax.broadcasted_iota(I32, (AT_H * qb, kb), 1) + hw)
    band = jnp.where(jnp.abs(rel) <= hw, 0.0, NEG_BIG)
    key_col = lax.broadcasted_iota(I32, (1, kb), 1)
    for res, qs in [(res, qs) for res in range(n_res) for qs in range(0, tq, qb)]:
        q = q_ref[res, qs:qs + qb, :]
        kk = k_s[res, qs:qs + kb, :]
        vv = v_s[res, qs:qs + kb, :]
        q4 = jnp.concatenate([q * q_sel[h] for h in range(AT_H)], axis=0)
        kpos = base + qs - hw + key_col
        edge = jnp.where((kpos >= 0) & (kpos < ld), 0.0, NEG_BIG)
        s = _dot_nt(q4, kk) + band + edge
        m = jnp.max(s, axis=-1, keepdims=True)
        p = jnp.exp(s - m)
        l = jnp.sum(p, axis=-1, keepdims=True)
        pv = _dot(p.astype(BF16), vv) * (1.0 / l)
        lse = m + jnp.log(l)
        o_all = jnp.zeros((qb, AT_GW), F32)
        l_all = jnp.zeros((qb, AT_GW), F32)
        for h in range(AT_H):
            o_all = jnp.where(out_head == h, pv[h * qb:(h + 1) * qb, :], o_all)
            l_all = jnp.where(out_head == h, lse[h * qb:(h + 1) * qb, :], l_all)
        o_ref[res, qs:qs + qb, :] = o_all.astype(o_ref.dtype)
        l_ref[res, qs:qs + qb, :] = l_all


def _attn_group(qkv):
    batch, dil, ld, _ = qkv.shape
    hw = AT_HALF
    tq = min(ld, 512)
    nq = ld // tq
    nh = ld // hw
    per = tq // hw
    n_res = min(dil, max(1, 512 // tq))
    main = lambda part: pl.BlockSpec((None, n_res, tq, AT_GW), lambda b, r, j: (b, r, j, part))
    left = lambda part: pl.BlockSpec(
        (None, n_res, hw, AT_GW), lambda b, r, j: (b, r, jnp.maximum(j * per - 1, 0), part))
    right = lambda part: pl.BlockSpec(
        (None, n_res, hw, AT_GW), lambda b, r, j: (b, r, jnp.minimum((j + 1) * per, nh - 1), part))
    out_spec = pl.BlockSpec((None, n_res, tq, AT_GW), lambda b, r, j: (b, r, j, 0))
    return pl.pallas_call(
        functools.partial(_attn_kernel, tq=tq, ld=ld),
        grid=(batch, dil // n_res, nq),
        in_specs=[main(0), left(1), main(1), right(1), left(2), main(2), right(2)],
        out_specs=[out_spec, out_spec],
        out_shape=[jax.ShapeDtypeStruct((batch, dil, ld, AT_GW), BF16),
                   jax.ShapeDtypeStruct((batch, dil, ld, AT_GW), F32)],
        scratch_shapes=[pltpu.VMEM((n_res, tq + 2 * hw, AT_GW), BF16),
                        pltpu.VMEM((n_res, tq + 2 * hw, AT_GW), BF16)],
        compiler_params=_cp(("parallel", "parallel", "parallel")),
        name=f"attn_d{dil}",
    )(qkv, qkv, qkv, qkv, qkv, qkv, qkv)


def _merge_kernel(ohg_ref, o0_ref, o1_ref, o2_ref, l0_ref, l1_ref, l2_ref, gt_ref, x_ref, mod_ref,
                  w_hg_ref, w_at_ref, w_out_ref, npost_ref, npre_ref, w_r_ref, b_r_ref,
                  x1_ref, h2_ref, ids_ref, wts_ref, cnt_ref,
                  so1_ref, sl1_ref, so2_ref, sl2_ref):
    tm = x_ref.shape[0]
    sub = tm // MERGE_SUBTILES
    rows = [slice(i * sub, (i + 1) * sub) for i in range(MERGE_SUBTILES)]
    gt1 = mod_ref[0, 2:3, :]
    sh2 = mod_ref[0, 3:4, :]
    sc2 = mod_ref[0, 4:5, :]

    for src_ref, dst_ref in ((o1_ref, so1_ref), (l1_ref, sl1_ref), (o2_ref, so2_ref), (l2_ref, sl2_ref)):
        dil = src_ref.shape[0]
        for res in range(dil):
            vals = src_ref[res].astype(F32)
            for c in range(AT_GW // LANES):
                dst_ref[c, pl.ds(res, tm // dil, stride=dil), :] = vals[:, c * LANES:(c + 1) * LANES]

    def staged(ref, r):
        return jnp.concatenate([ref[c, r, :] for c in range(AT_GW // LANES)], axis=1)

    def branches(r):
        o0, l0 = o0_ref[0, r, :].astype(F32), l0_ref[0, r, :]
        o1, l1 = staged(so1_ref, r), staged(sl1_ref, r)
        o2, l2 = staged(so2_ref, r), staged(sl2_ref, r)
        m = jnp.maximum(jnp.maximum(l0, l1), l2)
        e0, e1, e2 = jnp.exp(l0 - m), jnp.exp(l1 - m), jnp.exp(l2 - m)
        oa = (e0 * o0 + e1 * o1 + e2 * o2) / (e0 + e1 + e2)
        return _dot(ohg_ref[r, :], w_hg_ref[...]), _dot(oa.astype(BF16), w_at_ref[...])

    def out_proj(r, b_hg, b_at):
        g_hg = jax.nn.sigmoid(gt_ref[r, 0:D])
        g_at = jax.nn.sigmoid(gt_ref[r, D:2 * D])
        merged = g_hg * b_hg.astype(BF16) + g_at * b_at.astype(BF16)
        return _dot(merged, w_out_ref[...])

    def norms(r, y):
        x1 = x_ref[r, :] + gt1 * _rms(y, npost_ref[...])
        x1_ref[r, :] = x1
        h2 = _rms(x1, npre_ref[...]) * (1.0 + sc2) + sh2
        h2_ref[r, :] = _pack_rows(h2)
        return _dot(h2.astype(BF16), w_r_ref[...]) + b_r_ref[...]

    def top_k(r, logits):
        lane = lax.broadcasted_iota(I32, logits.shape, 1)
        lane_f = lane.astype(F32)
        work = logits
        vals, idxs = [], []
        sel = jnp.zeros(logits.shape, F32)
        for _ in range(TOP_K):
            mk = jnp.max(work, axis=-1, keepdims=True)
            ik = jnp.min(jnp.where(work == mk, lane_f, float(LANES)), axis=-1, keepdims=True)
            hit = lane_f == ik
            sel = jnp.where(hit, 1.0, sel)
            work = jnp.where(hit, -jnp.inf, work)
            vals.append(mk)
            idxs.append(ik)
        es = [jnp.exp(v - vals[0]) for v in vals]
        den = es[0] + es[1] + es[2] + es[3]
        ids = jnp.zeros(logits.shape, F32)
        wts = jnp.zeros(logits.shape, F32)
        for k in range(TOP_K):
            ids = jnp.where(lane == k, idxs[k], ids)
            wts = jnp.where(lane == k, es[k] / den, wts)
        ids_ref[r, :] = ids.astype(I32)
        wts_ref[r, :] = wts
        return jnp.sum(sel, axis=0, keepdims=True)

    br = [branches(r) for r in rows]
    ys = [out_proj(r, *b) for r, b in zip(rows, br)]
    logits = [norms(r, y) for r, y in zip(rows, ys)]
    counts = [top_k(r, lg) for r, lg in zip(rows, logits)]
    cnt_ref[0] = functools.reduce(lambda a, b: a + b, counts)


def _merge(ohg, att_o, att_l, gates, x2, mod3, w_hg, w_at, w_out, npost, npre, w_r, b_r, seq, tm):
    t = x2.shape[0]
    nt = t // tm
    row = lambda w: pl.BlockSpec((tm, w), lambda i: (i, 0))
    slab_i = jax.ShapeDtypeStruct((t, LANES), I32)
    slab_f = jax.ShapeDtypeStruct((t, LANES), F32)
    nblk = seq // tm
    grouped = [pl.BlockSpec((None, d, tm // d, AT_GW),
                            lambda i: ((i * tm) // seq, 0, i % nblk, 0)) for d in AT_DILS]
    return pl.pallas_call(
        _merge_kernel,
        grid=(nt,),
        in_specs=[row(HG_W)] + grouped + grouped + [row(2 * D), row(D),
                  pl.BlockSpec((1, 6, D), lambda i: ((i * tm) // seq, 0, 0)),
                  _const_spec(w_hg.shape), _const_spec(w_at.shape), _const_spec(w_out.shape),
                  _const_spec((1, D)), _const_spec((1, D)),
                  _const_spec(w_r.shape), _const_spec(b_r.shape)],
        out_specs=[row(D), row(HALF_D), row(LANES), row(LANES),
                   pl.BlockSpec((1, 1, LANES), lambda i: (i, 0, 0))],
        out_shape=[jax.ShapeDtypeStruct((t, D), F32), jax.ShapeDtypeStruct((t, HALF_D), U32),
                   slab_i, slab_f, jax.ShapeDtypeStruct((nt, 1, LANES), F32)],
        scratch_shapes=[pltpu.VMEM((AT_GW // LANES, tm, LANES), F32)] * 4,
        compiler_params=_cp(("parallel",)),
        name="merge",
    )(ohg, *att_o, *att_l, gates, x2, mod3, w_hg, w_at, w_out, npost, npre, w_r, b_r)


def _route_kernel(ids_ref, base_ref, tril_ref, dest_ref):
    ids = ids_ref[...]
    lane = lax.broadcasted_iota(I32, ids.shape, 1)
    hits = [lane == ids[:, k:k + 1] for k in range(TOP_K)]
    sel = jnp.zeros(ids.shape, F32)
    for hit in hits:
        sel = jnp.where(hit, 1.0, sel)
    before = _dot(tril_ref[...], sel.astype(BF16)) + base_ref[0]
    dest = jnp.zeros(ids.shape, I32)
    for k, hit in enumerate(hits):
        rk = jnp.sum(jnp.where(hit, before, 0.0), axis=-1, keepdims=True)
        dest = jnp.where(lane == k, rk.astype(I32), dest)
    dest_ref[...] = dest


def _route(ids, tile_base, tm):
    t = ids.shape[0]
    tril = jnp.asarray(np.tril(np.ones((tm, tm), np.float32), -1), BF16)
    return pl.pallas_call(
        _route_kernel,
        grid=(t // tm,),
        in_specs=[pl.BlockSpec((tm, LANES), lambda i: (i, 0)),
                  pl.BlockSpec((1, 1, LANES), lambda i: (i, 0, 0)),
                  _const_spec((tm, tm))],
        out_specs=pl.BlockSpec((tm, LANES), lambda i: (i, 0)),
        out_shape=jax.ShapeDtypeStruct((t, LANES), I32),
        compiler_params=_cp(("parallel",)),
        name="route",
    )(ids, tile_base, tril)


def _moe_kernel(be_ref, nu_ref, x_ref, wg_ref, wl_ref, bg_ref, bl_ref, wd_ref, bd_ref, o_ref):
    i = pl.program_id(0)

    @pl.when(i < nu_ref[0])
    def _():
        x_lo, x_hi = (a.astype(BF16) for a in _unpack_rows(x_ref[...]))
        gate = (_dot(x_lo, wg_ref[0, :HALF_D, :]) + _dot(x_hi, wg_ref[0, HALF_D:, :])
                + bg_ref[0])
        up = (_dot(x_lo, wl_ref[0, :HALF_D, :]) + _dot(x_hi, wl_ref[0, HALF_D:, :])
              + bl_ref[0])
        gate = jnp.minimum(gate, SWIGLU_LIMIT)
        up = jnp.clip(up, -SWIGLU_LIMIT, SWIGLU_LIMIT)
        act = (up + 1.0) * gate * jax.nn.sigmoid(SWIGLU_ALPHA * gate)
        y = _dot(act.astype(BF16), wd_ref[0]) + bd_ref[0]
        o_ref[...] = _pack_rows(y)

    @pl.when(i >= nu_ref[0])
    def _():
        o_ref[...] = jnp.zeros_like(o_ref)


def _moe(block_expert, n_used, xb, wg, wl, bg, bl, wd, bd, bm):
    r = xb.shape[0]
    nblk = r // bm
    ew = lambda shape: pl.BlockSpec((1,) + shape, lambda i, be, nu: (be[i], 0, 0))
    return pl.pallas_call(
        _moe_kernel,
        grid_spec=pltpu.PrefetchScalarGridSpec(
            num_scalar_prefetch=2,
            grid=(nblk,),
            in_specs=[pl.BlockSpec((bm, HALF_D), lambda i, be, nu: (i, 0)),
                      ew((D, D)), ew((D, D)), ew((1, D)), ew((1, D)), ew((D, D)), ew((1, D))],
            out_specs=pl.BlockSpec((bm, HALF_D), lambda i, be, nu: (i, 0)),
        ),
        out_shape=jax.ShapeDtypeStruct((r, HALF_D), U32),
        compiler_params=_cp(("arbitrary",)),
        name="moe",
    )(block_expert, n_used, xb, wg, wl, bg, bl, wd, bd)


def _final_kernel(y0_ref, y1_ref, y2_ref, y3_ref, wts_ref, x1_ref, mod_ref, npost_ref, o_ref):
    wts = wts_ref[...]
    y = jnp.zeros(x1_ref.shape, F32)
    for k, y_ref in enumerate((y0_ref, y1_ref, y2_ref, y3_ref)):
        y = y + wts[:, k:k + 1] * jnp.concatenate(_unpack_rows(y_ref[0]), axis=1)
    gt2 = mod_ref[0, 5:6, :]
    o_ref[...] = x1_ref[...] + gt2 * _rms(y, npost_ref[...])


def _final(yg, wts, x1, mod3, npost, seq, tm):
    t = x1.shape[0]
    slot = lambda k: pl.BlockSpec((1, tm, HALF_D), lambda i: (k, i, 0))
    return pl.pallas_call(
        _final_kernel,
        grid=(t // tm,),
        in_specs=[slot(0), slot(1), slot(2), slot(3),
                  pl.BlockSpec((tm, LANES), lambda i: (i, 0)),
                  pl.BlockSpec((tm, D), lambda i: (i, 0)),
                  pl.BlockSpec((1, 6, D), lambda i: ((i * tm) // seq, 0, 0)),
                  _const_spec((1, D))],
        out_specs=pl.BlockSpec((tm, D), lambda i: (i, 0)),
        out_shape=jax.ShapeDtypeStruct((t, D), F32),
        compiler_params=_cp(("parallel",)),
        name="final",
    )(yg, yg, yg, yg, wts, x1, mod3, npost)


def _tiles(seq):
    return dict(tm_in=min(seq, 512), tc=min(seq, 512), tm_merge=min(seq, 512),
                bm=512, tm_final=min(seq, 512))


def _prep_weights(w_in, w_hg_out, w_att_out, w_out, w_router, b_router, w_up, b_up, w_down, b_down):
    w = w_in[0]
    hw = HG_W
    a0 = 5 * hw
    aw = AT_GW * len(AT_DILS)
    q_at, k_at, v_at = (w[:, a0 + i * aw:a0 + (i + 1) * aw] for i in range(3))
    grp = lambda m, g: m[:, g * AT_GW:(g + 1) * AT_GW]
    w_at = jnp.concatenate(
        [jnp.concatenate([grp(q_at, g), grp(k_at, g), grp(v_at, g)], axis=1)
         for g in range(len(AT_DILS))], axis=1)
    pad = LANES - N_EXP
    wg, wl = _deinterleave(w_up[0])
    return dict(
        wg=wg, wl=wl,
        w_hga=jnp.concatenate([w[:, 0:hw], w[:, 3 * hw:4 * hw], w[:, 4 * hw:5 * hw]], 1).astype(BF16),
        w_hgf=w[:, hw:3 * hw].astype(BF16),
        w_at=w_at.astype(BF16),
        w_gt=w[:, a0 + 3 * aw:].astype(BF16),
        w_hg_out=w_hg_out[0].astype(BF16),
        w_att_out=w_att_out[0].astype(BF16),
        w_out=w_out[0].astype(BF16),
        w_r=jnp.pad(w_router[0], ((0, 0), (0, pad))).astype(BF16),
        b_r=jnp.pad(b_router[0], (0, pad), constant_values=NEG_BIG).reshape(1, LANES),
        bg=b_up[0][:, 0::2].reshape(N_EXP, 1, D),
        bl=b_up[0][:, 1::2].reshape(N_EXP, 1, D),
        wd=w_down[0].astype(BF16),
        bd=b_down[0].reshape(N_EXP, 1, D),
    )


def _trunk(x, mod, wts, norm_pre, norm_post, lb_logits, hg_gain):
    batch, seq, _ = x.shape
    t = batch * seq
    tl = _tiles(seq)
    x2 = x.reshape(t, D)
    mod3 = mod.reshape(batch, 6, D)
    rope = _rope_tables(seq)

    hga, hgf, at0, at1, at2, gates = _inproj(
        x2, mod3, norm_pre[0, 0].reshape(1, D), rope,
        wts["w_hga"], wts["w_hgf"], wts["w_at"], wts["w_gt"], seq, tl["tm_in"])

    o_fw = _hgrn(lb_logits[0], hga, hgf, batch, seq, tl["tc"], reverse=False)
    ohg = _hgrn(lb_logits[1], hga, hgf, batch, seq, tl["tc"], reverse=True,
                o_fw=o_fw, gain=hg_gain[0].reshape(1, HG_W))

    att = [_attn_group(a) for a in (at0, at1, at2)]

    x1, h2, ids, rw, cnt = _merge(
        ohg, [a[0] for a in att], [a[1] for a in att], gates, x2, mod3,
        wts["w_hg_out"], wts["w_att_out"], wts["w_out"],
        norm_post[0, 0].reshape(1, D), norm_pre[0, 1].reshape(1, D),
        wts["w_r"], wts["b_r"], seq, tl["tm_merge"])

    bm = tl["bm"]
    tmr = tl["tm_merge"]
    cnt_tiles = cnt.reshape(t // tmr, LANES).astype(I32)
    total = jnp.sum(cnt_tiles, axis=0)
    padded = (total + bm - 1) // bm * bm
    pad_end = jnp.cumsum(padded)
    pad_start = pad_end - padded
    tile_base = pad_start[None, :] + jnp.cumsum(cnt_tiles, axis=0) - cnt_tiles
    dest = _route(ids, tile_base.astype(F32).reshape(-1, 1, LANES), tmr)[:, :TOP_K]

    n_rows = t * TOP_K + N_EXP * bm
    n_blocks = n_rows // bm
    blk_start = jnp.arange(n_blocks, dtype=I32) * bm
    block_expert = jnp.minimum(
        jnp.sum(pad_end[None, :N_EXP] <= blk_start[:, None], axis=1), N_EXP - 1).astype(I32)
    n_used = (pad_end[N_EXP - 1] // bm).astype(I32).reshape(1)

    dest_kt = dest.T
    xb = _sc_scatter_rows(h2, dest_kt, n_rows)
    yb = _moe(block_expert, n_used, xb, wts["wg"], wts["wl"], wts["bg"], wts["bl"],
              wts["wd"], wts["bd"], bm)
    yg = _sc_gather_rows(yb, dest_kt.reshape(-1)).reshape(TOP_K, t, HALF_D)

    out = _final(yg, rw, x1, mod3, norm_post[0, 1].reshape(1, D), seq, tl["tm_final"])
    return out.reshape(batch, seq, D)


def kernel(x_prompt, x_sample, c_prompt, c_sample, w_ada, b_ada, norm_pre, norm_post, w_in,
           lb_logits, hg_norm_gain, w_hg_out, w_att_out, w_out, w_router, b_router,
           w_up, b_up, w_down, b_down):
    wts = _prep_weights(w_in, w_hg_out, w_att_out, w_out, w_router, b_router,
                        w_up, b_up, w_down, b_down)
    nb = c_prompt.shape[0]
    mod = _ada(jnp.concatenate([c_prompt, c_sample], axis=0), w_ada[0], b_ada[0])
    lb_l = lb_logits.astype(F32)
    y_p = _trunk(x_prompt, mod[:nb], wts, norm_pre, norm_post, lb_l, hg_norm_gain)
    y_s = _trunk(x_sample, mod[nb:], wts, norm_pre, norm_post, lb_l, hg_norm_gain)
    return (y_p, y_s)
```

```python
import functools

import numpy as np
import jax
import jax.numpy as jnp
from jax import lax
from jax.experimental import pallas as pl
from jax.experimental.pallas import tpu as pltpu
from jax.experimental.pallas import tpu_sc as plsc

F32 = jnp.float32
BF16 = jnp.bfloat16
I32 = jnp.int32
U32 = jnp.uint32

D = 1024
EPS = 1e-6
HG_H = 4
HG_DK = 128
HG_W = HG_H * HG_DK
HG_C = 64
HG_LEVELS = 6
HG_SAFE_LOGDECAY = -80.0
HG_UNROLL = 4
AT_DILS = (1, 4, 16)
AT_HALF = 64
AT_H = 4
AT_HD = 64
AT_GW = AT_H * AT_HD
ROPE_DIM = 16
ROPE_THETA = 500000.0
N_EXP = 32
TOP_K = 4
SWIGLU_LIMIT = 7.0
SWIGLU_ALPHA = 1.702
MERGE_SUBTILES = 4
LANES = 128
NEG_BIG = -1e30
HALF_D = D // 2
SC_CORES = 2
SC_SUBCORES = 16
SC_WORKERS = SC_CORES * SC_SUBCORES
SC_CHUNK = 128

VMEM_LIMIT = 56 * 1024 * 1024


def _cp(sem, vmem=VMEM_LIMIT):
    return pltpu.CompilerParams(dimension_semantics=sem, vmem_limit_bytes=vmem)


def _dot(a, b):
    return jnp.dot(a, b, preferred_element_type=F32)


def _dot_nt(a, b):
    return lax.dot_general(a, b, (((1,), (1,)), ((), ())), preferred_element_type=F32)


def _dot_tn(a, b):
    return lax.dot_general(a, b, (((0,), (0,)), ((), ())), preferred_element_type=F32)


def _split3(x):
    hi = x.astype(BF16)
    r = x - hi.astype(F32)
    mid = r.astype(BF16)
    lo = (r - mid.astype(F32)).astype(BF16)
    return hi, mid, lo


def _rms(x, gain):
    ms = jnp.mean(x * x, axis=-1, keepdims=True)
    return x * lax.rsqrt(ms + EPS) * gain


def _const_spec(shape):
    n = len(shape)
    return pl.BlockSpec(shape, lambda *_: (0,) * n)


def _pack_rows(y):
    bits = lambda a: lax.bitcast_convert_type(a.astype(BF16).astype(F32), U32)
    return (bits(y[:, :HALF_D]) >> 16) | (bits(y[:, HALF_D:]) & jnp.uint32(0xFFFF0000))


def _unpack_rows(w):
    lo = lax.bitcast_convert_type(w << 16, F32)
    hi = lax.bitcast_convert_type(w & jnp.uint32(0xFFFF0000), F32)
    return lo, hi


def _deint_kernel(w_ref, p_ref, g_ref, l_ref):
    p = p_ref[...]
    n_grp = w_ref.shape[2] // (2 * LANES)
    for b in range(n_grp):
        blk = w_ref[0, :, b * 2 * LANES:(b + 1) * 2 * LANES].astype(BF16)
        r = _dot(blk, p)
        g_ref[0, :, b * LANES:(b + 1) * LANES] = r[:, :LANES].astype(BF16)
        l_ref[0, :, b * LANES:(b + 1) * LANES] = r[:, LANES:].astype(BF16)


def _deinterleave(w_up):
    n_e, d, n2 = w_up.shape
    perm = np.zeros((2 * LANES, 2 * LANES), np.float32)
    perm[2 * np.arange(LANES), np.arange(LANES)] = 1.0
    perm[2 * np.arange(LANES) + 1, LANES + np.arange(LANES)] = 1.0
    tr = d
    out = jax.ShapeDtypeStruct((n_e, d, n2 // 2), BF16)
    return pl.pallas_call(
        _deint_kernel,
        grid=(n_e, d // tr),
        in_specs=[pl.BlockSpec((1, tr, n2), lambda e, i: (e, i, 0)),
                  _const_spec(perm.shape)],
        out_specs=[pl.BlockSpec((1, tr, n2 // 2), lambda e, i: (e, i, 0))] * 2,
        out_shape=[out, out],
        compiler_params=_cp(("parallel", "parallel")),
        name="deinterleave",
    )(w_up, jnp.asarray(perm, BF16))


def _sc_mesh():
    return plsc.VectorSubcoreMesh(core_axis_name="c", subcore_axis_name="s")


def _sc_worker():
    return lax.axis_index("s") * SC_CORES + lax.axis_index("c")


def _sc_scatter_rows(src, dest_kt, n_rows):
    t, w = src.shape
    n_slot = dest_kt.shape[0]
    nch = t // (SC_WORKERS * SC_CHUNK)
    idx = dest_kt.reshape(n_slot, SC_WORKERS, nch, SC_CHUNK).transpose(1, 0, 2, 3)

    @functools.partial(
        pl.kernel, mesh=_sc_mesh(),
        out_type=jax.ShapeDtypeStruct((n_rows, w), src.dtype),
        scratch_types=[pltpu.VMEM((n_slot, nch, SC_CHUNK), I32),
                       pltpu.VMEM((SC_CHUNK, w), src.dtype)],
        name="sc_scatter")
    def run(src_hbm, idx_hbm, out_hbm, idx_v, rows_v):
        wid = _sc_worker()
        pltpu.sync_copy(idx_hbm.at[wid], idx_v)

        @pl.loop(0, nch)
        def _(j):
            pltpu.sync_copy(src_hbm.at[pl.ds((wid * nch + j) * SC_CHUNK, SC_CHUNK)], rows_v)
            for k in range(n_slot):
                pltpu.sync_copy(rows_v, out_hbm.at[idx_v.at[k, j]])

    return run(src, idx)


def _sc_gather_rows(table, idx):
    n = idx.shape[0]
    w = table.shape[1]
    nch = n // (SC_WORKERS * SC_CHUNK)
    idx3 = idx.reshape(SC_WORKERS, nch, SC_CHUNK)

    @functools.partial(
        pl.kernel, mesh=_sc_mesh(),
        out_type=jax.ShapeDtypeStruct((n, w), table.dtype),
        scratch_types=[pltpu.VMEM((nch, SC_CHUNK), I32),
                       pltpu.VMEM((SC_CHUNK, w), table.dtype)],
        name="sc_gather")
    def run(table_hbm, idx_hbm, out_hbm, idx_v, rows_v):
        wid = _sc_worker()
        pltpu.sync_copy(idx_hbm.at[wid], idx_v)

        @pl.loop(0, nch)
        def _(j):
            pltpu.sync_copy(table_hbm.at[idx_v.at[j]], rows_v)
            pltpu.sync_copy(rows_v, out_hbm.at[pl.ds((wid * nch + j) * SC_CHUNK, SC_CHUNK)])

    return run(table, idx3)


def _ada_kernel(c_ref, w_ref, b_ref, o_ref):
    c = c_ref[...]
    a = c * jax.nn.sigmoid(c)
    w = w_ref[...]
    a_hi = a.astype(BF16)
    a_lo = (a - a_hi.astype(F32)).astype(BF16)
    w_hi = w.astype(BF16)
    w_lo = (w - w_hi.astype(F32)).astype(BF16)
    o_ref[...] = _dot(a_hi, w_hi) + _dot(a_lo, w_hi) + _dot(a_hi, w_lo) + b_ref[...]


def _ada(c, w, b):
    nb = c.shape[0]
    n = w.shape[1]
    tn = 1536
    return pl.pallas_call(
        _ada_kernel,
        grid=(n // tn,),
        in_specs=[pl.BlockSpec((nb, D), lambda j: (0, 0)),
                  pl.BlockSpec((D, tn), lambda j: (0, j)),
                  pl.BlockSpec((1, tn), lambda j: (0, j))],
        out_specs=pl.BlockSpec((nb, tn), lambda j: (0, j)),
        out_shape=jax.ShapeDtypeStruct((nb, n), F32),
        compiler_params=_cp(("parallel",)),
        name="ada",
    )(c, w, b.reshape(1, n))


def _inproj_kernel(x_ref, mod_ref, gain_ref, cos_ref, s1_ref, s2_ref,
                   w_hga_ref, w_hgf_ref, w_at_ref, w_gt_ref,
                   hga_ref, hgf_ref, at0_ref, at1_ref, at2_ref, gt_ref, stage_ref):
    x = x_ref[...]
    tm = x.shape[0]
    sh = mod_ref[0, 0:1, :]
    sc = mod_ref[0, 1:2, :]
    h = _rms(x, gain_ref[...]) * (1.0 + sc) + sh
    hb = h.astype(BF16)
    hga_ref[...] = _dot(hb, w_hga_ref[...]).astype(BF16)
    hgf_ref[...] = _dot(hb, w_hgf_ref[...])
    cos = cos_ref[...]
    s1 = s1_ref[...]
    s2 = s2_ref[...]
    for g, o_ref in enumerate((at0_ref, at1_ref, at2_ref)):
        acc = _dot(hb, w_at_ref[:, g * 3 * AT_GW:(g + 1) * 3 * AT_GW])
        parts = []
        for j in range(2 * AT_GW // LANES):
            a = acc[:, j * LANES:(j + 1) * LANES]
            parts.append(a * cos + pltpu.roll(a, ROPE_DIM // 2, 1) * s1
                         + pltpu.roll(a, LANES - ROPE_DIM // 2, 1) * s2)
        parts.append(acc[:, 2 * AT_GW:])
        vals = jnp.concatenate(parts, axis=1)
        dil = AT_DILS[g]
        if dil == 1:
            o_ref[0, 0] = vals.astype(BF16)
        else:
            n_lt = 3 * AT_GW // LANES
            for c in range(n_lt):
                stage_ref[c] = vals[:, c * LANES:(c + 1) * LANES]
            for r in range(dil):
                o_ref[0, r] = jnp.concatenate(
                    [stage_ref[c, pl.ds(r, tm // dil, stride=dil), :] for c in range(n_lt)],
                    axis=1).astype(BF16)
    gt_ref[...] = _dot(hb, w_gt_ref[...]).astype(BF16)


def _inproj(x2, mod3, gain, rope, w_hga, w_hgf, w_at, w_gt, seq, tm):
    t = x2.shape[0]
    n_pos_blk = seq // tm
    row = lambda w: pl.BlockSpec((tm, w), lambda i: (i, 0))
    pos = pl.BlockSpec((tm, LANES), lambda i: (i % n_pos_blk, 0))
    batch = t // seq
    dilated = lambda d: pl.BlockSpec((1, d, tm // d, 3 * AT_GW),
                                     lambda i: ((i * tm) // seq, 0, i % n_pos_blk, 0))
    at_shape = lambda d: jax.ShapeDtypeStruct((batch, d, seq // d, 3 * AT_GW), BF16)
    return pl.pallas_call(
        _inproj_kernel,
        grid=(t // tm,),
        in_specs=[row(D),
                  pl.BlockSpec((1, 6, D), lambda i: ((i * tm) // seq, 0, 0)),
                  _const_spec((1, D)), pos, pos, pos,
                  _const_spec(w_hga.shape), _const_spec(w_hgf.shape),
                  _const_spec(w_at.shape), _const_spec(w_gt.shape)],
        out_specs=[row(3 * HG_W), row(2 * HG_W)] + [dilated(d) for d in AT_DILS] + [row(2 * D)],
        out_shape=[jax.ShapeDtypeStruct((t, 3 * HG_W), BF16),
                   jax.ShapeDtypeStruct((t, 2 * HG_W), F32)]
                  + [at_shape(d) for d in AT_DILS]
                  + [jax.ShapeDtypeStruct((t, 2 * D), BF16)],
        scratch_shapes=[pltpu.VMEM((3 * AT_GW // LANES, tm, LANES), F32)],
        compiler_params=_cp(("parallel",)),
        name="inproj",
    )(x2, mod3, gain, *rope, w_hga, w_hgf, w_at, w_gt)


def _rope_tables(seq):
    half = ROPE_DIM // 2
    inv_freq = ROPE_THETA ** (-np.arange(half, dtype=np.float32) / half)
    ang = jnp.arange(seq, dtype=F32)[:, None] * jnp.asarray(inv_freq)[None, :]
    cos, sin = jnp.cos(ang), jnp.sin(ang)
    ones = jnp.ones((seq, AT_HD - ROPE_DIM), F32)
    zeros = jnp.zeros((seq, AT_HD - ROPE_DIM), F32)
    zh = jnp.zeros((seq, half), F32)
    c_head = jnp.concatenate([cos, cos, ones], axis=1)
    s1_head = jnp.concatenate([zh, sin, zeros], axis=1)
    s2_head = jnp.concatenate([-sin, zh, zeros], axis=1)
    rep = LANES // AT_HD
    return tuple(jnp.tile(a, (1, rep)) for a in (c_head, s1_head, s2_head))


def _hgrn_consts(reverse):
    c = HG_C
    t = np.arange(c)[:, None]
    u = np.arange(c)[None, :]
    tri = (u >= t) if reverse else (u <= t)
    sels = []
    for lvl in range(HG_LEVELS):
        blk = 2 << lvl
        start = (np.arange(c) // blk) * blk
        piv = start + (1 << lvl) - (0 if reverse else 1)
        sels.append(np.arange(c)[None, :] == piv[:, None])
    sel = np.concatenate(sels, axis=0)
    rep = lambda m, n: jnp.asarray(np.concatenate([m.astype(np.float32)] * n, axis=1), BF16)
    return rep(tri, 2), rep(sel, 3)


def _hgrn_kernel(*refs, reverse, final, n_chunks):
    if final:
        (lbl_ref, tri_ref, sel_ref, q_ref, f_ref, v_ref, ofw_ref, gh_ref, gain_ref,
         o_ref, st_ref, st0_ref) = refs
    else:
        (lbl_ref, tri_ref, sel_ref, q_ref, f_ref, v_ref,
         o_ref, st_ref, st0_ref) = refs
    c = HG_C

    @pl.when(pl.program_id(1) == 0)
    def _():
        st_ref[...] = jnp.zeros_like(st_ref)

    l0 = lbl_ref[0:1, :]
    l1 = lbl_ref[1:2, :]
    lm = jnp.maximum(l0, l1)
    e0 = jnp.exp(l0 - lm)
    e1 = jnp.exp(l1 - lm)
    lb = e0 / (e0 + e1)
    tri = tri_ref[...]
    piv_row = c // 2 if reverse else c // 2 - 1

    def gate_math(r0):
        ff = f_ref[pl.ds(r0, c), :]
        e = jnp.exp(-jnp.abs(ff))
        r = 1.0 / (1.0 + e)
        sg = jnp.where(ff >= 0.0, r, e * r)
        g = jnp.log(lb + (1.0 - lb) * sg)
        kin = (1.0 - lb) * (1.0 - sg)
        g_hi = g.astype(BF16)
        g_lo = (g - g_hi.astype(F32)).astype(BF16)
        b = _dot(tri, jnp.concatenate([g_hi, g_lo], axis=0))
        btot = b[0:1, :] if reverse else b[c - 1:c, :]
        return kin, b, btot

    ti = lax.broadcasted_iota(I32, (c, c), 0)
    si = lax.broadcasted_iota(I32, (c, c), 1)
    causal = (si >= ti) if reverse else (si <= ti)
    row = lax.broadcasted_iota(I32, (c, 1), 0)

    def att_robust(q, kin, b, piv):
        att = jnp.where(ti == si, _dot_nt(q.astype(BF16), kin.astype(BF16)), 0.0)
        for lvl in range(HG_LEVELS):
            p = piv[lvl * c:(lvl + 1) * c, :]
            bit = ((row >> lvl) & 1) == 1
            q_side = jnp.logical_not(bit) if reverse else bit
            qe = jnp.where(q_side, q * jnp.exp(jnp.minimum(b - p, 0.0)), 0.0).astype(BF16)
            ke = jnp.where(q_side, 0.0, kin * jnp.exp(jnp.minimum(p - b, 0.0))).astype(BF16)
            same = (ti >> (lvl + 1)) == (si >> (lvl + 1))
            att = att + jnp.where(same, _dot_nt(qe, ke), 0.0)
        return att

    heads = [slice(h * HG_DK, (h + 1) * HG_DK) for h in range(HG_H)]


    def emit(r0, outs):
        o = jnp.concatenate(outs, axis=1)
        if final:
            o = o + ofw_ref[pl.ds(r0, c), :]
            normed = []
            for sl in heads:
                ms = jnp.mean(o[:, sl] * o[:, sl], axis=-1, keepdims=True)
                normed.append(o[:, sl] * lax.rsqrt(ms + EPS))
            gh = gh_ref[pl.ds(r0, c), :].astype(F32)
            o = jnp.concatenate(normed, axis=1) * gain_ref[...] * (gh * jax.nn.sigmoid(gh))
        o_ref[pl.ds(r0, c), :] = o.astype(o_ref.dtype)

    def robust_step(i, carry):
        ci = (n_chunks - 1 - i) if reverse else i
        r0 = pl.multiple_of(ci * c, c)
        q = q_ref[pl.ds(r0, c), :].astype(F32)
        v = v_ref[pl.ds(r0, c), :]
        kin, b, btot = gate_math(r0)
        qe_all = (q * jnp.exp(b)).astype(BF16)
        kd_all = (kin * jnp.exp(btot - b)).astype(BF16)
        piv = _dot(sel_ref[...], jnp.concatenate(_split3(b), axis=0))
        outs = []
        for h, sl in enumerate(heads):
            att = att_robust(q[:, sl], kin[:, sl], b[:, sl], piv[:, sl])
            st = st_ref[h]
            outs.append(_dot_nt(qe_all[:, sl], st.astype(BF16)) + _dot(att.astype(BF16), v[:, sl]))
            st_ref[h] = st * jnp.exp(btot[:, sl]) + _dot_tn(v[:, sl], kd_all[:, sl])
        emit(r0, outs)
        return carry

    grp = HG_UNROLL

    def fast_group(i, bmin):
        gi = (n_chunks // grp - 1 - i) if reverse else i
        base = gi * (grp * c)
        order = range(grp - 1, -1, -1) if reverse else range(grp)
        chunks = []
        for j in order:
            r0 = pl.multiple_of(base + j * c, c)
            kin, b, btot = gate_math(r0)
            bp = b[piv_row:piv_row + 1, :]
            bmin = jnp.minimum(bmin, jnp.minimum(bp, btot - bp))
            to_piv = jnp.exp(b - bp)
            k_piv = kin / to_piv
            chunks.append(dict(
                r0=r0, v=v_ref[pl.ds(r0, c), :], dec=jnp.exp(btot), pdec=jnp.exp(bp),
                qe=q_ref[pl.ds(r0, c), :] * to_piv.astype(BF16),
                ke=k_piv.astype(BF16),
                kd=(k_piv * jnp.exp(btot - bp)).astype(BF16)))
        for ch in chunks:
            ch["att"] = [jnp.where(causal, _dot_nt(ch["qe"][:, sl], ch["ke"][:, sl]), 0.0).astype(BF16)
                         for sl in heads]
        for ch in chunks:
            ch["intra"] = [_dot(ch["att"][h], ch["v"][:, sl]) for h, sl in enumerate(heads)]
            ch["upd"] = [_dot_tn(ch["v"][:, sl], ch["kd"][:, sl]) for sl in heads]
        st = [st_ref[h] for h in range(HG_H)]
        for ch in chunks:
            outs = []
            for h, sl in enumerate(heads):
                st_piv = (st[h] * ch["pdec"][:, sl]).astype(BF16)
                outs.append(_dot_nt(ch["qe"][:, sl], st_piv) + ch["intra"][h])
                st[h] = st[h] * ch["dec"][:, sl] + ch["upd"][h]
            emit(ch["r0"], outs)
        for h in range(HG_H):
            st_ref[h] = st[h]
        return bmin

    st0_ref[...] = st_ref[...]
    bmin = lax.fori_loop(0, n_chunks // grp, fast_group, jnp.zeros((1, HG_W), F32))

    @pl.when(jnp.min(bmin) < HG_SAFE_LOGDECAY)
    def _():
        st_ref[...] = st0_ref[...]
        lax.fori_loop(0, n_chunks, robust_step, 0)


def _hgrn(lb_logits_d, hga, hgf, batch, seq, tc, reverse, o_fw=None, gain=None):
    final = o_fw is not None
    t = hga.shape[0]
    nblk = seq // tc
    tri, sel = _hgrn_consts(reverse)

    def rows(col):
        if reverse:
            return lambda b, j: (b * nblk + nblk - 1 - j, col)
        return lambda b, j: (b * nblk + j, col)

    blk = lambda col: pl.BlockSpec((tc, HG_W), rows(col))
    in_specs = [_const_spec((2, HG_W)), _const_spec(tri.shape), _const_spec(sel.shape),
                blk(0), blk(1 if reverse else 0), blk(1)]
    args = [lb_logits_d, tri, sel, hga, hgf, hga]
    if final:
        in_specs += [blk(0), blk(2), _const_spec((1, HG_W))]
        args += [o_fw, hga, gain]
    return pl.pallas_call(
        functools.partial(_hgrn_kernel, reverse=reverse, final=final, n_chunks=tc // HG_C),
        grid=(batch, nblk),
        in_specs=in_specs,
        out_specs=blk(0),
        out_shape=jax.ShapeDtypeStruct((t, HG_W), BF16 if final else F32),
        scratch_shapes=[pltpu.VMEM((HG_H, HG_DK, HG_DK), F32),
                        pltpu.VMEM((HG_H, HG_DK, HG_DK), F32)],
        compiler_params=_cp(("parallel", "arbitrary")),
        name="hgrn_bwd" if reverse else "hgrn_fwd",
    )(*args)


def _attn_kernel(q_ref, kl_ref, km_ref, kr_ref, vl_ref, vm_ref, vr_ref, o_ref, l_ref,
                 k_s, v_s, *, tq, ld):
    hw = AT_HALF
    n_res = q_ref.shape[0]
    for res in range(n_res):
        k_s[res, 0:hw, :] = kl_ref[res]
        k_s[res, hw:hw + tq, :] = km_ref[res]
        k_s[res, hw + tq:, :] = kr_ref[res]
        v_s[res, 0:hw, :] = vl_ref[res]
        v_s[res, hw:hw + tq, :] = vm_ref[res]
        v_s[res, hw + tq:, :] = vr_ref[res]
    qb = 2 * hw
    kb = 4 * hw
    base = pl.program_id(2) * tq
    lane_head = lax.broadcasted_iota(I32, (1, AT_GW), 1) // AT_HD
    q_sel = [jnp.where(lane_head == h, AT_HD ** -0.5, 0.0).astype(BF16) for h in range(AT_H)]
    out_head = lax.broadcasted_iota(I32, (qb, AT_GW), 1) // AT_HD
    rel = (lax.broadcasted_iota(I32, (AT_H * qb, kb), 0) % qb
           - lax.broadcasted_iota(I32, (AT_H * qb, kb), 1) + hw)
    band = jnp.where(jnp.abs(rel) <= hw, 0.0, NEG_BIG)
    key_col = lax.broadcasted_iota(I32, (1, kb), 1)
    for res, qs in [(res, qs) for res in range(n_res) for qs in range(0, tq, qb)]:
        q = q_ref[res, qs:qs + qb, :]
        kk = k_s[res, qs:qs + kb, :]
        vv = v_s[res, qs:qs + kb, :]
        q4 = jnp.concatenate([q * q_sel[h] for h in range(AT_H)], axis=0)
        kpos = base + qs - hw + key_col
        edge = jnp.where((kpos >= 0) & (kpos < ld), 0.0, NEG_BIG)
        s = _dot_nt(q4, kk) + band + edge
        m = jnp.max(s, axis=-1, keepdims=True)
        p = jnp.exp(s - m)
        l = jnp.sum(p, axis=-1, keepdims=True)
        pv = _dot(p.astype(BF16), vv) * (1.0 / l)
        lse = m + jnp.log(l)
        o_all = jnp.zeros((qb, AT_GW), F32)
        l_all = jnp.zeros((qb, AT_GW), F32)
        for h in range(AT_H):
            o_all = jnp.where(out_head == h, pv[h * qb:(h + 1) * qb, :], o_all)
            l_all = jnp.where(out_head == h, lse[h * qb:(h + 1) * qb, :], l_all)
        o_ref[res, qs:qs + qb, :] = o_all.astype(o_ref.dtype)
        l_ref[res, qs:qs + qb, :] = l_all


def _attn_group(qkv):
    batch, dil, ld, _ = qkv.shape
    hw = AT_HALF
    tq = min(ld, 512)
    nq = ld // tq
    nh = ld // hw
    per = tq // hw
    n_res = min(dil, max(1, 512 // tq))
    main = lambda part: pl.BlockSpec((None, n_res, tq, AT_GW), lambda b, r, j: (b, r, j, part))
    left = lambda part: pl.BlockSpec(
        (None, n_res, hw, AT_GW), lambda b, r, j: (b, r, jnp.maximum(j * per - 1, 0), part))
    right = lambda part: pl.BlockSpec(
        (None, n_res, hw, AT_GW), lambda b, r, j: (b, r, jnp.minimum((j + 1) * per, nh - 1), part))
    out_spec = pl.BlockSpec((None, n_res, tq, AT_GW), lambda b, r, j: (b, r, j, 0))
    return pl.pallas_call(
        functools.partial(_attn_kernel, tq=tq, ld=ld),
        grid=(batch, dil // n_res, nq),
        in_specs=[main(0), left(1), main(1), right(1), left(2), main(2), right(2)],
        out_specs=[out_spec, out_spec],
        out_shape=[jax.ShapeDtypeStruct((batch, dil, ld, AT_GW), BF16),
                   jax.ShapeDtypeStruct((batch, dil, ld, AT_GW), F32)],
        scratch_shapes=[pltpu.VMEM((n_res, tq + 2 * hw, AT_GW), BF16),
                        pltpu.VMEM((n_res, tq + 2 * hw, AT_GW), BF16)],
        compiler_params=_cp(("parallel", "parallel", "parallel")),
        name=f"attn_d{dil}",
    )(qkv, qkv, qkv, qkv, qkv, qkv, qkv)


def _merge_kernel(ohg_ref, o0_ref, o1_ref, o2_ref, l0_ref, l1_ref, l2_ref, gt_ref, x_ref, mod_ref,
                  w_hg_ref, w_at_ref, w_out_ref, npost_ref, npre_ref, w_r_ref, b_r_ref,
                  x1_ref, h2_ref, ids_ref, wts_ref, cnt_ref,
                  so1_ref, sl1_ref, so2_ref, sl2_ref):
    tm = x_ref.shape[0]
    sub = tm // MERGE_SUBTILES
    rows = [slice(i * sub, (i + 1) * sub) for i in range(MERGE_SUBTILES)]
    gt1 = mod_ref[0, 2:3, :]
    sh2 = mod_ref[0, 3:4, :]
    sc2 = mod_ref[0, 4:5, :]

    for src_ref, dst_ref in ((o1_ref, so1_ref), (l1_ref, sl1_ref), (o2_ref, so2_ref), (l2_ref, sl2_ref)):
        dil = src_ref.shape[0]
        for res in range(dil):
            vals = src_ref[res].astype(F32)
            for c in range(AT_GW // LANES):
                dst_ref[c, pl.ds(res, tm // dil, stride=dil), :] = vals[:, c * LANES:(c + 1) * LANES]

    def staged(ref, r):
        return jnp.concatenate([ref[c, r, :] for c in range(AT_GW // LANES)], axis=1)

    def branches(r):
        o0, l0 = o0_ref[0, r, :].astype(F32), l0_ref[0, r, :]
        o1, l1 = staged(so1_ref, r), staged(sl1_ref, r)
        o2, l2 = staged(so2_ref, r), staged(sl2_ref, r)
        m = jnp.maximum(jnp.maximum(l0, l1), l2)
        e0, e1, e2 = jnp.exp(l0 - m), jnp.exp(l1 - m), jnp.exp(l2 - m)
        oa = (e0 * o0 + e1 * o1 + e2 * o2) / (e0 + e1 + e2)
        return _dot(ohg_ref[r, :], w_hg_ref[...]), _dot(oa.astype(BF16), w_at_ref[...])

    def out_proj(r, b_hg, b_at):
        g_hg = jax.nn.sigmoid(gt_ref[r, 0:D])
        g_at = jax.nn.sigmoid(gt_ref[r, D:2 * D])
        merged = g_hg * b_hg.astype(BF16) + g_at * b_at.astype(BF16)
        return _dot(merged, w_out_ref[...])

    def norms(r, y):
        x1 = x_ref[r, :] + gt1 * _rms(y, npost_ref[...])
        x1_ref[r, :] = x1
        h2 = _rms(x1, npre_ref[...]) * (1.0 + sc2) + sh2
        h2_ref[r, :] = _pack_rows(h2)
        return _dot(h2.astype(BF16), w_r_ref[...]) + b_r_ref[...]

    def top_k(r, logits):
        lane = lax.broadcasted_iota(I32, logits.shape, 1)
        lane_f = lane.astype(F32)
        work = logits
        vals, idxs = [], []
        sel = jnp.zeros(logits.shape, F32)
        for _ in range(TOP_K):
            mk = jnp.max(work, axis=-1, keepdims=True)
            ik = jnp.min(jnp.where(work == mk, lane_f, float(LANES)), axis=-1, keepdims=True)
            hit = lane_f == ik
            sel = jnp.where(hit, 1.0, sel)
            work = jnp.where(hit, -jnp.inf, work)
            vals.append(mk)
            idxs.append(ik)
        es = [jnp.exp(v - vals[0]) for v in vals]
        den = es[0] + es[1] + es[2] + es[3]
        ids = jnp.zeros(logits.shape, F32)
        wts = jnp.zeros(logits.shape, F32)
        for k in range(TOP_K):
            ids = jnp.where(lane == k, idxs[k], ids)
            wts = jnp.where(lane == k, es[k] / den, wts)
        ids_ref[r, :] = ids.astype(I32)
        wts_ref[r, :] = wts
        return jnp.sum(sel, axis=0, keepdims=True)

    br = [branches(r) for r in rows]
    ys = [out_proj(r, *b) for r, b in zip(rows, br)]
    logits = [norms(r, y) for r, y in zip(rows, ys)]
    counts = [top_k(r, lg) for r, lg in zip(rows, logits)]
    cnt_ref[0] = functools.reduce(lambda a, b: a + b, counts)


def _merge(ohg, att_o, att_l, gates, x2, mod3, w_hg, w_at, w_out, npost, npre, w_r, b_r, seq, tm):
    t = x2.shape[0]
    nt = t // tm
    row = lambda w: pl.BlockSpec((tm, w), lambda i: (i, 0))
    slab_i = jax.ShapeDtypeStruct((t, LANES), I32)
    slab_f = jax.ShapeDtypeStruct((t, LANES), F32)
    nblk = seq // tm
    grouped = [pl.BlockSpec((None, d, tm // d, AT_GW),
                            lambda i: ((i * tm) // seq, 0, i % nblk, 0)) for d in AT_DILS]
    return pl.pallas_call(
        _merge_kernel,
        grid=(nt,),
        in_specs=[row(HG_W)] + grouped + grouped + [row(2 * D), row(D),
                  pl.BlockSpec((1, 6, D), lambda i: ((i * tm) // seq, 0, 0)),
                  _const_spec(w_hg.shape), _const_spec(w_at.shape), _const_spec(w_out.shape),
                  _const_spec((1, D)), _const_spec((1, D)),
                  _const_spec(w_r.shape), _const_spec(b_r.shape)],
        out_specs=[row(D), row(HALF_D), row(LANES), row(LANES),
                   pl.BlockSpec((1, 1, LANES), lambda i: (i, 0, 0))],
        out_shape=[jax.ShapeDtypeStruct((t, D), F32), jax.ShapeDtypeStruct((t, HALF_D), U32),
                   slab_i, slab_f, jax.ShapeDtypeStruct((nt, 1, LANES), F32)],
        scratch_shapes=[pltpu.VMEM((AT_GW // LANES, tm, LANES), F32)] * 4,
        compiler_params=_cp(("parallel",)),
        name="merge",
    )(ohg, *att_o, *att_l, gates, x2, mod3, w_hg, w_at, w_out, npost, npre, w_r, b_r)


def _route_kernel(ids_ref, base_ref, tril_ref, dest_ref):
    ids = ids_ref[...]
    lane = lax.broadcasted_iota(I32, ids.shape, 1)
    hits = [lane == ids[:, k:k + 1] for k in range(TOP_K)]
    sel = jnp.zeros(ids.shape, F32)
    for hit in hits:
        sel = jnp.where(hit, 1.0, sel)
    before = _dot(tril_ref[...], sel.astype(BF16)) + base_ref[0]
    dest = jnp.zeros(ids.shape, I32)
    for k, hit in enumerate(hits):
        rk = jnp.sum(jnp.where(hit, before, 0.0), axis=-1, keepdims=True)
        dest = jnp.where(lane == k, rk.astype(I32), dest)
    dest_ref[...] = dest


def _route(ids, tile_base, tm):
    t = ids.shape[0]
    tril = jnp.asarray(np.tril(np.ones((tm, tm), np.float32), -1), BF16)
    return pl.pallas_call(
        _route_kernel,
        grid=(t // tm,),
        in_specs=[pl.BlockSpec((tm, LANES), lambda i: (i, 0)),
                  pl.BlockSpec((1, 1, LANES), lambda i: (i, 0, 0)),
                  _const_spec((tm, tm))],
        out_specs=pl.BlockSpec((tm, LANES), lambda i: (i, 0)),
        out_shape=jax.ShapeDtypeStruct((t, LANES), I32),
        compiler_params=_cp(("parallel",)),
        name="route",
    )(ids, tile_base, tril)


def _moe_kernel(be_ref, nu_ref, x_ref, wg_ref, wl_ref, bg_ref, bl_ref, wd_ref, bd_ref, o_ref):
    i = pl.program_id(0)

    @pl.when(i < nu_ref[0])
    def _():
        x_lo, x_hi = (a.astype(BF16) for a in _unpack_rows(x_ref[...]))
        gate = (_dot(x_lo, wg_ref[0, :HALF_D, :]) + _dot(x_hi, wg_ref[0, HALF_D:, :])
                + bg_ref[0])
        up = (_dot(x_lo, wl_ref[0, :HALF_D, :]) + _dot(x_hi, wl_ref[0, HALF_D:, :])
              + bl_ref[0])
        gate = jnp.minimum(gate, SWIGLU_LIMIT)
        up = jnp.clip(up, -SWIGLU_LIMIT, SWIGLU_LIMIT)
        act = (up + 1.0) * gate * jax.nn.sigmoid(SWIGLU_ALPHA * gate)
        y = _dot(act.astype(BF16), wd_ref[0]) + bd_ref[0]
        o_ref[...] = _pack_rows(y)

    @pl.when(i >= nu_ref[0])
    def _():
        o_ref[...] = jnp.zeros_like(o_ref)


def _moe(block_expert, n_used, xb, wg, wl, bg, bl, wd, bd, bm):
    r = xb.shape[0]
    nblk = r // bm
    ew = lambda shape: pl.BlockSpec((1,) + shape, lambda i, be, nu: (be[i], 0, 0))
    return pl.pallas_call(
        _moe_kernel,
        grid_spec=pltpu.PrefetchScalarGridSpec(
            num_scalar_prefetch=2,
            grid=(nblk,),
            in_specs=[pl.BlockSpec((bm, HALF_D), lambda i, be, nu: (i, 0)),
                      ew((D, D)), ew((D, D)), ew((1, D)), ew((1, D)), ew((D, D)), ew((1, D))],
            out_specs=pl.BlockSpec((bm, HALF_D), lambda i, be, nu: (i, 0)),
        ),
        out_shape=jax.ShapeDtypeStruct((r, HALF_D), U32),
        compiler_params=_cp(("arbitrary",)),
        name="moe",
    )(block_expert, n_used, xb, wg, wl, bg, bl, wd, bd)


def _final_kernel(y0_ref, y1_ref, y2_ref, y3_ref, wts_ref, x1_ref, mod_ref, npost_ref, o_ref):
    wts = wts_ref[...]
    y = jnp.zeros(x1_ref.shape, F32)
    for k, y_ref in enumerate((y0_ref, y1_ref, y2_ref, y3_ref)):
        y = y + wts[:, k:k + 1] * jnp.concatenate(_unpack_rows(y_ref[0]), axis=1)
    gt2 = mod_ref[0, 5:6, :]
    o_ref[...] = x1_ref[...] + gt2 * _rms(y, npost_ref[...])


def _final(yg, wts, x1, mod3, npost, seq, tm):
    t = x1.shape[0]
    slot = lambda k: pl.BlockSpec((1, tm, HALF_D), lambda i: (k, i, 0))
    return pl.pallas_call(
        _final_kernel,
        grid=(t // tm,),
        in_specs=[slot(0), slot(1), slot(2), slot(3),
                  pl.BlockSpec((tm, LANES), lambda i: (i, 0)),
                  pl.BlockSpec((tm, D), lambda i: (i, 0)),
                  pl.BlockSpec((1, 6, D), lambda i: ((i * tm) // seq, 0, 0)),
                  _const_spec((1, D))],
        out_specs=pl.BlockSpec((tm, D), lambda i: (i, 0)),
        out_shape=jax.ShapeDtypeStruct((t, D), F32),
        compiler_params=_cp(("parallel",)),
        name="final",
    )(yg, yg, yg, yg, wts, x1, mod3, npost)


def _tiles(seq):
    return dict(tm_in=min(seq, 512), tc=min(seq, 512), tm_merge=min(seq, 512),
                bm=512, tm_final=min(seq, 512))


def _prep_weights(w_in, w_hg_out, w_att_out, w_out, w_router, b_router, w_up, b_up, w_down, b_down):
    w = w_in[0]
    hw = HG_W
    a0 = 5 * hw
    aw = AT_GW * len(AT_DILS)
    q_at, k_at, v_at = (w[:, a0 + i * aw:a0 + (i + 1) * aw] for i in range(3))
    grp = lambda m, g: m[:, g * AT_GW:(g + 1) * AT_GW]
    w_at = jnp.concatenate(
        [jnp.concatenate([grp(q_at, g), grp(k_at, g), grp(v_at, g)], axis=1)
         for g in range(len(AT_DILS))], axis=1)
    pad = LANES - N_EXP
    wg, wl = _deinterleave(w_up[0])
    return dict(
        wg=wg, wl=wl,
        w_hga=jnp.concatenate([w[:, 0:hw], w[:, 3 * hw:4 * hw], w[:, 4 * hw:5 * hw]], 1).astype(BF16),
        w_hgf=w[:, hw:3 * hw].astype(BF16),
        w_at=w_at.astype(BF16),
        w_gt=w[:, a0 + 3 * aw:].astype(BF16),
        w_hg_out=w_hg_out[0].astype(BF16),
        w_att_out=w_att_out[0].astype(BF16),
        w_out=w_out[0].astype(BF16),
        w_r=jnp.pad(w_router[0], ((0, 0), (0, pad))).astype(BF16),
        b_r=jnp.pad(b_router[0], (0, pad), constant_values=NEG_BIG).reshape(1, LANES),
        bg=b_up[0][:, 0::2].reshape(N_EXP, 1, D),
        bl=b_up[0][:, 1::2].reshape(N_EXP, 1, D),
        wd=w_down[0].astype(BF16),
        bd=b_down[0].reshape(N_EXP, 1, D),
    )


def _trunk(x, mod, wts, norm_pre, norm_post, lb_logits, hg_gain):
    batch, seq, _ = x.shape
    t = batch * seq
    tl = _tiles(seq)
    x2 = x.reshape(t, D)
    mod3 = mod.reshape(batch, 6, D)
    rope = _rope_tables(seq)

    hga, hgf, at0, at1, at2, gates = _inproj(
        x2, mod3, norm_pre[0, 0].reshape(1, D), rope,
        wts["w_hga"], wts["w_hgf"], wts["w_at"], wts["w_gt"], seq, tl["tm_in"])

    o_fw = _hgrn(lb_logits[0], hga, hgf, batch, seq, tl["tc"], reverse=False)
    ohg = _hgrn(lb_logits[1], hga, hgf, batch, seq, tl["tc"], reverse=True,
                o_fw=o_fw, gain=hg_gain[0].reshape(1, HG_W))

    att = [_attn_group(a) for a in (at0, at1, at2)]

    x1, h2, ids, rw, cnt = _merge(
        ohg, [a[0] for a in att], [a[1] for a in att], gates, x2, mod3,
        wts["w_hg_out"], wts["w_att_out"], wts["w_out"],
        norm_post[0, 0].reshape(1, D), norm_pre[0, 1].reshape(1, D),
        wts["w_r"], wts["b_r"], seq, tl["tm_merge"])

    bm = tl["bm"]
    tmr = tl["tm_merge"]
    cnt_tiles = cnt.reshape(t // tmr, LANES).astype(I32)
    total = jnp.sum(cnt_tiles, axis=0)
    padded = (total + bm - 1) // bm * bm
    pad_end = jnp.cumsum(padded)
    pad_start = pad_end - padded
    tile_base = pad_start[None, :] + jnp.cumsum(cnt_tiles, axis=0) - cnt_tiles
    dest = _route(ids, tile_base.astype(F32).reshape(-1, 1, LANES), tmr)[:, :TOP_K]

    n_rows = t * TOP_K + N_EXP * bm
    n_blocks = n_rows // bm
    blk_start = jnp.arange(n_blocks, dtype=I32) * bm
    block_expert = jnp.minimum(
        jnp.sum(pad_end[None, :N_EXP] <= blk_start[:, None], axis=1), N_EXP - 1).astype(I32)
    n_used = (pad_end[N_EXP - 1] // bm).astype(I32).reshape(1)

    dest_kt = dest.T
    xb = _sc_scatter_rows(h2, dest_kt, n_rows)
    yb = _moe(block_expert, n_used, xb, wts["wg"], wts["wl"], wts["bg"], wts["bl"],
              wts["wd"], wts["bd"], bm)
    yg = _sc_gather_rows(yb, dest_kt.reshape(-1)).reshape(TOP_K, t, HALF_D)

    out = _final(yg, rw, x1, mod3, norm_post[0, 1].reshape(1, D), seq, tl["tm_final"])
    return out.reshape(batch, seq, D)


def kernel(x_prompt, x_sample, c_prompt, c_sample, w_ada, b_ada, norm_pre, norm_post, w_in,
           lb_logits, hg_norm_gain, w_hg_out, w_att_out, w_out, w_router, b_router,
           w_up, b_up, w_down, b_down):
    wts = _prep_weights(w_in, w_hg_out, w_att_out, w_out, w_router, b_router,
                        w_up, b_up, w_down, b_down)
    nb = c_prompt.shape[0]
    mod = _ada(jnp.concatenate([c_prompt, c_sample], axis=0), w_ada[0], b_ada[0])
    lb_l = lb_logits.astype(F32)
    y_p = _trunk(x_prompt, mod[:nb], wts, norm_pre, norm_post, lb_l, hg_norm_gain)
    y_s = _trunk(x_sample, mod[nb:], wts, norm_pre, norm_post, lb_l, hg_norm_gain)
    return (y_p, y_s)
```

```python
import functools

import numpy as np
import jax
import jax.numpy as jnp
from jax import lax
from jax.experimental import pallas as pl
from jax.experimental.pallas import tpu as pltpu
from jax.experimental.pallas import tpu_sc as plsc

F32 = jnp.float32
BF16 = jnp.bfloat16
I32 = jnp.int32
U32 = jnp.uint32

D = 1024
EPS = 1e-6
HG_H = 4
HG_DK = 128
HG_W = HG_H * HG_DK
HG_C = 64
HG_LEVELS = 6
HG_SAFE_LOGDECAY = -80.0
HG_UNROLL = 4
AT_DILS = (1, 4, 16)
AT_HALF = 64
AT_H = 4
AT_HD = 64
AT_GW = AT_H * AT_HD
ROPE_DIM = 16
ROPE_THETA = 500000.0
N_EXP = 32
TOP_K = 4
SWIGLU_LIMIT = 7.0
SWIGLU_ALPHA = 1.702
MERGE_SUBTILES = 4
ROW_TILE = 512
STREAM_TILE = 1024
ADA_COL_TILE = 1536
MOE_BLOCK = 1024
LANES = 128
NEG_BIG = -1e30
HALF_D = D // 2
SC_CORES = 2
SC_SUBCORES = 16
SC_WORKERS = SC_CORES * SC_SUBCORES
SC_CHUNK = 128

VMEM_LIMIT = 56 * 1024 * 1024


def _cp(sem, vmem=VMEM_LIMIT):
    return pltpu.CompilerParams(dimension_semantics=sem, vmem_limit_bytes=vmem)


def _dot(a, b):
    return jnp.dot(a, b, preferred_element_type=F32)


def _dot_nt(a, b):
    return lax.dot_general(a, b, (((1,), (1,)), ((), ())), preferred_element_type=F32)


def _dot_tn(a, b):
    return lax.dot_general(a, b, (((0,), (0,)), ((), ())), preferred_element_type=F32)


def _split3(x):
    hi = x.astype(BF16)
    r = x - hi.astype(F32)
    mid = r.astype(BF16)
    lo = (r - mid.astype(F32)).astype(BF16)
    return hi, mid, lo


def _rms(x, gain):
    ms = jnp.mean(x * x, axis=-1, keepdims=True)
    return x * lax.rsqrt(ms + EPS) * gain


def _const_spec(shape):
    n = len(shape)
    return pl.BlockSpec(shape, lambda *_: (0,) * n)


def _pack_rows(y):
    bits = lambda a: lax.bitcast_convert_type(a.astype(BF16).astype(F32), U32)
    return (bits(y[:, :HALF_D]) >> 16) | (bits(y[:, HALF_D:]) & jnp.uint32(0xFFFF0000))


def _unpack_rows(w):
    lo = lax.bitcast_convert_type(w << 16, F32)
    hi = lax.bitcast_convert_type(w & jnp.uint32(0xFFFF0000), F32)
    return lo, hi


def _deint_kernel(w_ref, p_ref, g_ref, l_ref):
    p = p_ref[...]
    n_grp = w_ref.shape[2] // (2 * LANES)
    for b in range(n_grp):
        blk = w_ref[0, :, b * 2 * LANES:(b + 1) * 2 * LANES].astype(BF16)
        r = _dot(blk, p)
        g_ref[0, :, b * LANES:(b + 1) * LANES] = r[:, :LANES].astype(BF16)
        l_ref[0, :, b * LANES:(b + 1) * LANES] = r[:, LANES:].astype(BF16)


def _deinterleave(w_up):
    n_e, d, n2 = w_up.shape
    perm = np.zeros((2 * LANES, 2 * LANES), np.float32)
    perm[2 * np.arange(LANES), np.arange(LANES)] = 1.0
    perm[2 * np.arange(LANES) + 1, LANES + np.arange(LANES)] = 1.0
    tr = d
    out = jax.ShapeDtypeStruct((n_e, d, n2 // 2), BF16)
    return pl.pallas_call(
        _deint_kernel,
        grid=(n_e, d // tr),
        in_specs=[pl.BlockSpec((1, tr, n2), lambda e, i: (e, i, 0)),
                  _const_spec(perm.shape)],
        out_specs=[pl.BlockSpec((1, tr, n2 // 2), lambda e, i: (e, i, 0))] * 2,
        out_shape=[out, out],
        compiler_params=_cp(("parallel", "parallel")),
        name="deinterleave",
    )(w_up, jnp.asarray(perm, BF16))


def _sc_mesh():
    return plsc.VectorSubcoreMesh(core_axis_name="c", subcore_axis_name="s")


def _sc_worker():
    return lax.axis_index("s") * SC_CORES + lax.axis_index("c")


def _sc_scatter_rows(src, dest_kt, n_rows):
    t, w = src.shape
    n_slot = dest_kt.shape[0]
    assert t % (SC_WORKERS * SC_CHUNK) == 0
    nch = t // (SC_WORKERS * SC_CHUNK)
    idx = dest_kt.reshape(n_slot, SC_WORKERS, nch, SC_CHUNK).transpose(1, 0, 2, 3)

    @functools.partial(
        pl.kernel, mesh=_sc_mesh(),
        out_type=jax.ShapeDtypeStruct((n_rows, w), src.dtype),
        scratch_types=[pltpu.VMEM((n_slot, nch, SC_CHUNK), I32),
                       pltpu.VMEM((SC_CHUNK, w), src.dtype)],
        name="sc_scatter")
    def run(src_hbm, idx_hbm, out_hbm, idx_v, rows_v):
        wid = _sc_worker()
        pltpu.sync_copy(idx_hbm.at[wid], idx_v)

        @pl.loop(0, nch)
        def _(j):
            pltpu.sync_copy(src_hbm.at[pl.ds((wid * nch + j) * SC_CHUNK, SC_CHUNK)], rows_v)
            for k in range(n_slot):
                pltpu.sync_copy(rows_v, out_hbm.at[idx_v.at[k, j]])

    return run(src, idx)


def _sc_gather_rows(table, idx):
    n = idx.shape[0]
    w = table.shape[1]
    assert n % (SC_WORKERS * SC_CHUNK) == 0
    nch = n // (SC_WORKERS * SC_CHUNK)
    idx3 = idx.reshape(SC_WORKERS, nch, SC_CHUNK)

    @functools.partial(
        pl.kernel, mesh=_sc_mesh(),
        out_type=jax.ShapeDtypeStruct((n, w), table.dtype),
        scratch_types=[pltpu.VMEM((nch, SC_CHUNK), I32),
                       pltpu.VMEM((SC_CHUNK, w), table.dtype)],
        name="sc_gather")
    def run(table_hbm, idx_hbm, out_hbm, idx_v, rows_v):
        wid = _sc_worker()
        pltpu.sync_copy(idx_hbm.at[wid], idx_v)

        @pl.loop(0, nch)
        def _(j):
            pltpu.sync_copy(table_hbm.at[idx_v.at[j]], rows_v)
            pltpu.sync_copy(rows_v, out_hbm.at[pl.ds((wid * nch + j) * SC_CHUNK, SC_CHUNK)])

    return run(table, idx3)


def _ada_kernel(c_ref, w_ref, b_ref, o_ref):
    c = c_ref[...]
    a = c * jax.nn.sigmoid(c)
    w = w_ref[...]
    a_hi = a.astype(BF16)
    a_lo = (a - a_hi.astype(F32)).astype(BF16)
    w_hi = w.astype(BF16)
    w_lo = (w - w_hi.astype(F32)).astype(BF16)
    o_ref[...] = _dot(a_hi, w_hi) + _dot(a_lo, w_hi) + _dot(a_hi, w_lo) + b_ref[...]


def _ada(c, w, b):
    nb = c.shape[0]
    n = w.shape[1]
    tn = ADA_COL_TILE
    return pl.pallas_call(
        _ada_kernel,
        grid=(n // tn,),
        in_specs=[pl.BlockSpec((nb, D), lambda j: (0, 0)),
                  pl.BlockSpec((D, tn), lambda j: (0, j)),
                  pl.BlockSpec((1, tn), lambda j: (0, j))],
        out_specs=pl.BlockSpec((nb, tn), lambda j: (0, j)),
        out_shape=jax.ShapeDtypeStruct((nb, n), F32),
        compiler_params=_cp(("parallel",)),
        name="ada",
    )(c, w, b.reshape(1, n))


def _inproj_kernel(x_ref, mod_ref, gain_ref, cos_ref, s1_ref, s2_ref,
                   w_hga_ref, w_hgf_ref, w_at_ref, w_gt_ref,
                   hga_ref, hgf_ref, at0_ref, at1_ref, at2_ref, gt_ref, stage_ref):
    x = x_ref[...]
    tm = x.shape[0]
    sh = mod_ref[0, 0:1, :]
    sc = mod_ref[0, 1:2, :]
    h = _rms(x, gain_ref[...]) * (1.0 + sc) + sh
    hb = h.astype(BF16)
    hga_ref[...] = _dot(hb, w_hga_ref[...]).astype(BF16)
    hgf_ref[...] = _dot(hb, w_hgf_ref[...])
    cos = cos_ref[...]
    s1 = s1_ref[...]
    s2 = s2_ref[...]
    for g, o_ref in enumerate((at0_ref, at1_ref, at2_ref)):
        acc = _dot(hb, w_at_ref[:, g * 3 * AT_GW:(g + 1) * 3 * AT_GW])
        parts = []
        for j in range(2 * AT_GW // LANES):
            a = acc[:, j * LANES:(j + 1) * LANES]
            parts.append(a * cos + pltpu.roll(a, ROPE_DIM // 2, 1) * s1
                         + pltpu.roll(a, LANES - ROPE_DIM // 2, 1) * s2)
        parts.append(acc[:, 2 * AT_GW:])
        vals = jnp.concatenate(parts, axis=1)
        dil = AT_DILS[g]
        if dil == 1:
            o_ref[0, 0] = vals.astype(BF16)
        else:
            n_lt = 3 * AT_GW // LANES
            for c in range(n_lt):
                stage_ref[c] = vals[:, c * LANES:(c + 1) * LANES]
            for r in range(dil):
                o_ref[0, r] = jnp.concatenate(
                    [stage_ref[c, pl.ds(r, tm // dil, stride=dil), :] for c in range(n_lt)],
                    axis=1).astype(BF16)
    gt_ref[...] = _dot(hb, w_gt_ref[...]).astype(BF16)


def _inproj(x2, mod3, gain, rope, w_hga, w_hgf, w_at, w_gt, seq, tm):
    t = x2.shape[0]
    n_pos_blk = seq // tm
    row = lambda w: pl.BlockSpec((tm, w), lambda i: (i, 0))
    pos = pl.BlockSpec((tm, LANES), lambda i: (i % n_pos_blk, 0))
    batch = t // seq
    dilated = lambda d: pl.BlockSpec((1, d, tm // d, 3 * AT_GW),
                                     lambda i: ((i * tm) // seq, 0, i % n_pos_blk, 0))
    at_shape = lambda d: jax.ShapeDtypeStruct((batch, d, seq // d, 3 * AT_GW), BF16)
    return pl.pallas_call(
        _inproj_kernel,
        grid=(t // tm,),
        in_specs=[row(D),
                  pl.BlockSpec((1, 6, D), lambda i: ((i * tm) // seq, 0, 0)),
                  _const_spec((1, D)), pos, pos, pos,
                  _const_spec(w_hga.shape), _const_spec(w_hgf.shape),
                  _const_spec(w_at.shape), _const_spec(w_gt.shape)],
        out_specs=[row(3 * HG_W), row(2 * HG_W)] + [dilated(d) for d in AT_DILS] + [row(2 * D)],
        out_shape=[jax.ShapeDtypeStruct((t, 3 * HG_W), BF16),
                   jax.ShapeDtypeStruct((t, 2 * HG_W), F32)]
                  + [at_shape(d) for d in AT_DILS]
                  + [jax.ShapeDtypeStruct((t, 2 * D), BF16)],
        scratch_shapes=[pltpu.VMEM((3 * AT_GW // LANES, tm, LANES), F32)],
        compiler_params=_cp(("parallel",)),
        name="inproj",
    )(x2, mod3, gain, *rope, w_hga, w_hgf, w_at, w_gt)


def _rope_tables(seq):
    half = ROPE_DIM // 2
    inv_freq = ROPE_THETA ** (-np.arange(half, dtype=np.float32) / half)
    ang = jnp.arange(seq, dtype=F32)[:, None] * jnp.asarray(inv_freq)[None, :]
    cos, sin = jnp.cos(ang), jnp.sin(ang)
    ones = jnp.ones((seq, AT_HD - ROPE_DIM), F32)
    zeros = jnp.zeros((seq, AT_HD - ROPE_DIM), F32)
    zh = jnp.zeros((seq, half), F32)
    c_head = jnp.concatenate([cos, cos, ones], axis=1)
    s1_head = jnp.concatenate([zh, sin, zeros], axis=1)
    s2_head = jnp.concatenate([-sin, zh, zeros], axis=1)
    rep = LANES // AT_HD
    return tuple(jnp.tile(a, (1, rep)) for a in (c_head, s1_head, s2_head))


def _hgrn_consts(reverse):
    c = HG_C
    t = np.arange(c)[:, None]
    u = np.arange(c)[None, :]
    tri = (u >= t) if reverse else (u <= t)
    sels = []
    for lvl in range(HG_LEVELS):
        blk = 2 << lvl
        start = (np.arange(c) // blk) * blk
        piv = start + (1 << lvl) - (0 if reverse else 1)
        sels.append(np.arange(c)[None, :] == piv[:, None])
    sel = np.concatenate(sels, axis=0)
    rep = lambda m, n: jnp.asarray(np.concatenate([m.astype(np.float32)] * n, axis=1), BF16)
    return rep(tri, 2), rep(sel, 3)


def _hgrn_kernel(*refs, reverse, final, n_chunks):
    if final:
        (lbl_ref, tri_ref, sel_ref, q_ref, f_ref, v_ref, ofw_ref, gh_ref, gain_ref,
         o_ref, st_ref, st0_ref) = refs
    else:
        (lbl_ref, tri_ref, sel_ref, q_ref, f_ref, v_ref,
         o_ref, st_ref, st0_ref) = refs
    c = HG_C

    @pl.when(pl.program_id(1) == 0)
    def _():
        st_ref[...] = jnp.zeros_like(st_ref)

    l0 = lbl_ref[0:1, :]
    l1 = lbl_ref[1:2, :]
    lm = jnp.maximum(l0, l1)
    e0 = jnp.exp(l0 - lm)
    e1 = jnp.exp(l1 - lm)
    lb = e0 / (e0 + e1)
    tri = tri_ref[...]
    piv_row = c // 2 if reverse else c // 2 - 1

    def gate_math(r0):
        ff = f_ref[pl.ds(r0, c), :]
        e = jnp.exp(-jnp.abs(ff))
        r = 1.0 / (1.0 + e)
        sg = jnp.where(ff >= 0.0, r, e * r)
        g = jnp.log(lb + (1.0 - lb) * sg)
        kin = (1.0 - lb) * (1.0 - sg)
        g_hi = g.astype(BF16)
        g_lo = (g - g_hi.astype(F32)).astype(BF16)
        b = _dot(tri, jnp.concatenate([g_hi, g_lo], axis=0))
        btot = b[0:1, :] if reverse else b[c - 1:c, :]
        return kin, b, btot

    ti = lax.broadcasted_iota(I32, (c, c), 0)
    si = lax.broadcasted_iota(I32, (c, c), 1)
    causal = (si >= ti) if reverse else (si <= ti)
    row = lax.broadcasted_iota(I32, (c, 1), 0)

    def att_robust(q, kin, b, piv):
        att = jnp.where(ti == si, _dot_nt(q.astype(BF16), kin.astype(BF16)), 0.0)
        for lvl in range(HG_LEVELS):
            p = piv[lvl * c:(lvl + 1) * c, :]
            bit = ((row >> lvl) & 1) == 1
            q_side = jnp.logical_not(bit) if reverse else bit
            qe = jnp.where(q_side, q * jnp.exp(jnp.minimum(b - p, 0.0)), 0.0).astype(BF16)
            ke = jnp.where(q_side, 0.0, kin * jnp.exp(jnp.minimum(p - b, 0.0))).astype(BF16)
            same = (ti >> (lvl + 1)) == (si >> (lvl + 1))
            att = att + jnp.where(same, _dot_nt(qe, ke), 0.0)
        return att

    heads = [slice(h * HG_DK, (h + 1) * HG_DK) for h in range(HG_H)]


    def emit(r0, outs):
        o = jnp.concatenate(outs, axis=1)
        if final:
            o = o + ofw_ref[pl.ds(r0, c), :]
            normed = []
            for sl in heads:
                ms = jnp.mean(o[:, sl] * o[:, sl], axis=-1, keepdims=True)
                normed.append(o[:, sl] * lax.rsqrt(ms + EPS))
            gh = gh_ref[pl.ds(r0, c), :].astype(F32)
            o = jnp.concatenate(normed, axis=1) * gain_ref[...] * (gh * jax.nn.sigmoid(gh))
        o_ref[pl.ds(r0, c), :] = o.astype(o_ref.dtype)

    def robust_step(i, carry):
        ci = (n_chunks - 1 - i) if reverse else i
        r0 = pl.multiple_of(ci * c, c)
        q = q_ref[pl.ds(r0, c), :].astype(F32)
        v = v_ref[pl.ds(r0, c), :]
        kin, b, btot = gate_math(r0)
        qe_all = (q * jnp.exp(b)).astype(BF16)
        kd_all = (kin * jnp.exp(btot - b)).astype(BF16)
        piv = _dot(sel_ref[...], jnp.concatenate(_split3(b), axis=0))
        outs = []
        for h, sl in enumerate(heads):
            att = att_robust(q[:, sl], kin[:, sl], b[:, sl], piv[:, sl])
            st = st_ref[h]
            outs.append(_dot_nt(qe_all[:, sl], st.astype(BF16)) + _dot(att.astype(BF16), v[:, sl]))
            st_ref[h] = st * jnp.exp(btot[:, sl]) + _dot_tn(v[:, sl], kd_all[:, sl])
        emit(r0, outs)
        return carry

    grp = HG_UNROLL

    def fast_group(i, bmin):
        gi = (n_chunks // grp - 1 - i) if reverse else i
        base = gi * (grp * c)
        order = range(grp - 1, -1, -1) if reverse else range(grp)
        chunks = []
        for j in order:
            r0 = pl.multiple_of(base + j * c, c)
            kin, b, btot = gate_math(r0)
            bp = b[piv_row:piv_row + 1, :]
            bmin = jnp.minimum(bmin, jnp.minimum(bp, btot - bp))
            to_piv = jnp.exp(b - bp)
            k_piv = kin / to_piv
            chunks.append(dict(
                r0=r0, v=v_ref[pl.ds(r0, c), :], dec=jnp.exp(btot), pdec=jnp.exp(bp),
                qe=q_ref[pl.ds(r0, c), :] * to_piv.astype(BF16),
                ke=k_piv.astype(BF16),
                kd=(k_piv * jnp.exp(btot - bp)).astype(BF16)))
        for ch in chunks:
            ch["att"] = [jnp.where(causal, _dot_nt(ch["qe"][:, sl], ch["ke"][:, sl]), 0.0).astype(BF16)
                         for sl in heads]
        for ch in chunks:
            ch["intra"] = [_dot(ch["att"][h], ch["v"][:, sl]) for h, sl in enumerate(heads)]
            ch["upd"] = [_dot_tn(ch["v"][:, sl], ch["kd"][:, sl]) for sl in heads]
        st = [st_ref[h] for h in range(HG_H)]
        for ch in chunks:
            outs = []
            for h, sl in enumerate(heads):
                st_piv = (st[h] * ch["pdec"][:, sl]).astype(BF16)
                outs.append(_dot_nt(ch["qe"][:, sl], st_piv) + ch["intra"][h])
                st[h] = st[h] * ch["dec"][:, sl] + ch["upd"][h]
            emit(ch["r0"], outs)
        for h in range(HG_H):
            st_ref[h] = st[h]
        return bmin

    st0_ref[...] = st_ref[...]
    bmin = lax.fori_loop(0, n_chunks // grp, fast_group, jnp.zeros((1, HG_W), F32))

    @pl.when(jnp.min(bmin) < HG_SAFE_LOGDECAY)
    def _():
        st_ref[...] = st0_ref[...]
        lax.fori_loop(0, n_chunks, robust_step, 0)


def _hgrn(lb_logits_d, hga, hgf, batch, seq, tc, reverse, o_fw=None, gain=None):
    final = o_fw is not None
    t = hga.shape[0]
    nblk = seq // tc
    tri, sel = _hgrn_consts(reverse)

    def rows(col):
        if reverse:
            return lambda b, j: (b * nblk + nblk - 1 - j, col)
        return lambda b, j: (b * nblk + j, col)

    blk = lambda col: pl.BlockSpec((tc, HG_W), rows(col))
    in_specs = [_const_spec((2, HG_W)), _const_spec(tri.shape), _const_spec(sel.shape),
                blk(0), blk(1 if reverse else 0), blk(1)]
    args = [lb_logits_d, tri, sel, hga, hgf, hga]
    if final:
        in_specs += [blk(0), blk(2), _const_spec((1, HG_W))]
        args += [o_fw, hga, gain]
    return pl.pallas_call(
        functools.partial(_hgrn_kernel, reverse=reverse, final=final, n_chunks=tc // HG_C),
        grid=(batch, nblk),
        in_specs=in_specs,
        out_specs=blk(0),
        out_shape=jax.ShapeDtypeStruct((t, HG_W), BF16 if final else F32),
        scratch_shapes=[pltpu.VMEM((HG_H, HG_DK, HG_DK), F32),
                        pltpu.VMEM((HG_H, HG_DK, HG_DK), F32)],
        compiler_params=_cp(("parallel", "arbitrary")),
        name="hgrn_bwd" if reverse else "hgrn_fwd",
    )(*args)


def _attn_kernel(q_ref, kl_ref, km_ref, kr_ref, vl_ref, vm_ref, vr_ref, o_ref, l_ref,
                 k_s, v_s, *, tq, ld):
    hw = AT_HALF
    n_res = q_ref.shape[0]
    for res in range(n_res):
        k_s[res, 0:hw, :] = kl_ref[res]
        k_s[res, hw:hw + tq, :] = km_ref[res]
        k_s[res, hw + tq:, :] = kr_ref[res]
        v_s[res, 0:hw, :] = vl_ref[res]
        v_s[res, hw:hw + tq, :] = vm_ref[res]
        v_s[res, hw + tq:, :] = vr_ref[res]
    qb = 2 * hw
    kb = 4 * hw
    base = pl.program_id(2) * tq
    lane_head = lax.broadcasted_iota(I32, (1, AT_GW), 1) // AT_HD
    q_sel = [jnp.where(lane_head == h, AT_HD ** -0.5, 0.0).astype(BF16) for h in range(AT_H)]
    out_head = lax.broadcasted_iota(I32, (qb, AT_GW), 1) // AT_HD
    rel = (lax.broadcasted_iota(I32, (AT_H * qb, kb), 0) % qb
           - lax.broadcasted_iota(I32, (AT_H * qb, kb), 1) + hw)
    band = jnp.where(jnp.abs(rel) <= hw, 0.0, NEG_BIG)
    key_col = lax.broadcasted_iota(I32, (1, kb), 1)
    for res, qs in [(res, qs) for res in range(n_res) for qs in range(0, tq, qb)]:
        q = q_ref[res, qs:qs + qb, :]
        kk = k_s[res, qs:qs + kb, :]
        vv = v_s[res, qs:qs + kb, :]
        q4 = jnp.concatenate([q * q_sel[h] for h in range(AT_H)], axis=0)
        s = _dot_nt(q4, kk) + band
        if qs == 0 or qs + qb == tq:
            kpos = base + qs - hw + key_col
            s = s + jnp.where((kpos >= 0) & (kpos < ld), 0.0, NEG_BIG)
        m = jnp.max(s, axis=-1, keepdims=True)
        p = jnp.exp(s - m)
        l = jnp.sum(p, axis=-1, keepdims=True)
        pv = _dot(p.astype(BF16), vv) * (1.0 / l)
        lse = m + jnp.log(l)
        o_all = jnp.zeros((qb, AT_GW), F32)
        l_all = jnp.zeros((qb, AT_GW), F32)
        for h in range(AT_H):
            o_all = jnp.where(out_head == h, pv[h * qb:(h + 1) * qb, :], o_all)
            l_all = jnp.where(out_head == h, lse[h * qb:(h + 1) * qb, :], l_all)
        o_ref[res, qs:qs + qb, :] = o_all.astype(o_ref.dtype)
        l_ref[res, qs:qs + qb, :] = l_all


def _attn_group(qkv):
    batch, dil, ld, _ = qkv.shape
    hw = AT_HALF
    tq = min(ld, ROW_TILE)
    nq = ld // tq
    nh = ld // hw
    per = tq // hw
    n_res = min(dil, max(1, ROW_TILE // tq))
    main = lambda part: pl.BlockSpec((None, n_res, tq, AT_GW), lambda b, r, j: (b, r, j, part))
    left = lambda part: pl.BlockSpec(
        (None, n_res, hw, AT_GW), lambda b, r, j: (b, r, jnp.maximum(j * per - 1, 0), part))
    right = lambda part: pl.BlockSpec(
        (None, n_res, hw, AT_GW), lambda b, r, j: (b, r, jnp.minimum((j + 1) * per, nh - 1), part))
    out_spec = pl.BlockSpec((None, n_res, tq, AT_GW), lambda b, r, j: (b, r, j, 0))
    return pl.pallas_call(
        functools.partial(_attn_kernel, tq=tq, ld=ld),
        grid=(batch, dil // n_res, nq),
        in_specs=[main(0), left(1), main(1), right(1), left(2), main(2), right(2)],
        out_specs=[out_spec, out_spec],
        out_shape=[jax.ShapeDtypeStruct((batch, dil, ld, AT_GW), BF16),
                   jax.ShapeDtypeStruct((batch, dil, ld, AT_GW), F32)],
        scratch_shapes=[pltpu.VMEM((n_res, tq + 2 * hw, AT_GW), BF16),
                        pltpu.VMEM((n_res, tq + 2 * hw, AT_GW), BF16)],
        compiler_params=_cp(("parallel", "parallel", "parallel")),
        name=f"attn_d{dil}",
    )(qkv, qkv, qkv, qkv, qkv, qkv, qkv)


def _merge_kernel(ohg_ref, o0_ref, o1_ref, o2_ref, l0_ref, l1_ref, l2_ref, gt_ref, x_ref, mod_ref,
                  w_hg_ref, w_at_ref, w_out_ref, npost_ref, npre_ref, w_r_ref, b_r_ref,
                  x1_ref, h2_ref, ids_ref, wts_ref, cnt_ref,
                  so1_ref, sl1_ref, so2_ref, sl2_ref):
    tm = x_ref.shape[0]
    sub = tm // MERGE_SUBTILES
    rows = [slice(i * sub, (i + 1) * sub) for i in range(MERGE_SUBTILES)]
    gt1 = mod_ref[0, 2:3, :]
    sh2 = mod_ref[0, 3:4, :]
    sc2 = mod_ref[0, 4:5, :]

    for src_ref, dst_ref in ((o1_ref, so1_ref), (l1_ref, sl1_ref), (o2_ref, so2_ref), (l2_ref, sl2_ref)):
        dil = src_ref.shape[0]
        for res in range(dil):
            vals = src_ref[res].astype(F32)
            for c in range(AT_GW // LANES):
                dst_ref[c, pl.ds(res, tm // dil, stride=dil), :] = vals[:, c * LANES:(c + 1) * LANES]

    def staged(ref, r):
        return jnp.concatenate([ref[c, r, :] for c in range(AT_GW // LANES)], axis=1)

    def branches(r):
        o0, l0 = o0_ref[0, r, :].astype(F32), l0_ref[0, r, :]
        o1, l1 = staged(so1_ref, r), staged(sl1_ref, r)
        o2, l2 = staged(so2_ref, r), staged(sl2_ref, r)
        m = jnp.maximum(jnp.maximum(l0, l1), l2)
        e0, e1, e2 = jnp.exp(l0 - m), jnp.exp(l1 - m), jnp.exp(l2 - m)
        oa = (e0 * o0 + e1 * o1 + e2 * o2) / (e0 + e1 + e2)
        return _dot(ohg_ref[r, :], w_hg_ref[...]), _dot(oa.astype(BF16), w_at_ref[...])

    def out_proj(r, b_hg, b_at):
        g_hg = jax.nn.sigmoid(gt_ref[r, 0:D])
        g_at = jax.nn.sigmoid(gt_ref[r, D:2 * D])
        merged = g_hg * b_hg.astype(BF16) + g_at * b_at.astype(BF16)
        return _dot(merged, w_out_ref[...])

    def norms(r, y):
        x1 = x_ref[r, :] + gt1 * _rms(y, npost_ref[...])
        x1_ref[r, :] = x1
        h2 = _rms(x1, npre_ref[...]) * (1.0 + sc2) + sh2
        h2_ref[r, :] = _pack_rows(h2)
        return _dot(h2.astype(BF16), w_r_ref[...]) + b_r_ref[...]

    def top_k(r, logits):
        lane = lax.broadcasted_iota(I32, logits.shape, 1)
        lane_f = lane.astype(F32)
        work = logits
        vals, idxs = [], []
        sel = jnp.zeros(logits.shape, F32)
        for _ in range(TOP_K):
            mk = jnp.max(work, axis=-1, keepdims=True)
            ik = jnp.min(jnp.where(work == mk, lane_f, float(LANES)), axis=-1, keepdims=True)
            hit = lane_f == ik
            sel = jnp.where(hit, 1.0, sel)
            work = jnp.where(hit, -jnp.inf, work)
            vals.append(mk)
            idxs.append(ik)
        es = [jnp.exp(v - vals[0]) for v in vals]
        den = es[0] + es[1] + es[2] + es[3]
        ids = jnp.zeros(logits.shape, F32)
        wts = jnp.zeros(logits.shape, F32)
        for k in range(TOP_K):
            ids = jnp.where(lane == k, idxs[k], ids)
            wts = jnp.where(lane == k, es[k] / den, wts)
        ids_ref[r, :] = ids.astype(I32)
        wts_ref[r, :] = wts
        return jnp.sum(sel, axis=0, keepdims=True)

    br = [branches(r) for r in rows]
    ys = [out_proj(r, *b) for r, b in zip(rows, br)]
    logits = [norms(r, y) for r, y in zip(rows, ys)]
    counts = [top_k(r, lg) for r, lg in zip(rows, logits)]
    cnt_ref[0] = functools.reduce(lambda a, b: a + b, counts)


def _merge(ohg, att_o, att_l, gates, x2, mod3, w_hg, w_at, w_out, npost, npre, w_r, b_r, seq, tm):
    t = x2.shape[0]
    nt = t // tm
    row = lambda w: pl.BlockSpec((tm, w), lambda i: (i, 0))
    slab_i = jax.ShapeDtypeStruct((t, LANES), I32)
    slab_f = jax.ShapeDtypeStruct((t, LANES), F32)
    nblk = seq // tm
    grouped = [pl.BlockSpec((None, d, tm // d, AT_GW),
                            lambda i: ((i * tm) // seq, 0, i % nblk, 0)) for d in AT_DILS]
    return pl.pallas_call(
        _merge_kernel,
        grid=(nt,),
        in_specs=[row(HG_W)] + grouped + grouped + [row(2 * D), row(D),
                  pl.BlockSpec((1, 6, D), lambda i: ((i * tm) // seq, 0, 0)),
                  _const_spec(w_hg.shape), _const_spec(w_at.shape), _const_spec(w_out.shape),
                  _const_spec((1, D)), _const_spec((1, D)),
                  _const_spec(w_r.shape), _const_spec(b_r.shape)],
        out_specs=[row(D), row(HALF_D), row(LANES), row(LANES),
                   pl.BlockSpec((1, 1, LANES), lambda i: (i, 0, 0))],
        out_shape=[jax.ShapeDtypeStruct((t, D), F32), jax.ShapeDtypeStruct((t, HALF_D), U32),
                   slab_i, slab_f, jax.ShapeDtypeStruct((nt, 1, LANES), F32)],
        scratch_shapes=[pltpu.VMEM((AT_GW // LANES, tm, LANES), F32)] * 4,
        compiler_params=_cp(("parallel",)),
        name="merge",
    )(ohg, *att_o, *att_l, gates, x2, mod3, w_hg, w_at, w_out, npost, npre, w_r, b_r)


def _route_kernel(ids_ref, base_ref, tril_ref, dest_ref):
    ids = ids_ref[...]
    lane = lax.broadcasted_iota(I32, ids.shape, 1)
    hits = [lane == ids[:, k:k + 1] for k in range(TOP_K)]
    sel = jnp.zeros(ids.shape, F32)
    for hit in hits:
        sel = jnp.where(hit, 1.0, sel)
    before = _dot(tril_ref[...], sel.astype(BF16)) + base_ref[0]
    dest = jnp.zeros(ids.shape, I32)
    for k, hit in enumerate(hits):
        rk = jnp.sum(jnp.where(hit, before, 0.0), axis=-1, keepdims=True)
        dest = jnp.where(lane == k, rk.astype(I32), dest)
    dest_ref[...] = dest


def _route(ids, tile_base, tm):
    t = ids.shape[0]
    tril = jnp.asarray(np.tril(np.ones((tm, tm), np.float32), -1), BF16)
    return pl.pallas_call(
        _route_kernel,
        grid=(t // tm,),
        in_specs=[pl.BlockSpec((tm, LANES), lambda i: (i, 0)),
                  pl.BlockSpec((1, 1, LANES), lambda i: (i, 0, 0)),
                  _const_spec((tm, tm))],
        out_specs=pl.BlockSpec((tm, LANES), lambda i: (i, 0)),
        out_shape=jax.ShapeDtypeStruct((t, LANES), I32),
        compiler_params=_cp(("parallel",)),
        name="route",
    )(ids, tile_base, tril)


def _moe_kernel(be_ref, rows_ref, x_ref, wg_ref, wl_ref, bg_ref, bl_ref, wd_ref, bd_ref, o_ref):
    def ffn(x_packed):
        x_lo, x_hi = (a.astype(BF16) for a in _unpack_rows(x_packed))
        gate = (_dot(x_lo, wg_ref[0, :HALF_D, :]) + _dot(x_hi, wg_ref[0, HALF_D:, :])
                + bg_ref[0])
        up = (_dot(x_lo, wl_ref[0, :HALF_D, :]) + _dot(x_hi, wl_ref[0, HALF_D:, :])
              + bl_ref[0])
        gate = jnp.minimum(gate, SWIGLU_LIMIT)
        up = jnp.clip(up, -SWIGLU_LIMIT, SWIGLU_LIMIT)
        act = (up + 1.0) * gate * jax.nn.sigmoid(SWIGLU_ALPHA * gate)
        return _pack_rows(_dot(act.astype(BF16), wd_ref[0]) + bd_ref[0])

    valid = rows_ref[pl.program_id(0)]
    half = x_ref.shape[0] // 2

    @pl.when(valid > half)
    def _():
        o_ref[...] = ffn(x_ref[...])

    @pl.when((valid > 0) & (valid <= half))
    def _():
        o_ref[:half, :] = ffn(x_ref[:half, :])
        o_ref[half:, :] = jnp.zeros((half, HALF_D), U32)

    @pl.when(valid == 0)
    def _():
        o_ref[...] = jnp.zeros_like(o_ref)


def _moe(block_expert, block_rows, xb, wg, wl, bg, bl, wd, bd, bm):
    r = xb.shape[0]
    nblk = r // bm
    ew = lambda shape: pl.BlockSpec((1,) + shape, lambda i, be, br: (be[i], 0, 0))
    return pl.pallas_call(
        _moe_kernel,
        grid_spec=pltpu.PrefetchScalarGridSpec(
            num_scalar_prefetch=2,
            grid=(nblk,),
            in_specs=[pl.BlockSpec((bm, HALF_D), lambda i, be, br: (i, 0)),
                      ew((D, D)), ew((D, D)), ew((1, D)), ew((1, D)), ew((D, D)), ew((1, D))],
            out_specs=pl.BlockSpec((bm, HALF_D), lambda i, be, br: (i, 0)),
        ),
        out_shape=jax.ShapeDtypeStruct((r, HALF_D), U32),
        compiler_params=_cp(("arbitrary",)),
        name="moe",
    )(block_expert, block_rows, xb, wg, wl, bg, bl, wd, bd)


def _final_kernel(y0_ref, y1_ref, y2_ref, y3_ref, wts_ref, x1_ref, mod_ref, npost_ref, o_ref):
    wts = wts_ref[...]
    y = jnp.zeros(x1_ref.shape, F32)
    for k, y_ref in enumerate((y0_ref, y1_ref, y2_ref, y3_ref)):
        y = y + wts[:, k:k + 1] * jnp.concatenate(_unpack_rows(y_ref[0]), axis=1)
    gt2 = mod_ref[0, 5:6, :]
    o_ref[...] = x1_ref[...] + gt2 * _rms(y, npost_ref[...])


def _final(yg, wts, x1, mod3, npost, seq, tm):
    t = x1.shape[0]
    slot = lambda k: pl.BlockSpec((1, tm, HALF_D), lambda i: (k, i, 0))
    return pl.pallas_call(
        _final_kernel,
        grid=(t // tm,),
        in_specs=[slot(0), slot(1), slot(2), slot(3),
                  pl.BlockSpec((tm, LANES), lambda i: (i, 0)),
                  pl.BlockSpec((tm, D), lambda i: (i, 0)),
                  pl.BlockSpec((1, 6, D), lambda i: ((i * tm) // seq, 0, 0)),
                  _const_spec((1, D))],
        out_specs=pl.BlockSpec((tm, D), lambda i: (i, 0)),
        out_shape=jax.ShapeDtypeStruct((t, D), F32),
        compiler_params=_cp(("parallel",)),
        name="final",
    )(yg, yg, yg, yg, wts, x1, mod3, npost)


def _tiles(seq):
    return dict(tm_in=min(seq, ROW_TILE), tc=min(seq, STREAM_TILE), tm_merge=min(seq, ROW_TILE),
                bm=MOE_BLOCK, tm_final=min(seq, STREAM_TILE))


def _prep_weights(w_in, w_hg_out, w_att_out, w_out, w_router, b_router, w_up, b_up, w_down, b_down):
    w = w_in[0]
    hw = HG_W
    a0 = 5 * hw
    aw = AT_GW * len(AT_DILS)
    q_at, k_at, v_at = (w[:, a0 + i * aw:a0 + (i + 1) * aw] for i in range(3))
    grp = lambda m, g: m[:, g * AT_GW:(g + 1) * AT_GW]
    w_at = jnp.concatenate(
        [jnp.concatenate([grp(q_at, g), grp(k_at, g), grp(v_at, g)], axis=1)
         for g in range(len(AT_DILS))], axis=1)
    pad = LANES - N_EXP
    wg, wl = _deinterleave(w_up[0])
    return dict(
        wg=wg, wl=wl,
        w_hga=jnp.concatenate([w[:, 0:hw], w[:, 3 * hw:4 * hw], w[:, 4 * hw:5 * hw]], 1).astype(BF16),
        w_hgf=w[:, hw:3 * hw].astype(BF16),
        w_at=w_at.astype(BF16),
        w_gt=w[:, a0 + 3 * aw:].astype(BF16),
        w_hg_out=w_hg_out[0].astype(BF16),
        w_att_out=w_att_out[0].astype(BF16),
        w_out=w_out[0].astype(BF16),
        w_r=jnp.pad(w_router[0], ((0, 0), (0, pad))).astype(BF16),
        b_r=jnp.pad(b_router[0], (0, pad), constant_values=NEG_BIG).reshape(1, LANES),
        bg=b_up[0][:, 0::2].reshape(N_EXP, 1, D),
        bl=b_up[0][:, 1::2].reshape(N_EXP, 1, D),
        wd=w_down[0].astype(BF16),
        bd=b_down[0].reshape(N_EXP, 1, D),
    )


def _trunk(x, mod, wts, norm_pre, norm_post, lb_logits, hg_gain):
    batch, seq, width = x.shape
    t = batch * seq
    tl = _tiles(seq)
    assert width == D and seq % STREAM_TILE == 0 and seq // max(AT_DILS) >= 2 * AT_HALF
    x2 = x.reshape(t, D)
    mod3 = mod.reshape(batch, 6, D)
    rope = _rope_tables(seq)

    hga, hgf, at0, at1, at2, gates = _inproj(
        x2, mod3, norm_pre[0, 0].reshape(1, D), rope,
        wts["w_hga"], wts["w_hgf"], wts["w_at"], wts["w_gt"], seq, tl["tm_in"])

    o_fw = _hgrn(lb_logits[0], hga, hgf, batch, seq, tl["tc"], reverse=False)
    ohg = _hgrn(lb_logits[1], hga, hgf, batch, seq, tl["tc"], reverse=True,
                o_fw=o_fw, gain=hg_gain[0].reshape(1, HG_W))

    att = [_attn_group(a) for a in (at0, at1, at2)]

    x1, h2, ids, rw, cnt = _merge(
        ohg, [a[0] for a in att], [a[1] for a in att], gates, x2, mod3,
        wts["w_hg_out"], wts["w_att_out"], wts["w_out"],
        norm_post[0, 0].reshape(1, D), norm_pre[0, 1].reshape(1, D),
        wts["w_r"], wts["b_r"], seq, tl["tm_merge"])

    bm = tl["bm"]
    tmr = tl["tm_merge"]
    cnt_tiles = cnt.reshape(t // tmr, LANES).astype(I32)
    total = jnp.sum(cnt_tiles, axis=0)
    padded = (total + bm - 1) // bm * bm
    pad_end = jnp.cumsum(padded)
    pad_start = pad_end - padded
    tile_base = pad_start[None, :] + jnp.cumsum(cnt_tiles, axis=0) - cnt_tiles
    dest = _route(ids, tile_base.astype(F32).reshape(-1, 1, LANES), tmr)[:, :TOP_K]

    n_rows = t * TOP_K + N_EXP * bm
    n_blocks = n_rows // bm
    blk_start = jnp.arange(n_blocks, dtype=I32) * bm
    block_expert = jnp.minimum(
        jnp.sum(pad_end[None, :N_EXP] <= blk_start[:, None], axis=1), N_EXP - 1).astype(I32)
    block_rows = jnp.clip(pad_start[block_expert] + total[block_expert] - blk_start, 0, bm)
    block_rows = jnp.where(blk_start < pad_end[N_EXP - 1], block_rows, 0).astype(I32)

    dest_kt = dest.T
    xb = _sc_scatter_rows(h2, dest_kt, n_rows)
    yb = _moe(block_expert, block_rows, xb, wts["wg"], wts["wl"], wts["bg"], wts["bl"],
              wts["wd"], wts["bd"], bm)
    yg = _sc_gather_rows(yb, dest_kt.reshape(-1)).reshape(TOP_K, t, HALF_D)

    out = _final(yg, rw, x1, mod3, norm_post[0, 1].reshape(1, D), seq, tl["tm_final"])
    return out.reshape(batch, seq, D)


def kernel(x_prompt, x_sample, c_prompt, c_sample, w_ada, b_ada, norm_pre, norm_post, w_in,
           lb_logits, hg_norm_gain, w_hg_out, w_att_out, w_out, w_router, b_router,
           w_up, b_up, w_down, b_down):
    wts = _prep_weights(w_in, w_hg_out, w_att_out, w_out, w_router, b_router,
                        w_up, b_up, w_down, b_down)
    nb = c_prompt.shape[0]
    mod = _ada(jnp.concatenate([c_prompt, c_sample], axis=0), w_ada[0], b_ada[0])
    lb_l = lb_logits.astype(F32)
    y_p = _trunk(x_prompt, mod[:nb], wts, norm_pre, norm_post, lb_l, hg_norm_gain)
    y_s = _trunk(x_sample, mod[nb:], wts, norm_pre, norm_post, lb_l, hg_norm_gain)
    return (y_p, y_s)
```

```python
import functools

import numpy as np
import jax
import jax.numpy as jnp
from jax import lax
from jax.experimental import pallas as pl
from jax.experimental.pallas import tpu as pltpu
from jax.experimental.pallas import tpu_sc as plsc

F32 = jnp.float32
BF16 = jnp.bfloat16
I32 = jnp.int32
U32 = jnp.uint32

D = 1024
EPS = 1e-6
HG_H = 4
HG_DK = 128
HG_W = HG_H * HG_DK
HG_C = 64
HG_LEVELS = 6
HG_SAFE_LOGDECAY = -80.0
HG_UNROLL = 4
AT_DILS = (1, 4, 16)
AT_HALF = 64
AT_H = 4
AT_HD = 64
AT_GW = AT_H * AT_HD
ROPE_DIM = 16
ROPE_THETA = 500000.0
N_EXP = 32
TOP_K = 4
SWIGLU_LIMIT = 7.0
SWIGLU_ALPHA = 1.702
MERGE_SUBTILES = 4
ROW_TILE = 512
STREAM_TILE = 1024
ADA_COL_TILE = 1536
MOE_BLOCK = 1024
LANES = 128
NEG_BIG = -1e30
HALF_D = D // 2
SC_CORES = 2
SC_SUBCORES = 16
SC_WORKERS = SC_CORES * SC_SUBCORES
SC_CHUNK = 128

VMEM_LIMIT = 56 * 1024 * 1024


def _cp(sem, vmem=VMEM_LIMIT):
    return pltpu.CompilerParams(dimension_semantics=sem, vmem_limit_bytes=vmem)


def _dot(a, b):
    return jnp.dot(a, b, preferred_element_type=F32)


def _dot_nt(a, b):
    return lax.dot_general(a, b, (((1,), (1,)), ((), ())), preferred_element_type=F32)


def _dot_tn(a, b):
    return lax.dot_general(a, b, (((0,), (0,)), ((), ())), preferred_element_type=F32)


def _split3(x):
    hi = x.astype(BF16)
    r = x - hi.astype(F32)
    mid = r.astype(BF16)
    lo = (r - mid.astype(F32)).astype(BF16)
    return hi, mid, lo


def _rms(x, gain):
    ms = jnp.mean(x * x, axis=-1, keepdims=True)
    return x * lax.rsqrt(ms + EPS) * gain


def _const_spec(shape):
    n = len(shape)
    return pl.BlockSpec(shape, lambda *_: (0,) * n)


def _pack_rows(y):
    bits = lambda a: lax.bitcast_convert_type(a.astype(BF16).astype(F32), U32)
    return (bits(y[:, :HALF_D]) >> 16) | (bits(y[:, HALF_D:]) & jnp.uint32(0xFFFF0000))


def _unpack_rows(w):
    lo = lax.bitcast_convert_type(w << 16, F32)
    hi = lax.bitcast_convert_type(w & jnp.uint32(0xFFFF0000), F32)
    return lo, hi


def _deint_kernel(w_ref, p_ref, g_ref, l_ref):
    p = p_ref[...]
    n_grp = w_ref.shape[2] // (2 * LANES)
    for b in range(n_grp):
        blk = w_ref[0, :, b * 2 * LANES:(b + 1) * 2 * LANES].astype(BF16)
        r = _dot(blk, p)
        g_ref[0, :, b * LANES:(b + 1) * LANES] = r[:, :LANES].astype(BF16)
        l_ref[0, :, b * LANES:(b + 1) * LANES] = r[:, LANES:].astype(BF16)


def _deinterleave(w_up):
    n_e, d, n2 = w_up.shape
    perm = np.zeros((2 * LANES, 2 * LANES), np.float32)
    perm[2 * np.arange(LANES), np.arange(LANES)] = 1.0
    perm[2 * np.arange(LANES) + 1, LANES + np.arange(LANES)] = 1.0
    tr = d
    out = jax.ShapeDtypeStruct((n_e, d, n2 // 2), BF16)
    return pl.pallas_call(
        _deint_kernel,
        grid=(n_e, d // tr),
        in_specs=[pl.BlockSpec((1, tr, n2), lambda e, i: (e, i, 0)),
                  _const_spec(perm.shape)],
        out_specs=[pl.BlockSpec((1, tr, n2 // 2), lambda e, i: (e, i, 0))] * 2,
        out_shape=[out, out],
        compiler_params=_cp(("parallel", "parallel")),
        name="deinterleave",
    )(w_up, jnp.asarray(perm, BF16))


def _sc_mesh():
    return plsc.VectorSubcoreMesh(core_axis_name="c", subcore_axis_name="s")


def _sc_worker():
    return lax.axis_index("s") * SC_CORES + lax.axis_index("c")


def _sc_scatter_rows(src, dest_kt, n_rows):
    t, w = src.shape
    n_slot = dest_kt.shape[0]
    assert t % (SC_WORKERS * SC_CHUNK) == 0
    nch = t // (SC_WORKERS * SC_CHUNK)
    idx = dest_kt.reshape(n_slot, SC_WORKERS, nch, SC_CHUNK).transpose(1, 0, 2, 3)

    @functools.partial(
        pl.kernel, mesh=_sc_mesh(),
        out_type=jax.ShapeDtypeStruct((n_rows, w), src.dtype),
        scratch_types=[pltpu.VMEM((n_slot, nch, SC_CHUNK), I32),
                       pltpu.VMEM((SC_CHUNK, w), src.dtype)],
        name="sc_scatter")
    def run(src_hbm, idx_hbm, out_hbm, idx_v, rows_v):
        wid = _sc_worker()
        pltpu.sync_copy(idx_hbm.at[wid], idx_v)

        @pl.loop(0, nch)
        def _(j):
            pltpu.sync_copy(src_hbm.at[pl.ds((wid * nch + j) * SC_CHUNK, SC_CHUNK)], rows_v)
            for k in range(n_slot):
                pltpu.sync_copy(rows_v, out_hbm.at[idx_v.at[k, j]])

    return run(src, idx)


def _sc_gather_rows(table, idx):
    n = idx.shape[0]
    w = table.shape[1]
    assert n % (SC_WORKERS * SC_CHUNK) == 0
    nch = n // (SC_WORKERS * SC_CHUNK)
    idx3 = idx.reshape(SC_WORKERS, nch, SC_CHUNK)

    @functools.partial(
        pl.kernel, mesh=_sc_mesh(),
        out_type=jax.ShapeDtypeStruct((n, w), table.dtype),
        scratch_types=[pltpu.VMEM((nch, SC_CHUNK), I32),
                       pltpu.VMEM((SC_CHUNK, w), table.dtype)],
        name="sc_gather")
    def run(table_hbm, idx_hbm, out_hbm, idx_v, rows_v):
        wid = _sc_worker()
        pltpu.sync_copy(idx_hbm.at[wid], idx_v)

        @pl.loop(0, nch)
        def _(j):
            pltpu.sync_copy(table_hbm.at[idx_v.at[j]], rows_v)
            pltpu.sync_copy(rows_v, out_hbm.at[pl.ds((wid * nch + j) * SC_CHUNK, SC_CHUNK)])

    return run(table, idx3)


def _ada_kernel(c_ref, w_ref, b_ref, o_ref):
    c = c_ref[...]
    a = c * jax.nn.sigmoid(c)
    w = w_ref[...]
    a_hi = a.astype(BF16)
    a_lo = (a - a_hi.astype(F32)).astype(BF16)
    w_hi = w.astype(BF16)
    w_lo = (w - w_hi.astype(F32)).astype(BF16)
    o_ref[...] = _dot(a_hi, w_hi) + _dot(a_lo, w_hi) + _dot(a_hi, w_lo) + b_ref[...]


def _ada(c, w, b):
    nb = c.shape[0]
    n = w.shape[1]
    tn = ADA_COL_TILE
    return pl.pallas_call(
        _ada_kernel,
        grid=(n // tn,),
        in_specs=[pl.BlockSpec((nb, D), lambda j: (0, 0)),
                  pl.BlockSpec((D, tn), lambda j: (0, j)),
                  pl.BlockSpec((1, tn), lambda j: (0, j))],
        out_specs=pl.BlockSpec((nb, tn), lambda j: (0, j)),
        out_shape=jax.ShapeDtypeStruct((nb, n), F32),
        compiler_params=_cp(("parallel",)),
        name="ada",
    )(c, w, b.reshape(1, n))


def _inproj_kernel(x_ref, mod_ref, gain_ref, cos_ref, s1_ref, s2_ref,
                   w_hga_ref, w_hgf_ref, w_at_ref, w_gt_ref,
                   hga_ref, hgf_ref, at0_ref, at1_ref, at2_ref, gt_ref, stage_ref):
    x = x_ref[...]
    tm = x.shape[0]
    sh = mod_ref[0, 0:1, :]
    sc = mod_ref[0, 1:2, :]
    h = _rms(x, gain_ref[...]) * (1.0 + sc) + sh
    hb = h.astype(BF16)
    hga_ref[...] = _dot(hb, w_hga_ref[...]).astype(BF16)
    hgf_ref[...] = _dot(hb, w_hgf_ref[...])
    cos = cos_ref[...]
    s1 = s1_ref[...]
    s2 = s2_ref[...]
    for g, o_ref in enumerate((at0_ref, at1_ref, at2_ref)):
        acc = _dot(hb, w_at_ref[:, g * 3 * AT_GW:(g + 1) * 3 * AT_GW])
        parts = []
        for j in range(2 * AT_GW // LANES):
            a = acc[:, j * LANES:(j + 1) * LANES]
            parts.append(a * cos + pltpu.roll(a, ROPE_DIM // 2, 1) * s1
                         + pltpu.roll(a, LANES - ROPE_DIM // 2, 1) * s2)
        parts.append(acc[:, 2 * AT_GW:])
        vals = jnp.concatenate(parts, axis=1)
        dil = AT_DILS[g]
        if dil == 1:
            o_ref[0, 0] = vals.astype(BF16)
        else:
            n_lt = 3 * AT_GW // LANES
            for c in range(n_lt):
                stage_ref[c] = vals[:, c * LANES:(c + 1) * LANES]
            for r in range(dil):
                o_ref[0, r] = jnp.concatenate(
                    [stage_ref[c, pl.ds(r, tm // dil, stride=dil), :] for c in range(n_lt)],
                    axis=1).astype(BF16)
    gt_ref[...] = _dot(hb, w_gt_ref[...]).astype(BF16)


def _inproj(x2, mod3, gain, rope, w_hga, w_hgf, w_at, w_gt, seq, tm):
    t = x2.shape[0]
    n_pos_blk = seq // tm
    row = lambda w: pl.BlockSpec((tm, w), lambda i: (i, 0))
    pos = pl.BlockSpec((tm, LANES), lambda i: (i % n_pos_blk, 0))
    batch = t // seq
    dilated = lambda d: pl.BlockSpec((1, d, tm // d, 3 * AT_GW),
                                     lambda i: ((i * tm) // seq, 0, i % n_pos_blk, 0))
    at_shape = lambda d: jax.ShapeDtypeStruct((batch, d, seq // d, 3 * AT_GW), BF16)
    return pl.pallas_call(
        _inproj_kernel,
        grid=(t // tm,),
        in_specs=[row(D),
                  pl.BlockSpec((1, 6, D), lambda i: ((i * tm) // seq, 0, 0)),
                  _const_spec((1, D)), pos, pos, pos,
                  _const_spec(w_hga.shape), _const_spec(w_hgf.shape),
                  _const_spec(w_at.shape), _const_spec(w_gt.shape)],
        out_specs=[row(3 * HG_W), row(2 * HG_W)] + [dilated(d) for d in AT_DILS] + [row(2 * D)],
        out_shape=[jax.ShapeDtypeStruct((t, 3 * HG_W), BF16),
                   jax.ShapeDtypeStruct((t, 2 * HG_W), F32)]
                  + [at_shape(d) for d in AT_DILS]
                  + [jax.ShapeDtypeStruct((t, 2 * D), BF16)],
        scratch_shapes=[pltpu.VMEM((3 * AT_GW // LANES, tm, LANES), F32)],
        compiler_params=_cp(("parallel",)),
        name="inproj",
    )(x2, mod3, gain, *rope, w_hga, w_hgf, w_at, w_gt)


def _rope_tables(seq):
    half = ROPE_DIM // 2
    inv_freq = ROPE_THETA ** (-np.arange(half, dtype=np.float32) / half)
    ang = jnp.arange(seq, dtype=F32)[:, None] * jnp.asarray(inv_freq)[None, :]
    cos, sin = jnp.cos(ang), jnp.sin(ang)
    ones = jnp.ones((seq, AT_HD - ROPE_DIM), F32)
    zeros = jnp.zeros((seq, AT_HD - ROPE_DIM), F32)
    zh = jnp.zeros((seq, half), F32)
    c_head = jnp.concatenate([cos, cos, ones], axis=1)
    s1_head = jnp.concatenate([zh, sin, zeros], axis=1)
    s2_head = jnp.concatenate([-sin, zh, zeros], axis=1)
    rep = LANES // AT_HD
    return tuple(jnp.tile(a, (1, rep)) for a in (c_head, s1_head, s2_head))


def _hgrn_consts(reverse):
    c = HG_C
    t = np.arange(c)[:, None]
    u = np.arange(c)[None, :]
    tri = (u >= t) if reverse else (u <= t)
    sels = []
    for lvl in range(HG_LEVELS):
        blk = 2 << lvl
        start = (np.arange(c) // blk) * blk
        piv = start + (1 << lvl) - (0 if reverse else 1)
        sels.append(np.arange(c)[None, :] == piv[:, None])
    sel = np.concatenate(sels, axis=0)
    rep = lambda m, n: jnp.asarray(np.concatenate([m.astype(np.float32)] * n, axis=1), BF16)
    return rep(tri, 2), rep(sel, 3)


def _hgrn_kernel(*refs, reverse, final, n_chunks):
    if final:
        (lbl_ref, tri_ref, sel_ref, q_ref, f_ref, v_ref, ofw_ref, gh_ref, gain_ref,
         o_ref, st_ref, st0_ref) = refs
    else:
        (lbl_ref, tri_ref, sel_ref, q_ref, f_ref, v_ref,
         o_ref, st_ref, st0_ref) = refs
    c = HG_C

    @pl.when(pl.program_id(1) == 0)
    def _():
        st_ref[...] = jnp.zeros_like(st_ref)

    l0 = lbl_ref[0:1, :]
    l1 = lbl_ref[1:2, :]
    lm = jnp.maximum(l0, l1)
    e0 = jnp.exp(l0 - lm)
    e1 = jnp.exp(l1 - lm)
    lb = e0 / (e0 + e1)
    tri = tri_ref[...]
    piv_row = c // 2 if reverse else c // 2 - 1

    def gate_math(r0):
        ff = f_ref[pl.ds(r0, c), :]
        e = jnp.exp(-jnp.abs(ff))
        r = 1.0 / (1.0 + e)
        sg = jnp.where(ff >= 0.0, r, e * r)
        g = jnp.log(lb + (1.0 - lb) * sg)
        kin = (1.0 - lb) * (1.0 - sg)
        g_hi = g.astype(BF16)
        g_lo = (g - g_hi.astype(F32)).astype(BF16)
        b = _dot(tri, jnp.concatenate([g_hi, g_lo], axis=0))
        btot = b[0:1, :] if reverse else b[c - 1:c, :]
        return kin, b, btot

    ti = lax.broadcasted_iota(I32, (c, c), 0)
    si = lax.broadcasted_iota(I32, (c, c), 1)
    causal = (si >= ti) if reverse else (si <= ti)
    row = lax.broadcasted_iota(I32, (c, 1), 0)

    def att_robust(q, kin, b, piv):
        att = jnp.where(ti == si, _dot_nt(q.astype(BF16), kin.astype(BF16)), 0.0)
        for lvl in range(HG_LEVELS):
            p = piv[lvl * c:(lvl + 1) * c, :]
            bit = ((row >> lvl) & 1) == 1
            q_side = jnp.logical_not(bit) if reverse else bit
            qe = jnp.where(q_side, q * jnp.exp(jnp.minimum(b - p, 0.0)), 0.0).astype(BF16)
            ke = jnp.where(q_side, 0.0, kin * jnp.exp(jnp.minimum(p - b, 0.0))).astype(BF16)
            same = (ti >> (lvl + 1)) == (si >> (lvl + 1))
            att = att + jnp.where(same, _dot_nt(qe, ke), 0.0)
        return att

    heads = [slice(h * HG_DK, (h + 1) * HG_DK) for h in range(HG_H)]


    def emit(r0, outs):
        o = jnp.concatenate(outs, axis=1)
        if final:
            o = o + ofw_ref[pl.ds(r0, c), :]
            normed = []
            for sl in heads:
                ms = jnp.mean(o[:, sl] * o[:, sl], axis=-1, keepdims=True)
                normed.append(o[:, sl] * lax.rsqrt(ms + EPS))
            gh = gh_ref[pl.ds(r0, c), :].astype(F32)
            o = jnp.concatenate(normed, axis=1) * gain_ref[...] * (gh * jax.nn.sigmoid(gh))
        o_ref[pl.ds(r0, c), :] = o.astype(o_ref.dtype)

    def robust_step(i, carry):
        ci = (n_chunks - 1 - i) if reverse else i
        r0 = pl.multiple_of(ci * c, c)
        q = q_ref[pl.ds(r0, c), :].astype(F32)
        v = v_ref[pl.ds(r0, c), :]
        kin, b, btot = gate_math(r0)
        qe_all = (q * jnp.exp(b)).astype(BF16)
        kd_all = (kin * jnp.exp(btot - b)).astype(BF16)
        piv = _dot(sel_ref[...], jnp.concatenate(_split3(b), axis=0))
        outs = []
        for h, sl in enumerate(heads):
            att = att_robust(q[:, sl], kin[:, sl], b[:, sl], piv[:, sl])
            st = st_ref[h]
            outs.append(_dot_nt(qe_all[:, sl], st.astype(BF16)) + _dot(att.astype(BF16), v[:, sl]))
            st_ref[h] = st * jnp.exp(btot[:, sl]) + _dot_tn(v[:, sl], kd_all[:, sl])
        emit(r0, outs)
        return carry

    grp = HG_UNROLL

    def fast_group(i, bmin):
        gi = (n_chunks // grp - 1 - i) if reverse else i
        base = gi * (grp * c)
        order = range(grp - 1, -1, -1) if reverse else range(grp)
        chunks = []
        for j in order:
            r0 = pl.multiple_of(base + j * c, c)
            kin, b, btot = gate_math(r0)
            bp = b[piv_row:piv_row + 1, :]
            bmin = jnp.minimum(bmin, jnp.minimum(bp, btot - bp))
            to_piv = jnp.exp(b - bp)
            k_piv = kin / to_piv
            chunks.append(dict(
                r0=r0, v=v_ref[pl.ds(r0, c), :], dec=jnp.exp(btot), pdec=jnp.exp(bp),
                qe=q_ref[pl.ds(r0, c), :] * to_piv.astype(BF16),
                ke=k_piv.astype(BF16),
                kd=(k_piv * jnp.exp(btot - bp)).astype(BF16)))
        for ch in chunks:
            ch["att"] = [jnp.where(causal, _dot_nt(ch["qe"][:, sl], ch["ke"][:, sl]), 0.0).astype(BF16)
                         for sl in heads]
        for ch in chunks:
            ch["intra"] = [_dot(ch["att"][h], ch["v"][:, sl]) for h, sl in enumerate(heads)]
            ch["upd"] = [_dot_tn(ch["v"][:, sl], ch["kd"][:, sl]) for sl in heads]
        st = [st_ref[h] for h in range(HG_H)]
        for ch in chunks:
            outs = []
            for h, sl in enumerate(heads):
                st_piv = (st[h] * ch["pdec"][:, sl]).astype(BF16)
                outs.append(_dot_nt(ch["qe"][:, sl], st_piv) + ch["intra"][h])
                st[h] = st[h] * ch["dec"][:, sl] + ch["upd"][h]
            emit(ch["r0"], outs)
        for h in range(HG_H):
            st_ref[h] = st[h]
        return bmin

    st0_ref[...] = st_ref[...]
    bmin = lax.fori_loop(0, n_chunks // grp, fast_group, jnp.zeros((1, HG_W), F32))

    @pl.when(jnp.min(bmin) < HG_SAFE_LOGDECAY)
    def _():
        st_ref[...] = st0_ref[...]
        lax.fori_loop(0, n_chunks, robust_step, 0)


def _hgrn(lb_logits_d, hga, hgf, batch, seq, tc, reverse, o_fw=None, gain=None):
    final = o_fw is not None
    t = hga.shape[0]
    nblk = seq // tc
    tri, sel = _hgrn_consts(reverse)

    def rows(col):
        if reverse:
            return lambda b, j: (b * nblk + nblk - 1 - j, col)
        return lambda b, j: (b * nblk + j, col)

    blk = lambda col: pl.BlockSpec((tc, HG_W), rows(col))
    in_specs = [_const_spec((2, HG_W)), _const_spec(tri.shape), _const_spec(sel.shape),
                blk(0), blk(1 if reverse else 0), blk(1)]
    args = [lb_logits_d, tri, sel, hga, hgf, hga]
    if final:
        in_specs += [blk(0), blk(2), _const_spec((1, HG_W))]
        args += [o_fw, hga, gain]
    return pl.pallas_call(
        functools.partial(_hgrn_kernel, reverse=reverse, final=final, n_chunks=tc // HG_C),
        grid=(batch, nblk),
        in_specs=in_specs,
        out_specs=blk(0),
        out_shape=jax.ShapeDtypeStruct((t, HG_W), BF16 if final else F32),
        scratch_shapes=[pltpu.VMEM((HG_H, HG_DK, HG_DK), F32),
                        pltpu.VMEM((HG_H, HG_DK, HG_DK), F32)],
        compiler_params=_cp(("parallel", "arbitrary")),
        name="hgrn_bwd" if reverse else "hgrn_fwd",
    )(*args)


def _attn_kernel(q_ref, kl_ref, km_ref, kr_ref, vl_ref, vm_ref, vr_ref, o_ref, l_ref,
                 k_s, v_s, *, tq, ld):
    hw = AT_HALF
    n_res = q_ref.shape[0]
    for res in range(n_res):
        k_s[res, 0:hw, :] = kl_ref[res]
        k_s[res, hw:hw + tq, :] = km_ref[res]
        k_s[res, hw + tq:, :] = kr_ref[res]
        v_s[res, 0:hw, :] = vl_ref[res]
        v_s[res, hw:hw + tq, :] = vm_ref[res]
        v_s[res, hw + tq:, :] = vr_ref[res]
    qb = 2 * hw
    kb = 4 * hw
    base = pl.program_id(2) * tq
    lane_head = lax.broadcasted_iota(I32, (1, AT_GW), 1) // AT_HD
    q_sel = [jnp.where(lane_head == h, AT_HD ** -0.5, 0.0).astype(BF16) for h in range(AT_H)]
    out_head = lax.broadcasted_iota(I32, (qb, AT_GW), 1) // AT_HD
    rel = (lax.broadcasted_iota(I32, (AT_H * qb, kb), 0) % qb
           - lax.broadcasted_iota(I32, (AT_H * qb, kb), 1) + hw)
    band = jnp.where(jnp.abs(rel) <= hw, 0.0, NEG_BIG)
    key_col = lax.broadcasted_iota(I32, (1, kb), 1)
    for res, qs in [(res, qs) for res in range(n_res) for qs in range(0, tq, qb)]:
        q = q_ref[res, qs:qs + qb, :]
        kk = k_s[res, qs:qs + kb, :]
        vv = v_s[res, qs:qs + kb, :]
        q4 = jnp.concatenate([q * q_sel[h] for h in range(AT_H)], axis=0)
        s = _dot_nt(q4, kk) + band
        if qs == 0 or qs + qb == tq:
            kpos = base + qs - hw + key_col
            s = s + jnp.where((kpos >= 0) & (kpos < ld), 0.0, NEG_BIG)
        m = jnp.max(s, axis=-1, keepdims=True)
        p = jnp.exp(s - m)
        l = jnp.sum(p, axis=-1, keepdims=True)
        pv = _dot(p.astype(BF16), vv) * (1.0 / l)
        lse = m + jnp.log(l)
        o_all = jnp.zeros((qb, AT_GW), F32)
        l_all = jnp.zeros((qb, AT_GW), F32)
        for h in range(AT_H):
            o_all = jnp.where(out_head == h, pv[h * qb:(h + 1) * qb, :], o_all)
            l_all = jnp.where(out_head == h, lse[h * qb:(h + 1) * qb, :], l_all)
        o_ref[res, qs:qs + qb, :] = o_all.astype(o_ref.dtype)
        l_ref[res, qs:qs + qb, :] = l_all


def _attn_group(qkv):
    batch, dil, ld, _ = qkv.shape
    hw = AT_HALF
    tq = min(ld, STREAM_TILE)
    nq = ld // tq
    nh = ld // hw
    per = tq // hw
    n_res = min(dil, max(1, STREAM_TILE // tq))
    main = lambda part: pl.BlockSpec((None, n_res, tq, AT_GW), lambda b, r, j: (b, r, j, part))
    left = lambda part: pl.BlockSpec(
        (None, n_res, hw, AT_GW), lambda b, r, j: (b, r, jnp.maximum(j * per - 1, 0), part))
    right = lambda part: pl.BlockSpec(
        (None, n_res, hw, AT_GW), lambda b, r, j: (b, r, jnp.minimum((j + 1) * per, nh - 1), part))
    out_spec = pl.BlockSpec((None, n_res, tq, AT_GW), lambda b, r, j: (b, r, j, 0))
    return pl.pallas_call(
        functools.partial(_attn_kernel, tq=tq, ld=ld),
        grid=(batch, dil // n_res, nq),
        in_specs=[main(0), left(1), main(1), right(1), left(2), main(2), right(2)],
        out_specs=[out_spec, out_spec],
        out_shape=[jax.ShapeDtypeStruct((batch, dil, ld, AT_GW), BF16),
                   jax.ShapeDtypeStruct((batch, dil, ld, AT_GW), F32)],
        scratch_shapes=[pltpu.VMEM((n_res, tq + 2 * hw, AT_GW), BF16),
                        pltpu.VMEM((n_res, tq + 2 * hw, AT_GW), BF16)],
        compiler_params=_cp(("parallel", "parallel", "parallel")),
        name=f"attn_d{dil}",
    )(qkv, qkv, qkv, qkv, qkv, qkv, qkv)


def _merge_kernel(ohg_ref, o0_ref, o1_ref, o2_ref, l0_ref, l1_ref, l2_ref, gt_ref, x_ref, mod_ref,
                  w_hg_ref, w_at_ref, w_out_ref, npost_ref, npre_ref, w_r_ref, b_r_ref,
                  x1_ref, h2_ref, ids_ref, wts_ref, cnt_ref,
                  so1_ref, sl1_ref, so2_ref, sl2_ref):
    tm = x_ref.shape[0]
    sub = tm // MERGE_SUBTILES
    rows = [slice(i * sub, (i + 1) * sub) for i in range(MERGE_SUBTILES)]
    gt1 = mod_ref[0, 2:3, :]
    sh2 = mod_ref[0, 3:4, :]
    sc2 = mod_ref[0, 4:5, :]

    for src_ref, dst_ref in ((o1_ref, so1_ref), (l1_ref, sl1_ref), (o2_ref, so2_ref), (l2_ref, sl2_ref)):
        dil = src_ref.shape[0]
        for res in range(dil):
            vals = src_ref[res].astype(F32)
            for c in range(AT_GW // LANES):
                dst_ref[c, pl.ds(res, tm // dil, stride=dil), :] = vals[:, c * LANES:(c + 1) * LANES]

    def staged(ref, r):
        return jnp.concatenate([ref[c, r, :] for c in range(AT_GW // LANES)], axis=1)

    def branches(r):
        o0, l0 = o0_ref[0, r, :].astype(F32), l0_ref[0, r, :]
        o1, l1 = staged(so1_ref, r), staged(sl1_ref, r)
        o2, l2 = staged(so2_ref, r), staged(sl2_ref, r)
        m = jnp.maximum(jnp.maximum(l0, l1), l2)
        e0, e1, e2 = jnp.exp(l0 - m), jnp.exp(l1 - m), jnp.exp(l2 - m)
        oa = (e0 * o0 + e1 * o1 + e2 * o2) / (e0 + e1 + e2)
        return _dot(ohg_ref[r, :], w_hg_ref[...]), _dot(oa.astype(BF16), w_at_ref[...])

    def out_proj(r, b_hg, b_at):
        g_hg = jax.nn.sigmoid(gt_ref[r, 0:D])
        g_at = jax.nn.sigmoid(gt_ref[r, D:2 * D])
        merged = g_hg * b_hg.astype(BF16) + g_at * b_at.astype(BF16)
        return _dot(merged, w_out_ref[...])

    def norms(r, y):
        x1 = x_ref[r, :] + gt1 * _rms(y, npost_ref[...])
        x1_ref[r, :] = x1
        h2 = _rms(x1, npre_ref[...]) * (1.0 + sc2) + sh2
        h2_ref[r, :] = _pack_rows(h2)
        return _dot(h2.astype(BF16), w_r_ref[...]) + b_r_ref[...]

    def top_k(r, logits):
        lane = lax.broadcasted_iota(I32, logits.shape, 1)
        lane_f = lane.astype(F32)
        work = logits
        vals, idxs = [], []
        sel = jnp.zeros(logits.shape, F32)
        for _ in range(TOP_K):
            mk = jnp.max(work, axis=-1, keepdims=True)
            ik = jnp.min(jnp.where(work == mk, lane_f, float(LANES)), axis=-1, keepdims=True)
            hit = lane_f == ik
            sel = jnp.where(hit, 1.0, sel)
            work = jnp.where(hit, -jnp.inf, work)
            vals.append(mk)
            idxs.append(ik)
        es = [jnp.exp(v - vals[0]) for v in vals]
        den = es[0] + es[1] + es[2] + es[3]
        ids = jnp.zeros(logits.shape, F32)
        wts = jnp.zeros(logits.shape, F32)
        for k in range(TOP_K):
            ids = jnp.where(lane == k, idxs[k], ids)
            wts = jnp.where(lane == k, es[k] / den, wts)
        ids_ref[r, :] = ids.astype(I32)
        wts_ref[r, :] = wts
        return jnp.sum(sel, axis=0, keepdims=True)

    br = [branches(r) for r in rows]
    ys = [out_proj(r, *b) for r, b in zip(rows, br)]
    logits = [norms(r, y) for r, y in zip(rows, ys)]
    counts = [top_k(r, lg) for r, lg in zip(rows, logits)]
    cnt_ref[0] = functools.reduce(lambda a, b: a + b, counts)


def _merge(ohg, att_o, att_l, gates, x2, mod3, w_hg, w_at, w_out, npost, npre, w_r, b_r, seq, tm):
    t = x2.shape[0]
    nt = t // tm
    row = lambda w: pl.BlockSpec((tm, w), lambda i: (i, 0))
    slab_i = jax.ShapeDtypeStruct((t, LANES), I32)
    slab_f = jax.ShapeDtypeStruct((t, LANES), F32)
    nblk = seq // tm
    grouped = [pl.BlockSpec((None, d, tm // d, AT_GW),
                            lambda i: ((i * tm) // seq, 0, i % nblk, 0)) for d in AT_DILS]
    return pl.pallas_call(
        _merge_kernel,
        grid=(nt,),
        in_specs=[row(HG_W)] + grouped + grouped + [row(2 * D), row(D),
                  pl.BlockSpec((1, 6, D), lambda i: ((i * tm) // seq, 0, 0)),
                  _const_spec(w_hg.shape), _const_spec(w_at.shape), _const_spec(w_out.shape),
                  _const_spec((1, D)), _const_spec((1, D)),
                  _const_spec(w_r.shape), _const_spec(b_r.shape)],
        out_specs=[row(D), row(HALF_D), row(LANES), row(LANES),
                   pl.BlockSpec((1, 1, LANES), lambda i: (i, 0, 0))],
        out_shape=[jax.ShapeDtypeStruct((t, D), F32), jax.ShapeDtypeStruct((t, HALF_D), U32),
                   slab_i, slab_f, jax.ShapeDtypeStruct((nt, 1, LANES), F32)],
        scratch_shapes=[pltpu.VMEM((AT_GW // LANES, tm, LANES), F32)] * 4,
        compiler_params=_cp(("parallel",)),
        name="merge",
    )(ohg, *att_o, *att_l, gates, x2, mod3, w_hg, w_at, w_out, npost, npre, w_r, b_r)


def _route_kernel(ids_ref, base_ref, tril_ref, dest_ref):
    ids = ids_ref[...]
    lane = lax.broadcasted_iota(I32, ids.shape, 1)
    hits = [lane == ids[:, k:k + 1] for k in range(TOP_K)]
    sel = jnp.zeros(ids.shape, F32)
    for hit in hits:
        sel = jnp.where(hit, 1.0, sel)
    before = _dot(tril_ref[...], sel.astype(BF16)) + base_ref[0]
    dest = jnp.zeros(ids.shape, I32)
    for k, hit in enumerate(hits):
        rk = jnp.sum(jnp.where(hit, before, 0.0), axis=-1, keepdims=True)
        dest = jnp.where(lane == k, rk.astype(I32), dest)
    dest_ref[...] = dest


def _route(ids, tile_base, tm):
    t = ids.shape[0]
    tril = jnp.asarray(np.tril(np.ones((tm, tm), np.float32), -1), BF16)
    return pl.pallas_call(
        _route_kernel,
        grid=(t // tm,),
        in_specs=[pl.BlockSpec((tm, LANES), lambda i: (i, 0)),
                  pl.BlockSpec((1, 1, LANES), lambda i: (i, 0, 0)),
                  _const_spec((tm, tm))],
        out_specs=pl.BlockSpec((tm, LANES), lambda i: (i, 0)),
        out_shape=jax.ShapeDtypeStruct((t, LANES), I32),
        compiler_params=_cp(("parallel",)),
        name="route",
    )(ids, tile_base, tril)


def _moe_kernel(be_ref, rows_ref, x_ref, wg_ref, wl_ref, bg_ref, bl_ref, wd_ref, bd_ref, o_ref):
    def ffn(x_packed):
        x_lo, x_hi = (a.astype(BF16) for a in _unpack_rows(x_packed))
        gate = (_dot(x_lo, wg_ref[0, :HALF_D, :]) + _dot(x_hi, wg_ref[0, HALF_D:, :])
                + bg_ref[0])
        up = (_dot(x_lo, wl_ref[0, :HALF_D, :]) + _dot(x_hi, wl_ref[0, HALF_D:, :])
              + bl_ref[0])
        gate = jnp.minimum(gate, SWIGLU_LIMIT)
        up = jnp.clip(up, -SWIGLU_LIMIT, SWIGLU_LIMIT)
        act = (up + 1.0) * gate * jax.nn.sigmoid(SWIGLU_ALPHA * gate)
        return _pack_rows(_dot(act.astype(BF16), wd_ref[0]) + bd_ref[0])

    valid = rows_ref[pl.program_id(0)]
    half = x_ref.shape[0] // 2

    @pl.when(valid > half)
    def _():
        o_ref[...] = ffn(x_ref[...])

    @pl.when((valid > 0) & (valid <= half))
    def _():
        o_ref[:half, :] = ffn(x_ref[:half, :])
        o_ref[half:, :] = jnp.zeros((half, HALF_D), U32)

    @pl.when(valid == 0)
    def _():
        o_ref[...] = jnp.zeros_like(o_ref)


def _moe(block_expert, block_rows, xb, wg, wl, bg, bl, wd, bd, bm):
    r = xb.shape[0]
    nblk = r // bm
    ew = lambda shape: pl.BlockSpec((1,) + shape, lambda i, be, br: (be[i], 0, 0))
    return pl.pallas_call(
        _moe_kernel,
        grid_spec=pltpu.PrefetchScalarGridSpec(
            num_scalar_prefetch=2,
            grid=(nblk,),
            in_specs=[pl.BlockSpec((bm, HALF_D), lambda i, be, br: (i, 0)),
                      ew((D, D)), ew((D, D)), ew((1, D)), ew((1, D)), ew((D, D)), ew((1, D))],
            out_specs=pl.BlockSpec((bm, HALF_D), lambda i, be, br: (i, 0)),
        ),
        out_shape=jax.ShapeDtypeStruct((r, HALF_D), U32),
        compiler_params=_cp(("arbitrary",)),
        name="moe",
    )(block_expert, block_rows, xb, wg, wl, bg, bl, wd, bd)


def _final_kernel(y0_ref, y1_ref, y2_ref, y3_ref, wts_ref, x1_ref, mod_ref, npost_ref, o_ref):
    wts = wts_ref[...]
    y = jnp.zeros(x1_ref.shape, F32)
    for k, y_ref in enumerate((y0_ref, y1_ref, y2_ref, y3_ref)):
        y = y + wts[:, k:k + 1] * jnp.concatenate(_unpack_rows(y_ref[0]), axis=1)
    gt2 = mod_ref[0, 5:6, :]
    o_ref[...] = x1_ref[...] + gt2 * _rms(y, npost_ref[...])


def _final(yg, wts, x1, mod3, npost, seq, tm):
    t = x1.shape[0]
    slot = lambda k: pl.BlockSpec((1, tm, HALF_D), lambda i: (k, i, 0))
    return pl.pallas_call(
        _final_kernel,
        grid=(t // tm,),
        in_specs=[slot(0), slot(1), slot(2), slot(3),
                  pl.BlockSpec((tm, LANES), lambda i: (i, 0)),
                  pl.BlockSpec((tm, D), lambda i: (i, 0)),
                  pl.BlockSpec((1, 6, D), lambda i: ((i * tm) // seq, 0, 0)),
                  _const_spec((1, D))],
        out_specs=pl.BlockSpec((tm, D), lambda i: (i, 0)),
        out_shape=jax.ShapeDtypeStruct((t, D), F32),
        compiler_params=_cp(("parallel",)),
        name="final",
    )(yg, yg, yg, yg, wts, x1, mod3, npost)


def _tiles(seq):
    return dict(tm_in=min(seq, ROW_TILE), tc=min(seq, STREAM_TILE), tm_merge=min(seq, STREAM_TILE),
                bm=MOE_BLOCK, tm_final=min(seq, STREAM_TILE))


def _prep_weights(w_in, w_hg_out, w_att_out, w_out, w_router, b_router, w_up, b_up, w_down, b_down):
    w = w_in[0]
    hw = HG_W
    a0 = 5 * hw
    aw = AT_GW * len(AT_DILS)
    q_at, k_at, v_at = (w[:, a0 + i * aw:a0 + (i + 1) * aw] for i in range(3))
    grp = lambda m, g: m[:, g * AT_GW:(g + 1) * AT_GW]
    w_at = jnp.concatenate(
        [jnp.concatenate([grp(q_at, g), grp(k_at, g), grp(v_at, g)], axis=1)
         for g in range(len(AT_DILS))], axis=1)
    pad = LANES - N_EXP
    wg, wl = _deinterleave(w_up[0])
    return dict(
        wg=wg, wl=wl,
        w_hga=jnp.concatenate([w[:, 0:hw], w[:, 3 * hw:4 * hw], w[:, 4 * hw:5 * hw]], 1).astype(BF16),
        w_hgf=w[:, hw:3 * hw].astype(BF16),
        w_at=w_at.astype(BF16),
        w_gt=w[:, a0 + 3 * aw:].astype(BF16),
        w_hg_out=w_hg_out[0].astype(BF16),
        w_att_out=w_att_out[0].astype(BF16),
        w_out=w_out[0].astype(BF16),
        w_r=jnp.pad(w_router[0], ((0, 0), (0, pad))).astype(BF16),
        b_r=jnp.pad(b_router[0], (0, pad), constant_values=NEG_BIG).reshape(1, LANES),
        bg=b_up[0][:, 0::2].reshape(N_EXP, 1, D),
        bl=b_up[0][:, 1::2].reshape(N_EXP, 1, D),
        wd=w_down[0].astype(BF16),
        bd=b_down[0].reshape(N_EXP, 1, D),
    )


def _trunk(x, mod, wts, norm_pre, norm_post, lb_logits, hg_gain):
    batch, seq, width = x.shape
    t = batch * seq
    tl = _tiles(seq)
    assert width == D and seq % STREAM_TILE == 0 and seq // max(AT_DILS) >= 2 * AT_HALF
    x2 = x.reshape(t, D)
    mod3 = mod.reshape(batch, 6, D)
    rope = _rope_tables(seq)

    hga, hgf, at0, at1, at2, gates = _inproj(
        x2, mod3, norm_pre[0, 0].reshape(1, D), rope,
        wts["w_hga"], wts["w_hgf"], wts["w_at"], wts["w_gt"], seq, tl["tm_in"])

    o_fw = _hgrn(lb_logits[0], hga, hgf, batch, seq, tl["tc"], reverse=False)
    ohg = _hgrn(lb_logits[1], hga, hgf, batch, seq, tl["tc"], reverse=True,
                o_fw=o_fw, gain=hg_gain[0].reshape(1, HG_W))

    att = [_attn_group(a) for a in (at0, at1, at2)]

    x1, h2, ids, rw, cnt = _merge(
        ohg, [a[0] for a in att], [a[1] for a in att], gates, x2, mod3,
        wts["w_hg_out"], wts["w_att_out"], wts["w_out"],
        norm_post[0, 0].reshape(1, D), norm_pre[0, 1].reshape(1, D),
        wts["w_r"], wts["b_r"], seq, tl["tm_merge"])

    bm = tl["bm"]
    tmr = tl["tm_merge"]
    cnt_tiles = cnt.reshape(t // tmr, LANES).astype(I32)
    total = jnp.sum(cnt_tiles, axis=0)
    padded = (total + bm - 1) // bm * bm
    pad_end = jnp.cumsum(padded)
    pad_start = pad_end - padded
    tile_base = pad_start[None, :] + jnp.cumsum(cnt_tiles, axis=0) - cnt_tiles
    dest = _route(ids, tile_base.astype(F32).reshape(-1, 1, LANES), tmr)[:, :TOP_K]

    n_rows = t * TOP_K + N_EXP * bm
    n_blocks = n_rows // bm
    blk_start = jnp.arange(n_blocks, dtype=I32) * bm
    block_expert = jnp.minimum(
        jnp.sum(pad_end[None, :N_EXP] <= blk_start[:, None], axis=1), N_EXP - 1).astype(I32)
    block_rows = jnp.clip(pad_start[block_expert] + total[block_expert] - blk_start, 0, bm)
    block_rows = jnp.where(blk_start < pad_end[N_EXP - 1], block_rows, 0).astype(I32)

    dest_kt = dest.T
    xb = _sc_scatter_rows(h2, dest_kt, n_rows)
    yb = _moe(block_expert, block_rows, xb, wts["wg"], wts["wl"], wts["bg"], wts["bl"],
              wts["wd"], wts["bd"], bm)
    yg = _sc_gather_rows(yb, dest_kt.reshape(-1)).reshape(TOP_K, t, HALF_D)

    out = _final(yg, rw, x1, mod3, norm_post[0, 1].reshape(1, D), seq, tl["tm_final"])
    return out.reshape(batch, seq, D)


def kernel(x_prompt, x_sample, c_prompt, c_sample, w_ada, b_ada, norm_pre, norm_post, w_in,
           lb_logits, hg_norm_gain, w_hg_out, w_att_out, w_out, w_router, b_router,
           w_up, b_up, w_down, b_down):
    wts = _prep_weights(w_in, w_hg_out, w_att_out, w_out, w_router, b_router,
                        w_up, b_up, w_down, b_down)
    nb = c_prompt.shape[0]
    mod = _ada(jnp.concatenate([c_prompt, c_sample], axis=0), w_ada[0], b_ada[0])
    lb_l = lb_logits.astype(F32)
    y_p = _trunk(x_prompt, mod[:nb], wts, norm_pre, norm_post, lb_l, hg_norm_gain)
    y_s = _trunk(x_sample, mod[nb:], wts, norm_pre, norm_post, lb_l, hg_norm_gain)
    return (y_p, y_s)
```

```python
import functools

import numpy as np
import jax
import jax.numpy as jnp
from jax import lax
from jax.experimental import pallas as pl
from jax.experimental.pallas import tpu as pltpu
from jax.experimental.pallas import tpu_sc as plsc

F32 = jnp.float32
BF16 = jnp.bfloat16
I32 = jnp.int32
U32 = jnp.uint32

D = 1024
EPS = 1e-6
HG_H = 4
HG_DK = 128
HG_W = HG_H * HG_DK
HG_C = 64
HG_LEVELS = 6
HG_SAFE_LOGDECAY = -80.0
HG_UNROLL = 8
AT_DILS = (1, 4, 16)
AT_HALF = 64
AT_H = 4
AT_HD = 64
AT_GW = AT_H * AT_HD
ROPE_DIM = 16
ROPE_THETA = 500000.0
N_EXP = 32
TOP_K = 4
SWIGLU_LIMIT = 7.0
SWIGLU_ALPHA = 1.702
MERGE_SUBTILES = 4
ROW_TILE = 512
STREAM_TILE = 1024
SEQ_TILE = 2048
ADA_COL_TILE = 1536
MOE_BLOCK = 1024
LANES = 128
NEG_BIG = -1e30
HALF_D = D // 2
SC_CORES = 2
SC_SUBCORES = 16
SC_WORKERS = SC_CORES * SC_SUBCORES
SC_CHUNK = 128

VMEM_LIMIT = 56 * 1024 * 1024


def _cp(sem, vmem=VMEM_LIMIT):
    return pltpu.CompilerParams(dimension_semantics=sem, vmem_limit_bytes=vmem)


def _dot(a, b):
    return jnp.dot(a, b, preferred_element_type=F32)


def _dot_nt(a, b):
    return lax.dot_general(a, b, (((1,), (1,)), ((), ())), preferred_element_type=F32)


def _dot_tn(a, b):
    return lax.dot_general(a, b, (((0,), (0,)), ((), ())), preferred_element_type=F32)


def _split3(x):
    hi = x.astype(BF16)
    r = x - hi.astype(F32)
    mid = r.astype(BF16)
    lo = (r - mid.astype(F32)).astype(BF16)
    return hi, mid, lo


def _rms(x, gain):
    ms = jnp.mean(x * x, axis=-1, keepdims=True)
    return x * lax.rsqrt(ms + EPS) * gain


def _const_spec(shape):
    n = len(shape)
    return pl.BlockSpec(shape, lambda *_: (0,) * n)


def _pack_rows(y):
    bits = lambda a: lax.bitcast_convert_type(a.astype(BF16).astype(F32), U32)
    return (bits(y[:, :HALF_D]) >> 16) | (bits(y[:, HALF_D:]) & jnp.uint32(0xFFFF0000))


def _unpack_rows(w):
    lo = lax.bitcast_convert_type(w << 16, F32)
    hi = lax.bitcast_convert_type(w & jnp.uint32(0xFFFF0000), F32)
    return lo, hi


def _deint_kernel(w_ref, p_ref, g_ref, l_ref):
    p = p_ref[...]
    n_grp = w_ref.shape[2] // (2 * LANES)
    for b in range(n_grp):
        blk = w_ref[0, :, b * 2 * LANES:(b + 1) * 2 * LANES].astype(BF16)
        r = _dot(blk, p)
        g_ref[0, :, b * LANES:(b + 1) * LANES] = r[:, :LANES].astype(BF16)
        l_ref[0, :, b * LANES:(b + 1) * LANES] = r[:, LANES:].astype(BF16)


def _deinterleave(w_up):
    n_e, d, n2 = w_up.shape
    perm = np.zeros((2 * LANES, 2 * LANES), np.float32)
    perm[2 * np.arange(LANES), np.arange(LANES)] = 1.0
    perm[2 * np.arange(LANES) + 1, LANES + np.arange(LANES)] = 1.0
    tr = d
    out = jax.ShapeDtypeStruct((n_e, d, n2 // 2), BF16)
    return pl.pallas_call(
        _deint_kernel,
        grid=(n_e, d // tr),
        in_specs=[pl.BlockSpec((1, tr, n2), lambda e, i: (e, i, 0)),
                  _const_spec(perm.shape)],
        out_specs=[pl.BlockSpec((1, tr, n2 // 2), lambda e, i: (e, i, 0))] * 2,
        out_shape=[out, out],
        compiler_params=_cp(("parallel", "parallel")),
        name="deinterleave",
    )(w_up, jnp.asarray(perm, BF16))


def _sc_mesh():
    return plsc.VectorSubcoreMesh(core_axis_name="c", subcore_axis_name="s")


def _sc_worker():
    return lax.axis_index("s") * SC_CORES + lax.axis_index("c")


def _sc_scatter_rows(src, dest_kt, n_rows):
    t, w = src.shape
    n_slot = dest_kt.shape[0]
    assert t % (SC_WORKERS * SC_CHUNK) == 0
    nch = t // (SC_WORKERS * SC_CHUNK)
    idx = dest_kt.reshape(n_slot, SC_WORKERS, nch, SC_CHUNK).transpose(1, 0, 2, 3)

    @functools.partial(
        pl.kernel, mesh=_sc_mesh(),
        out_type=jax.ShapeDtypeStruct((n_rows, w), src.dtype),
        scratch_types=[pltpu.VMEM((n_slot, nch, SC_CHUNK), I32),
                       pltpu.VMEM((SC_CHUNK, w), src.dtype)],
        name="sc_scatter")
    def run(src_hbm, idx_hbm, out_hbm, idx_v, rows_v):
        wid = _sc_worker()
        pltpu.sync_copy(idx_hbm.at[wid], idx_v)

        @pl.loop(0, nch)
        def _(j):
            pltpu.sync_copy(src_hbm.at[pl.ds((wid * nch + j) * SC_CHUNK, SC_CHUNK)], rows_v)
            for k in range(n_slot):
                pltpu.sync_copy(rows_v, out_hbm.at[idx_v.at[k, j]])

    return run(src, idx)


def _sc_gather_rows(table, idx):
    n = idx.shape[0]
    w = table.shape[1]
    assert n % (SC_WORKERS * SC_CHUNK) == 0
    nch = n // (SC_WORKERS * SC_CHUNK)
    idx3 = idx.reshape(SC_WORKERS, nch, SC_CHUNK)

    @functools.partial(
        pl.kernel, mesh=_sc_mesh(),
        out_type=jax.ShapeDtypeStruct((n, w), table.dtype),
        scratch_types=[pltpu.VMEM((nch, SC_CHUNK), I32),
                       pltpu.VMEM((SC_CHUNK, w), table.dtype)],
        name="sc_gather")
    def run(table_hbm, idx_hbm, out_hbm, idx_v, rows_v):
        wid = _sc_worker()
        pltpu.sync_copy(idx_hbm.at[wid], idx_v)

        @pl.loop(0, nch)
        def _(j):
            pltpu.sync_copy(table_hbm.at[idx_v.at[j]], rows_v)
            pltpu.sync_copy(rows_v, out_hbm.at[pl.ds((wid * nch + j) * SC_CHUNK, SC_CHUNK)])

    return run(table, idx3)


def _ada_kernel(c_ref, w_ref, b_ref, o_ref):
    c = c_ref[...]
    a = c * jax.nn.sigmoid(c)
    w = w_ref[...]
    a_hi = a.astype(BF16)
    a_lo = (a - a_hi.astype(F32)).astype(BF16)
    w_hi = w.astype(BF16)
    w_lo = (w - w_hi.astype(F32)).astype(BF16)
    o_ref[...] = _dot(a_hi, w_hi) + _dot(a_lo, w_hi) + _dot(a_hi, w_lo) + b_ref[...]


def _ada(c, w, b):
    nb = c.shape[0]
    n = w.shape[1]
    tn = ADA_COL_TILE
    return pl.pallas_call(
        _ada_kernel,
        grid=(n // tn,),
        in_specs=[pl.BlockSpec((nb, D), lambda j: (0, 0)),
                  pl.BlockSpec((D, tn), lambda j: (0, j)),
                  pl.BlockSpec((1, tn), lambda j: (0, j))],
        out_specs=pl.BlockSpec((nb, tn), lambda j: (0, j)),
        out_shape=jax.ShapeDtypeStruct((nb, n), F32),
        compiler_params=_cp(("parallel",)),
        name="ada",
    )(c, w, b.reshape(1, n))


def _inproj_kernel(x_ref, mod_ref, gain_ref, cos_ref, s1_ref, s2_ref,
                   w_hga_ref, w_hgf_ref, w_at_ref, w_gt_ref,
                   hga_ref, hgf_ref, at0_ref, at1_ref, at2_ref, gt_ref, stage_ref):
    x = x_ref[...]
    tm = x.shape[0]
    sh = mod_ref[0, 0:1, :]
    sc = mod_ref[0, 1:2, :]
    h = _rms(x, gain_ref[...]) * (1.0 + sc) + sh
    hb = h.astype(BF16)
    hga_ref[...] = _dot(hb, w_hga_ref[...]).astype(BF16)
    hgf_ref[...] = _dot(hb, w_hgf_ref[...])
    cos = cos_ref[...]
    s1 = s1_ref[...]
    s2 = s2_ref[...]
    for g, o_ref in enumerate((at0_ref, at1_ref, at2_ref)):
        acc = _dot(hb, w_at_ref[:, g * 3 * AT_GW:(g + 1) * 3 * AT_GW])
        parts = []
        for j in range(2 * AT_GW // LANES):
            a = acc[:, j * LANES:(j + 1) * LANES]
            parts.append(a * cos + pltpu.roll(a, ROPE_DIM // 2, 1) * s1
                         + pltpu.roll(a, LANES - ROPE_DIM // 2, 1) * s2)
        parts.append(acc[:, 2 * AT_GW:])
        vals = jnp.concatenate(parts, axis=1)
        dil = AT_DILS[g]
        if dil == 1:
            o_ref[0, 0] = vals.astype(BF16)
        else:
            n_lt = 3 * AT_GW // LANES
            for c in range(n_lt):
                stage_ref[c] = vals[:, c * LANES:(c + 1) * LANES]
            for r in range(dil):
                o_ref[0, r] = jnp.concatenate(
                    [stage_ref[c, pl.ds(r, tm // dil, stride=dil), :] for c in range(n_lt)],
                    axis=1).astype(BF16)
    gt_ref[...] = _dot(hb, w_gt_ref[...]).astype(BF16)


def _inproj(x2, mod3, gain, rope, w_hga, w_hgf, w_at, w_gt, seq, tm):
    t = x2.shape[0]
    n_pos_blk = seq // tm
    row = lambda w: pl.BlockSpec((tm, w), lambda i: (i, 0))
    pos = pl.BlockSpec((tm, LANES), lambda i: (i % n_pos_blk, 0))
    batch = t // seq
    dilated = lambda d: pl.BlockSpec((1, d, tm // d, 3 * AT_GW),
                                     lambda i: ((i * tm) // seq, 0, i % n_pos_blk, 0))
    at_shape = lambda d: jax.ShapeDtypeStruct((batch, d, seq // d, 3 * AT_GW), BF16)
    return pl.pallas_call(
        _inproj_kernel,
        grid=(t // tm,),
        in_specs=[row(D),
                  pl.BlockSpec((1, 6, D), lambda i: ((i * tm) // seq, 0, 0)),
                  _const_spec((1, D)), pos, pos, pos,
                  _const_spec(w_hga.shape), _const_spec(w_hgf.shape),
                  _const_spec(w_at.shape), _const_spec(w_gt.shape)],
        out_specs=[row(3 * HG_W), row(2 * HG_W)] + [dilated(d) for d in AT_DILS] + [row(2 * D)],
        out_shape=[jax.ShapeDtypeStruct((t, 3 * HG_W), BF16),
                   jax.ShapeDtypeStruct((t, 2 * HG_W), F32)]
                  + [at_shape(d) for d in AT_DILS]
                  + [jax.ShapeDtypeStruct((t, 2 * D), BF16)],
        scratch_shapes=[pltpu.VMEM((3 * AT_GW // LANES, tm, LANES), F32)],
        compiler_params=_cp(("parallel",)),
        name="inproj",
    )(x2, mod3, gain, *rope, w_hga, w_hgf, w_at, w_gt)


def _rope_tables(seq):
    half = ROPE_DIM // 2
    inv_freq = ROPE_THETA ** (-np.arange(half, dtype=np.float32) / half)
    ang = jnp.arange(seq, dtype=F32)[:, None] * jnp.asarray(inv_freq)[None, :]
    cos, sin = jnp.cos(ang), jnp.sin(ang)
    ones = jnp.ones((seq, AT_HD - ROPE_DIM), F32)
    zeros = jnp.zeros((seq, AT_HD - ROPE_DIM), F32)
    zh = jnp.zeros((seq, half), F32)
    c_head = jnp.concatenate([cos, cos, ones], axis=1)
    s1_head = jnp.concatenate([zh, sin, zeros], axis=1)
    s2_head = jnp.concatenate([-sin, zh, zeros], axis=1)
    rep = LANES // AT_HD
    return tuple(jnp.tile(a, (1, rep)) for a in (c_head, s1_head, s2_head))


def _hgrn_consts(reverse):
    c = HG_C
    t = np.arange(c)[:, None]
    u = np.arange(c)[None, :]
    tri = (u >= t) if reverse else (u <= t)
    sels = []
    for lvl in range(HG_LEVELS):
        blk = 2 << lvl
        start = (np.arange(c) // blk) * blk
        piv = start + (1 << lvl) - (0 if reverse else 1)
        sels.append(np.arange(c)[None, :] == piv[:, None])
    sel = np.concatenate(sels, axis=0)
    rep = lambda m, n: jnp.asarray(np.concatenate([m.astype(np.float32)] * n, axis=1), BF16)
    return rep(tri, 2), rep(sel, 3)


def _hgrn_kernel(*refs, reverse, final, n_chunks):
    if final:
        (lbl_ref, tri_ref, sel_ref, q_ref, f_ref, v_ref, ofw_ref, gh_ref, gain_ref,
         o_ref, st_ref, st0_ref) = refs
    else:
        (lbl_ref, tri_ref, sel_ref, q_ref, f_ref, v_ref,
         o_ref, st_ref, st0_ref) = refs
    c = HG_C

    @pl.when(pl.program_id(1) == 0)
    def _():
        st_ref[...] = jnp.zeros_like(st_ref)

    l0 = lbl_ref[0:1, :]
    l1 = lbl_ref[1:2, :]
    lm = jnp.maximum(l0, l1)
    e0 = jnp.exp(l0 - lm)
    e1 = jnp.exp(l1 - lm)
    lb = e0 / (e0 + e1)
    tri = tri_ref[...]
    piv_row = c // 2 if reverse else c // 2 - 1

    def gate_math(r0):
        ff = f_ref[pl.ds(r0, c), :]
        e = jnp.exp(-jnp.abs(ff))
        r = 1.0 / (1.0 + e)
        sg = jnp.where(ff >= 0.0, r, e * r)
        g = jnp.log(lb + (1.0 - lb) * sg)
        kin = (1.0 - lb) * (1.0 - sg)
        g_hi = g.astype(BF16)
        g_lo = (g - g_hi.astype(F32)).astype(BF16)
        b = _dot(tri, jnp.concatenate([g_hi, g_lo], axis=0))
        btot = b[0:1, :] if reverse else b[c - 1:c, :]
        return kin, b, btot

    ti = lax.broadcasted_iota(I32, (c, c), 0)
    si = lax.broadcasted_iota(I32, (c, c), 1)
    causal = (si >= ti) if reverse else (si <= ti)
    row = lax.broadcasted_iota(I32, (c, 1), 0)

    def att_robust(q, kin, b, piv):
        att = jnp.where(ti == si, _dot_nt(q.astype(BF16), kin.astype(BF16)), 0.0)
        for lvl in range(HG_LEVELS):
            p = piv[lvl * c:(lvl + 1) * c, :]
            bit = ((row >> lvl) & 1) == 1
            q_side = jnp.logical_not(bit) if reverse else bit
            qe = jnp.where(q_side, q * jnp.exp(jnp.minimum(b - p, 0.0)), 0.0).astype(BF16)
            ke = jnp.where(q_side, 0.0, kin * jnp.exp(jnp.minimum(p - b, 0.0))).astype(BF16)
            same = (ti >> (lvl + 1)) == (si >> (lvl + 1))
            att = att + jnp.where(same, _dot_nt(qe, ke), 0.0)
        return att

    heads = [slice(h * HG_DK, (h + 1) * HG_DK) for h in range(HG_H)]


    def emit(r0, outs):
        o = jnp.concatenate(outs, axis=1)
        if final:
            o = o + ofw_ref[pl.ds(r0, c), :]
            normed = []
            for sl in heads:
                ms = jnp.mean(o[:, sl] * o[:, sl], axis=-1, keepdims=True)
                normed.append(o[:, sl] * lax.rsqrt(ms + EPS))
            gh = gh_ref[pl.ds(r0, c), :].astype(F32)
            o = jnp.concatenate(normed, axis=1) * gain_ref[...] * (gh * jax.nn.sigmoid(gh))
        o_ref[pl.ds(r0, c), :] = o.astype(o_ref.dtype)

    def robust_step(i, carry):
        ci = (n_chunks - 1 - i) if reverse else i
        r0 = pl.multiple_of(ci * c, c)
        q = q_ref[pl.ds(r0, c), :].astype(F32)
        v = v_ref[pl.ds(r0, c), :]
        kin, b, btot = gate_math(r0)
        qe_all = (q * jnp.exp(b)).astype(BF16)
        kd_all = (kin * jnp.exp(btot - b)).astype(BF16)
        piv = _dot(sel_ref[...], jnp.concatenate(_split3(b), axis=0))
        outs = []
        for h, sl in enumerate(heads):
            att = att_robust(q[:, sl], kin[:, sl], b[:, sl], piv[:, sl])
            st = st_ref[h]
            outs.append(_dot_nt(qe_all[:, sl], st.astype(BF16)) + _dot(att.astype(BF16), v[:, sl]))
            st_ref[h] = st * jnp.exp(btot[:, sl]) + _dot_tn(v[:, sl], kd_all[:, sl])
        emit(r0, outs)
        return carry

    grp = HG_UNROLL

    def fast_group(i, bmin):
        gi = (n_chunks // grp - 1 - i) if reverse else i
        base = gi * (grp * c)
        order = range(grp - 1, -1, -1) if reverse else range(grp)
        chunks = []
        for j in order:
            r0 = pl.multiple_of(base + j * c, c)
            kin, b, btot = gate_math(r0)
            bp = b[piv_row:piv_row + 1, :]
            bmin = jnp.minimum(bmin, jnp.minimum(bp, btot - bp))
            to_piv = jnp.exp(b - bp)
            k_piv = kin / to_piv
            chunks.append(dict(
                r0=r0, v=v_ref[pl.ds(r0, c), :], dec=jnp.exp(btot), pdec=jnp.exp(bp),
                qe=q_ref[pl.ds(r0, c), :] * to_piv.astype(BF16),
                ke=k_piv.astype(BF16),
                kd=(k_piv * jnp.exp(btot - bp)).astype(BF16)))
        for ch in chunks:
            ch["att"] = [jnp.where(causal, _dot_nt(ch["qe"][:, sl], ch["ke"][:, sl]), 0.0).astype(BF16)
                         for sl in heads]
        for ch in chunks:
            ch["intra"] = [_dot(ch["att"][h], ch["v"][:, sl]) for h, sl in enumerate(heads)]
            ch["upd"] = [_dot_tn(ch["v"][:, sl], ch["kd"][:, sl]) for sl in heads]
        st = [st_ref[h] for h in range(HG_H)]
        for ch in chunks:
            outs = []
            for h, sl in enumerate(heads):
                st_piv = (st[h] * ch["pdec"][:, sl]).astype(BF16)
                outs.append(_dot_nt(ch["qe"][:, sl], st_piv) + ch["intra"][h])
                st[h] = st[h] * ch["dec"][:, sl] + ch["upd"][h]
            emit(ch["r0"], outs)
        for h in range(HG_H):
            st_ref[h] = st[h]
        return bmin

    st0_ref[...] = st_ref[...]
    bmin = lax.fori_loop(0, n_chunks // grp, fast_group, jnp.zeros((1, HG_W), F32))

    @pl.when(jnp.min(bmin) < HG_SAFE_LOGDECAY)
    def _():
        st_ref[...] = st0_ref[...]
        lax.fori_loop(0, n_chunks, robust_step, 0)


def _hgrn(lb_logits_d, hga, hgf, batch, seq, tc, reverse, o_fw=None, gain=None):
    final = o_fw is not None
    t = hga.shape[0]
    nblk = seq // tc
    tri, sel = _hgrn_consts(reverse)

    def rows(col):
        if reverse:
            return lambda b, j: (b * nblk + nblk - 1 - j, col)
        return lambda b, j: (b * nblk + j, col)

    blk = lambda col: pl.BlockSpec((tc, HG_W), rows(col))
    in_specs = [_const_spec((2, HG_W)), _const_spec(tri.shape), _const_spec(sel.shape),
                blk(0), blk(1 if reverse else 0), blk(1)]
    args = [lb_logits_d, tri, sel, hga, hgf, hga]
    if final:
        in_specs += [blk(0), blk(2), _const_spec((1, HG_W))]
        args += [o_fw, hga, gain]
    return pl.pallas_call(
        functools.partial(_hgrn_kernel, reverse=reverse, final=final, n_chunks=tc // HG_C),
        grid=(batch, nblk),
        in_specs=in_specs,
        out_specs=blk(0),
        out_shape=jax.ShapeDtypeStruct((t, HG_W), BF16 if final else F32),
        scratch_shapes=[pltpu.VMEM((HG_H, HG_DK, HG_DK), F32),
                        pltpu.VMEM((HG_H, HG_DK, HG_DK), F32)],
        compiler_params=_cp(("parallel", "arbitrary")),
        name="hgrn_bwd" if reverse else "hgrn_fwd",
    )(*args)


def _attn_kernel(q_ref, kl_ref, km_ref, kr_ref, vl_ref, vm_ref, vr_ref, o_ref, l_ref,
                 k_s, v_s, *, tq, ld):
    hw = AT_HALF
    n_res = q_ref.shape[0]
    for res in range(n_res):
        k_s[res, 0:hw, :] = kl_ref[res]
        k_s[res, hw:hw + tq, :] = km_ref[res]
        k_s[res, hw + tq:, :] = kr_ref[res]
        v_s[res, 0:hw, :] = vl_ref[res]
        v_s[res, hw:hw + tq, :] = vm_ref[res]
        v_s[res, hw + tq:, :] = vr_ref[res]
    qb = 2 * hw
    kb = 4 * hw
    base = pl.program_id(2) * tq
    lane_head = lax.broadcasted_iota(I32, (1, AT_GW), 1) // AT_HD
    q_sel = [jnp.where(lane_head == h, AT_HD ** -0.5, 0.0).astype(BF16) for h in range(AT_H)]
    out_head = lax.broadcasted_iota(I32, (qb, AT_GW), 1) // AT_HD
    rel = (lax.broadcasted_iota(I32, (AT_H * qb, kb), 0) % qb
           - lax.broadcasted_iota(I32, (AT_H * qb, kb), 1) + hw)
    band = jnp.where(jnp.abs(rel) <= hw, 0.0, NEG_BIG)
    key_col = lax.broadcasted_iota(I32, (1, kb), 1)
    for res, qs in [(res, qs) for res in range(n_res) for qs in range(0, tq, qb)]:
        q = q_ref[res, qs:qs + qb, :]
        kk = k_s[res, qs:qs + kb, :]
        vv = v_s[res, qs:qs + kb, :]
        q4 = jnp.concatenate([q * q_sel[h] for h in range(AT_H)], axis=0)
        s = _dot_nt(q4, kk) + band
        if qs == 0 or qs + qb == tq:
            kpos = base + qs - hw + key_col
            s = s + jnp.where((kpos >= 0) & (kpos < ld), 0.0, NEG_BIG)
        m = jnp.max(s, axis=-1, keepdims=True)
        p = jnp.exp(s - m)
        l = jnp.sum(p, axis=-1, keepdims=True)
        pv = _dot(p.astype(BF16), vv) * (1.0 / l)
        lse = m + jnp.log(l)
        o_all = jnp.zeros((qb, AT_GW), F32)
        l_all = jnp.zeros((qb, AT_GW), F32)
        for h in range(AT_H):
            o_all = jnp.where(out_head == h, pv[h * qb:(h + 1) * qb, :], o_all)
            l_all = jnp.where(out_head == h, lse[h * qb:(h + 1) * qb, :], l_all)
        o_ref[res, qs:qs + qb, :] = o_all.astype(o_ref.dtype)
        l_ref[res, qs:qs + qb, :] = l_all


def _attn_group(qkv):
    batch, dil, ld, _ = qkv.shape
    hw = AT_HALF
    tq = min(ld, SEQ_TILE)
    nq = ld // tq
    nh = ld // hw
    per = tq // hw
    n_res = min(dil, max(1, SEQ_TILE // tq))
    main = lambda part: pl.BlockSpec((None, n_res, tq, AT_GW), lambda b, r, j: (b, r, j, part))
    left = lambda part: pl.BlockSpec(
        (None, n_res, hw, AT_GW), lambda b, r, j: (b, r, jnp.maximum(j * per - 1, 0), part))
    right = lambda part: pl.BlockSpec(
        (None, n_res, hw, AT_GW), lambda b, r, j: (b, r, jnp.minimum((j + 1) * per, nh - 1), part))
    out_spec = pl.BlockSpec((None, n_res, tq, AT_GW), lambda b, r, j: (b, r, j, 0))
    return pl.pallas_call(
        functools.partial(_attn_kernel, tq=tq, ld=ld),
        grid=(batch, dil // n_res, nq),
        in_specs=[main(0), left(1), main(1), right(1), left(2), main(2), right(2)],
        out_specs=[out_spec, out_spec],
        out_shape=[jax.ShapeDtypeStruct((batch, dil, ld, AT_GW), BF16),
                   jax.ShapeDtypeStruct((batch, dil, ld, AT_GW), F32)],
        scratch_shapes=[pltpu.VMEM((n_res, tq + 2 * hw, AT_GW), BF16),
                        pltpu.VMEM((n_res, tq + 2 * hw, AT_GW), BF16)],
        compiler_params=_cp(("parallel", "parallel", "parallel")),
        name=f"attn_d{dil}",
    )(qkv, qkv, qkv, qkv, qkv, qkv, qkv)


def _merge_kernel(ohg_ref, o0_ref, o1_ref, o2_ref, l0_ref, l1_ref, l2_ref, gt_ref, x_ref, mod_ref,
                  w_hg_ref, w_at_ref, w_out_ref, npost_ref, npre_ref, w_r_ref, b_r_ref,
                  x1_ref, h2_ref, ids_ref, wts_ref, cnt_ref,
                  so1_ref, sl1_ref, so2_ref, sl2_ref):
    tm = x_ref.shape[0]
    sub = tm // MERGE_SUBTILES
    rows = [slice(i * sub, (i + 1) * sub) for i in range(MERGE_SUBTILES)]
    gt1 = mod_ref[0, 2:3, :]
    sh2 = mod_ref[0, 3:4, :]
    sc2 = mod_ref[0, 4:5, :]

    for src_ref, dst_ref in ((o1_ref, so1_ref), (l1_ref, sl1_ref), (o2_ref, so2_ref), (l2_ref, sl2_ref)):
        dil = src_ref.shape[0]
        for res in range(dil):
            vals = src_ref[res].astype(F32)
            for c in range(AT_GW // LANES):
                dst_ref[c, pl.ds(res, tm // dil, stride=dil), :] = vals[:, c * LANES:(c + 1) * LANES]

    def staged(ref, r):
        return jnp.concatenate([ref[c, r, :] for c in range(AT_GW // LANES)], axis=1)

    def branches(r):
        o0, l0 = o0_ref[0, r, :].astype(F32), l0_ref[0, r, :]
        o1, l1 = staged(so1_ref, r), staged(sl1_ref, r)
        o2, l2 = staged(so2_ref, r), staged(sl2_ref, r)
        m = jnp.maximum(jnp.maximum(l0, l1), l2)
        e0, e1, e2 = jnp.exp(l0 - m), jnp.exp(l1 - m), jnp.exp(l2 - m)
        oa = (e0 * o0 + e1 * o1 + e2 * o2) / (e0 + e1 + e2)
        return _dot(ohg_ref[r, :], w_hg_ref[...]), _dot(oa.astype(BF16), w_at_ref[...])

    def out_proj(r, b_hg, b_at):
        g_hg = jax.nn.sigmoid(gt_ref[r, 0:D])
        g_at = jax.nn.sigmoid(gt_ref[r, D:2 * D])
        merged = g_hg * b_hg.astype(BF16) + g_at * b_at.astype(BF16)
        return _dot(merged, w_out_ref[...])

    def norms(r, y):
        x1 = x_ref[r, :] + gt1 * _rms(y, npost_ref[...])
        x1_ref[r, :] = x1
        h2 = _rms(x1, npre_ref[...]) * (1.0 + sc2) + sh2
        h2_ref[r, :] = _pack_rows(h2)
        return _dot(h2.astype(BF16), w_r_ref[...]) + b_r_ref[...]

    def top_k(r, logits):
        lane = lax.broadcasted_iota(I32, logits.shape, 1)
        lane_f = lane.astype(F32)
        work = logits
        vals, idxs = [], []
        sel = jnp.zeros(logits.shape, F32)
        for _ in range(TOP_K):
            mk = jnp.max(work, axis=-1, keepdims=True)
            ik = jnp.min(jnp.where(work == mk, lane_f, float(LANES)), axis=-1, keepdims=True)
            hit = lane_f == ik
            sel = jnp.where(hit, 1.0, sel)
            work = jnp.where(hit, -jnp.inf, work)
            vals.append(mk)
            idxs.append(ik)
        es = [jnp.exp(v - vals[0]) for v in vals]
        den = es[0] + es[1] + es[2] + es[3]
        ids = jnp.zeros(logits.shape, F32)
        wts = jnp.zeros(logits.shape, F32)
        for k in range(TOP_K):
            ids = jnp.where(lane == k, idxs[k], ids)
            wts = jnp.where(lane == k, es[k] / den, wts)
        ids_ref[r, :] = ids.astype(I32)
        wts_ref[r, :] = wts
        return jnp.sum(sel, axis=0, keepdims=True)

    br = [branches(r) for r in rows]
    ys = [out_proj(r, *b) for r, b in zip(rows, br)]
    logits = [norms(r, y) for r, y in zip(rows, ys)]
    counts = [top_k(r, lg) for r, lg in zip(rows, logits)]
    cnt_ref[0] = functools.reduce(lambda a, b: a + b, counts)


def _merge(ohg, att_o, att_l, gates, x2, mod3, w_hg, w_at, w_out, npost, npre, w_r, b_r, seq, tm):
    t = x2.shape[0]
    nt = t // tm
    row = lambda w: pl.BlockSpec((tm, w), lambda i: (i, 0))
    slab_i = jax.ShapeDtypeStruct((t, LANES), I32)
    slab_f = jax.ShapeDtypeStruct((t, LANES), F32)
    nblk = seq // tm
    grouped = [pl.BlockSpec((None, d, tm // d, AT_GW),
                            lambda i: ((i * tm) // seq, 0, i % nblk, 0)) for d in AT_DILS]
    return pl.pallas_call(
        _merge_kernel,
        grid=(nt,),
        in_specs=[row(HG_W)] + grouped + grouped + [row(2 * D), row(D),
                  pl.BlockSpec((1, 6, D), lambda i: ((i * tm) // seq, 0, 0)),
                  _const_spec(w_hg.shape), _const_spec(w_at.shape), _const_spec(w_out.shape),
                  _const_spec((1, D)), _const_spec((1, D)),
                  _const_spec(w_r.shape), _const_spec(b_r.shape)],
        out_specs=[row(D), row(HALF_D), row(LANES), row(LANES),
                   pl.BlockSpec((1, 1, LANES), lambda i: (i, 0, 0))],
        out_shape=[jax.ShapeDtypeStruct((t, D), F32), jax.ShapeDtypeStruct((t, HALF_D), U32),
                   slab_i, slab_f, jax.ShapeDtypeStruct((nt, 1, LANES), F32)],
        scratch_shapes=[pltpu.VMEM((AT_GW // LANES, tm, LANES), F32)] * 4,
        compiler_params=_cp(("parallel",)),
        name="merge",
    )(ohg, *att_o, *att_l, gates, x2, mod3, w_hg, w_at, w_out, npost, npre, w_r, b_r)


def _route_kernel(ids_ref, base_ref, tril_ref, dest_ref):
    ids = ids_ref[...]
    lane = lax.broadcasted_iota(I32, ids.shape, 1)
    hits = [lane == ids[:, k:k + 1] for k in range(TOP_K)]
    sel = jnp.zeros(ids.shape, F32)
    for hit in hits:
        sel = jnp.where(hit, 1.0, sel)
    before = _dot(tril_ref[...], sel.astype(BF16)) + base_ref[0]
    dest = jnp.zeros(ids.shape, I32)
    for k, hit in enumerate(hits):
        rk = jnp.sum(jnp.where(hit, before, 0.0), axis=-1, keepdims=True)
        dest = jnp.where(lane == k, rk.astype(I32), dest)
    dest_ref[...] = dest


def _route(ids, tile_base, tm):
    t = ids.shape[0]
    tril = jnp.asarray(np.tril(np.ones((tm, tm), np.float32), -1), BF16)
    return pl.pallas_call(
        _route_kernel,
        grid=(t // tm,),
        in_specs=[pl.BlockSpec((tm, LANES), lambda i: (i, 0)),
                  pl.BlockSpec((1, 1, LANES), lambda i: (i, 0, 0)),
                  _const_spec((tm, tm))],
        out_specs=pl.BlockSpec((tm, LANES), lambda i: (i, 0)),
        out_shape=jax.ShapeDtypeStruct((t, LANES), I32),
        compiler_params=_cp(("parallel",)),
        name="route",
    )(ids, tile_base, tril)


def _moe_kernel(be_ref, rows_ref, x_ref, wg_ref, wl_ref, bg_ref, bl_ref, wd_ref, bd_ref, o_ref):
    def ffn(x_packed):
        x_lo, x_hi = (a.astype(BF16) for a in _unpack_rows(x_packed))
        gate = (_dot(x_lo, wg_ref[0, :HALF_D, :]) + _dot(x_hi, wg_ref[0, HALF_D:, :])
                + bg_ref[0])
        up = (_dot(x_lo, wl_ref[0, :HALF_D, :]) + _dot(x_hi, wl_ref[0, HALF_D:, :])
              + bl_ref[0])
        gate = jnp.minimum(gate, SWIGLU_LIMIT)
        up = jnp.clip(up, -SWIGLU_LIMIT, SWIGLU_LIMIT)
        act = (up + 1.0) * gate * jax.nn.sigmoid(SWIGLU_ALPHA * gate)
        return _pack_rows(_dot(act.astype(BF16), wd_ref[0]) + bd_ref[0])

    valid = rows_ref[pl.program_id(0)]
    half = x_ref.shape[0] // 2

    @pl.when(valid > half)
    def _():
        o_ref[...] = ffn(x_ref[...])

    @pl.when((valid > 0) & (valid <= half))
    def _():
        o_ref[:half, :] = ffn(x_ref[:half, :])
        o_ref[half:, :] = jnp.zeros((half, HALF_D), U32)

    @pl.when(valid == 0)
    def _():
        o_ref[...] = jnp.zeros_like(o_ref)


def _moe(block_expert, block_rows, xb, wg, wl, bg, bl, wd, bd, bm):
    r = xb.shape[0]
    nblk = r // bm
    ew = lambda shape: pl.BlockSpec((1,) + shape, lambda i, be, br: (be[i], 0, 0))
    return pl.pallas_call(
        _moe_kernel,
        grid_spec=pltpu.PrefetchScalarGridSpec(
            num_scalar_prefetch=2,
            grid=(nblk,),
            in_specs=[pl.BlockSpec((bm, HALF_D), lambda i, be, br: (i, 0)),
                      ew((D, D)), ew((D, D)), ew((1, D)), ew((1, D)), ew((D, D)), ew((1, D))],
            out_specs=pl.BlockSpec((bm, HALF_D), lambda i, be, br: (i, 0)),
        ),
        out_shape=jax.ShapeDtypeStruct((r, HALF_D), U32),
        compiler_params=_cp(("arbitrary",)),
        name="moe",
    )(block_expert, block_rows, xb, wg, wl, bg, bl, wd, bd)


def _final_kernel(y0_ref, y1_ref, y2_ref, y3_ref, wts_ref, x1_ref, mod_ref, npost_ref, o_ref):
    wts = wts_ref[...]
    y = jnp.zeros(x1_ref.shape, F32)
    for k, y_ref in enumerate((y0_ref, y1_ref, y2_ref, y3_ref)):
        y = y + wts[:, k:k + 1] * jnp.concatenate(_unpack_rows(y_ref[0]), axis=1)
    gt2 = mod_ref[0, 5:6, :]
    o_ref[...] = x1_ref[...] + gt2 * _rms(y, npost_ref[...])


def _final(yg, wts, x1, mod3, npost, seq, tm):
    t = x1.shape[0]
    slot = lambda k: pl.BlockSpec((1, tm, HALF_D), lambda i: (k, i, 0))
    return pl.pallas_call(
        _final_kernel,
        grid=(t // tm,),
        in_specs=[slot(0), slot(1), slot(2), slot(3),
                  pl.BlockSpec((tm, LANES), lambda i: (i, 0)),
                  pl.BlockSpec((tm, D), lambda i: (i, 0)),
                  pl.BlockSpec((1, 6, D), lambda i: ((i * tm) // seq, 0, 0)),
                  _const_spec((1, D))],
        out_specs=pl.BlockSpec((tm, D), lambda i: (i, 0)),
        out_shape=jax.ShapeDtypeStruct((t, D), F32),
        compiler_params=_cp(("parallel",)),
        name="final",
    )(yg, yg, yg, yg, wts, x1, mod3, npost)


def _tiles(seq):
    return dict(tm_in=min(seq, ROW_TILE), tc=min(seq, SEQ_TILE), tm_merge=min(seq, STREAM_TILE),
                bm=MOE_BLOCK, tm_final=min(seq, STREAM_TILE))


def _prep_weights(w_in, w_hg_out, w_att_out, w_out, w_router, b_router, w_up, b_up, w_down, b_down):
    w = w_in[0]
    hw = HG_W
    a0 = 5 * hw
    aw = AT_GW * len(AT_DILS)
    q_at, k_at, v_at = (w[:, a0 + i * aw:a0 + (i + 1) * aw] for i in range(3))
    grp = lambda m, g: m[:, g * AT_GW:(g + 1) * AT_GW]
    w_at = jnp.concatenate(
        [jnp.concatenate([grp(q_at, g), grp(k_at, g), grp(v_at, g)], axis=1)
         for g in range(len(AT_DILS))], axis=1)
    pad = LANES - N_EXP
    wg, wl = _deinterleave(w_up[0])
    return dict(
        wg=wg, wl=wl,
        w_hga=jnp.concatenate([w[:, 0:hw], w[:, 3 * hw:4 * hw], w[:, 4 * hw:5 * hw]], 1).astype(BF16),
        w_hgf=w[:, hw:3 * hw].astype(BF16),
        w_at=w_at.astype(BF16),
        w_gt=w[:, a0 + 3 * aw:].astype(BF16),
        w_hg_out=w_hg_out[0].astype(BF16),
        w_att_out=w_att_out[0].astype(BF16),
        w_out=w_out[0].astype(BF16),
        w_r=jnp.pad(w_router[0], ((0, 0), (0, pad))).astype(BF16),
        b_r=jnp.pad(b_router[0], (0, pad), constant_values=NEG_BIG).reshape(1, LANES),
        bg=b_up[0][:, 0::2].reshape(N_EXP, 1, D),
        bl=b_up[0][:, 1::2].reshape(N_EXP, 1, D),
        wd=w_down[0].astype(BF16),
        bd=b_down[0].reshape(N_EXP, 1, D),
    )


def _trunk(x, mod, wts, norm_pre, norm_post, lb_logits, hg_gain):
    batch, seq, width = x.shape
    t = batch * seq
    tl = _tiles(seq)
    assert width == D and seq % SEQ_TILE == 0 and seq // max(AT_DILS) >= 2 * AT_HALF
    x2 = x.reshape(t, D)
    mod3 = mod.reshape(batch, 6, D)
    rope = _rope_tables(seq)

    hga, hgf, at0, at1, at2, gates = _inproj(
        x2, mod3, norm_pre[0, 0].reshape(1, D), rope,
        wts["w_hga"], wts["w_hgf"], wts["w_at"], wts["w_gt"], seq, tl["tm_in"])

    o_fw = _hgrn(lb_logits[0], hga, hgf, batch, seq, tl["tc"], reverse=False)
    ohg = _hgrn(lb_logits[1], hga, hgf, batch, seq, tl["tc"], reverse=True,
                o_fw=o_fw, gain=hg_gain[0].reshape(1, HG_W))

    att = [_attn_group(a) for a in (at0, at1, at2)]

    x1, h2, ids, rw, cnt = _merge(
        ohg, [a[0] for a in att], [a[1] for a in att], gates, x2, mod3,
        wts["w_hg_out"], wts["w_att_out"], wts["w_out"],
        norm_post[0, 0].reshape(1, D), norm_pre[0, 1].reshape(1, D),
        wts["w_r"], wts["b_r"], seq, tl["tm_merge"])

    bm = tl["bm"]
    tmr = tl["tm_merge"]
    cnt_tiles = cnt.reshape(t // tmr, LANES).astype(I32)
    total = jnp.sum(cnt_tiles, axis=0)
    padded = (total + bm - 1) // bm * bm
    pad_end = jnp.cumsum(padded)
    pad_start = pad_end - padded
    tile_base = pad_start[None, :] + jnp.cumsum(cnt_tiles, axis=0) - cnt_tiles
    dest = _route(ids, tile_base.astype(F32).reshape(-1, 1, LANES), tmr)[:, :TOP_K]

    n_rows = t * TOP_K + N_EXP * bm
    n_blocks = n_rows // bm
    blk_start = jnp.arange(n_blocks, dtype=I32) * bm
    block_expert = jnp.minimum(
        jnp.sum(pad_end[None, :N_EXP] <= blk_start[:, None], axis=1), N_EXP - 1).astype(I32)
    block_rows = jnp.clip(pad_start[block_expert] + total[block_expert] - blk_start, 0, bm)
    block_rows = jnp.where(blk_start < pad_end[N_EXP - 1], block_rows, 0).astype(I32)

    dest_kt = dest.T
    xb = _sc_scatter_rows(h2, dest_kt, n_rows)
    yb = _moe(block_expert, block_rows, xb, wts["wg"], wts["wl"], wts["bg"], wts["bl"],
              wts["wd"], wts["bd"], bm)
    yg = _sc_gather_rows(yb, dest_kt.reshape(-1)).reshape(TOP_K, t, HALF_D)

    out = _final(yg, rw, x1, mod3, norm_post[0, 1].reshape(1, D), seq, tl["tm_final"])
    return out.reshape(batch, seq, D)


def kernel(x_prompt, x_sample, c_prompt, c_sample, w_ada, b_ada, norm_pre, norm_post, w_in,
           lb_logits, hg_norm_gain, w_hg_out, w_att_out, w_out, w_router, b_router,
           w_up, b_up, w_down, b_down):
    wts = _prep_weights(w_in, w_hg_out, w_att_out, w_out, w_router, b_router,
                        w_up, b_up, w_down, b_down)
    nb = c_prompt.shape[0]
    mod = _ada(jnp.concatenate([c_prompt, c_sample], axis=0), w_ada[0], b_ada[0])
    lb_l = lb_logits.astype(F32)
    y_p = _trunk(x_prompt, mod[:nb], wts, norm_pre, norm_post, lb_l, hg_norm_gain)
    y_s = _trunk(x_sample, mod[nb:], wts, norm_pre, norm_post, lb_l, hg_norm_gain)
    return (y_p, y_s)
```

```python
import functools

import numpy as np
import jax
import jax.numpy as jnp
from jax import lax
from jax.experimental import pallas as pl
from jax.experimental.pallas import tpu as pltpu
from jax.experimental.pallas import tpu_sc as plsc

F32 = jnp.float32
BF16 = jnp.bfloat16
I32 = jnp.int32
U32 = jnp.uint32

D = 1024
EPS = 1e-6
HG_H = 4
HG_DK = 128
HG_W = HG_H * HG_DK
HG_C = 64
HG_LEVELS = 6
HG_SAFE_LOGDECAY = -80.0
HG_UNROLL = 8
AT_DILS = (1, 4, 16)
AT_HALF = 64
AT_H = 4
AT_HD = 64
AT_GW = AT_H * AT_HD
ROPE_DIM = 16
ROPE_THETA = 500000.0
N_EXP = 32
TOP_K = 4
SWIGLU_LIMIT = 7.0
SWIGLU_ALPHA = 1.702
MERGE_SUBTILES = 4
ROW_TILE = 512
STREAM_TILE = 1024
SEQ_TILE = 2048
ADA_COL_TILE = 1536
MOE_BLOCK = 1024
LANES = 128
NEG_BIG = -1e30
HALF_D = D // 2
SC_CORES = 2
SC_SUBCORES = 16
SC_WORKERS = SC_CORES * SC_SUBCORES
SC_CHUNK = 128

VMEM_LIMIT = 56 * 1024 * 1024


def _cp(sem, vmem=VMEM_LIMIT):
    return pltpu.CompilerParams(dimension_semantics=sem, vmem_limit_bytes=vmem)


def _dot(a, b):
    return jnp.dot(a, b, preferred_element_type=F32)


def _dot_nt(a, b):
    return lax.dot_general(a, b, (((1,), (1,)), ((), ())), preferred_element_type=F32)


def _dot_tn(a, b):
    return lax.dot_general(a, b, (((0,), (0,)), ((), ())), preferred_element_type=F32)


def _split3(x):
    hi = x.astype(BF16)
    r = x - hi.astype(F32)
    mid = r.astype(BF16)
    lo = (r - mid.astype(F32)).astype(BF16)
    return hi, mid, lo


def _rms(x, gain):
    ms = jnp.mean(x * x, axis=-1, keepdims=True)
    return x * lax.rsqrt(ms + EPS) * gain


def _const_spec(shape):
    n = len(shape)
    return pl.BlockSpec(shape, lambda *_: (0,) * n)


def _pack_rows(y):
    bits = lambda a: lax.bitcast_convert_type(a.astype(BF16).astype(F32), U32)
    return (bits(y[:, :HALF_D]) >> 16) | (bits(y[:, HALF_D:]) & jnp.uint32(0xFFFF0000))


def _unpack_rows(w):
    lo = lax.bitcast_convert_type(w << 16, F32)
    hi = lax.bitcast_convert_type(w & jnp.uint32(0xFFFF0000), F32)
    return lo, hi


def _deinterleave_into(w_ref, p_ref, g_ref, l_ref):
    p = p_ref[...]
    n_grp = w_ref.shape[2] // (2 * LANES)
    for b in range(n_grp):
        blk = w_ref[0, :, b * 2 * LANES:(b + 1) * 2 * LANES].astype(BF16)
        r = _dot(blk, p)
        g_ref[:, b * LANES:(b + 1) * LANES] = r[:, :LANES].astype(BF16)
        l_ref[:, b * LANES:(b + 1) * LANES] = r[:, LANES:].astype(BF16)


def _pair_split_perm():
    perm = np.zeros((2 * LANES, 2 * LANES), np.float32)
    perm[2 * np.arange(LANES), np.arange(LANES)] = 1.0
    perm[2 * np.arange(LANES) + 1, LANES + np.arange(LANES)] = 1.0
    return jnp.asarray(perm, BF16)


def _sc_mesh():
    return plsc.VectorSubcoreMesh(core_axis_name="c", subcore_axis_name="s")


def _sc_worker():
    return lax.axis_index("s") * SC_CORES + lax.axis_index("c")


def _sc_scatter_rows(src, dest_kt, n_rows):
    t, w = src.shape
    n_slot = dest_kt.shape[0]
    assert t % (SC_WORKERS * SC_CHUNK) == 0
    nch = t // (SC_WORKERS * SC_CHUNK)
    idx = dest_kt.reshape(n_slot, SC_WORKERS, nch, SC_CHUNK).transpose(1, 0, 2, 3)

    @functools.partial(
        pl.kernel, mesh=_sc_mesh(),
        out_type=jax.ShapeDtypeStruct((n_rows, w), src.dtype),
        scratch_types=[pltpu.VMEM((n_slot, nch, SC_CHUNK), I32),
                       pltpu.VMEM((SC_CHUNK, w), src.dtype)],
        name="sc_scatter")
    def run(src_hbm, idx_hbm, out_hbm, idx_v, rows_v):
        wid = _sc_worker()
        pltpu.sync_copy(idx_hbm.at[wid], idx_v)

        @pl.loop(0, nch)
        def _(j):
            pltpu.sync_copy(src_hbm.at[pl.ds((wid * nch + j) * SC_CHUNK, SC_CHUNK)], rows_v)
            for k in range(n_slot):
                pltpu.sync_copy(rows_v, out_hbm.at[idx_v.at[k, j]])

    return run(src, idx)


def _sc_gather_rows(table, idx):
    n = idx.shape[0]
    w = table.shape[1]
    assert n % (SC_WORKERS * SC_CHUNK) == 0
    nch = n // (SC_WORKERS * SC_CHUNK)
    idx3 = idx.reshape(SC_WORKERS, nch, SC_CHUNK)

    @functools.partial(
        pl.kernel, mesh=_sc_mesh(),
        out_type=jax.ShapeDtypeStruct((n, w), table.dtype),
        scratch_types=[pltpu.VMEM((nch, SC_CHUNK), I32),
                       pltpu.VMEM((SC_CHUNK, w), table.dtype)],
        name="sc_gather")
    def run(table_hbm, idx_hbm, out_hbm, idx_v, rows_v):
        wid = _sc_worker()
        pltpu.sync_copy(idx_hbm.at[wid], idx_v)

        @pl.loop(0, nch)
        def _(j):
            pltpu.sync_copy(table_hbm.at[idx_v.at[j]], rows_v)
            pltpu.sync_copy(rows_v, out_hbm.at[pl.ds((wid * nch + j) * SC_CHUNK, SC_CHUNK)])

    return run(table, idx3)


def _ada_kernel(c_ref, w_ref, b_ref, o_ref):
    c = c_ref[...]
    a = c * jax.nn.sigmoid(c)
    w = w_ref[...]
    a_hi = a.astype(BF16)
    a_lo = (a - a_hi.astype(F32)).astype(BF16)
    w_hi = w.astype(BF16)
    w_lo = (w - w_hi.astype(F32)).astype(BF16)
    o_ref[...] = _dot(a_hi, w_hi) + _dot(a_lo, w_hi) + _dot(a_hi, w_lo) + b_ref[...]


def _ada(c, w, b):
    nb = c.shape[0]
    n = w.shape[1]
    tn = ADA_COL_TILE
    return pl.pallas_call(
        _ada_kernel,
        grid=(n // tn,),
        in_specs=[pl.BlockSpec((nb, D), lambda j: (0, 0)),
                  pl.BlockSpec((D, tn), lambda j: (0, j)),
                  pl.BlockSpec((1, tn), lambda j: (0, j))],
        out_specs=pl.BlockSpec((nb, tn), lambda j: (0, j)),
        out_shape=jax.ShapeDtypeStruct((nb, n), F32),
        compiler_params=_cp(("parallel",)),
        name="ada",
    )(c, w, b.reshape(1, n))


def _inproj_kernel(x_ref, mod_ref, gain_ref, cos_ref, s1_ref, s2_ref,
                   w_hga_ref, w_hgf_ref, w_at_ref, w_gt_ref,
                   hga_ref, hgf_ref, at0_ref, at1_ref, at2_ref, gt_ref, stage_ref):
    x = x_ref[...]
    tm = x.shape[0]
    sh = mod_ref[0, 0:1, :]
    sc = mod_ref[0, 1:2, :]
    h = _rms(x, gain_ref[...]) * (1.0 + sc) + sh
    hb = h.astype(BF16)
    hga_ref[...] = _dot(hb, w_hga_ref[...]).astype(BF16)
    hgf_ref[...] = _dot(hb, w_hgf_ref[...])
    cos = cos_ref[...]
    s1 = s1_ref[...]
    s2 = s2_ref[...]
    for g, o_ref in enumerate((at0_ref, at1_ref, at2_ref)):
        acc = _dot(hb, w_at_ref[:, g * 3 * AT_GW:(g + 1) * 3 * AT_GW])
        parts = []
        for j in range(2 * AT_GW // LANES):
            a = acc[:, j * LANES:(j + 1) * LANES]
            parts.append(a * cos + pltpu.roll(a, ROPE_DIM // 2, 1) * s1
                         + pltpu.roll(a, LANES - ROPE_DIM // 2, 1) * s2)
        parts.append(acc[:, 2 * AT_GW:])
        vals = jnp.concatenate(parts, axis=1)
        dil = AT_DILS[g]
        if dil == 1:
            o_ref[0, 0] = vals.astype(BF16)
        else:
            n_lt = 3 * AT_GW // LANES
            for c in range(n_lt):
                stage_ref[c] = vals[:, c * LANES:(c + 1) * LANES]
            for r in range(dil):
                o_ref[0, r] = jnp.concatenate(
                    [stage_ref[c, pl.ds(r, tm // dil, stride=dil), :] for c in range(n_lt)],
                    axis=1).astype(BF16)
    gt_ref[...] = _dot(hb, w_gt_ref[...]).astype(BF16)


def _inproj(x2, mod3, gain, rope, w_hga, w_hgf, w_at, w_gt, seq, tm):
    t = x2.shape[0]
    n_pos_blk = seq // tm
    row = lambda w: pl.BlockSpec((tm, w), lambda i: (i, 0))
    pos = pl.BlockSpec((tm, LANES), lambda i: (i % n_pos_blk, 0))
    batch = t // seq
    dilated = lambda d: pl.BlockSpec((1, d, tm // d, 3 * AT_GW),
                                     lambda i: ((i * tm) // seq, 0, i % n_pos_blk, 0))
    at_shape = lambda d: jax.ShapeDtypeStruct((batch, d, seq // d, 3 * AT_GW), BF16)
    return pl.pallas_call(
        _inproj_kernel,
        grid=(t // tm,),
        in_specs=[row(D),
                  pl.BlockSpec((1, 6, D), lambda i: ((i * tm) // seq, 0, 0)),
                  _const_spec((1, D)), pos, pos, pos,
                  _const_spec(w_hga.shape), _const_spec(w_hgf.shape),
                  _const_spec(w_at.shape), _const_spec(w_gt.shape)],
        out_specs=[row(3 * HG_W), row(2 * HG_W)] + [dilated(d) for d in AT_DILS] + [row(2 * D)],
        out_shape=[jax.ShapeDtypeStruct((t, 3 * HG_W), BF16),
                   jax.ShapeDtypeStruct((t, 2 * HG_W), F32)]
                  + [at_shape(d) for d in AT_DILS]
                  + [jax.ShapeDtypeStruct((t, 2 * D), BF16)],
        scratch_shapes=[pltpu.VMEM((3 * AT_GW // LANES, tm, LANES), F32)],
        compiler_params=_cp(("parallel",)),
        name="inproj",
    )(x2, mod3, gain, *rope, w_hga, w_hgf, w_at, w_gt)


def _rope_tables(seq):
    half = ROPE_DIM // 2
    inv_freq = ROPE_THETA ** (-np.arange(half, dtype=np.float32) / half)
    ang = jnp.arange(seq, dtype=F32)[:, None] * jnp.asarray(inv_freq)[None, :]
    cos, sin = jnp.cos(ang), jnp.sin(ang)
    ones = jnp.ones((seq, AT_HD - ROPE_DIM), F32)
    zeros = jnp.zeros((seq, AT_HD - ROPE_DIM), F32)
    zh = jnp.zeros((seq, half), F32)
    c_head = jnp.concatenate([cos, cos, ones], axis=1)
    s1_head = jnp.concatenate([zh, sin, zeros], axis=1)
    s2_head = jnp.concatenate([-sin, zh, zeros], axis=1)
    rep = LANES // AT_HD
    return tuple(jnp.tile(a, (1, rep)) for a in (c_head, s1_head, s2_head))


def _hgrn_consts(reverse):
    c = HG_C
    t = np.arange(c)[:, None]
    u = np.arange(c)[None, :]
    tri = (u >= t) if reverse else (u <= t)
    sels = []
    for lvl in range(HG_LEVELS):
        blk = 2 << lvl
        start = (np.arange(c) // blk) * blk
        piv = start + (1 << lvl) - (0 if reverse else 1)
        sels.append(np.arange(c)[None, :] == piv[:, None])
    sel = np.concatenate(sels, axis=0)
    rep = lambda m, n: jnp.asarray(np.concatenate([m.astype(np.float32)] * n, axis=1), BF16)
    return rep(tri, 2), rep(sel, 3)


def _hgrn_kernel(*refs, reverse, final, n_chunks):
    if final:
        (lbl_ref, tri_ref, sel_ref, q_ref, f_ref, v_ref, ofw_ref, gh_ref, gain_ref,
         o_ref, st_ref, st0_ref) = refs
    else:
        (lbl_ref, tri_ref, sel_ref, q_ref, f_ref, v_ref,
         o_ref, st_ref, st0_ref) = refs
    c = HG_C

    @pl.when(pl.program_id(1) == 0)
    def _():
        st_ref[...] = jnp.zeros_like(st_ref)

    l0 = lbl_ref[0:1, :]
    l1 = lbl_ref[1:2, :]
    lm = jnp.maximum(l0, l1)
    e0 = jnp.exp(l0 - lm)
    e1 = jnp.exp(l1 - lm)
    lb = e0 / (e0 + e1)
    tri = tri_ref[...]
    piv_row = c // 2 if reverse else c // 2 - 1

    def gate_math(r0):
        ff = f_ref[pl.ds(r0, c), :]
        e = jnp.exp(-jnp.abs(ff))
        r = 1.0 / (1.0 + e)
        sg = jnp.where(ff >= 0.0, r, e * r)
        g = jnp.log(lb + (1.0 - lb) * sg)
        kin = (1.0 - lb) * (1.0 - sg)
        g_hi = g.astype(BF16)
        g_lo = (g - g_hi.astype(F32)).astype(BF16)
        b = _dot(tri, jnp.concatenate([g_hi, g_lo], axis=0))
        btot = b[0:1, :] if reverse else b[c - 1:c, :]
        return kin, b, btot

    ti = lax.broadcasted_iota(I32, (c, c), 0)
    si = lax.broadcasted_iota(I32, (c, c), 1)
    causal = (si >= ti) if reverse else (si <= ti)
    row = lax.broadcasted_iota(I32, (c, 1), 0)

    def att_robust(q, kin, b, piv):
        att = jnp.where(ti == si, _dot_nt(q.astype(BF16), kin.astype(BF16)), 0.0)
        for lvl in range(HG_LEVELS):
            p = piv[lvl * c:(lvl + 1) * c, :]
            bit = ((row >> lvl) & 1) == 1
            q_side = jnp.logical_not(bit) if reverse else bit
            qe = jnp.where(q_side, q * jnp.exp(jnp.minimum(b - p, 0.0)), 0.0).astype(BF16)
            ke = jnp.where(q_side, 0.0, kin * jnp.exp(jnp.minimum(p - b, 0.0))).astype(BF16)
            same = (ti >> (lvl + 1)) == (si >> (lvl + 1))
            att = att + jnp.where(same, _dot_nt(qe, ke), 0.0)
        return att

    heads = [slice(h * HG_DK, (h + 1) * HG_DK) for h in range(HG_H)]


    def emit(r0, outs):
        o = jnp.concatenate(outs, axis=1)
        if final:
            o = o + ofw_ref[pl.ds(r0, c), :]
            normed = []
            for sl in heads:
                ms = jnp.mean(o[:, sl] * o[:, sl], axis=-1, keepdims=True)
                normed.append(o[:, sl] * lax.rsqrt(ms + EPS))
            gh = gh_ref[pl.ds(r0, c), :].astype(F32)
            o = jnp.concatenate(normed, axis=1) * gain_ref[...] * (gh * jax.nn.sigmoid(gh))
        o_ref[pl.ds(r0, c), :] = o.astype(o_ref.dtype)

    def robust_step(i, carry):
        ci = (n_chunks - 1 - i) if reverse else i
        r0 = pl.multiple_of(ci * c, c)
        q = q_ref[pl.ds(r0, c), :].astype(F32)
        v = v_ref[pl.ds(r0, c), :]
        kin, b, btot = gate_math(r0)
        qe_all = (q * jnp.exp(b)).astype(BF16)
        kd_all = (kin * jnp.exp(btot - b)).astype(BF16)
        piv = _dot(sel_ref[...], jnp.concatenate(_split3(b), axis=0))
        outs = []
        for h, sl in enumerate(heads):
            att = att_robust(q[:, sl], kin[:, sl], b[:, sl], piv[:, sl])
            st = st_ref[h]
            outs.append(_dot_nt(qe_all[:, sl], st.astype(BF16)) + _dot(att.astype(BF16), v[:, sl]))
            st_ref[h] = st * jnp.exp(btot[:, sl]) + _dot_tn(v[:, sl], kd_all[:, sl])
        emit(r0, outs)
        return carry

    grp = HG_UNROLL

    def fast_group(i, bmin):
        gi = (n_chunks // grp - 1 - i) if reverse else i
        base = gi * (grp * c)
        order = range(grp - 1, -1, -1) if reverse else range(grp)
        chunks = []
        for j in order:
            r0 = pl.multiple_of(base + j * c, c)
            kin, b, btot = gate_math(r0)
            bp = b[piv_row:piv_row + 1, :]
            bmin = jnp.minimum(bmin, jnp.minimum(bp, btot - bp))
            to_piv = jnp.exp(b - bp)
            k_piv = kin / to_piv
            chunks.append(dict(
                r0=r0, v=v_ref[pl.ds(r0, c), :], dec=jnp.exp(btot), pdec=jnp.exp(bp),
                qe=q_ref[pl.ds(r0, c), :] * to_piv.astype(BF16),
                ke=k_piv.astype(BF16),
                kd=(k_piv * jnp.exp(btot - bp)).astype(BF16)))
        for ch in chunks:
            ch["att"] = [jnp.where(causal, _dot_nt(ch["qe"][:, sl], ch["ke"][:, sl]), 0.0).astype(BF16)
                         for sl in heads]
        for ch in chunks:
            ch["intra"] = [_dot(ch["att"][h], ch["v"][:, sl]) for h, sl in enumerate(heads)]
            ch["upd"] = [_dot_tn(ch["v"][:, sl], ch["kd"][:, sl]) for sl in heads]
        st = [st_ref[h] for h in range(HG_H)]
        for ch in chunks:
            outs = []
            for h, sl in enumerate(heads):
                st_piv = (st[h] * ch["pdec"][:, sl]).astype(BF16)
                outs.append(_dot_nt(ch["qe"][:, sl], st_piv) + ch["intra"][h])
                st[h] = st[h] * ch["dec"][:, sl] + ch["upd"][h]
            emit(ch["r0"], outs)
        for h in range(HG_H):
            st_ref[h] = st[h]
        return bmin

    st0_ref[...] = st_ref[...]
    bmin = lax.fori_loop(0, n_chunks // grp, fast_group, jnp.zeros((1, HG_W), F32))

    @pl.when(jnp.min(bmin) < HG_SAFE_LOGDECAY)
    def _():
        st_ref[...] = st0_ref[...]
        lax.fori_loop(0, n_chunks, robust_step, 0)


def _hgrn(lb_logits_d, hga, hgf, batch, seq, tc, reverse, o_fw=None, gain=None):
    final = o_fw is not None
    t = hga.shape[0]
    nblk = seq // tc
    tri, sel = _hgrn_consts(reverse)

    def rows(col):
        if reverse:
            return lambda b, j: (b * nblk + nblk - 1 - j, col)
        return lambda b, j: (b * nblk + j, col)

    blk = lambda col: pl.BlockSpec((tc, HG_W), rows(col))
    in_specs = [_const_spec((2, HG_W)), _const_spec(tri.shape), _const_spec(sel.shape),
                blk(0), blk(1 if reverse else 0), blk(1)]
    args = [lb_logits_d, tri, sel, hga, hgf, hga]
    if final:
        in_specs += [blk(0), blk(2), _const_spec((1, HG_W))]
        args += [o_fw, hga, gain]
    return pl.pallas_call(
        functools.partial(_hgrn_kernel, reverse=reverse, final=final, n_chunks=tc // HG_C),
        grid=(batch, nblk),
        in_specs=in_specs,
        out_specs=blk(0),
        out_shape=jax.ShapeDtypeStruct((t, HG_W), BF16 if final else F32),
        scratch_shapes=[pltpu.VMEM((HG_H, HG_DK, HG_DK), F32),
                        pltpu.VMEM((HG_H, HG_DK, HG_DK), F32)],
        compiler_params=_cp(("parallel", "arbitrary")),
        name="hgrn_bwd" if reverse else "hgrn_fwd",
    )(*args)


def _attn_kernel(q_ref, kl_ref, km_ref, kr_ref, vl_ref, vm_ref, vr_ref, o_ref, l_ref,
                 k_s, v_s, *, tq, ld):
    hw = AT_HALF
    n_res = q_ref.shape[0]
    for res in range(n_res):
        k_s[res, 0:hw, :] = kl_ref[res]
        k_s[res, hw:hw + tq, :] = km_ref[res]
        k_s[res, hw + tq:, :] = kr_ref[res]
        v_s[res, 0:hw, :] = vl_ref[res]
        v_s[res, hw:hw + tq, :] = vm_ref[res]
        v_s[res, hw + tq:, :] = vr_ref[res]
    qb = 2 * hw
    kb = 4 * hw
    base = pl.program_id(2) * tq
    lane_head = lax.broadcasted_iota(I32, (1, AT_GW), 1) // AT_HD
    q_sel = [jnp.where(lane_head == h, AT_HD ** -0.5, 0.0).astype(BF16) for h in range(AT_H)]
    out_head = lax.broadcasted_iota(I32, (qb, AT_GW), 1) // AT_HD
    rel = (lax.broadcasted_iota(I32, (AT_H * qb, kb), 0) % qb
           - lax.broadcasted_iota(I32, (AT_H * qb, kb), 1) + hw)
    band = jnp.where(jnp.abs(rel) <= hw, 0.0, NEG_BIG)
    key_col = lax.broadcasted_iota(I32, (1, kb), 1)
    for res, qs in [(res, qs) for res in range(n_res) for qs in range(0, tq, qb)]:
        q = q_ref[res, qs:qs + qb, :]
        kk = k_s[res, qs:qs + kb, :]
        vv = v_s[res, qs:qs + kb, :]
        q4 = jnp.concatenate([q * q_sel[h] for h in range(AT_H)], axis=0)
        s = _dot_nt(q4, kk) + band
        if qs == 0 or qs + qb == tq:
            kpos = base + qs - hw + key_col
            s = s + jnp.where((kpos >= 0) & (kpos < ld), 0.0, NEG_BIG)
        m = jnp.max(s, axis=-1, keepdims=True)
        p = jnp.exp(s - m)
        l = jnp.sum(p, axis=-1, keepdims=True)
        pv = _dot(p.astype(BF16), vv) * (1.0 / l)
        lse = m + jnp.log(l)
        o_all = jnp.zeros((qb, AT_GW), F32)
        l_all = jnp.zeros((qb, AT_GW), F32)
        for h in range(AT_H):
            o_all = jnp.where(out_head == h, pv[h * qb:(h + 1) * qb, :], o_all)
            l_all = jnp.where(out_head == h, lse[h * qb:(h + 1) * qb, :], l_all)
        o_ref[res, qs:qs + qb, :] = o_all.astype(o_ref.dtype)
        l_ref[res, qs:qs + qb, :] = l_all


def _attn_group(qkv):
    batch, dil, ld, _ = qkv.shape
    hw = AT_HALF
    tq = min(ld, SEQ_TILE)
    nq = ld // tq
    nh = ld // hw
    per = tq // hw
    n_res = min(dil, max(1, SEQ_TILE // tq))
    main = lambda part: pl.BlockSpec((None, n_res, tq, AT_GW), lambda b, r, j: (b, r, j, part))
    left = lambda part: pl.BlockSpec(
        (None, n_res, hw, AT_GW), lambda b, r, j: (b, r, jnp.maximum(j * per - 1, 0), part))
    right = lambda part: pl.BlockSpec(
        (None, n_res, hw, AT_GW), lambda b, r, j: (b, r, jnp.minimum((j + 1) * per, nh - 1), part))
    out_spec = pl.BlockSpec((None, n_res, tq, AT_GW), lambda b, r, j: (b, r, j, 0))
    return pl.pallas_call(
        functools.partial(_attn_kernel, tq=tq, ld=ld),
        grid=(batch, dil // n_res, nq),
        in_specs=[main(0), left(1), main(1), right(1), left(2), main(2), right(2)],
        out_specs=[out_spec, out_spec],
        out_shape=[jax.ShapeDtypeStruct((batch, dil, ld, AT_GW), BF16),
                   jax.ShapeDtypeStruct((batch, dil, ld, AT_GW), F32)],
        scratch_shapes=[pltpu.VMEM((n_res, tq + 2 * hw, AT_GW), BF16),
                        pltpu.VMEM((n_res, tq + 2 * hw, AT_GW), BF16)],
        compiler_params=_cp(("parallel", "parallel", "parallel")),
        name=f"attn_d{dil}",
    )(qkv, qkv, qkv, qkv, qkv, qkv, qkv)


def _merge_kernel(ohg_ref, o0_ref, o1_ref, o2_ref, l0_ref, l1_ref, l2_ref, gt_ref, x_ref, mod_ref,
                  w_hg_ref, w_at_ref, w_out_ref, npost_ref, npre_ref, w_r_ref, b_r_ref,
                  x1_ref, h2_ref, ids_ref, wts_ref, cnt_ref,
                  so1_ref, sl1_ref, so2_ref, sl2_ref):
    tm = x_ref.shape[0]
    sub = tm // MERGE_SUBTILES
    rows = [slice(i * sub, (i + 1) * sub) for i in range(MERGE_SUBTILES)]
    gt1 = mod_ref[0, 2:3, :]
    sh2 = mod_ref[0, 3:4, :]
    sc2 = mod_ref[0, 4:5, :]

    for src_ref, dst_ref in ((o1_ref, so1_ref), (l1_ref, sl1_ref), (o2_ref, so2_ref), (l2_ref, sl2_ref)):
        dil = src_ref.shape[0]
        for res in range(dil):
            vals = src_ref[res].astype(F32)
            for c in range(AT_GW // LANES):
                dst_ref[c, pl.ds(res, tm // dil, stride=dil), :] = vals[:, c * LANES:(c + 1) * LANES]

    def staged(ref, r):
        return jnp.concatenate([ref[c, r, :] for c in range(AT_GW // LANES)], axis=1)

    def branches(r):
        o0, l0 = o0_ref[0, r, :].astype(F32), l0_ref[0, r, :]
        o1, l1 = staged(so1_ref, r), staged(sl1_ref, r)
        o2, l2 = staged(so2_ref, r), staged(sl2_ref, r)
        m = jnp.maximum(jnp.maximum(l0, l1), l2)
        e0, e1, e2 = jnp.exp(l0 - m), jnp.exp(l1 - m), jnp.exp(l2 - m)
        oa = (e0 * o0 + e1 * o1 + e2 * o2) / (e0 + e1 + e2)
        return _dot(ohg_ref[r, :], w_hg_ref[...]), _dot(oa.astype(BF16), w_at_ref[...])

    def out_proj(r, b_hg, b_at):
        g_hg = jax.nn.sigmoid(gt_ref[r, 0:D])
        g_at = jax.nn.sigmoid(gt_ref[r, D:2 * D])
        merged = g_hg * b_hg.astype(BF16) + g_at * b_at.astype(BF16)
        return _dot(merged, w_out_ref[...])

    def norms(r, y):
        x1 = x_ref[r, :] + gt1 * _rms(y, npost_ref[...])
        x1_ref[r, :] = x1
        h2 = _rms(x1, npre_ref[...]) * (1.0 + sc2) + sh2
        h2_ref[r, :] = _pack_rows(h2)
        return _dot(h2.astype(BF16), w_r_ref[...]) + b_r_ref[...]

    def top_k(r, logits):
        lane = lax.broadcasted_iota(I32, logits.shape, 1)
        lane_f = lane.astype(F32)
        work = logits
        vals, idxs = [], []
        sel = jnp.zeros(logits.shape, F32)
        for _ in range(TOP_K):
            mk = jnp.max(work, axis=-1, keepdims=True)
            ik = jnp.min(jnp.where(work == mk, lane_f, float(LANES)), axis=-1, keepdims=True)
            hit = lane_f == ik
            sel = jnp.where(hit, 1.0, sel)
            work = jnp.where(hit, -jnp.inf, work)
            vals.append(mk)
            idxs.append(ik)
        es = [jnp.exp(v - vals[0]) for v in vals]
        den = es[0] + es[1] + es[2] + es[3]
        ids = jnp.zeros(logits.shape, F32)
        wts = jnp.zeros(logits.shape, F32)
        for k in range(TOP_K):
            ids = jnp.where(lane == k, idxs[k], ids)
            wts = jnp.where(lane == k, es[k] / den, wts)
        ids_ref[r, :] = ids.astype(I32)
        wts_ref[r, :] = wts
        return jnp.sum(sel, axis=0, keepdims=True)

    br = [branches(r) for r in rows]
    ys = [out_proj(r, *b) for r, b in zip(rows, br)]
    logits = [norms(r, y) for r, y in zip(rows, ys)]
    counts = [top_k(r, lg) for r, lg in zip(rows, logits)]
    cnt_ref[0] = functools.reduce(lambda a, b: a + b, counts)


def _merge(ohg, att_o, att_l, gates, x2, mod3, w_hg, w_at, w_out, npost, npre, w_r, b_r, seq, tm):
    t = x2.shape[0]
    nt = t // tm
    row = lambda w: pl.BlockSpec((tm, w), lambda i: (i, 0))
    slab_i = jax.ShapeDtypeStruct((t, LANES), I32)
    slab_f = jax.ShapeDtypeStruct((t, LANES), F32)
    nblk = seq // tm
    grouped = [pl.BlockSpec((None, d, tm // d, AT_GW),
                            lambda i: ((i * tm) // seq, 0, i % nblk, 0)) for d in AT_DILS]
    return pl.pallas_call(
        _merge_kernel,
        grid=(nt,),
        in_specs=[row(HG_W)] + grouped + grouped + [row(2 * D), row(D),
                  pl.BlockSpec((1, 6, D), lambda i: ((i * tm) // seq, 0, 0)),
                  _const_spec(w_hg.shape), _const_spec(w_at.shape), _const_spec(w_out.shape),
                  _const_spec((1, D)), _const_spec((1, D)),
                  _const_spec(w_r.shape), _const_spec(b_r.shape)],
        out_specs=[row(D), row(HALF_D), row(LANES), row(LANES),
                   pl.BlockSpec((1, 1, LANES), lambda i: (i, 0, 0))],
        out_shape=[jax.ShapeDtypeStruct((t, D), F32), jax.ShapeDtypeStruct((t, HALF_D), U32),
                   slab_i, slab_f, jax.ShapeDtypeStruct((nt, 1, LANES), F32)],
        scratch_shapes=[pltpu.VMEM((AT_GW // LANES, tm, LANES), F32)] * 4,
        compiler_params=_cp(("parallel",)),
        name="merge",
    )(ohg, *att_o, *att_l, gates, x2, mod3, w_hg, w_at, w_out, npost, npre, w_r, b_r)


def _route_kernel(ids_ref, base_ref, tril_ref, dest_ref):
    ids = ids_ref[...]
    lane = lax.broadcasted_iota(I32, ids.shape, 1)
    hits = [lane == ids[:, k:k + 1] for k in range(TOP_K)]
    sel = jnp.zeros(ids.shape, F32)
    for hit in hits:
        sel = jnp.where(hit, 1.0, sel)
    before = _dot(tril_ref[...], sel.astype(BF16)) + base_ref[0]
    dest = jnp.zeros(ids.shape, I32)
    for k, hit in enumerate(hits):
        rk = jnp.sum(jnp.where(hit, before, 0.0), axis=-1, keepdims=True)
        dest = jnp.where(lane == k, rk.astype(I32), dest)
    dest_ref[...] = dest


def _route(ids, tile_base, tm):
    t = ids.shape[0]
    tril = jnp.asarray(np.tril(np.ones((tm, tm), np.float32), -1), BF16)
    return pl.pallas_call(
        _route_kernel,
        grid=(t // tm,),
        in_specs=[pl.BlockSpec((tm, LANES), lambda i: (i, 0)),
                  pl.BlockSpec((1, 1, LANES), lambda i: (i, 0, 0)),
                  _const_spec((tm, tm))],
        out_specs=pl.BlockSpec((tm, LANES), lambda i: (i, 0)),
        out_shape=jax.ShapeDtypeStruct((t, LANES), I32),
        compiler_params=_cp(("parallel",)),
        name="route",
    )(ids, tile_base, tril)


def _moe_kernel(be_ref, rows_ref, x_ref, wup_ref, bg_ref, bl_ref, wdn_ref, bd_ref, perm_ref, o_ref,
                wg_s, wl_s, wd_s):
    i = pl.program_id(0)

    @pl.when((i == 0) | (be_ref[i] != be_ref[jnp.maximum(i - 1, 0)]))
    def _():
        _deinterleave_into(wup_ref, perm_ref, wg_s, wl_s)
        wd_s[...] = wdn_ref[0].astype(BF16)

    def ffn(x_packed):
        x_lo, x_hi = (a.astype(BF16) for a in _unpack_rows(x_packed))
        gate = _dot(x_lo, wg_s[:HALF_D, :]) + _dot(x_hi, wg_s[HALF_D:, :]) + bg_ref[0]
        up = _dot(x_lo, wl_s[:HALF_D, :]) + _dot(x_hi, wl_s[HALF_D:, :]) + bl_ref[0]
        gate = jnp.minimum(gate, SWIGLU_LIMIT)
        up = jnp.clip(up, -SWIGLU_LIMIT, SWIGLU_LIMIT)
        act = (up + 1.0) * gate * jax.nn.sigmoid(SWIGLU_ALPHA * gate)
        return _pack_rows(_dot(act.astype(BF16), wd_s[...]) + bd_ref[0])

    valid = rows_ref[i]
    half = x_ref.shape[0] // 2

    @pl.when(valid > half)
    def _():
        o_ref[...] = ffn(x_ref[...])

    @pl.when((valid > 0) & (valid <= half))
    def _():
        o_ref[:half, :] = ffn(x_ref[:half, :])
        o_ref[half:, :] = jnp.zeros((half, HALF_D), U32)

    @pl.when(valid == 0)
    def _():
        o_ref[...] = jnp.zeros_like(o_ref)


def _moe(block_expert, block_rows, xb, w_up, bg, bl, w_down, bd, bm):
    r = xb.shape[0]
    nblk = r // bm
    perm = _pair_split_perm()
    ew = lambda shape: pl.BlockSpec((1,) + shape, lambda i, be, br: (be[i], 0, 0))
    return pl.pallas_call(
        _moe_kernel,
        grid_spec=pltpu.PrefetchScalarGridSpec(
            num_scalar_prefetch=2,
            grid=(nblk,),
            in_specs=[pl.BlockSpec((bm, HALF_D), lambda i, be, br: (i, 0)),
                      ew((D, 2 * D)), ew((1, D)), ew((1, D)), ew((D, D)), ew((1, D)),
                      pl.BlockSpec(perm.shape, lambda i, be, br: (0, 0))],
            out_specs=pl.BlockSpec((bm, HALF_D), lambda i, be, br: (i, 0)),
            scratch_shapes=[pltpu.VMEM((D, D), BF16)] * 3,
        ),
        out_shape=jax.ShapeDtypeStruct((r, HALF_D), U32),
        compiler_params=_cp(("arbitrary",)),
        name="moe",
    )(block_expert, block_rows, xb, w_up, bg, bl, w_down, bd, perm)


def _final_kernel(y0_ref, y1_ref, y2_ref, y3_ref, wts_ref, x1_ref, mod_ref, npost_ref, o_ref):
    wts = wts_ref[...]
    y = jnp.zeros(x1_ref.shape, F32)
    for k, y_ref in enumerate((y0_ref, y1_ref, y2_ref, y3_ref)):
        y = y + wts[:, k:k + 1] * jnp.concatenate(_unpack_rows(y_ref[0]), axis=1)
    gt2 = mod_ref[0, 5:6, :]
    o_ref[...] = x1_ref[...] + gt2 * _rms(y, npost_ref[...])


def _final(yg, wts, x1, mod3, npost, seq, tm):
    t = x1.shape[0]
    slot = lambda k: pl.BlockSpec((1, tm, HALF_D), lambda i: (k, i, 0))
    return pl.pallas_call(
        _final_kernel,
        grid=(t // tm,),
        in_specs=[slot(0), slot(1), slot(2), slot(3),
                  pl.BlockSpec((tm, LANES), lambda i: (i, 0)),
                  pl.BlockSpec((tm, D), lambda i: (i, 0)),
                  pl.BlockSpec((1, 6, D), lambda i: ((i * tm) // seq, 0, 0)),
                  _const_spec((1, D))],
        out_specs=pl.BlockSpec((tm, D), lambda i: (i, 0)),
        out_shape=jax.ShapeDtypeStruct((t, D), F32),
        compiler_params=_cp(("parallel",)),
        name="final",
    )(yg, yg, yg, yg, wts, x1, mod3, npost)


def _tiles(seq):
    return dict(tm_in=min(seq, ROW_TILE), tc=min(seq, SEQ_TILE), tm_merge=min(seq, STREAM_TILE),
                bm=MOE_BLOCK, tm_final=min(seq, STREAM_TILE))


def _prep_weights(w_in, w_hg_out, w_att_out, w_out, w_router, b_router, w_up, b_up, w_down, b_down):
    w = w_in[0]
    hw = HG_W
    a0 = 5 * hw
    aw = AT_GW * len(AT_DILS)
    q_at, k_at, v_at = (w[:, a0 + i * aw:a0 + (i + 1) * aw] for i in range(3))
    grp = lambda m, g: m[:, g * AT_GW:(g + 1) * AT_GW]
    w_at = jnp.concatenate(
        [jnp.concatenate([grp(q_at, g), grp(k_at, g), grp(v_at, g)], axis=1)
         for g in range(len(AT_DILS))], axis=1)
    pad = LANES - N_EXP
    return dict(
        w_up=w_up[0],
        w_hga=jnp.concatenate([w[:, 0:hw], w[:, 3 * hw:4 * hw], w[:, 4 * hw:5 * hw]], 1).astype(BF16),
        w_hgf=w[:, hw:3 * hw].astype(BF16),
        w_at=w_at.astype(BF16),
        w_gt=w[:, a0 + 3 * aw:].astype(BF16),
        w_hg_out=w_hg_out[0].astype(BF16),
        w_att_out=w_att_out[0].astype(BF16),
        w_out=w_out[0].astype(BF16),
        w_r=jnp.pad(w_router[0], ((0, 0), (0, pad))).astype(BF16),
        b_r=jnp.pad(b_router[0], (0, pad), constant_values=NEG_BIG).reshape(1, LANES),
        bg=b_up[0][:, 0::2].reshape(N_EXP, 1, D),
        bl=b_up[0][:, 1::2].reshape(N_EXP, 1, D),
        w_down=w_down[0],
        bd=b_down[0].reshape(N_EXP, 1, D),
    )


def _trunk(x, mod, wts, norm_pre, norm_post, lb_logits, hg_gain):
    batch, seq, width = x.shape
    t = batch * seq
    tl = _tiles(seq)
    assert width == D and seq % SEQ_TILE == 0 and seq // max(AT_DILS) >= 2 * AT_HALF
    x2 = x.reshape(t, D)
    mod3 = mod.reshape(batch, 6, D)
    rope = _rope_tables(seq)

    hga, hgf, at0, at1, at2, gates = _inproj(
        x2, mod3, norm_pre[0, 0].reshape(1, D), rope,
        wts["w_hga"], wts["w_hgf"], wts["w_at"], wts["w_gt"], seq, tl["tm_in"])

    o_fw = _hgrn(lb_logits[0], hga, hgf, batch, seq, tl["tc"], reverse=False)
    ohg = _hgrn(lb_logits[1], hga, hgf, batch, seq, tl["tc"], reverse=True,
                o_fw=o_fw, gain=hg_gain[0].reshape(1, HG_W))

    att = [_attn_group(a) for a in (at0, at1, at2)]

    x1, h2, ids, rw, cnt = _merge(
        ohg, [a[0] for a in att], [a[1] for a in att], gates, x2, mod3,
        wts["w_hg_out"], wts["w_att_out"], wts["w_out"],
        norm_post[0, 0].reshape(1, D), norm_pre[0, 1].reshape(1, D),
        wts["w_r"], wts["b_r"], seq, tl["tm_merge"])

    bm = tl["bm"]
    tmr = tl["tm_merge"]
    cnt_tiles = cnt.reshape(t // tmr, LANES).astype(I32)
    total = jnp.sum(cnt_tiles, axis=0)
    padded = (total + bm - 1) // bm * bm
    pad_end = jnp.cumsum(padded)
    pad_start = pad_end - padded
    tile_base = pad_start[None, :] + jnp.cumsum(cnt_tiles, axis=0) - cnt_tiles
    dest = _route(ids, tile_base.astype(F32).reshape(-1, 1, LANES), tmr)[:, :TOP_K]

    n_rows = t * TOP_K + N_EXP * bm
    n_blocks = n_rows // bm
    blk_start = jnp.arange(n_blocks, dtype=I32) * bm
    block_expert = jnp.minimum(
        jnp.sum(pad_end[None, :N_EXP] <= blk_start[:, None], axis=1), N_EXP - 1).astype(I32)
    block_rows = jnp.clip(pad_start[block_expert] + total[block_expert] - blk_start, 0, bm)
    block_rows = jnp.where(blk_start < pad_end[N_EXP - 1], block_rows, 0).astype(I32)

    dest_kt = dest.T
    xb = _sc_scatter_rows(h2, dest_kt, n_rows)
    yb = _moe(block_expert, block_rows, xb, wts["w_up"], wts["bg"], wts["bl"],
              wts["w_down"], wts["bd"], bm)
    yg = _sc_gather_rows(yb, dest_kt.reshape(-1)).reshape(TOP_K, t, HALF_D)

    out = _final(yg, rw, x1, mod3, norm_post[0, 1].reshape(1, D), seq, tl["tm_final"])
    return out.reshape(batch, seq, D)


def kernel(x_prompt, x_sample, c_prompt, c_sample, w_ada, b_ada, norm_pre, norm_post, w_in,
           lb_logits, hg_norm_gain, w_hg_out, w_att_out, w_out, w_router, b_router,
           w_up, b_up, w_down, b_down):
    wts = _prep_weights(w_in, w_hg_out, w_att_out, w_out, w_router, b_router,
                        w_up, b_up, w_down, b_down)
    nb = c_prompt.shape[0]
    mod = _ada(jnp.concatenate([c_prompt, c_sample], axis=0), w_ada[0], b_ada[0])
    lb_l = lb_logits.astype(F32)
    y_p = _trunk(x_prompt, mod[:nb], wts, norm_pre, norm_post, lb_l, hg_norm_gain)
    y_s = _trunk(x_sample, mod[nb:], wts, norm_pre, norm_post, lb_l, hg_norm_gain)
    return (y_p, y_s)
```

```python
import functools

import numpy as np
import jax
import jax.numpy as jnp
from jax import lax
from jax.experimental import pallas as pl
from jax.experimental.pallas import tpu as pltpu
from jax.experimental.pallas import tpu_sc as plsc

F32 = jnp.float32
BF16 = jnp.bfloat16
I32 = jnp.int32
U32 = jnp.uint32

D = 1024
EPS = 1e-6
HG_H = 4
HG_DK = 128
HG_W = HG_H * HG_DK
HG_C = 64
HG_LEVELS = 6
HG_SAFE_LOGDECAY = -80.0
HG_UNROLL = 8
AT_DILS = (1, 4, 16)
AT_HALF = 64
AT_H = 4
AT_HD = 64
AT_GW = AT_H * AT_HD
ROPE_DIM = 16
ROPE_THETA = 500000.0
N_EXP = 32
TOP_K = 4
SWIGLU_LIMIT = 7.0
SWIGLU_ALPHA = 1.702
MERGE_SUBTILES = 4
ROW_TILE = 512
STREAM_TILE = 1024
SEQ_TILE = 2048
ADA_COL_TILE = 1536
MOE_BLOCK = 1024
MOE_W_SLABS = 4
LANES = 128
NEG_BIG = -1e30
HALF_D = D // 2
SC_CORES = 2
SC_SUBCORES = 16
SC_WORKERS = SC_CORES * SC_SUBCORES
SC_CHUNK = 128

VMEM_LIMIT = 56 * 1024 * 1024


def _cp(sem, vmem=VMEM_LIMIT):
    return pltpu.CompilerParams(dimension_semantics=sem, vmem_limit_bytes=vmem)


def _dot(a, b):
    return jnp.dot(a, b, preferred_element_type=F32)


def _dot_nt(a, b):
    return lax.dot_general(a, b, (((1,), (1,)), ((), ())), preferred_element_type=F32)


def _dot_tn(a, b):
    return lax.dot_general(a, b, (((0,), (0,)), ((), ())), preferred_element_type=F32)


def _split3(x):
    hi = x.astype(BF16)
    r = x - hi.astype(F32)
    mid = r.astype(BF16)
    lo = (r - mid.astype(F32)).astype(BF16)
    return hi, mid, lo


def _rms(x, gain):
    ms = jnp.mean(x * x, axis=-1, keepdims=True)
    return x * lax.rsqrt(ms + EPS) * gain


def _const_spec(shape):
    n = len(shape)
    return pl.BlockSpec(shape, lambda *_: (0,) * n)


def _pack_rows(y):
    bits = lambda a: lax.bitcast_convert_type(a.astype(BF16).astype(F32), U32)
    return (bits(y[:, :HALF_D]) >> 16) | (bits(y[:, HALF_D:]) & jnp.uint32(0xFFFF0000))


def _unpack_rows(w):
    lo = lax.bitcast_convert_type(w << 16, F32)
    hi = lax.bitcast_convert_type(w & jnp.uint32(0xFFFF0000), F32)
    return lo, hi


def _deinterleave_into(w_refs, p_ref, g_ref, l_ref):
    p = p_ref[...]
    b = 0
    for w_ref in w_refs:
        for j in range(w_ref.shape[2] // (2 * LANES)):
            blk = w_ref[0, :, j * 2 * LANES:(j + 1) * 2 * LANES].astype(BF16)
            r = _dot(blk, p)
            g_ref[:, b * LANES:(b + 1) * LANES] = r[:, :LANES].astype(BF16)
            l_ref[:, b * LANES:(b + 1) * LANES] = r[:, LANES:].astype(BF16)
            b += 1


def _pair_split_perm():
    perm = np.zeros((2 * LANES, 2 * LANES), np.float32)
    perm[2 * np.arange(LANES), np.arange(LANES)] = 1.0
    perm[2 * np.arange(LANES) + 1, LANES + np.arange(LANES)] = 1.0
    return jnp.asarray(perm, BF16)


def _sc_mesh():
    return plsc.VectorSubcoreMesh(core_axis_name="c", subcore_axis_name="s")


def _sc_worker():
    return lax.axis_index("s") * SC_CORES + lax.axis_index("c")


def _sc_scatter_rows(src, dest_kt, n_rows):
    t, w = src.shape
    n_slot = dest_kt.shape[0]
    assert t % (SC_WORKERS * SC_CHUNK) == 0
    nch = t // (SC_WORKERS * SC_CHUNK)
    idx = dest_kt.reshape(n_slot, SC_WORKERS, nch, SC_CHUNK).transpose(1, 0, 2, 3)

    @functools.partial(
        pl.kernel, mesh=_sc_mesh(),
        out_type=jax.ShapeDtypeStruct((n_rows, w), src.dtype),
        scratch_types=[pltpu.VMEM((n_slot, nch, SC_CHUNK), I32),
                       pltpu.VMEM((SC_CHUNK, w), src.dtype)],
        name="sc_scatter")
    def run(src_hbm, idx_hbm, out_hbm, idx_v, rows_v):
        wid = _sc_worker()
        pltpu.sync_copy(idx_hbm.at[wid], idx_v)

        @pl.loop(0, nch)
        def _(j):
            pltpu.sync_copy(src_hbm.at[pl.ds((wid * nch + j) * SC_CHUNK, SC_CHUNK)], rows_v)
            for k in range(n_slot):
                pltpu.sync_copy(rows_v, out_hbm.at[idx_v.at[k, j]])

    return run(src, idx)


def _sc_gather_rows(table, idx):
    n = idx.shape[0]
    w = table.shape[1]
    assert n % (SC_WORKERS * SC_CHUNK) == 0
    nch = n // (SC_WORKERS * SC_CHUNK)
    idx3 = idx.reshape(SC_WORKERS, nch, SC_CHUNK)

    @functools.partial(
        pl.kernel, mesh=_sc_mesh(),
        out_type=jax.ShapeDtypeStruct((n, w), table.dtype),
        scratch_types=[pltpu.VMEM((nch, SC_CHUNK), I32),
                       pltpu.VMEM((SC_CHUNK, w), table.dtype)],
        name="sc_gather")
    def run(table_hbm, idx_hbm, out_hbm, idx_v, rows_v):
        wid = _sc_worker()
        pltpu.sync_copy(idx_hbm.at[wid], idx_v)

        @pl.loop(0, nch)
        def _(j):
            pltpu.sync_copy(table_hbm.at[idx_v.at[j]], rows_v)
            pltpu.sync_copy(rows_v, out_hbm.at[pl.ds((wid * nch + j) * SC_CHUNK, SC_CHUNK)])

    return run(table, idx3)


def _ada_kernel(c_ref, w_ref, b_ref, o_ref):
    c = c_ref[...]
    a = c * jax.nn.sigmoid(c)
    w = w_ref[...]
    a_hi = a.astype(BF16)
    a_lo = (a - a_hi.astype(F32)).astype(BF16)
    w_hi = w.astype(BF16)
    w_lo = (w - w_hi.astype(F32)).astype(BF16)
    o_ref[...] = _dot(a_hi, w_hi) + _dot(a_lo, w_hi) + _dot(a_hi, w_lo) + b_ref[...]


def _ada(c, w, b):
    nb = c.shape[0]
    n = w.shape[1]
    tn = ADA_COL_TILE
    return pl.pallas_call(
        _ada_kernel,
        grid=(n // tn,),
        in_specs=[pl.BlockSpec((nb, D), lambda j: (0, 0)),
                  pl.BlockSpec((D, tn), lambda j: (0, j)),
                  pl.BlockSpec((1, tn), lambda j: (0, j))],
        out_specs=pl.BlockSpec((nb, tn), lambda j: (0, j)),
        out_shape=jax.ShapeDtypeStruct((nb, n), F32),
        compiler_params=_cp(("parallel",)),
        name="ada",
    )(c, w, b.reshape(1, n))


def _inproj_kernel(x_ref, mod_ref, gain_ref, cos_ref, s1_ref, s2_ref,
                   w_hga_ref, w_hgf_ref, w_at_ref, w_gt_ref,
                   hga_ref, hgf_ref, at0_ref, at1_ref, at2_ref, gt_ref, stage_ref):
    x = x_ref[...]
    tm = x.shape[0]
    sh = mod_ref[0, 0:1, :]
    sc = mod_ref[0, 1:2, :]
    h = _rms(x, gain_ref[...]) * (1.0 + sc) + sh
    hb = h.astype(BF16)
    hga_ref[...] = _dot(hb, w_hga_ref[...]).astype(BF16)
    hgf_ref[...] = _dot(hb, w_hgf_ref[...])
    cos = cos_ref[...]
    s1 = s1_ref[...]
    s2 = s2_ref[...]
    for g, o_ref in enumerate((at0_ref, at1_ref, at2_ref)):
        acc = _dot(hb, w_at_ref[:, g * 3 * AT_GW:(g + 1) * 3 * AT_GW])
        parts = []
        for j in range(2 * AT_GW // LANES):
            a = acc[:, j * LANES:(j + 1) * LANES]
            parts.append(a * cos + pltpu.roll(a, ROPE_DIM // 2, 1) * s1
                         + pltpu.roll(a, LANES - ROPE_DIM // 2, 1) * s2)
        parts.append(acc[:, 2 * AT_GW:])
        vals = jnp.concatenate(parts, axis=1)
        dil = AT_DILS[g]
        if dil == 1:
            o_ref[0, 0] = vals.astype(BF16)
        else:
            n_lt = 3 * AT_GW // LANES
            for c in range(n_lt):
                stage_ref[c] = vals[:, c * LANES:(c + 1) * LANES]
            for r in range(dil):
                o_ref[0, r] = jnp.concatenate(
                    [stage_ref[c, pl.ds(r, tm // dil, stride=dil), :] for c in range(n_lt)],
                    axis=1).astype(BF16)
    gt_ref[...] = _dot(hb, w_gt_ref[...]).astype(BF16)


def _inproj(x2, mod3, gain, rope, w_hga, w_hgf, w_at, w_gt, seq, tm):
    t = x2.shape[0]
    n_pos_blk = seq // tm
    row = lambda w: pl.BlockSpec((tm, w), lambda i: (i, 0))
    pos = pl.BlockSpec((tm, LANES), lambda i: (i % n_pos_blk, 0))
    batch = t // seq
    dilated = lambda d: pl.BlockSpec((1, d, tm // d, 3 * AT_GW),
                                     lambda i: ((i * tm) // seq, 0, i % n_pos_blk, 0))
    at_shape = lambda d: jax.ShapeDtypeStruct((batch, d, seq // d, 3 * AT_GW), BF16)
    return pl.pallas_call(
        _inproj_kernel,
        grid=(t // tm,),
        in_specs=[row(D),
                  pl.BlockSpec((1, 6, D), lambda i: ((i * tm) // seq, 0, 0)),
                  _const_spec((1, D)), pos, pos, pos,
                  _const_spec(w_hga.shape), _const_spec(w_hgf.shape),
                  _const_spec(w_at.shape), _const_spec(w_gt.shape)],
        out_specs=[row(3 * HG_W), row(2 * HG_W)] + [dilated(d) for d in AT_DILS] + [row(2 * D)],
        out_shape=[jax.ShapeDtypeStruct((t, 3 * HG_W), BF16),
                   jax.ShapeDtypeStruct((t, 2 * HG_W), F32)]
                  + [at_shape(d) for d in AT_DILS]
                  + [jax.ShapeDtypeStruct((t, 2 * D), BF16)],
        scratch_shapes=[pltpu.VMEM((3 * AT_GW // LANES, tm, LANES), F32)],
        compiler_params=_cp(("parallel",)),
        name="inproj",
    )(x2, mod3, gain, *rope, w_hga, w_hgf, w_at, w_gt)


def _rope_tables(seq):
    half = ROPE_DIM // 2
    inv_freq = ROPE_THETA ** (-np.arange(half, dtype=np.float32) / half)
    ang = jnp.arange(seq, dtype=F32)[:, None] * jnp.asarray(inv_freq)[None, :]
    cos, sin = jnp.cos(ang), jnp.sin(ang)
    ones = jnp.ones((seq, AT_HD - ROPE_DIM), F32)
    zeros = jnp.zeros((seq, AT_HD - ROPE_DIM), F32)
    zh = jnp.zeros((seq, half), F32)
    c_head = jnp.concatenate([cos, cos, ones], axis=1)
    s1_head = jnp.concatenate([zh, sin, zeros], axis=1)
    s2_head = jnp.concatenate([-sin, zh, zeros], axis=1)
    rep = LANES // AT_HD
    return tuple(jnp.tile(a, (1, rep)) for a in (c_head, s1_head, s2_head))


def _hgrn_consts(reverse):
    c = HG_C
    t = np.arange(c)[:, None]
    u = np.arange(c)[None, :]
    tri = (u >= t) if reverse else (u <= t)
    sels = []
    for lvl in range(HG_LEVELS):
        blk = 2 << lvl
        start = (np.arange(c) // blk) * blk
        piv = start + (1 << lvl) - (0 if reverse else 1)
        sels.append(np.arange(c)[None, :] == piv[:, None])
    sel = np.concatenate(sels, axis=0)
    rep = lambda m, n: jnp.asarray(np.concatenate([m.astype(np.float32)] * n, axis=1), BF16)
    return rep(tri, 2), rep(sel, 3)


def _hgrn_kernel(*refs, reverse, final, n_chunks):
    if final:
        (lbl_ref, tri_ref, sel_ref, q_ref, f_ref, v_ref, ofw_ref, gh_ref, gain_ref,
         o_ref, st_ref, st0_ref) = refs
    else:
        (lbl_ref, tri_ref, sel_ref, q_ref, f_ref, v_ref,
         o_ref, st_ref, st0_ref) = refs
    c = HG_C

    @pl.when(pl.program_id(1) == 0)
    def _():
        st_ref[...] = jnp.zeros_like(st_ref)

    l0 = lbl_ref[0:1, :]
    l1 = lbl_ref[1:2, :]
    lm = jnp.maximum(l0, l1)
    e0 = jnp.exp(l0 - lm)
    e1 = jnp.exp(l1 - lm)
    lb = e0 / (e0 + e1)
    tri = tri_ref[...]
    piv_row = c // 2 if reverse else c // 2 - 1

    def gate_math(r0):
        ff = f_ref[pl.ds(r0, c), :]
        e = jnp.exp(-jnp.abs(ff))
        r = 1.0 / (1.0 + e)
        sg = jnp.where(ff >= 0.0, r, e * r)
        g = jnp.log(lb + (1.0 - lb) * sg)
        kin = (1.0 - lb) * (1.0 - sg)
        g_hi = g.astype(BF16)
        g_lo = (g - g_hi.astype(F32)).astype(BF16)
        b = _dot(tri, jnp.concatenate([g_hi, g_lo], axis=0))
        btot = b[0:1, :] if reverse else b[c - 1:c, :]
        return kin, b, btot

    ti = lax.broadcasted_iota(I32, (c, c), 0)
    si = lax.broadcasted_iota(I32, (c, c), 1)
    causal = (si >= ti) if reverse else (si <= ti)
    row = lax.broadcasted_iota(I32, (c, 1), 0)

    def att_robust(q, kin, b, piv):
        att = jnp.where(ti == si, _dot_nt(q.astype(BF16), kin.astype(BF16)), 0.0)
        for lvl in range(HG_LEVELS):
            p = piv[lvl * c:(lvl + 1) * c, :]
            bit = ((row >> lvl) & 1) == 1
            q_side = jnp.logical_not(bit) if reverse else bit
            qe = jnp.where(q_side, q * jnp.exp(jnp.minimum(b - p, 0.0)), 0.0).astype(BF16)
            ke = jnp.where(q_side, 0.0, kin * jnp.exp(jnp.minimum(p - b, 0.0))).astype(BF16)
            same = (ti >> (lvl + 1)) == (si >> (lvl + 1))
            att = att + jnp.where(same, _dot_nt(qe, ke), 0.0)
        return att

    heads = [slice(h * HG_DK, (h + 1) * HG_DK) for h in range(HG_H)]


    def emit(r0, outs):
        o = jnp.concatenate(outs, axis=1)
        if final:
            o = o + ofw_ref[pl.ds(r0, c), :]
            normed = []
            for sl in heads:
                ms = jnp.mean(o[:, sl] * o[:, sl], axis=-1, keepdims=True)
                normed.append(o[:, sl] * lax.rsqrt(ms + EPS))
            gh = gh_ref[pl.ds(r0, c), :].astype(F32)
            o = jnp.concatenate(normed, axis=1) * gain_ref[...] * (gh * jax.nn.sigmoid(gh))
        o_ref[pl.ds(r0, c), :] = o.astype(o_ref.dtype)

    def robust_step(i, carry):
        ci = (n_chunks - 1 - i) if reverse else i
        r0 = pl.multiple_of(ci * c, c)
        q = q_ref[pl.ds(r0, c), :].astype(F32)
        v = v_ref[pl.ds(r0, c), :]
        kin, b, btot = gate_math(r0)
        qe_all = (q * jnp.exp(b)).astype(BF16)
        kd_all = (kin * jnp.exp(btot - b)).astype(BF16)
        piv = _dot(sel_ref[...], jnp.concatenate(_split3(b), axis=0))
        outs = []
        for h, sl in enumerate(heads):
            att = att_robust(q[:, sl], kin[:, sl], b[:, sl], piv[:, sl])
            st = st_ref[h]
            outs.append(_dot_nt(qe_all[:, sl], st.astype(BF16)) + _dot(att.astype(BF16), v[:, sl]))
            st_ref[h] = st * jnp.exp(btot[:, sl]) + _dot_tn(v[:, sl], kd_all[:, sl])
        emit(r0, outs)
        return carry

    grp = HG_UNROLL

    def fast_group(i, bmin):
        gi = (n_chunks // grp - 1 - i) if reverse else i
        base = gi * (grp * c)
        order = range(grp - 1, -1, -1) if reverse else range(grp)
        chunks = []
        for j in order:
            r0 = pl.multiple_of(base + j * c, c)
            kin, b, btot = gate_math(r0)
            bp = b[piv_row:piv_row + 1, :]
            bmin = jnp.minimum(bmin, jnp.minimum(bp, btot - bp))
            to_piv = jnp.exp(b - bp)
            k_piv = kin / to_piv
            chunks.append(dict(
                r0=r0, v=v_ref[pl.ds(r0, c), :], dec=jnp.exp(btot), pdec=jnp.exp(bp),
                qe=q_ref[pl.ds(r0, c), :] * to_piv.astype(BF16),
                ke=k_piv.astype(BF16),
                kd=(k_piv * jnp.exp(btot - bp)).astype(BF16)))
        for ch in chunks:
            ch["att"] = [jnp.where(causal, _dot_nt(ch["qe"][:, sl], ch["ke"][:, sl]), 0.0).astype(BF16)
                         for sl in heads]
        for ch in chunks:
            ch["intra"] = [_dot(ch["att"][h], ch["v"][:, sl]) for h, sl in enumerate(heads)]
            ch["upd"] = [_dot_tn(ch["v"][:, sl], ch["kd"][:, sl]) for sl in heads]
        st = [st_ref[h] for h in range(HG_H)]
        for ch in chunks:
            outs = []
            for h, sl in enumerate(heads):
                st_piv = (st[h] * ch["pdec"][:, sl]).astype(BF16)
                outs.append(_dot_nt(ch["qe"][:, sl], st_piv) + ch["intra"][h])
                st[h] = st[h] * ch["dec"][:, sl] + ch["upd"][h]
            emit(ch["r0"], outs)
        for h in range(HG_H):
            st_ref[h] = st[h]
        return bmin

    st0_ref[...] = st_ref[...]
    bmin = lax.fori_loop(0, n_chunks // grp, fast_group, jnp.zeros((1, HG_W), F32))

    @pl.when(jnp.min(bmin) < HG_SAFE_LOGDECAY)
    def _():
        st_ref[...] = st0_ref[...]
        lax.fori_loop(0, n_chunks, robust_step, 0)


def _hgrn(lb_logits_d, hga, hgf, batch, seq, tc, reverse, o_fw=None, gain=None):
    final = o_fw is not None
    t = hga.shape[0]
    nblk = seq // tc
    tri, sel = _hgrn_consts(reverse)

    def rows(col):
        if reverse:
            return lambda b, j: (b * nblk + nblk - 1 - j, col)
        return lambda b, j: (b * nblk + j, col)

    blk = lambda col: pl.BlockSpec((tc, HG_W), rows(col))
    in_specs = [_const_spec((2, HG_W)), _const_spec(tri.shape), _const_spec(sel.shape),
                blk(0), blk(1 if reverse else 0), blk(1)]
    args = [lb_logits_d, tri, sel, hga, hgf, hga]
    if final:
        in_specs += [blk(0), blk(2), _const_spec((1, HG_W))]
        args += [o_fw, hga, gain]
    return pl.pallas_call(
        functools.partial(_hgrn_kernel, reverse=reverse, final=final, n_chunks=tc // HG_C),
        grid=(batch, nblk),
        in_specs=in_specs,
        out_specs=blk(0),
        out_shape=jax.ShapeDtypeStruct((t, HG_W), BF16 if final else F32),
        scratch_shapes=[pltpu.VMEM((HG_H, HG_DK, HG_DK), F32),
                        pltpu.VMEM((HG_H, HG_DK, HG_DK), F32)],
        compiler_params=_cp(("parallel", "arbitrary")),
        name="hgrn_bwd" if reverse else "hgrn_fwd",
    )(*args)


def _attn_kernel(q_ref, kl_ref, km_ref, kr_ref, vl_ref, vm_ref, vr_ref, o_ref, l_ref,
                 k_s, v_s, *, tq, ld):
    hw = AT_HALF
    n_res = q_ref.shape[0]
    for res in range(n_res):
        k_s[res, 0:hw, :] = kl_ref[res]
        k_s[res, hw:hw + tq, :] = km_ref[res]
        k_s[res, hw + tq:, :] = kr_ref[res]
        v_s[res, 0:hw, :] = vl_ref[res]
        v_s[res, hw:hw + tq, :] = vm_ref[res]
        v_s[res, hw + tq:, :] = vr_ref[res]
    qb = 2 * hw
    kb = 4 * hw
    base = pl.program_id(2) * tq
    lane_head = lax.broadcasted_iota(I32, (1, AT_GW), 1) // AT_HD
    q_sel = [jnp.where(lane_head == h, AT_HD ** -0.5, 0.0).astype(BF16) for h in range(AT_H)]
    out_head = lax.broadcasted_iota(I32, (qb, AT_GW), 1) // AT_HD
    rel = (lax.broadcasted_iota(I32, (AT_H * qb, kb), 0) % qb
           - lax.broadcasted_iota(I32, (AT_H * qb, kb), 1) + hw)
    band = jnp.where(jnp.abs(rel) <= hw, 0.0, NEG_BIG)
    key_col = lax.broadcasted_iota(I32, (1, kb), 1)
    for res, qs in [(res, qs) for res in range(n_res) for qs in range(0, tq, qb)]:
        q = q_ref[res, qs:qs + qb, :]
        kk = k_s[res, qs:qs + kb, :]
        vv = v_s[res, qs:qs + kb, :]
        q4 = jnp.concatenate([q * q_sel[h] for h in range(AT_H)], axis=0)
        s = _dot_nt(q4, kk) + band
        if qs == 0 or qs + qb == tq:
            kpos = base + qs - hw + key_col
            s = s + jnp.where((kpos >= 0) & (kpos < ld), 0.0, NEG_BIG)
        m = jnp.max(s, axis=-1, keepdims=True)
        p = jnp.exp(s - m)
        l = jnp.sum(p, axis=-1, keepdims=True)
        pv = _dot(p.astype(BF16), vv) * (1.0 / l)
        lse = m + jnp.log(l)
        o_all = jnp.zeros((qb, AT_GW), F32)
        l_all = jnp.zeros((qb, AT_GW), F32)
        for h in range(AT_H):
            o_all = jnp.where(out_head == h, pv[h * qb:(h + 1) * qb, :], o_all)
            l_all = jnp.where(out_head == h, lse[h * qb:(h + 1) * qb, :], l_all)
        o_ref[res, qs:qs + qb, :] = o_all.astype(o_ref.dtype)
        l_ref[res, qs:qs + qb, :] = l_all


def _attn_group(qkv):
    batch, dil, ld, _ = qkv.shape
    hw = AT_HALF
    tq = min(ld, SEQ_TILE)
    nq = ld // tq
    nh = ld // hw
    per = tq // hw
    n_res = min(dil, max(1, SEQ_TILE // tq))
    main = lambda part: pl.BlockSpec((None, n_res, tq, AT_GW), lambda b, r, j: (b, r, j, part))
    left = lambda part: pl.BlockSpec(
        (None, n_res, hw, AT_GW), lambda b, r, j: (b, r, jnp.maximum(j * per - 1, 0), part))
    right = lambda part: pl.BlockSpec(
        (None, n_res, hw, AT_GW), lambda b, r, j: (b, r, jnp.minimum((j + 1) * per, nh - 1), part))
    out_spec = pl.BlockSpec((None, n_res, tq, AT_GW), lambda b, r, j: (b, r, j, 0))
    return pl.pallas_call(
        functools.partial(_attn_kernel, tq=tq, ld=ld),
        grid=(batch, dil // n_res, nq),
        in_specs=[main(0), left(1), main(1), right(1), left(2), main(2), right(2)],
        out_specs=[out_spec, out_spec],
        out_shape=[jax.ShapeDtypeStruct((batch, dil, ld, AT_GW), BF16),
                   jax.ShapeDtypeStruct((batch, dil, ld, AT_GW), F32)],
        scratch_shapes=[pltpu.VMEM((n_res, tq + 2 * hw, AT_GW), BF16),
                        pltpu.VMEM((n_res, tq + 2 * hw, AT_GW), BF16)],
        compiler_params=_cp(("parallel", "parallel", "parallel")),
        name=f"attn_d{dil}",
    )(qkv, qkv, qkv, qkv, qkv, qkv, qkv)


def _merge_kernel(ohg_ref, o0_ref, o1_ref, o2_ref, l0_ref, l1_ref, l2_ref, gt_ref, x_ref, mod_ref,
                  w_hg_ref, w_at_ref, w_out_ref, npost_ref, npre_ref, w_r_ref, b_r_ref,
                  x1_ref, h2_ref, ids_ref, wts_ref, cnt_ref,
                  so1_ref, sl1_ref, so2_ref, sl2_ref):
    tm = x_ref.shape[0]
    sub = tm // MERGE_SUBTILES
    rows = [slice(i * sub, (i + 1) * sub) for i in range(MERGE_SUBTILES)]
    gt1 = mod_ref[0, 2:3, :]
    sh2 = mod_ref[0, 3:4, :]
    sc2 = mod_ref[0, 4:5, :]

    for src_ref, dst_ref in ((o1_ref, so1_ref), (l1_ref, sl1_ref), (o2_ref, so2_ref), (l2_ref, sl2_ref)):
        dil = src_ref.shape[0]
        for res in range(dil):
            vals = src_ref[res].astype(F32)
            for c in range(AT_GW // LANES):
                dst_ref[c, pl.ds(res, tm // dil, stride=dil), :] = vals[:, c * LANES:(c + 1) * LANES]

    def staged(ref, r):
        return jnp.concatenate([ref[c, r, :] for c in range(AT_GW // LANES)], axis=1)

    def branches(r):
        o0, l0 = o0_ref[0, r, :].astype(F32), l0_ref[0, r, :]
        o1, l1 = staged(so1_ref, r), staged(sl1_ref, r)
        o2, l2 = staged(so2_ref, r), staged(sl2_ref, r)
        m = jnp.maximum(jnp.maximum(l0, l1), l2)
        e0, e1, e2 = jnp.exp(l0 - m), jnp.exp(l1 - m), jnp.exp(l2 - m)
        oa = (e0 * o0 + e1 * o1 + e2 * o2) / (e0 + e1 + e2)
        return _dot(ohg_ref[r, :], w_hg_ref[...]), _dot(oa.astype(BF16), w_at_ref[...])

    def out_proj(r, b_hg, b_at):
        g_hg = jax.nn.sigmoid(gt_ref[r, 0:D])
        g_at = jax.nn.sigmoid(gt_ref[r, D:2 * D])
        merged = g_hg * b_hg.astype(BF16) + g_at * b_at.astype(BF16)
        return _dot(merged, w_out_ref[...])

    def norms(r, y):
        x1 = x_ref[r, :] + gt1 * _rms(y, npost_ref[...])
        x1_ref[r, :] = x1
        h2 = _rms(x1, npre_ref[...]) * (1.0 + sc2) + sh2
        h2_ref[r, :] = _pack_rows(h2)
        return _dot(h2.astype(BF16), w_r_ref[...]) + b_r_ref[...]

    def top_k(r, logits):
        lane = lax.broadcasted_iota(I32, logits.shape, 1)
        lane_f = lane.astype(F32)
        work = logits
        vals, idxs = [], []
        sel = jnp.zeros(logits.shape, F32)
        for _ in range(TOP_K):
            mk = jnp.max(work, axis=-1, keepdims=True)
            ik = jnp.min(jnp.where(work == mk, lane_f, float(LANES)), axis=-1, keepdims=True)
            hit = lane_f == ik
            sel = jnp.where(hit, 1.0, sel)
            work = jnp.where(hit, -jnp.inf, work)
            vals.append(mk)
            idxs.append(ik)
        es = [jnp.exp(v - vals[0]) for v in vals]
        den = es[0] + es[1] + es[2] + es[3]
        ids = jnp.zeros(logits.shape, F32)
        wts = jnp.zeros(logits.shape, F32)
        for k in range(TOP_K):
            ids = jnp.where(lane == k, idxs[k], ids)
            wts = jnp.where(lane == k, es[k] / den, wts)
        ids_ref[r, :] = ids.astype(I32)
        wts_ref[r, :] = wts
        return jnp.sum(sel, axis=0, keepdims=True)

    br = [branches(r) for r in rows]
    ys = [out_proj(r, *b) for r, b in zip(rows, br)]
    logits = [norms(r, y) for r, y in zip(rows, ys)]
    counts = [top_k(r, lg) for r, lg in zip(rows, logits)]
    cnt_ref[0] = functools.reduce(lambda a, b: a + b, counts)


def _merge(ohg, att_o, att_l, gates, x2, mod3, w_hg, w_at, w_out, npost, npre, w_r, b_r, seq, tm):
    t = x2.shape[0]
    nt = t // tm
    row = lambda w: pl.BlockSpec((tm, w), lambda i: (i, 0))
    slab_i = jax.ShapeDtypeStruct((t, LANES), I32)
    slab_f = jax.ShapeDtypeStruct((t, LANES), F32)
    nblk = seq // tm
    grouped = [pl.BlockSpec((None, d, tm // d, AT_GW),
                            lambda i: ((i * tm) // seq, 0, i % nblk, 0)) for d in AT_DILS]
    return pl.pallas_call(
        _merge_kernel,
        grid=(nt,),
        in_specs=[row(HG_W)] + grouped + grouped + [row(2 * D), row(D),
                  pl.BlockSpec((1, 6, D), lambda i: ((i * tm) // seq, 0, 0)),
                  _const_spec(w_hg.shape), _const_spec(w_at.shape), _const_spec(w_out.shape),
                  _const_spec((1, D)), _const_spec((1, D)),
                  _const_spec(w_r.shape), _const_spec(b_r.shape)],
        out_specs=[row(D), row(HALF_D), row(LANES), row(LANES),
                   pl.BlockSpec((1, 1, LANES), lambda i: (i, 0, 0))],
        out_shape=[jax.ShapeDtypeStruct((t, D), F32), jax.ShapeDtypeStruct((t, HALF_D), U32),
                   slab_i, slab_f, jax.ShapeDtypeStruct((nt, 1, LANES), F32)],
        scratch_shapes=[pltpu.VMEM((AT_GW // LANES, tm, LANES), F32)] * 4,
        compiler_params=_cp(("parallel",)),
        name="merge",
    )(ohg, *att_o, *att_l, gates, x2, mod3, w_hg, w_at, w_out, npost, npre, w_r, b_r)


def _route_kernel(ids_ref, base_ref, tril_ref, dest_ref):
    ids = ids_ref[...]
    lane = lax.broadcasted_iota(I32, ids.shape, 1)
    hits = [lane == ids[:, k:k + 1] for k in range(TOP_K)]
    sel = jnp.zeros(ids.shape, F32)
    for hit in hits:
        sel = jnp.where(hit, 1.0, sel)
    before = _dot(tril_ref[...], sel.astype(BF16)) + base_ref[0]
    dest = jnp.zeros(ids.shape, I32)
    for k, hit in enumerate(hits):
        rk = jnp.sum(jnp.where(hit, before, 0.0), axis=-1, keepdims=True)
        dest = jnp.where(lane == k, rk.astype(I32), dest)
    dest_ref[...] = dest


def _route(ids, tile_base, tm):
    t = ids.shape[0]
    tril = jnp.asarray(np.tril(np.ones((tm, tm), np.float32), -1), BF16)
    return pl.pallas_call(
        _route_kernel,
        grid=(t // tm,),
        in_specs=[pl.BlockSpec((tm, LANES), lambda i: (i, 0)),
                  pl.BlockSpec((1, 1, LANES), lambda i: (i, 0, 0)),
                  _const_spec((tm, tm))],
        out_specs=pl.BlockSpec((tm, LANES), lambda i: (i, 0)),
        out_shape=jax.ShapeDtypeStruct((t, LANES), I32),
        compiler_params=_cp(("parallel",)),
        name="route",
    )(ids, tile_base, tril)


def _moe_kernel(be_ref, rows_ref, x_ref, *refs):
    wup_refs = refs[:MOE_W_SLABS]
    bg_ref, bl_ref = refs[MOE_W_SLABS:MOE_W_SLABS + 2]
    wdn_refs = refs[MOE_W_SLABS + 2:2 * MOE_W_SLABS + 2]
    bd_ref, perm_ref, o_ref, wg_s, wl_s, wd_s = refs[2 * MOE_W_SLABS + 2:]
    i = pl.program_id(0)

    @pl.when((i == 0) | (be_ref[i] != be_ref[jnp.maximum(i - 1, 0)]))
    def _():
        _deinterleave_into(wup_refs, perm_ref, wg_s, wl_s)
        rows = D // MOE_W_SLABS
        for c, wdn_ref in enumerate(wdn_refs):
            wd_s[c * rows:(c + 1) * rows, :] = wdn_ref[0].astype(BF16)

    def ffn(x_packed):
        x_lo, x_hi = (a.astype(BF16) for a in _unpack_rows(x_packed))
        gate = _dot(x_lo, wg_s[:HALF_D, :]) + _dot(x_hi, wg_s[HALF_D:, :]) + bg_ref[0]
        up = _dot(x_lo, wl_s[:HALF_D, :]) + _dot(x_hi, wl_s[HALF_D:, :]) + bl_ref[0]
        gate = jnp.minimum(gate, SWIGLU_LIMIT)
        up = jnp.clip(up, -SWIGLU_LIMIT, SWIGLU_LIMIT)
        act = (up + 1.0) * gate * jax.nn.sigmoid(SWIGLU_ALPHA * gate)
        return _pack_rows(_dot(act.astype(BF16), wd_s[...]) + bd_ref[0])

    valid = rows_ref[i]
    half = x_ref.shape[0] // 2

    @pl.when(valid > half)
    def _():
        o_ref[...] = ffn(x_ref[...])

    @pl.when((valid > 0) & (valid <= half))
    def _():
        o_ref[:half, :] = ffn(x_ref[:half, :])
        o_ref[half:, :] = jnp.zeros((half, HALF_D), U32)

    @pl.when(valid == 0)
    def _():
        o_ref[...] = jnp.zeros_like(o_ref)


def _moe(block_expert, block_rows, xb, w_up, bg, bl, w_down, bd, bm):
    r = xb.shape[0]
    nblk = r // bm
    perm = _pair_split_perm()
    ns = MOE_W_SLABS
    ew = lambda shape: pl.BlockSpec((1,) + shape, lambda i, be, br: (be[i], 0, 0))
    up_slab = lambda c: pl.BlockSpec((1, D, 2 * D // ns), lambda i, be, br: (be[i], 0, c))
    down_slab = lambda c: pl.BlockSpec((1, D // ns, D), lambda i, be, br: (be[i], c, 0))
    return pl.pallas_call(
        _moe_kernel,
        grid_spec=pltpu.PrefetchScalarGridSpec(
            num_scalar_prefetch=2,
            grid=(nblk,),
            in_specs=[pl.BlockSpec((bm, HALF_D), lambda i, be, br: (i, 0))]
                     + [up_slab(c) for c in range(ns)] + [ew((1, D)), ew((1, D))]
                     + [down_slab(c) for c in range(ns)] + [ew((1, D))]
                     + [pl.BlockSpec(perm.shape, lambda i, be, br: (0, 0))],
            out_specs=pl.BlockSpec((bm, HALF_D), lambda i, be, br: (i, 0)),
            scratch_shapes=[pltpu.VMEM((D, D), BF16)] * 3,
        ),
        out_shape=jax.ShapeDtypeStruct((r, HALF_D), U32),
        compiler_params=_cp(("arbitrary",)),
        name="moe",
    )(block_expert, block_rows, xb, *([w_up] * ns), bg, bl, *([w_down] * ns), bd, perm)


def _final_kernel(y0_ref, y1_ref, y2_ref, y3_ref, wts_ref, x1_ref, mod_ref, npost_ref, o_ref):
    wts = wts_ref[...]
    y = jnp.zeros(x1_ref.shape, F32)
    for k, y_ref in enumerate((y0_ref, y1_ref, y2_ref, y3_ref)):
        y = y + wts[:, k:k + 1] * jnp.concatenate(_unpack_rows(y_ref[0]), axis=1)
    gt2 = mod_ref[0, 5:6, :]
    o_ref[...] = x1_ref[...] + gt2 * _rms(y, npost_ref[...])


def _final(yg, wts, x1, mod3, npost, seq, tm):
    t = x1.shape[0]
    slot = lambda k: pl.BlockSpec((1, tm, HALF_D), lambda i: (k, i, 0))
    return pl.pallas_call(
        _final_kernel,
        grid=(t // tm,),
        in_specs=[slot(0), slot(1), slot(2), slot(3),
                  pl.BlockSpec((tm, LANES), lambda i: (i, 0)),
                  pl.BlockSpec((tm, D), lambda i: (i, 0)),
                  pl.BlockSpec((1, 6, D), lambda i: ((i * tm) // seq, 0, 0)),
                  _const_spec((1, D))],
        out_specs=pl.BlockSpec((tm, D), lambda i: (i, 0)),
        out_shape=jax.ShapeDtypeStruct((t, D), F32),
        compiler_params=_cp(("parallel",)),
        name="final",
    )(yg, yg, yg, yg, wts, x1, mod3, npost)


def _tiles(seq):
    return dict(tm_in=min(seq, ROW_TILE), tc=min(seq, SEQ_TILE), tm_merge=min(seq, STREAM_TILE),
                bm=MOE_BLOCK, tm_final=min(seq, STREAM_TILE))


def _prep_weights(w_in, w_hg_out, w_att_out, w_out, w_router, b_router, w_up, b_up, w_down, b_down):
    w = w_in[0]
    hw = HG_W
    a0 = 5 * hw
    aw = AT_GW * len(AT_DILS)
    q_at, k_at, v_at = (w[:, a0 + i * aw:a0 + (i + 1) * aw] for i in range(3))
    grp = lambda m, g: m[:, g * AT_GW:(g + 1) * AT_GW]
    w_at = jnp.concatenate(
        [jnp.concatenate([grp(q_at, g), grp(k_at, g), grp(v_at, g)], axis=1)
         for g in range(len(AT_DILS))], axis=1)
    pad = LANES - N_EXP
    return dict(
        w_up=w_up[0],
        w_hga=jnp.concatenate([w[:, 0:hw], w[:, 3 * hw:4 * hw], w[:, 4 * hw:5 * hw]], 1).astype(BF16),
        w_hgf=w[:, hw:3 * hw].astype(BF16),
        w_at=w_at.astype(BF16),
        w_gt=w[:, a0 + 3 * aw:].astype(BF16),
        w_hg_out=w_hg_out[0].astype(BF16),
        w_att_out=w_att_out[0].astype(BF16),
        w_out=w_out[0].astype(BF16),
        w_r=jnp.pad(w_router[0], ((0, 0), (0, pad))).astype(BF16),
        b_r=jnp.pad(b_router[0], (0, pad), constant_values=NEG_BIG).reshape(1, LANES),
        bg=b_up[0][:, 0::2].reshape(N_EXP, 1, D),
        bl=b_up[0][:, 1::2].reshape(N_EXP, 1, D),
        w_down=w_down[0],
        bd=b_down[0].reshape(N_EXP, 1, D),
    )


def _trunk(x, mod, wts, norm_pre, norm_post, lb_logits, hg_gain):
    batch, seq, width = x.shape
    t = batch * seq
    tl = _tiles(seq)
    assert width == D and seq % SEQ_TILE == 0 and seq // max(AT_DILS) >= 2 * AT_HALF
    x2 = x.reshape(t, D)
    mod3 = mod.reshape(batch, 6, D)
    rope = _rope_tables(seq)

    hga, hgf, at0, at1, at2, gates = _inproj(
        x2, mod3, norm_pre[0, 0].reshape(1, D), rope,
        wts["w_hga"], wts["w_hgf"], wts["w_at"], wts["w_gt"], seq, tl["tm_in"])

    o_fw = _hgrn(lb_logits[0], hga, hgf, batch, seq, tl["tc"], reverse=False)
    ohg = _hgrn(lb_logits[1], hga, hgf, batch, seq, tl["tc"], reverse=True,
                o_fw=o_fw, gain=hg_gain[0].reshape(1, HG_W))

    att = [_attn_group(a) for a in (at0, at1, at2)]

    x1, h2, ids, rw, cnt = _merge(
        ohg, [a[0] for a in att], [a[1] for a in att], gates, x2, mod3,
        wts["w_hg_out"], wts["w_att_out"], wts["w_out"],
        norm_post[0, 0].reshape(1, D), norm_pre[0, 1].reshape(1, D),
        wts["w_r"], wts["b_r"], seq, tl["tm_merge"])

    bm = tl["bm"]
    tmr = tl["tm_merge"]
    cnt_tiles = cnt.reshape(t // tmr, LANES).astype(I32)
    total = jnp.sum(cnt_tiles, axis=0)
    padded = (total + bm - 1) // bm * bm
    pad_end = jnp.cumsum(padded)
    pad_start = pad_end - padded
    tile_base = pad_start[None, :] + jnp.cumsum(cnt_tiles, axis=0) - cnt_tiles
    dest = _route(ids, tile_base.astype(F32).reshape(-1, 1, LANES), tmr)[:, :TOP_K]

    n_rows = t * TOP_K + N_EXP * bm
    n_blocks = n_rows // bm
    blk_start = jnp.arange(n_blocks, dtype=I32) * bm
    block_expert = jnp.minimum(
        jnp.sum(pad_end[None, :N_EXP] <= blk_start[:, None], axis=1), N_EXP - 1).astype(I32)
    block_rows = jnp.clip(pad_start[block_expert] + total[block_expert] - blk_start, 0, bm)
    block_rows = jnp.where(blk_start < pad_end[N_EXP - 1], block_rows, 0).astype(I32)

    dest_kt = dest.T
    xb = _sc_scatter_rows(h2, dest_kt, n_rows)
    yb = _moe(block_expert, block_rows, xb, wts["w_up"], wts["bg"], wts["bl"],
              wts["w_down"], wts["bd"], bm)
    yg = _sc_gather_rows(yb, dest_kt.reshape(-1)).reshape(TOP_K, t, HALF_D)

    out = _final(yg, rw, x1, mod3, norm_post[0, 1].reshape(1, D), seq, tl["tm_final"])
    return out.reshape(batch, seq, D)


def kernel(x_prompt, x_sample, c_prompt, c_sample, w_ada, b_ada, norm_pre, norm_post, w_in,
           lb_logits, hg_norm_gain, w_hg_out, w_att_out, w_out, w_router, b_router,
           w_up, b_up, w_down, b_down):
    wts = _prep_weights(w_in, w_hg_out, w_att_out, w_out, w_router, b_router,
                        w_up, b_up, w_down, b_down)
    nb = c_prompt.shape[0]
    mod = _ada(jnp.concatenate([c_prompt, c_sample], axis=0), w_ada[0], b_ada[0])
    lb_l = lb_logits.astype(F32)
    y_p = _trunk(x_prompt, mod[:nb], wts, norm_pre, norm_post, lb_l, hg_norm_gain)
    y_s = _trunk(x_sample, mod[nb:], wts, norm_pre, norm_post, lb_l, hg_norm_gain)
    return (y_p, y_s)
```

```python
import functools

import numpy as np
import jax
import jax.numpy as jnp
from jax import lax
from jax.experimental import pallas as pl
from jax.experimental.pallas import tpu as pltpu
from jax.experimental.pallas import tpu_sc as plsc

F32 = jnp.float32
BF16 = jnp.bfloat16
I32 = jnp.int32
U32 = jnp.uint32

D = 1024
EPS = 1e-6
HG_H = 4
HG_DK = 128
HG_W = HG_H * HG_DK
HG_C = 64
HG_LEVELS = 6
HG_SAFE_LOGDECAY = -80.0
HG_UNROLL = 8
AT_DILS = (1, 4, 16)
AT_HALF = 64
AT_H = 4
AT_HD = 64
AT_GW = AT_H * AT_HD
ROPE_DIM = 16
ROPE_THETA = 500000.0
N_EXP = 32
TOP_K = 4
SWIGLU_LIMIT = 7.0
SWIGLU_ALPHA = 1.702
MERGE_SUBTILES = 4
ROW_TILE = 512
STREAM_TILE = 1024
SEQ_TILE = 2048
ADA_COL_TILE = 1536
MOE_BLOCK = 1024
LANES = 128
NEG_BIG = -1e30
HALF_D = D // 2
SC_CORES = 2
SC_SUBCORES = 16
SC_WORKERS = SC_CORES * SC_SUBCORES
SC_CHUNK = 128

VMEM_LIMIT = 56 * 1024 * 1024


def _cp(sem, vmem=VMEM_LIMIT):
    return pltpu.CompilerParams(dimension_semantics=sem, vmem_limit_bytes=vmem)


def _dot(a, b):
    return jnp.dot(a, b, preferred_element_type=F32)


def _dot_nt(a, b):
    return lax.dot_general(a, b, (((1,), (1,)), ((), ())), preferred_element_type=F32)


def _dot_tn(a, b):
    return lax.dot_general(a, b, (((0,), (0,)), ((), ())), preferred_element_type=F32)


def _split3(x):
    hi = x.astype(BF16)
    r = x - hi.astype(F32)
    mid = r.astype(BF16)
    lo = (r - mid.astype(F32)).astype(BF16)
    return hi, mid, lo


def _rms(x, gain):
    ms = jnp.mean(x * x, axis=-1, keepdims=True)
    return x * lax.rsqrt(ms + EPS) * gain


def _const_spec(shape):
    n = len(shape)
    return pl.BlockSpec(shape, lambda *_: (0,) * n)


def _pack_rows(y):
    bits = lambda a: lax.bitcast_convert_type(a.astype(BF16).astype(F32), U32)
    return (bits(y[:, :HALF_D]) >> 16) | (bits(y[:, HALF_D:]) & jnp.uint32(0xFFFF0000))


def _unpack_rows(w):
    lo = lax.bitcast_convert_type(w << 16, F32)
    hi = lax.bitcast_convert_type(w & jnp.uint32(0xFFFF0000), F32)
    return lo, hi


def _deinterleave_into(w_ref, p_ref, g_ref, l_ref):
    p = p_ref[...]
    n_grp = w_ref.shape[2] // (2 * LANES)
    for b in range(n_grp):
        blk = w_ref[0, :, b * 2 * LANES:(b + 1) * 2 * LANES].astype(BF16)
        r = _dot(blk, p)
        g_ref[:, b * LANES:(b + 1) * LANES] = r[:, :LANES].astype(BF16)
        l_ref[:, b * LANES:(b + 1) * LANES] = r[:, LANES:].astype(BF16)


def _pair_split_perm():
    perm = np.zeros((2 * LANES, 2 * LANES), np.float32)
    perm[2 * np.arange(LANES), np.arange(LANES)] = 1.0
    perm[2 * np.arange(LANES) + 1, LANES + np.arange(LANES)] = 1.0
    return jnp.asarray(perm, BF16)


def _sc_mesh():
    return plsc.VectorSubcoreMesh(core_axis_name="c", subcore_axis_name="s")


def _sc_worker():
    return lax.axis_index("s") * SC_CORES + lax.axis_index("c")


def _sc_scatter_rows(src, dest_kt, n_rows):
    t, w = src.shape
    n_slot = dest_kt.shape[0]
    assert t % (SC_WORKERS * SC_CHUNK) == 0
    nch = t // (SC_WORKERS * SC_CHUNK)
    idx = dest_kt.reshape(n_slot, SC_WORKERS, nch, SC_CHUNK).transpose(1, 0, 2, 3)

    @functools.partial(
        pl.kernel, mesh=_sc_mesh(),
        out_type=jax.ShapeDtypeStruct((n_rows, w), src.dtype),
        scratch_types=[pltpu.VMEM((n_slot, nch, SC_CHUNK), I32),
                       pltpu.VMEM((SC_CHUNK, w), src.dtype)],
        name="sc_scatter")
    def run(src_hbm, idx_hbm, out_hbm, idx_v, rows_v):
        wid = _sc_worker()
        pltpu.sync_copy(idx_hbm.at[wid], idx_v)

        @pl.loop(0, nch)
        def _(j):
            pltpu.sync_copy(src_hbm.at[pl.ds((wid * nch + j) * SC_CHUNK, SC_CHUNK)], rows_v)
            for k in range(n_slot):
                pltpu.sync_copy(rows_v, out_hbm.at[idx_v.at[k, j]])

    return run(src, idx)


def _sc_gather_rows(table, idx):
    n = idx.shape[0]
    w = table.shape[1]
    assert n % (SC_WORKERS * SC_CHUNK) == 0
    nch = n // (SC_WORKERS * SC_CHUNK)
    idx3 = idx.reshape(SC_WORKERS, nch, SC_CHUNK)

    @functools.partial(
        pl.kernel, mesh=_sc_mesh(),
        out_type=jax.ShapeDtypeStruct((n, w), table.dtype),
        scratch_types=[pltpu.VMEM((nch, SC_CHUNK), I32),
                       pltpu.VMEM((SC_CHUNK, w), table.dtype)],
        name="sc_gather")
    def run(table_hbm, idx_hbm, out_hbm, idx_v, rows_v):
        wid = _sc_worker()
        pltpu.sync_copy(idx_hbm.at[wid], idx_v)

        @pl.loop(0, nch)
        def _(j):
            pltpu.sync_copy(table_hbm.at[idx_v.at[j]], rows_v)
            pltpu.sync_copy(rows_v, out_hbm.at[pl.ds((wid * nch + j) * SC_CHUNK, SC_CHUNK)])

    return run(table, idx3)


def _ada_kernel(c_ref, w_ref, b_ref, o_ref):
    c = c_ref[...]
    a = c * jax.nn.sigmoid(c)
    w = w_ref[...]
    a_hi = a.astype(BF16)
    a_lo = (a - a_hi.astype(F32)).astype(BF16)
    w_hi = w.astype(BF16)
    w_lo = (w - w_hi.astype(F32)).astype(BF16)
    o_ref[...] = _dot(a_hi, w_hi) + _dot(a_lo, w_hi) + _dot(a_hi, w_lo) + b_ref[...]


def _ada(c, w, b):
    nb = c.shape[0]
    n = w.shape[1]
    tn = ADA_COL_TILE
    return pl.pallas_call(
        _ada_kernel,
        grid=(n // tn,),
        in_specs=[pl.BlockSpec((nb, D), lambda j: (0, 0)),
                  pl.BlockSpec((D, tn), lambda j: (0, j)),
                  pl.BlockSpec((1, tn), lambda j: (0, j))],
        out_specs=pl.BlockSpec((nb, tn), lambda j: (0, j)),
        out_shape=jax.ShapeDtypeStruct((nb, n), F32),
        compiler_params=_cp(("parallel",)),
        name="ada",
    )(c, w, b.reshape(1, n))


def _inproj_kernel(x_ref, mod_ref, gain_ref, cos_ref, s1_ref, s2_ref,
                   w_hga_ref, w_hgf_ref, w_at_ref, w_gt_ref,
                   hga_ref, hgf_ref, at0_ref, at1_ref, at2_ref, gt_ref, stage_ref):
    x = x_ref[...]
    tm = x.shape[0]
    sh = mod_ref[0, 0:1, :]
    sc = mod_ref[0, 1:2, :]
    h = _rms(x, gain_ref[...]) * (1.0 + sc) + sh
    hb = h.astype(BF16)
    hga_ref[...] = _dot(hb, w_hga_ref[...]).astype(BF16)
    hgf_ref[...] = _dot(hb, w_hgf_ref[...])
    cos = cos_ref[...]
    s1 = s1_ref[...]
    s2 = s2_ref[...]
    for g, o_ref in enumerate((at0_ref, at1_ref, at2_ref)):
        acc = _dot(hb, w_at_ref[:, g * 3 * AT_GW:(g + 1) * 3 * AT_GW])
        parts = []
        for j in range(2 * AT_GW // LANES):
            a = acc[:, j * LANES:(j + 1) * LANES]
            parts.append(a * cos + pltpu.roll(a, ROPE_DIM // 2, 1) * s1
                         + pltpu.roll(a, LANES - ROPE_DIM // 2, 1) * s2)
        parts.append(acc[:, 2 * AT_GW:])
        vals = jnp.concatenate(parts, axis=1)
        dil = AT_DILS[g]
        if dil == 1:
            o_ref[0, 0] = vals.astype(BF16)
        else:
            n_lt = 3 * AT_GW // LANES
            for c in range(n_lt):
                stage_ref[c] = vals[:, c * LANES:(c + 1) * LANES]
            for r in range(dil):
                o_ref[0, r] = jnp.concatenate(
                    [stage_ref[c, pl.ds(r, tm // dil, stride=dil), :] for c in range(n_lt)],
                    axis=1).astype(BF16)
    gt_ref[...] = _dot(hb, w_gt_ref[...]).astype(BF16)


def _inproj(x2, mod3, gain, rope, w_hga, w_hgf, w_at, w_gt, seq, tm):
    t = x2.shape[0]
    n_pos_blk = seq // tm
    row = lambda w: pl.BlockSpec((tm, w), lambda i: (i, 0))
    pos = pl.BlockSpec((tm, LANES), lambda i: (i % n_pos_blk, 0))
    batch = t // seq
    dilated = lambda d: pl.BlockSpec((1, d, tm // d, 3 * AT_GW),
                                     lambda i: ((i * tm) // seq, 0, i % n_pos_blk, 0))
    at_shape = lambda d: jax.ShapeDtypeStruct((batch, d, seq // d, 3 * AT_GW), BF16)
    return pl.pallas_call(
        _inproj_kernel,
        grid=(t // tm,),
        in_specs=[row(D),
                  pl.BlockSpec((1, 6, D), lambda i: ((i * tm) // seq, 0, 0)),
                  _const_spec((1, D)), pos, pos, pos,
                  _const_spec(w_hga.shape), _const_spec(w_hgf.shape),
                  _const_spec(w_at.shape), _const_spec(w_gt.shape)],
        out_specs=[row(3 * HG_W), row(2 * HG_W)] + [dilated(d) for d in AT_DILS] + [row(2 * D)],
        out_shape=[jax.ShapeDtypeStruct((t, 3 * HG_W), BF16),
                   jax.ShapeDtypeStruct((t, 2 * HG_W), F32)]
                  + [at_shape(d) for d in AT_DILS]
                  + [jax.ShapeDtypeStruct((t, 2 * D), BF16)],
        scratch_shapes=[pltpu.VMEM((3 * AT_GW // LANES, tm, LANES), F32)],
        compiler_params=_cp(("parallel",)),
        name="inproj",
    )(x2, mod3, gain, *rope, w_hga, w_hgf, w_at, w_gt)


def _rope_tables(seq):
    half = ROPE_DIM // 2
    inv_freq = ROPE_THETA ** (-np.arange(half, dtype=np.float32) / half)
    ang = jnp.arange(seq, dtype=F32)[:, None] * jnp.asarray(inv_freq)[None, :]
    cos, sin = jnp.cos(ang), jnp.sin(ang)
    ones = jnp.ones((seq, AT_HD - ROPE_DIM), F32)
    zeros = jnp.zeros((seq, AT_HD - ROPE_DIM), F32)
    zh = jnp.zeros((seq, half), F32)
    c_head = jnp.concatenate([cos, cos, ones], axis=1)
    s1_head = jnp.concatenate([zh, sin, zeros], axis=1)
    s2_head = jnp.concatenate([-sin, zh, zeros], axis=1)
    rep = LANES // AT_HD
    return tuple(jnp.tile(a, (1, rep)) for a in (c_head, s1_head, s2_head))


def _hgrn_consts(reverse):
    c = HG_C
    t = np.arange(c)[:, None]
    u = np.arange(c)[None, :]
    tri = (u >= t) if reverse else (u <= t)
    sels = []
    for lvl in range(HG_LEVELS):
        blk = 2 << lvl
        start = (np.arange(c) // blk) * blk
        piv = start + (1 << lvl) - (0 if reverse else 1)
        sels.append(np.arange(c)[None, :] == piv[:, None])
    sel = np.concatenate(sels, axis=0)
    rep = lambda m, n: jnp.asarray(np.concatenate([m.astype(np.float32)] * n, axis=1), BF16)
    return rep(tri, 2), rep(sel, 3)


def _hgrn_kernel(*refs, reverse, final, n_chunks):
    if final:
        (lbl_ref, tri_ref, sel_ref, q_ref, f_ref, v_ref, ofw_ref, gh_ref, gain_ref,
         o_ref, st_ref, st0_ref) = refs
    else:
        (lbl_ref, tri_ref, sel_ref, q_ref, f_ref, v_ref,
         o_ref, st_ref, st0_ref) = refs
    c = HG_C

    @pl.when(pl.program_id(1) == 0)
    def _():
        st_ref[...] = jnp.zeros_like(st_ref)

    l0 = lbl_ref[0:1, :]
    l1 = lbl_ref[1:2, :]
    lm = jnp.maximum(l0, l1)
    e0 = jnp.exp(l0 - lm)
    e1 = jnp.exp(l1 - lm)
    lb = e0 / (e0 + e1)
    tri = tri_ref[...]
    piv_row = c // 2 if reverse else c // 2 - 1

    def gate_math(r0):
        ff = f_ref[pl.ds(r0, c), :]
        e = jnp.exp(-jnp.abs(ff))
        r = 1.0 / (1.0 + e)
        sg = jnp.where(ff >= 0.0, r, e * r)
        g = jnp.log(lb + (1.0 - lb) * sg)
        kin = (1.0 - lb) * (1.0 - sg)
        g_hi = g.astype(BF16)
        g_lo = (g - g_hi.astype(F32)).astype(BF16)
        b = _dot(tri, jnp.concatenate([g_hi, g_lo], axis=0))
        btot = b[0:1, :] if reverse else b[c - 1:c, :]
        return kin, b, btot

    ti = lax.broadcasted_iota(I32, (c, c), 0)
    si = lax.broadcasted_iota(I32, (c, c), 1)
    causal = (si >= ti) if reverse else (si <= ti)
    row = lax.broadcasted_iota(I32, (c, 1), 0)

    def att_robust(q, kin, b, piv):
        att = jnp.where(ti == si, _dot_nt(q.astype(BF16), kin.astype(BF16)), 0.0)
        for lvl in range(HG_LEVELS):
            p = piv[lvl * c:(lvl + 1) * c, :]
            bit = ((row >> lvl) & 1) == 1
            q_side = jnp.logical_not(bit) if reverse else bit
            qe = jnp.where(q_side, q * jnp.exp(jnp.minimum(b - p, 0.0)), 0.0).astype(BF16)
            ke = jnp.where(q_side, 0.0, kin * jnp.exp(jnp.minimum(p - b, 0.0))).astype(BF16)
            same = (ti >> (lvl + 1)) == (si >> (lvl + 1))
            att = att + jnp.where(same, _dot_nt(qe, ke), 0.0)
        return att

    heads = [slice(h * HG_DK, (h + 1) * HG_DK) for h in range(HG_H)]


    def emit(r0, outs):
        o = jnp.concatenate(outs, axis=1)
        if final:
            o = o + ofw_ref[pl.ds(r0, c), :]
            normed = []
            for sl in heads:
                ms = jnp.mean(o[:, sl] * o[:, sl], axis=-1, keepdims=True)
                normed.append(o[:, sl] * lax.rsqrt(ms + EPS))
            gh = gh_ref[pl.ds(r0, c), :].astype(F32)
            o = jnp.concatenate(normed, axis=1) * gain_ref[...] * (gh * jax.nn.sigmoid(gh))
        o_ref[pl.ds(r0, c), :] = o.astype(o_ref.dtype)

    def robust_step(i, carry):
        ci = (n_chunks - 1 - i) if reverse else i
        r0 = pl.multiple_of(ci * c, c)
        q = q_ref[pl.ds(r0, c), :].astype(F32)
        v = v_ref[pl.ds(r0, c), :]
        kin, b, btot = gate_math(r0)
        qe_all = (q * jnp.exp(b)).astype(BF16)
        kd_all = (kin * jnp.exp(btot - b)).astype(BF16)
        piv = _dot(sel_ref[...], jnp.concatenate(_split3(b), axis=0))
        outs = []
        for h, sl in enumerate(heads):
            att = att_robust(q[:, sl], kin[:, sl], b[:, sl], piv[:, sl])
            st = st_ref[h]
            outs.append(_dot_nt(qe_all[:, sl], st.astype(BF16)) + _dot(att.astype(BF16), v[:, sl]))
            st_ref[h] = st * jnp.exp(btot[:, sl]) + _dot_tn(v[:, sl], kd_all[:, sl])
        emit(r0, outs)
        return carry

    grp = HG_UNROLL

    def fast_group(i, bmin):
        gi = (n_chunks // grp - 1 - i) if reverse else i
        base = gi * (grp * c)
        order = range(grp - 1, -1, -1) if reverse else range(grp)
        chunks = []
        for j in order:
            r0 = pl.multiple_of(base + j * c, c)
            kin, b, btot = gate_math(r0)
            bp = b[piv_row:piv_row + 1, :]
            bmin = jnp.minimum(bmin, jnp.minimum(bp, btot - bp))
            to_piv = jnp.exp(b - bp)
            k_piv = kin / to_piv
            chunks.append(dict(
                r0=r0, v=v_ref[pl.ds(r0, c), :], dec=jnp.exp(btot), pdec=jnp.exp(bp),
                qe=q_ref[pl.ds(r0, c), :] * to_piv.astype(BF16),
                ke=k_piv.astype(BF16),
                kd=(k_piv * jnp.exp(btot - bp)).astype(BF16)))
        for ch in chunks:
            ch["att"] = [jnp.where(causal, _dot_nt(ch["qe"][:, sl], ch["ke"][:, sl]), 0.0).astype(BF16)
                         for sl in heads]
        for ch in chunks:
            ch["intra"] = [_dot(ch["att"][h], ch["v"][:, sl]) for h, sl in enumerate(heads)]
            ch["upd"] = [_dot_tn(ch["v"][:, sl], ch["kd"][:, sl]) for sl in heads]
        st = [st_ref[h] for h in range(HG_H)]
        for ch in chunks:
            outs = []
            for h, sl in enumerate(heads):
                st_piv = (st[h] * ch["pdec"][:, sl]).astype(BF16)
                outs.append(_dot_nt(ch["qe"][:, sl], st_piv) + ch["intra"][h])
                st[h] = st[h] * ch["dec"][:, sl] + ch["upd"][h]
            emit(ch["r0"], outs)
        for h in range(HG_H):
            st_ref[h] = st[h]
        return bmin

    st0_ref[...] = st_ref[...]
    bmin = lax.fori_loop(0, n_chunks // grp, fast_group, jnp.zeros((1, HG_W), F32))

    @pl.when(jnp.min(bmin) < HG_SAFE_LOGDECAY)
    def _():
        st_ref[...] = st0_ref[...]
        lax.fori_loop(0, n_chunks, robust_step, 0)


def _hgrn(lb_logits_d, hga, hgf, batch, seq, tc, reverse, o_fw=None, gain=None):
    final = o_fw is not None
    t = hga.shape[0]
    nblk = seq // tc
    tri, sel = _hgrn_consts(reverse)

    def rows(col):
        if reverse:
            return lambda b, j: (b * nblk + nblk - 1 - j, col)
        return lambda b, j: (b * nblk + j, col)

    blk = lambda col: pl.BlockSpec((tc, HG_W), rows(col))
    in_specs = [_const_spec((2, HG_W)), _const_spec(tri.shape), _const_spec(sel.shape),
                blk(0), blk(1 if reverse else 0), blk(1)]
    args = [lb_logits_d, tri, sel, hga, hgf, hga]
    if final:
        in_specs += [blk(0), blk(2), _const_spec((1, HG_W))]
        args += [o_fw, hga, gain]
    return pl.pallas_call(
        functools.partial(_hgrn_kernel, reverse=reverse, final=final, n_chunks=tc // HG_C),
        grid=(batch, nblk),
        in_specs=in_specs,
        out_specs=blk(0),
        out_shape=jax.ShapeDtypeStruct((t, HG_W), BF16 if final else F32),
        scratch_shapes=[pltpu.VMEM((HG_H, HG_DK, HG_DK), F32),
                        pltpu.VMEM((HG_H, HG_DK, HG_DK), F32)],
        compiler_params=_cp(("parallel", "arbitrary")),
        name="hgrn_bwd" if reverse else "hgrn_fwd",
    )(*args)


def _attn_kernel(q_ref, kl_ref, km_ref, kr_ref, vl_ref, vm_ref, vr_ref, o_ref, l_ref,
                 k_s, v_s, *, tq, ld):
    hw = AT_HALF
    n_res = q_ref.shape[0]
    for res in range(n_res):
        k_s[res, 0:hw, :] = kl_ref[res]
        k_s[res, hw:hw + tq, :] = km_ref[res]
        k_s[res, hw + tq:, :] = kr_ref[res]
        v_s[res, 0:hw, :] = vl_ref[res]
        v_s[res, hw:hw + tq, :] = vm_ref[res]
        v_s[res, hw + tq:, :] = vr_ref[res]
    qb = 2 * hw
    kb = 4 * hw
    base = pl.program_id(2) * tq
    lane_head = lax.broadcasted_iota(I32, (1, AT_GW), 1) // AT_HD
    q_sel = [jnp.where(lane_head == h, AT_HD ** -0.5, 0.0).astype(BF16) for h in range(AT_H)]
    out_head = lax.broadcasted_iota(I32, (qb, AT_GW), 1) // AT_HD
    rel = (lax.broadcasted_iota(I32, (AT_H * qb, kb), 0) % qb
           - lax.broadcasted_iota(I32, (AT_H * qb, kb), 1) + hw)
    band = jnp.where(jnp.abs(rel) <= hw, 0.0, NEG_BIG)
    key_col = lax.broadcasted_iota(I32, (1, kb), 1)
    for res, qs in [(res, qs) for res in range(n_res) for qs in range(0, tq, qb)]:
        q = q_ref[res, qs:qs + qb, :]
        kk = k_s[res, qs:qs + kb, :]
        vv = v_s[res, qs:qs + kb, :]
        q4 = jnp.concatenate([q * q_sel[h] for h in range(AT_H)], axis=0)
        s = _dot_nt(q4, kk) + band
        if qs == 0 or qs + qb == tq:
            kpos = base + qs - hw + key_col
            s = s + jnp.where((kpos >= 0) & (kpos < ld), 0.0, NEG_BIG)
        m = jnp.max(s, axis=-1, keepdims=True)
        p = jnp.exp(s - m)
        l = jnp.sum(p, axis=-1, keepdims=True)
        pv = _dot(p.astype(BF16), vv) * (1.0 / l)
        lse = m + jnp.log(l)
        o_all = jnp.zeros((qb, AT_GW), F32)
        l_all = jnp.zeros((qb, AT_GW), F32)
        for h in range(AT_H):
            o_all = jnp.where(out_head == h, pv[h * qb:(h + 1) * qb, :], o_all)
            l_all = jnp.where(out_head == h, lse[h * qb:(h + 1) * qb, :], l_all)
        o_ref[res, qs:qs + qb, :] = o_all.astype(o_ref.dtype)
        l_ref[res, qs:qs + qb, :] = l_all


def _attn_group(qkv):
    batch, dil, ld, _ = qkv.shape
    hw = AT_HALF
    tq = min(ld, SEQ_TILE)
    nq = ld // tq
    nh = ld // hw
    per = tq // hw
    n_res = min(dil, max(1, SEQ_TILE // tq))
    main = lambda part: pl.BlockSpec((None, n_res, tq, AT_GW), lambda b, r, j: (b, r, j, part))
    left = lambda part: pl.BlockSpec(
        (None, n_res, hw, AT_GW), lambda b, r, j: (b, r, jnp.maximum(j * per - 1, 0), part))
    right = lambda part: pl.BlockSpec(
        (None, n_res, hw, AT_GW), lambda b, r, j: (b, r, jnp.minimum((j + 1) * per, nh - 1), part))
    out_spec = pl.BlockSpec((None, n_res, tq, AT_GW), lambda b, r, j: (b, r, j, 0))
    return pl.pallas_call(
        functools.partial(_attn_kernel, tq=tq, ld=ld),
        grid=(batch, dil // n_res, nq),
        in_specs=[main(0), left(1), main(1), right(1), left(2), main(2), right(2)],
        out_specs=[out_spec, out_spec],
        out_shape=[jax.ShapeDtypeStruct((batch, dil, ld, AT_GW), BF16),
                   jax.ShapeDtypeStruct((batch, dil, ld, AT_GW), F32)],
        scratch_shapes=[pltpu.VMEM((n_res, tq + 2 * hw, AT_GW), BF16),
                        pltpu.VMEM((n_res, tq + 2 * hw, AT_GW), BF16)],
        compiler_params=_cp(("parallel", "parallel", "parallel")),
        name=f"attn_d{dil}",
    )(qkv, qkv, qkv, qkv, qkv, qkv, qkv)


def _merge_kernel(ohg_ref, o0_ref, o1_ref, o2_ref, l0_ref, l1_ref, l2_ref, gt_ref, x_ref, mod_ref,
                  w_hg_ref, w_at_ref, w_out_ref, npost_ref, npre_ref, w_r_ref, b_r_ref,
                  x1_ref, h2_ref, ids_ref, wts_ref, cnt_ref,
                  so1_ref, sl1_ref, so2_ref, sl2_ref):
    tm = x_ref.shape[0]
    sub = tm // MERGE_SUBTILES
    rows = [slice(i * sub, (i + 1) * sub) for i in range(MERGE_SUBTILES)]
    gt1 = mod_ref[0, 2:3, :]
    sh2 = mod_ref[0, 3:4, :]
    sc2 = mod_ref[0, 4:5, :]

    for src_ref, dst_ref in ((o1_ref, so1_ref), (l1_ref, sl1_ref), (o2_ref, so2_ref), (l2_ref, sl2_ref)):
        dil = src_ref.shape[0]
        for res in range(dil):
            vals = src_ref[res].astype(F32)
            for c in range(AT_GW // LANES):
                dst_ref[c, pl.ds(res, tm // dil, stride=dil), :] = vals[:, c * LANES:(c + 1) * LANES]

    def staged(ref, r):
        return jnp.concatenate([ref[c, r, :] for c in range(AT_GW // LANES)], axis=1)

    def branches(r):
        o0, l0 = o0_ref[0, r, :].astype(F32), l0_ref[0, r, :]
        o1, l1 = staged(so1_ref, r), staged(sl1_ref, r)
        o2, l2 = staged(so2_ref, r), staged(sl2_ref, r)
        m = jnp.maximum(jnp.maximum(l0, l1), l2)
        e0, e1, e2 = jnp.exp(l0 - m), jnp.exp(l1 - m), jnp.exp(l2 - m)
        oa = (e0 * o0 + e1 * o1 + e2 * o2) / (e0 + e1 + e2)
        return _dot(ohg_ref[r, :], w_hg_ref[...]), _dot(oa.astype(BF16), w_at_ref[...])

    def out_proj(r, b_hg, b_at):
        g_hg = jax.nn.sigmoid(gt_ref[r, 0:D])
        g_at = jax.nn.sigmoid(gt_ref[r, D:2 * D])
        merged = g_hg * b_hg.astype(BF16) + g_at * b_at.astype(BF16)
        return _dot(merged, w_out_ref[...])

    def norms(r, y):
        x1 = x_ref[r, :] + gt1 * _rms(y, npost_ref[...])
        x1_ref[r, :] = x1
        h2 = _rms(x1, npre_ref[...]) * (1.0 + sc2) + sh2
        h2_ref[r, :] = _pack_rows(h2)
        return _dot(h2.astype(BF16), w_r_ref[...]) + b_r_ref[...]

    def top_k(r, logits):
        lane = lax.broadcasted_iota(I32, logits.shape, 1)
        lane_f = lane.astype(F32)
        work = logits
        vals, idxs = [], []
        sel = jnp.zeros(logits.shape, F32)
        for _ in range(TOP_K):
            mk = jnp.max(work, axis=-1, keepdims=True)
            ik = jnp.min(jnp.where(work == mk, lane_f, float(LANES)), axis=-1, keepdims=True)
            hit = lane_f == ik
            sel = jnp.where(hit, 1.0, sel)
            work = jnp.where(hit, -jnp.inf, work)
            vals.append(mk)
            idxs.append(ik)
        es = [jnp.exp(v - vals[0]) for v in vals]
        den = es[0] + es[1] + es[2] + es[3]
        ids = jnp.zeros(logits.shape, F32)
        wts = jnp.zeros(logits.shape, F32)
        for k in range(TOP_K):
            ids = jnp.where(lane == k, idxs[k], ids)
            wts = jnp.where(lane == k, es[k] / den, wts)
        ids_ref[r, :] = ids.astype(I32)
        wts_ref[r, :] = wts
        return jnp.sum(sel, axis=0, keepdims=True)

    br = [branches(r) for r in rows]
    ys = [out_proj(r, *b) for r, b in zip(rows, br)]
    logits = [norms(r, y) for r, y in zip(rows, ys)]
    counts = [top_k(r, lg) for r, lg in zip(rows, logits)]
    cnt_ref[0] = functools.reduce(lambda a, b: a + b, counts)


def _merge(ohg, att_o, att_l, gates, x2, mod3, w_hg, w_at, w_out, npost, npre, w_r, b_r, seq, tm):
    t = x2.shape[0]
    nt = t // tm
    row = lambda w: pl.BlockSpec((tm, w), lambda i: (i, 0))
    slab_i = jax.ShapeDtypeStruct((t, LANES), I32)
    slab_f = jax.ShapeDtypeStruct((t, LANES), F32)
    nblk = seq // tm
    grouped = [pl.BlockSpec((None, d, tm // d, AT_GW),
                            lambda i: ((i * tm) // seq, 0, i % nblk, 0)) for d in AT_DILS]
    return pl.pallas_call(
        _merge_kernel,
        grid=(nt,),
        in_specs=[row(HG_W)] + grouped + grouped + [row(2 * D), row(D),
                  pl.BlockSpec((1, 6, D), lambda i: ((i * tm) // seq, 0, 0)),
                  _const_spec(w_hg.shape), _const_spec(w_at.shape), _const_spec(w_out.shape),
                  _const_spec((1, D)), _const_spec((1, D)),
                  _const_spec(w_r.shape), _const_spec(b_r.shape)],
        out_specs=[row(D), row(HALF_D), row(LANES), row(LANES),
                   pl.BlockSpec((1, 1, LANES), lambda i: (i, 0, 0))],
        out_shape=[jax.ShapeDtypeStruct((t, D), F32), jax.ShapeDtypeStruct((t, HALF_D), U32),
                   slab_i, slab_f, jax.ShapeDtypeStruct((nt, 1, LANES), F32)],
        scratch_shapes=[pltpu.VMEM((AT_GW // LANES, tm, LANES), F32)] * 4,
        compiler_params=_cp(("parallel",)),
        name="merge",
    )(ohg, *att_o, *att_l, gates, x2, mod3, w_hg, w_at, w_out, npost, npre, w_r, b_r)


def _route_kernel(ids_ref, base_ref, tril_ref, dest_ref):
    ids = ids_ref[...]
    lane = lax.broadcasted_iota(I32, ids.shape, 1)
    hits = [lane == ids[:, k:k + 1] for k in range(TOP_K)]
    sel = jnp.zeros(ids.shape, F32)
    for hit in hits:
        sel = jnp.where(hit, 1.0, sel)
    before = _dot(tril_ref[...], sel.astype(BF16)) + base_ref[0]
    dest = jnp.zeros(ids.shape, I32)
    for k, hit in enumerate(hits):
        rk = jnp.sum(jnp.where(hit, before, 0.0), axis=-1, keepdims=True)
        dest = jnp.where(lane == k, rk.astype(I32), dest)
    dest_ref[...] = dest


def _route(ids, tile_base, tm):
    t = ids.shape[0]
    tril = jnp.asarray(np.tril(np.ones((tm, tm), np.float32), -1), BF16)
    return pl.pallas_call(
        _route_kernel,
        grid=(t // tm,),
        in_specs=[pl.BlockSpec((tm, LANES), lambda i: (i, 0)),
                  pl.BlockSpec((1, 1, LANES), lambda i: (i, 0, 0)),
                  _const_spec((tm, tm))],
        out_specs=pl.BlockSpec((tm, LANES), lambda i: (i, 0)),
        out_shape=jax.ShapeDtypeStruct((t, LANES), I32),
        compiler_params=_cp(("parallel",)),
        name="route",
    )(ids, tile_base, tril)


def _moe_kernel(be_ref, rows_ref, x_ref, *refs, prepare):
    i = pl.program_id(0)
    if prepare:
        wup_ref, bg_ref, bl_ref, wdn_ref, bd_ref, perm_ref, o_ref, wg_ref, wl_ref, wd_ref = refs

        @pl.when((i == 0) | (be_ref[i] != be_ref[jnp.maximum(i - 1, 0)]))
        def _():
            _deinterleave_into(wup_ref, perm_ref, wg_ref.at[0], wl_ref.at[0])
            wd_ref[0] = wdn_ref[0].astype(BF16)
    else:
        wg_ref, wl_ref, bg_ref, bl_ref, wd_ref, bd_ref, o_ref = refs

    def ffn(x_packed):
        x_lo, x_hi = (a.astype(BF16) for a in _unpack_rows(x_packed))
        gate = _dot(x_lo, wg_ref[0, :HALF_D, :]) + _dot(x_hi, wg_ref[0, HALF_D:, :]) + bg_ref[0]
        up = _dot(x_lo, wl_ref[0, :HALF_D, :]) + _dot(x_hi, wl_ref[0, HALF_D:, :]) + bl_ref[0]
        gate = jnp.minimum(gate, SWIGLU_LIMIT)
        up = jnp.clip(up, -SWIGLU_LIMIT, SWIGLU_LIMIT)
        act = (up + 1.0) * gate * jax.nn.sigmoid(SWIGLU_ALPHA * gate)
        return _pack_rows(_dot(act.astype(BF16), wd_ref[0]) + bd_ref[0])

    valid = rows_ref[i]
    half = x_ref.shape[0] // 2

    @pl.when(valid > half)
    def _():
        o_ref[...] = ffn(x_ref[...])

    @pl.when((valid > 0) & (valid <= half))
    def _():
        o_ref[:half, :] = ffn(x_ref[:half, :])
        o_ref[half:, :] = jnp.zeros((half, HALF_D), U32)

    @pl.when(valid == 0)
    def _():
        o_ref[...] = jnp.zeros_like(o_ref)


def _moe(block_expert, block_rows, xb, wts, bm, prepared):
    r = xb.shape[0]
    nblk = r // bm
    rows = pl.BlockSpec((bm, HALF_D), lambda i, be, br: (i, 0))
    ew = lambda shape: pl.BlockSpec((1,) + shape, lambda i, be, br: (be[i], 0, 0))
    y_shape = jax.ShapeDtypeStruct((r, HALF_D), U32)
    if prepared is None:
        perm = _pair_split_perm()
        w_shape = jax.ShapeDtypeStruct((N_EXP, D, D), BF16)
        in_specs = [rows, ew((D, 2 * D)), ew((1, D)), ew((1, D)), ew((D, D)), ew((1, D)),
                    pl.BlockSpec(perm.shape, lambda i, be, br: (0, 0))]
        args = (wts["w_up"], wts["bg"], wts["bl"], wts["w_down"], wts["bd"], perm)
        out_specs = [rows, ew((D, D)), ew((D, D)), ew((D, D))]
        out_shape = [y_shape, w_shape, w_shape, w_shape]
    else:
        in_specs = [rows, ew((D, D)), ew((D, D)), ew((1, D)), ew((1, D)), ew((D, D)), ew((1, D))]
        args = (prepared[0], prepared[1], wts["bg"], wts["bl"], prepared[2], wts["bd"])
        out_specs = [rows]
        out_shape = [y_shape]
    outs = pl.pallas_call(
        functools.partial(_moe_kernel, prepare=prepared is None),
        grid_spec=pltpu.PrefetchScalarGridSpec(
            num_scalar_prefetch=2, grid=(nblk,), in_specs=in_specs, out_specs=out_specs),
        out_shape=out_shape,
        compiler_params=_cp(("arbitrary",)),
        name="moe_prepare" if prepared is None else "moe",
    )(block_expert, block_rows, xb, *args)
    return outs[0], (tuple(outs[1:]) if prepared is None else prepared)


def _final_kernel(y0_ref, y1_ref, y2_ref, y3_ref, wts_ref, x1_ref, mod_ref, npost_ref, o_ref):
    wts = wts_ref[...]
    y = jnp.zeros(x1_ref.shape, F32)
    for k, y_ref in enumerate((y0_ref, y1_ref, y2_ref, y3_ref)):
        y = y + wts[:, k:k + 1] * jnp.concatenate(_unpack_rows(y_ref[0]), axis=1)
    gt2 = mod_ref[0, 5:6, :]
    o_ref[...] = x1_ref[...] + gt2 * _rms(y, npost_ref[...])


def _final(yg, wts, x1, mod3, npost, seq, tm):
    t = x1.shape[0]
    slot = lambda k: pl.BlockSpec((1, tm, HALF_D), lambda i: (k, i, 0))
    return pl.pallas_call(
        _final_kernel,
        grid=(t // tm,),
        in_specs=[slot(0), slot(1), slot(2), slot(3),
                  pl.BlockSpec((tm, LANES), lambda i: (i, 0)),
                  pl.BlockSpec((tm, D), lambda i: (i, 0)),
                  pl.BlockSpec((1, 6, D), lambda i: ((i * tm) // seq, 0, 0)),
                  _const_spec((1, D))],
        out_specs=pl.BlockSpec((tm, D), lambda i: (i, 0)),
        out_shape=jax.ShapeDtypeStruct((t, D), F32),
        compiler_params=_cp(("parallel",)),
        name="final",
    )(yg, yg, yg, yg, wts, x1, mod3, npost)


def _tiles(seq):
    return dict(tm_in=min(seq, ROW_TILE), tc=min(seq, SEQ_TILE), tm_merge=min(seq, STREAM_TILE),
                bm=MOE_BLOCK, tm_final=min(seq, STREAM_TILE))


def _prep_weights(w_in, w_hg_out, w_att_out, w_out, w_router, b_router, w_up, b_up, w_down, b_down):
    w = w_in[0]
    hw = HG_W
    a0 = 5 * hw
    aw = AT_GW * len(AT_DILS)
    q_at, k_at, v_at = (w[:, a0 + i * aw:a0 + (i + 1) * aw] for i in range(3))
    grp = lambda m, g: m[:, g * AT_GW:(g + 1) * AT_GW]
    w_at = jnp.concatenate(
        [jnp.concatenate([grp(q_at, g), grp(k_at, g), grp(v_at, g)], axis=1)
         for g in range(len(AT_DILS))], axis=1)
    pad = LANES - N_EXP
    return dict(
        w_up=w_up[0],
        w_hga=jnp.concatenate([w[:, 0:hw], w[:, 3 * hw:4 * hw], w[:, 4 * hw:5 * hw]], 1).astype(BF16),
        w_hgf=w[:, hw:3 * hw].astype(BF16),
        w_at=w_at.astype(BF16),
        w_gt=w[:, a0 + 3 * aw:].astype(BF16),
        w_hg_out=w_hg_out[0].astype(BF16),
        w_att_out=w_att_out[0].astype(BF16),
        w_out=w_out[0].astype(BF16),
        w_r=jnp.pad(w_router[0], ((0, 0), (0, pad))).astype(BF16),
        b_r=jnp.pad(b_router[0], (0, pad), constant_values=NEG_BIG).reshape(1, LANES),
        bg=b_up[0][:, 0::2].reshape(N_EXP, 1, D),
        bl=b_up[0][:, 1::2].reshape(N_EXP, 1, D),
        w_down=w_down[0],
        bd=b_down[0].reshape(N_EXP, 1, D),
    )


def _trunk(x, mod, wts, norm_pre, norm_post, lb_logits, hg_gain, prepared=None):
    batch, seq, width = x.shape
    t = batch * seq
    tl = _tiles(seq)
    assert width == D and seq % SEQ_TILE == 0 and seq // max(AT_DILS) >= 2 * AT_HALF
    x2 = x.reshape(t, D)
    mod3 = mod.reshape(batch, 6, D)
    rope = _rope_tables(seq)

    hga, hgf, at0, at1, at2, gates = _inproj(
        x2, mod3, norm_pre[0, 0].reshape(1, D), rope,
        wts["w_hga"], wts["w_hgf"], wts["w_at"], wts["w_gt"], seq, tl["tm_in"])

    o_fw = _hgrn(lb_logits[0], hga, hgf, batch, seq, tl["tc"], reverse=False)
    ohg = _hgrn(lb_logits[1], hga, hgf, batch, seq, tl["tc"], reverse=True,
                o_fw=o_fw, gain=hg_gain[0].reshape(1, HG_W))

    att = [_attn_group(a) for a in (at0, at1, at2)]

    x1, h2, ids, rw, cnt = _merge(
        ohg, [a[0] for a in att], [a[1] for a in att], gates, x2, mod3,
        wts["w_hg_out"], wts["w_att_out"], wts["w_out"],
        norm_post[0, 0].reshape(1, D), norm_pre[0, 1].reshape(1, D),
        wts["w_r"], wts["b_r"], seq, tl["tm_merge"])

    bm = tl["bm"]
    tmr = tl["tm_merge"]
    cnt_tiles = cnt.reshape(t // tmr, LANES).astype(I32)
    total = jnp.sum(cnt_tiles, axis=0)
    min_blocks = (jnp.arange(LANES) < N_EXP).astype(I32)
    padded = jnp.maximum((total + bm - 1) // bm, min_blocks) * bm
    pad_end = jnp.cumsum(padded)
    pad_start = pad_end - padded
    tile_base = pad_start[None, :] + jnp.cumsum(cnt_tiles, axis=0) - cnt_tiles
    dest = _route(ids, tile_base.astype(F32).reshape(-1, 1, LANES), tmr)[:, :TOP_K]

    n_rows = t * TOP_K + N_EXP * bm
    n_blocks = n_rows // bm
    blk_start = jnp.arange(n_blocks, dtype=I32) * bm
    block_expert = jnp.minimum(
        jnp.sum(pad_end[None, :N_EXP] <= blk_start[:, None], axis=1), N_EXP - 1).astype(I32)
    block_rows = jnp.clip(pad_start[block_expert] + total[block_expert] - blk_start, 0, bm)
    block_rows = jnp.where(blk_start < pad_end[N_EXP - 1], block_rows, 0).astype(I32)

    dest_kt = dest.T
    xb = _sc_scatter_rows(h2, dest_kt, n_rows)
    yb, prepared = _moe(block_expert, block_rows, xb, wts, bm, prepared)
    yg = _sc_gather_rows(yb, dest_kt.reshape(-1)).reshape(TOP_K, t, HALF_D)

    out = _final(yg, rw, x1, mod3, norm_post[0, 1].reshape(1, D), seq, tl["tm_final"])
    return out.reshape(batch, seq, D), prepared


def kernel(x_prompt, x_sample, c_prompt, c_sample, w_ada, b_ada, norm_pre, norm_post, w_in,
           lb_logits, hg_norm_gain, w_hg_out, w_att_out, w_out, w_router, b_router,
           w_up, b_up, w_down, b_down):
    wts = _prep_weights(w_in, w_hg_out, w_att_out, w_out, w_router, b_router,
                        w_up, b_up, w_down, b_down)
    nb = c_prompt.shape[0]
    mod = _ada(jnp.concatenate([c_prompt, c_sample], axis=0), w_ada[0], b_ada[0])
    lb_l = lb_logits.astype(F32)
    y_p, prepared = _trunk(x_prompt, mod[:nb], wts, norm_pre, norm_post, lb_l, hg_norm_gain)
    y_s, _ = _trunk(x_sample, mod[nb:], wts, norm_pre, norm_post, lb_l, hg_norm_gain, prepared)
    return (y_p, y_s)
```

```python
import functools

import numpy as np
import jax
import jax.numpy as jnp
from jax import lax
from jax.experimental import pallas as pl
from jax.experimental.pallas import tpu as pltpu
from jax.experimental.pallas import tpu_sc as plsc

F32 = jnp.float32
BF16 = jnp.bfloat16
I32 = jnp.int32
U32 = jnp.uint32

D = 1024
EPS = 1e-6
HG_H = 4
HG_DK = 128
HG_W = HG_H * HG_DK
HG_C = 64
HG_LEVELS = 6
HG_SAFE_LOGDECAY = -80.0
HG_UNROLL = 8
AT_DILS = (1, 4, 16)
AT_HALF = 64
AT_H = 4
AT_HD = 64
AT_GW = AT_H * AT_HD
ROPE_DIM = 16
ROPE_THETA = 500000.0
N_EXP = 32
TOP_K = 4
SWIGLU_LIMIT = 7.0
SWIGLU_ALPHA = 1.702
MERGE_SUBTILES = 4
ROW_TILE = 512
STREAM_TILE = 1024
SEQ_TILE = 2048
ADA_COL_TILE = 1536
MOE_BLOCK = 1024
ROUTE_ROWS = 8
LANES = 128
NEG_BIG = -1e30
HALF_D = D // 2
SC_CORES = 2
SC_SUBCORES = 16
SC_WORKERS = SC_CORES * SC_SUBCORES
SC_CHUNK = 128

VMEM_LIMIT = 56 * 1024 * 1024


def _cp(sem, vmem=VMEM_LIMIT):
    return pltpu.CompilerParams(dimension_semantics=sem, vmem_limit_bytes=vmem)


def _dot(a, b):
    return jnp.dot(a, b, preferred_element_type=F32)


def _dot_nt(a, b):
    return lax.dot_general(a, b, (((1,), (1,)), ((), ())), preferred_element_type=F32)


def _dot_tn(a, b):
    return lax.dot_general(a, b, (((0,), (0,)), ((), ())), preferred_element_type=F32)


def _split3(x):
    hi = x.astype(BF16)
    r = x - hi.astype(F32)
    mid = r.astype(BF16)
    lo = (r - mid.astype(F32)).astype(BF16)
    return hi, mid, lo


def _rms(x, gain):
    ms = jnp.mean(x * x, axis=-1, keepdims=True)
    return x * lax.rsqrt(ms + EPS) * gain


def _const_spec(shape):
    n = len(shape)
    return pl.BlockSpec(shape, lambda *_: (0,) * n)


def _pack_rows(y):
    bits = lambda a: lax.bitcast_convert_type(a.astype(BF16).astype(F32), U32)
    return (bits(y[:, :HALF_D]) >> 16) | (bits(y[:, HALF_D:]) & jnp.uint32(0xFFFF0000))


def _unpack_rows(w):
    lo = lax.bitcast_convert_type(w << 16, F32)
    hi = lax.bitcast_convert_type(w & jnp.uint32(0xFFFF0000), F32)
    return lo, hi


def _deinterleave_into(w_ref, p_ref, g_ref, l_ref):
    p = p_ref[...]
    n_grp = w_ref.shape[2] // (2 * LANES)
    for b in range(n_grp):
        blk = w_ref[0, :, b * 2 * LANES:(b + 1) * 2 * LANES].astype(BF16)
        r = _dot(blk, p)
        g_ref[:, b * LANES:(b + 1) * LANES] = r[:, :LANES].astype(BF16)
        l_ref[:, b * LANES:(b + 1) * LANES] = r[:, LANES:].astype(BF16)


def _pair_split_perm():
    perm = np.zeros((2 * LANES, 2 * LANES), np.float32)
    perm[2 * np.arange(LANES), np.arange(LANES)] = 1.0
    perm[2 * np.arange(LANES) + 1, LANES + np.arange(LANES)] = 1.0
    return jnp.asarray(perm, BF16)


def _sc_mesh():
    return plsc.VectorSubcoreMesh(core_axis_name="c", subcore_axis_name="s")


def _sc_worker():
    return lax.axis_index("s") * SC_CORES + lax.axis_index("c")


def _sc_scatter_rows(src, dest_kt, n_rows):
    t, w = src.shape
    n_slot = dest_kt.shape[0]
    assert t % (SC_WORKERS * SC_CHUNK) == 0
    nch = t // (SC_WORKERS * SC_CHUNK)
    idx = dest_kt.reshape(n_slot, SC_WORKERS, nch, SC_CHUNK).transpose(1, 0, 2, 3)

    @functools.partial(
        pl.kernel, mesh=_sc_mesh(),
        out_type=jax.ShapeDtypeStruct((n_rows, w), src.dtype),
        scratch_types=[pltpu.VMEM((n_slot, nch, SC_CHUNK), I32),
                       pltpu.VMEM((SC_CHUNK, w), src.dtype)],
        name="sc_scatter")
    def run(src_hbm, idx_hbm, out_hbm, idx_v, rows_v):
        wid = _sc_worker()
        pltpu.sync_copy(idx_hbm.at[wid], idx_v)

        @pl.loop(0, nch)
        def _(j):
            pltpu.sync_copy(src_hbm.at[pl.ds((wid * nch + j) * SC_CHUNK, SC_CHUNK)], rows_v)
            for k in range(n_slot):
                pltpu.sync_copy(rows_v, out_hbm.at[idx_v.at[k, j]])

    return run(src, idx)


def _sc_gather_rows(table, idx):
    n = idx.shape[0]
    w = table.shape[1]
    assert n % (SC_WORKERS * SC_CHUNK) == 0
    nch = n // (SC_WORKERS * SC_CHUNK)
    idx3 = idx.reshape(SC_WORKERS, nch, SC_CHUNK)

    @functools.partial(
        pl.kernel, mesh=_sc_mesh(),
        out_type=jax.ShapeDtypeStruct((n, w), table.dtype),
        scratch_types=[pltpu.VMEM((nch, SC_CHUNK), I32),
                       pltpu.VMEM((SC_CHUNK, w), table.dtype)],
        name="sc_gather")
    def run(table_hbm, idx_hbm, out_hbm, idx_v, rows_v):
        wid = _sc_worker()
        pltpu.sync_copy(idx_hbm.at[wid], idx_v)

        @pl.loop(0, nch)
        def _(j):
            pltpu.sync_copy(table_hbm.at[idx_v.at[j]], rows_v)
            pltpu.sync_copy(rows_v, out_hbm.at[pl.ds((wid * nch + j) * SC_CHUNK, SC_CHUNK)])

    return run(table, idx3)


def _ada_kernel(c_ref, w_ref, b_ref, o_ref):
    c = c_ref[...]
    a = c * jax.nn.sigmoid(c)
    w = w_ref[...]
    a_hi = a.astype(BF16)
    a_lo = (a - a_hi.astype(F32)).astype(BF16)
    w_hi = w.astype(BF16)
    w_lo = (w - w_hi.astype(F32)).astype(BF16)
    o_ref[...] = _dot(a_hi, w_hi) + _dot(a_lo, w_hi) + _dot(a_hi, w_lo) + b_ref[...]


def _ada(c, w, b):
    nb = c.shape[0]
    n = w.shape[1]
    tn = ADA_COL_TILE
    return pl.pallas_call(
        _ada_kernel,
        grid=(n // tn,),
        in_specs=[pl.BlockSpec((nb, D), lambda j: (0, 0)),
                  pl.BlockSpec((D, tn), lambda j: (0, j)),
                  pl.BlockSpec((1, tn), lambda j: (0, j))],
        out_specs=pl.BlockSpec((nb, tn), lambda j: (0, j)),
        out_shape=jax.ShapeDtypeStruct((nb, n), F32),
        compiler_params=_cp(("parallel",)),
        name="ada",
    )(c, w, b.reshape(1, n))


def _inproj_kernel(x_ref, mod_ref, gain_ref, cos_ref, s1_ref, s2_ref,
                   w_hga_ref, w_hgf_ref, w_at_ref, w_gt_ref,
                   hga_ref, hgf_ref, at0_ref, at1_ref, at2_ref, gt_ref, stage_ref):
    x = x_ref[...]
    tm = x.shape[0]
    sh = mod_ref[0, 0:1, :]
    sc = mod_ref[0, 1:2, :]
    h = _rms(x, gain_ref[...]) * (1.0 + sc) + sh
    hb = h.astype(BF16)
    hga_ref[...] = _dot(hb, w_hga_ref[...]).astype(BF16)
    hgf_ref[...] = _dot(hb, w_hgf_ref[...])
    cos = cos_ref[...]
    s1 = s1_ref[...]
    s2 = s2_ref[...]
    for g, o_ref in enumerate((at0_ref, at1_ref, at2_ref)):
        acc = _dot(hb, w_at_ref[:, g * 3 * AT_GW:(g + 1) * 3 * AT_GW])
        parts = []
        for j in range(2 * AT_GW // LANES):
            a = acc[:, j * LANES:(j + 1) * LANES]
            parts.append(a * cos + pltpu.roll(a, ROPE_DIM // 2, 1) * s1
                         + pltpu.roll(a, LANES - ROPE_DIM // 2, 1) * s2)
        parts.append(acc[:, 2 * AT_GW:])
        vals = jnp.concatenate(parts, axis=1)
        dil = AT_DILS[g]
        if dil == 1:
            o_ref[0, 0] = vals.astype(BF16)
        else:
            n_lt = 3 * AT_GW // LANES
            for c in range(n_lt):
                stage_ref[c] = vals[:, c * LANES:(c + 1) * LANES]
            for r in range(dil):
                o_ref[0, r] = jnp.concatenate(
                    [stage_ref[c, pl.ds(r, tm // dil, stride=dil), :] for c in range(n_lt)],
                    axis=1).astype(BF16)
    gt_ref[...] = _dot(hb, w_gt_ref[...]).astype(BF16)


def _inproj(x2, mod3, gain, rope, w_hga, w_hgf, w_at, w_gt, seq, tm):
    t = x2.shape[0]
    n_pos_blk = seq // tm
    row = lambda w: pl.BlockSpec((tm, w), lambda i: (i, 0))
    pos = pl.BlockSpec((tm, LANES), lambda i: (i % n_pos_blk, 0))
    batch = t // seq
    dilated = lambda d: pl.BlockSpec((1, d, tm // d, 3 * AT_GW),
                                     lambda i: ((i * tm) // seq, 0, i % n_pos_blk, 0))
    at_shape = lambda d: jax.ShapeDtypeStruct((batch, d, seq // d, 3 * AT_GW), BF16)
    return pl.pallas_call(
        _inproj_kernel,
        grid=(t // tm,),
        in_specs=[row(D),
                  pl.BlockSpec((1, 6, D), lambda i: ((i * tm) // seq, 0, 0)),
                  _const_spec((1, D)), pos, pos, pos,
                  _const_spec(w_hga.shape), _const_spec(w_hgf.shape),
                  _const_spec(w_at.shape), _const_spec(w_gt.shape)],
        out_specs=[row(3 * HG_W), row(2 * HG_W)] + [dilated(d) for d in AT_DILS] + [row(2 * D)],
        out_shape=[jax.ShapeDtypeStruct((t, 3 * HG_W), BF16),
                   jax.ShapeDtypeStruct((t, 2 * HG_W), F32)]
                  + [at_shape(d) for d in AT_DILS]
                  + [jax.ShapeDtypeStruct((t, 2 * D), BF16)],
        scratch_shapes=[pltpu.VMEM((3 * AT_GW // LANES, tm, LANES), F32)],
        compiler_params=_cp(("parallel",)),
        name="inproj",
    )(x2, mod3, gain, *rope, w_hga, w_hgf, w_at, w_gt)


def _rope_tables(seq):
    half = ROPE_DIM // 2
    inv_freq = ROPE_THETA ** (-np.arange(half, dtype=np.float32) / half)
    ang = jnp.arange(seq, dtype=F32)[:, None] * jnp.asarray(inv_freq)[None, :]
    cos, sin = jnp.cos(ang), jnp.sin(ang)
    ones = jnp.ones((seq, AT_HD - ROPE_DIM), F32)
    zeros = jnp.zeros((seq, AT_HD - ROPE_DIM), F32)
    zh = jnp.zeros((seq, half), F32)
    c_head = jnp.concatenate([cos, cos, ones], axis=1)
    s1_head = jnp.concatenate([zh, sin, zeros], axis=1)
    s2_head = jnp.concatenate([-sin, zh, zeros], axis=1)
    rep = LANES // AT_HD
    return tuple(jnp.tile(a, (1, rep)) for a in (c_head, s1_head, s2_head))


def _hgrn_consts(reverse):
    c = HG_C
    t = np.arange(c)[:, None]
    u = np.arange(c)[None, :]
    tri = (u >= t) if reverse else (u <= t)
    sels = []
    for lvl in range(HG_LEVELS):
        blk = 2 << lvl
        start = (np.arange(c) // blk) * blk
        piv = start + (1 << lvl) - (0 if reverse else 1)
        sels.append(np.arange(c)[None, :] == piv[:, None])
    sel = np.concatenate(sels, axis=0)
    rep = lambda m, n: jnp.asarray(np.concatenate([m.astype(np.float32)] * n, axis=1), BF16)
    return rep(tri, 2), rep(sel, 3)


def _hgrn_kernel(*refs, reverse, final, n_chunks):
    if final:
        (lbl_ref, tri_ref, sel_ref, q_ref, f_ref, v_ref, ofw_ref, gh_ref, gain_ref,
         o_ref, st_ref, st0_ref) = refs
    else:
        (lbl_ref, tri_ref, sel_ref, q_ref, f_ref, v_ref,
         o_ref, st_ref, st0_ref) = refs
    c = HG_C

    @pl.when(pl.program_id(1) == 0)
    def _():
        st_ref[...] = jnp.zeros_like(st_ref)

    l0 = lbl_ref[0:1, :]
    l1 = lbl_ref[1:2, :]
    lm = jnp.maximum(l0, l1)
    e0 = jnp.exp(l0 - lm)
    e1 = jnp.exp(l1 - lm)
    lb = e0 / (e0 + e1)
    tri = tri_ref[...]
    piv_row = c // 2 if reverse else c // 2 - 1

    def gate_math(r0):
        ff = f_ref[pl.ds(r0, c), :]
        e = jnp.exp(-jnp.abs(ff))
        r = 1.0 / (1.0 + e)
        sg = jnp.where(ff >= 0.0, r, e * r)
        g = jnp.log(lb + (1.0 - lb) * sg)
        kin = (1.0 - lb) * (1.0 - sg)
        g_hi = g.astype(BF16)
        g_lo = (g - g_hi.astype(F32)).astype(BF16)
        b = _dot(tri, jnp.concatenate([g_hi, g_lo], axis=0))
        btot = b[0:1, :] if reverse else b[c - 1:c, :]
        return kin, b, btot

    ti = lax.broadcasted_iota(I32, (c, c), 0)
    si = lax.broadcasted_iota(I32, (c, c), 1)
    causal = (si >= ti) if reverse else (si <= ti)
    row = lax.broadcasted_iota(I32, (c, 1), 0)

    def att_robust(q, kin, b, piv):
        att = jnp.where(ti == si, _dot_nt(q.astype(BF16), kin.astype(BF16)), 0.0)
        for lvl in range(HG_LEVELS):
            p = piv[lvl * c:(lvl + 1) * c, :]
            bit = ((row >> lvl) & 1) == 1
            q_side = jnp.logical_not(bit) if reverse else bit
            qe = jnp.where(q_side, q * jnp.exp(jnp.minimum(b - p, 0.0)), 0.0).astype(BF16)
            ke = jnp.where(q_side, 0.0, kin * jnp.exp(jnp.minimum(p - b, 0.0))).astype(BF16)
            same = (ti >> (lvl + 1)) == (si >> (lvl + 1))
            att = att + jnp.where(same, _dot_nt(qe, ke), 0.0)
        return att

    heads = [slice(h * HG_DK, (h + 1) * HG_DK) for h in range(HG_H)]


    def emit(r0, outs):
        o = jnp.concatenate(outs, axis=1)
        if final:
            o = o + ofw_ref[pl.ds(r0, c), :]
            normed = []
            for sl in heads:
                ms = jnp.mean(o[:, sl] * o[:, sl], axis=-1, keepdims=True)
                normed.append(o[:, sl] * lax.rsqrt(ms + EPS))
            gh = gh_ref[pl.ds(r0, c), :].astype(F32)
            o = jnp.concatenate(normed, axis=1) * gain_ref[...] * (gh * jax.nn.sigmoid(gh))
        o_ref[pl.ds(r0, c), :] = o.astype(o_ref.dtype)

    def robust_step(i, carry):
        ci = (n_chunks - 1 - i) if reverse else i
        r0 = pl.multiple_of(ci * c, c)
        q = q_ref[pl.ds(r0, c), :].astype(F32)
        v = v_ref[pl.ds(r0, c), :]
        kin, b, btot = gate_math(r0)
        qe_all = (q * jnp.exp(b)).astype(BF16)
        kd_all = (kin * jnp.exp(btot - b)).astype(BF16)
        piv = _dot(sel_ref[...], jnp.concatenate(_split3(b), axis=0))
        outs = []
        for h, sl in enumerate(heads):
            att = att_robust(q[:, sl], kin[:, sl], b[:, sl], piv[:, sl])
            st = st_ref[h]
            outs.append(_dot_nt(qe_all[:, sl], st.astype(BF16)) + _dot(att.astype(BF16), v[:, sl]))
            st_ref[h] = st * jnp.exp(btot[:, sl]) + _dot_tn(v[:, sl], kd_all[:, sl])
        emit(r0, outs)
        return carry

    grp = HG_UNROLL

    def fast_group(i, bmin):
        gi = (n_chunks // grp - 1 - i) if reverse else i
        base = gi * (grp * c)
        order = range(grp - 1, -1, -1) if reverse else range(grp)
        chunks = []
        for j in order:
            r0 = pl.multiple_of(base + j * c, c)
            kin, b, btot = gate_math(r0)
            bp = b[piv_row:piv_row + 1, :]
            bmin = jnp.minimum(bmin, jnp.minimum(bp, btot - bp))
            to_piv = jnp.exp(b - bp)
            k_piv = kin / to_piv
            chunks.append(dict(
                r0=r0, v=v_ref[pl.ds(r0, c), :], dec=jnp.exp(btot), pdec=jnp.exp(bp),
                qe=q_ref[pl.ds(r0, c), :] * to_piv.astype(BF16),
                ke=k_piv.astype(BF16),
                kd=(k_piv * jnp.exp(btot - bp)).astype(BF16)))
        for ch in chunks:
            ch["att"] = [jnp.where(causal, _dot_nt(ch["qe"][:, sl], ch["ke"][:, sl]), 0.0).astype(BF16)
                         for sl in heads]
        for ch in chunks:
            ch["intra"] = [_dot(ch["att"][h], ch["v"][:, sl]) for h, sl in enumerate(heads)]
            ch["upd"] = [_dot_tn(ch["v"][:, sl], ch["kd"][:, sl]) for sl in heads]
        st = [st_ref[h] for h in range(HG_H)]
        for ch in chunks:
            outs = []
            for h, sl in enumerate(heads):
                st_piv = (st[h] * ch["pdec"][:, sl]).astype(BF16)
                outs.append(_dot_nt(ch["qe"][:, sl], st_piv) + ch["intra"][h])
                st[h] = st[h] * ch["dec"][:, sl] + ch["upd"][h]
            emit(ch["r0"], outs)
        for h in range(HG_H):
            st_ref[h] = st[h]
        return bmin

    st0_ref[...] = st_ref[...]
    bmin = lax.fori_loop(0, n_chunks // grp, fast_group, jnp.zeros((1, HG_W), F32))

    @pl.when(jnp.min(bmin) < HG_SAFE_LOGDECAY)
    def _():
        st_ref[...] = st0_ref[...]
        lax.fori_loop(0, n_chunks, robust_step, 0)


def _hgrn(lb_logits_d, hga, hgf, batch, seq, tc, reverse, o_fw=None, gain=None):
    final = o_fw is not None
    t = hga.shape[0]
    nblk = seq // tc
    tri, sel = _hgrn_consts(reverse)

    def rows(col):
        if reverse:
            return lambda b, j: (b * nblk + nblk - 1 - j, col)
        return lambda b, j: (b * nblk + j, col)

    blk = lambda col: pl.BlockSpec((tc, HG_W), rows(col))
    in_specs = [_const_spec((2, HG_W)), _const_spec(tri.shape), _const_spec(sel.shape),
                blk(0), blk(1 if reverse else 0), blk(1)]
    args = [lb_logits_d, tri, sel, hga, hgf, hga]
    if final:
        in_specs += [blk(0), blk(2), _const_spec((1, HG_W))]
        args += [o_fw, hga, gain]
    return pl.pallas_call(
        functools.partial(_hgrn_kernel, reverse=reverse, final=final, n_chunks=tc // HG_C),
        grid=(batch, nblk),
        in_specs=in_specs,
        out_specs=blk(0),
        out_shape=jax.ShapeDtypeStruct((t, HG_W), BF16 if final else F32),
        scratch_shapes=[pltpu.VMEM((HG_H, HG_DK, HG_DK), F32),
                        pltpu.VMEM((HG_H, HG_DK, HG_DK), F32)],
        compiler_params=_cp(("parallel", "arbitrary")),
        name="hgrn_bwd" if reverse else "hgrn_fwd",
    )(*args)


def _attn_kernel(q_ref, kl_ref, km_ref, kr_ref, vl_ref, vm_ref, vr_ref, o_ref, l_ref,
                 k_s, v_s, *, tq, ld):
    hw = AT_HALF
    n_res = q_ref.shape[0]
    for res in range(n_res):
        k_s[res, 0:hw, :] = kl_ref[res]
        k_s[res, hw:hw + tq, :] = km_ref[res]
        k_s[res, hw + tq:, :] = kr_ref[res]
        v_s[res, 0:hw, :] = vl_ref[res]
        v_s[res, hw:hw + tq, :] = vm_ref[res]
        v_s[res, hw + tq:, :] = vr_ref[res]
    qb = 2 * hw
    kb = 4 * hw
    base = pl.program_id(2) * tq
    lane_head = lax.broadcasted_iota(I32, (1, AT_GW), 1) // AT_HD
    q_sel = [jnp.where(lane_head == h, AT_HD ** -0.5, 0.0).astype(BF16) for h in range(AT_H)]
    out_head = lax.broadcasted_iota(I32, (qb, AT_GW), 1) // AT_HD
    rel = (lax.broadcasted_iota(I32, (AT_H * qb, kb), 0) % qb
           - lax.broadcasted_iota(I32, (AT_H * qb, kb), 1) + hw)
    band = jnp.where(jnp.abs(rel) <= hw, 0.0, NEG_BIG)
    key_col = lax.broadcasted_iota(I32, (1, kb), 1)
    for res, qs in [(res, qs) for res in range(n_res) for qs in range(0, tq, qb)]:
        q = q_ref[res, qs:qs + qb, :]
        kk = k_s[res, qs:qs + kb, :]
        vv = v_s[res, qs:qs + kb, :]
        q4 = jnp.concatenate([q * q_sel[h] for h in range(AT_H)], axis=0)
        s = _dot_nt(q4, kk) + band
        if qs == 0 or qs + qb == tq:
            kpos = base + qs - hw + key_col
            s = s + jnp.where((kpos >= 0) & (kpos < ld), 0.0, NEG_BIG)
        m = jnp.max(s, axis=-1, keepdims=True)
        p = jnp.exp(s - m)
        l = jnp.sum(p, axis=-1, keepdims=True)
        pv = _dot(p.astype(BF16), vv) * (1.0 / l)
        lse = m + jnp.log(l)
        o_all = jnp.zeros((qb, AT_GW), F32)
        l_all = jnp.zeros((qb, AT_GW), F32)
        for h in range(AT_H):
            o_all = jnp.where(out_head == h, pv[h * qb:(h + 1) * qb, :], o_all)
            l_all = jnp.where(out_head == h, lse[h * qb:(h + 1) * qb, :], l_all)
        o_ref[res, qs:qs + qb, :] = o_all.astype(o_ref.dtype)
        l_ref[res, qs:qs + qb, :] = l_all


def _attn_group(qkv):
    batch, dil, ld, _ = qkv.shape
    hw = AT_HALF
    tq = min(ld, SEQ_TILE)
    nq = ld // tq
    nh = ld // hw
    per = tq // hw
    n_res = min(dil, max(1, SEQ_TILE // tq))
    main = lambda part: pl.BlockSpec((None, n_res, tq, AT_GW), lambda b, r, j: (b, r, j, part))
    left = lambda part: pl.BlockSpec(
        (None, n_res, hw, AT_GW), lambda b, r, j: (b, r, jnp.maximum(j * per - 1, 0), part))
    right = lambda part: pl.BlockSpec(
        (None, n_res, hw, AT_GW), lambda b, r, j: (b, r, jnp.minimum((j + 1) * per, nh - 1), part))
    out_spec = pl.BlockSpec((None, n_res, tq, AT_GW), lambda b, r, j: (b, r, j, 0))
    return pl.pallas_call(
        functools.partial(_attn_kernel, tq=tq, ld=ld),
        grid=(batch, dil // n_res, nq),
        in_specs=[main(0), left(1), main(1), right(1), left(2), main(2), right(2)],
        out_specs=[out_spec, out_spec],
        out_shape=[jax.ShapeDtypeStruct((batch, dil, ld, AT_GW), BF16),
                   jax.ShapeDtypeStruct((batch, dil, ld, AT_GW), F32)],
        scratch_shapes=[pltpu.VMEM((n_res, tq + 2 * hw, AT_GW), BF16),
                        pltpu.VMEM((n_res, tq + 2 * hw, AT_GW), BF16)],
        compiler_params=_cp(("parallel", "parallel", "parallel")),
        name=f"attn_d{dil}",
    )(qkv, qkv, qkv, qkv, qkv, qkv, qkv)


def _merge_kernel(ohg_ref, o0_ref, o1_ref, o2_ref, l0_ref, l1_ref, l2_ref, gt_ref, x_ref, mod_ref,
                  w_hg_ref, w_at_ref, w_out_ref, npost_ref, npre_ref, w_r_ref, b_r_ref,
                  x1_ref, h2_ref, ids_ref, wts_ref, cnt_ref,
                  so1_ref, sl1_ref, so2_ref, sl2_ref):
    tm = x_ref.shape[0]
    sub = tm // MERGE_SUBTILES
    rows = [slice(i * sub, (i + 1) * sub) for i in range(MERGE_SUBTILES)]
    gt1 = mod_ref[0, 2:3, :]
    sh2 = mod_ref[0, 3:4, :]
    sc2 = mod_ref[0, 4:5, :]

    for src_ref, dst_ref in ((o1_ref, so1_ref), (l1_ref, sl1_ref), (o2_ref, so2_ref), (l2_ref, sl2_ref)):
        dil = src_ref.shape[0]
        for res in range(dil):
            vals = src_ref[res].astype(F32)
            for c in range(AT_GW // LANES):
                dst_ref[c, pl.ds(res, tm // dil, stride=dil), :] = vals[:, c * LANES:(c + 1) * LANES]

    def staged(ref, r):
        return jnp.concatenate([ref[c, r, :] for c in range(AT_GW // LANES)], axis=1)

    def branches(r):
        o0, l0 = o0_ref[0, r, :].astype(F32), l0_ref[0, r, :]
        o1, l1 = staged(so1_ref, r), staged(sl1_ref, r)
        o2, l2 = staged(so2_ref, r), staged(sl2_ref, r)
        m = jnp.maximum(jnp.maximum(l0, l1), l2)
        e0, e1, e2 = jnp.exp(l0 - m), jnp.exp(l1 - m), jnp.exp(l2 - m)
        oa = (e0 * o0 + e1 * o1 + e2 * o2) / (e0 + e1 + e2)
        return _dot(ohg_ref[r, :], w_hg_ref[...]), _dot(oa.astype(BF16), w_at_ref[...])

    def out_proj(r, b_hg, b_at):
        g_hg = jax.nn.sigmoid(gt_ref[r, 0:D])
        g_at = jax.nn.sigmoid(gt_ref[r, D:2 * D])
        merged = g_hg * b_hg.astype(BF16) + g_at * b_at.astype(BF16)
        return _dot(merged, w_out_ref[...])

    def norms(r, y):
        x1 = x_ref[r, :] + gt1 * _rms(y, npost_ref[...])
        x1_ref[r, :] = x1
        h2 = _rms(x1, npre_ref[...]) * (1.0 + sc2) + sh2
        h2_ref[r, :] = _pack_rows(h2)
        return _dot(h2.astype(BF16), w_r_ref[...]) + b_r_ref[...]

    def top_k(r, logits):
        lane = lax.broadcasted_iota(I32, logits.shape, 1)
        lane_f = lane.astype(F32)
        work = logits
        vals, idxs = [], []
        sel = jnp.zeros(logits.shape, F32)
        for _ in range(TOP_K):
            mk = jnp.max(work, axis=-1, keepdims=True)
            ik = jnp.min(jnp.where(work == mk, lane_f, float(LANES)), axis=-1, keepdims=True)
            hit = lane_f == ik
            sel = jnp.where(hit, 1.0, sel)
            work = jnp.where(hit, -jnp.inf, work)
            vals.append(mk)
            idxs.append(ik)
        es = [jnp.exp(v - vals[0]) for v in vals]
        den = es[0] + es[1] + es[2] + es[3]
        ids = jnp.zeros(logits.shape, F32)
        wts = jnp.zeros(logits.shape, F32)
        for k in range(TOP_K):
            ids = jnp.where(lane == k, idxs[k], ids)
            wts = jnp.where(lane == k, es[k] / den, wts)
        ids_ref[r, :] = ids.astype(I32)
        wts_ref[r, :] = wts
        return jnp.sum(sel, axis=0, keepdims=True)

    br = [branches(r) for r in rows]
    ys = [out_proj(r, *b) for r, b in zip(rows, br)]
    logits = [norms(r, y) for r, y in zip(rows, ys)]
    counts = [top_k(r, lg) for r, lg in zip(rows, logits)]
    cnt_ref[0] = functools.reduce(lambda a, b: a + b, counts)


def _merge(ohg, att_o, att_l, gates, x2, mod3, w_hg, w_at, w_out, npost, npre, w_r, b_r, seq, tm):
    t = x2.shape[0]
    nt = t // tm
    row = lambda w: pl.BlockSpec((tm, w), lambda i: (i, 0))
    slab_i = jax.ShapeDtypeStruct((t, LANES), I32)
    slab_f = jax.ShapeDtypeStruct((t, LANES), F32)
    nblk = seq // tm
    grouped = [pl.BlockSpec((None, d, tm // d, AT_GW),
                            lambda i: ((i * tm) // seq, 0, i % nblk, 0)) for d in AT_DILS]
    return pl.pallas_call(
        _merge_kernel,
        grid=(nt,),
        in_specs=[row(HG_W)] + grouped + grouped + [row(2 * D), row(D),
                  pl.BlockSpec((1, 6, D), lambda i: ((i * tm) // seq, 0, 0)),
                  _const_spec(w_hg.shape), _const_spec(w_at.shape), _const_spec(w_out.shape),
                  _const_spec((1, D)), _const_spec((1, D)),
                  _const_spec(w_r.shape), _const_spec(b_r.shape)],
        out_specs=[row(D), row(HALF_D), row(LANES), row(LANES),
                   pl.BlockSpec((1, 1, LANES), lambda i: (i, 0, 0))],
        out_shape=[jax.ShapeDtypeStruct((t, D), F32), jax.ShapeDtypeStruct((t, HALF_D), U32),
                   slab_i, slab_f, jax.ShapeDtypeStruct((nt, 1, LANES), F32)],
        scratch_shapes=[pltpu.VMEM((AT_GW // LANES, tm, LANES), F32)] * 4,
        compiler_params=_cp(("parallel",)),
        name="merge",
    )(ohg, *att_o, *att_l, gates, x2, mod3, w_hg, w_at, w_out, npost, npre, w_r, b_r)


def _route_kernel(ids_ref, base_ref, tril_ref, dest_ref):
    ids = ids_ref[...]
    lane = lax.broadcasted_iota(I32, ids.shape, 1)
    hits = [lane == ids[:, k:k + 1] for k in range(TOP_K)]
    sel = jnp.zeros(ids.shape, F32)
    for hit in hits:
        sel = jnp.where(hit, 1.0, sel)
    before = _dot(tril_ref[...], sel.astype(BF16)) + base_ref[0]
    dest = jnp.zeros(ids.shape, F32)
    for k, hit in enumerate(hits):
        rk = jnp.sum(jnp.where(hit, before, 0.0), axis=-1, keepdims=True)
        dest = jnp.where(lane == k, rk, dest)
    dest_ref[...] = dest.T[:ROUTE_ROWS, :].astype(I32)


def _route(ids, tile_base, tm):
    t = ids.shape[0]
    tril = jnp.asarray(np.tril(np.ones((tm, tm), np.float32), -1), BF16)
    return pl.pallas_call(
        _route_kernel,
        grid=(t // tm,),
        in_specs=[pl.BlockSpec((tm, LANES), lambda i: (i, 0)),
                  pl.BlockSpec((1, 1, LANES), lambda i: (i, 0, 0)),
                  _const_spec((tm, tm))],
        out_specs=pl.BlockSpec((ROUTE_ROWS, tm), lambda i: (0, i)),
        out_shape=jax.ShapeDtypeStruct((ROUTE_ROWS, t), I32),
        compiler_params=_cp(("parallel",)),
        name="route",
    )(ids, tile_base, tril)


def _moe_kernel(be_ref, rows_ref, x_ref, *refs, prepare):
    i = pl.program_id(0)
    if prepare:
        wup_ref, bg_ref, bl_ref, wdn_ref, bd_ref, perm_ref, o_ref, wg_ref, wl_ref, wd_ref = refs

        @pl.when((i == 0) | (be_ref[i] != be_ref[jnp.maximum(i - 1, 0)]))
        def _():
            _deinterleave_into(wup_ref, perm_ref, wg_ref.at[0], wl_ref.at[0])
            wd_ref[0] = wdn_ref[0].astype(BF16)
    else:
        wg_ref, wl_ref, bg_ref, bl_ref, wd_ref, bd_ref, o_ref = refs

    def ffn(x_packed):
        x_lo, x_hi = (a.astype(BF16) for a in _unpack_rows(x_packed))
        gate = _dot(x_lo, wg_ref[0, :HALF_D, :]) + _dot(x_hi, wg_ref[0, HALF_D:, :]) + bg_ref[0]
        up = _dot(x_lo, wl_ref[0, :HALF_D, :]) + _dot(x_hi, wl_ref[0, HALF_D:, :]) + bl_ref[0]
        gate = jnp.minimum(gate, SWIGLU_LIMIT)
        up = jnp.clip(up, -SWIGLU_LIMIT, SWIGLU_LIMIT)
        act = (up + 1.0) * gate * jax.nn.sigmoid(SWIGLU_ALPHA * gate)
        return _pack_rows(_dot(act.astype(BF16), wd_ref[0]) + bd_ref[0])

    valid = rows_ref[i]
    half = x_ref.shape[0] // 2

    @pl.when(valid > half)
    def _():
        o_ref[...] = ffn(x_ref[...])

    @pl.when((valid > 0) & (valid <= half))
    def _():
        o_ref[:half, :] = ffn(x_ref[:half, :])
        o_ref[half:, :] = jnp.zeros((half, HALF_D), U32)

    @pl.when(valid == 0)
    def _():
        o_ref[...] = jnp.zeros_like(o_ref)


def _moe(block_expert, block_rows, xb, wts, bm, prepared):
    r = xb.shape[0]
    nblk = r // bm
    rows = pl.BlockSpec((bm, HALF_D), lambda i, be, br: (i, 0))
    ew = lambda shape: pl.BlockSpec((1,) + shape, lambda i, be, br: (be[i], 0, 0))
    y_shape = jax.ShapeDtypeStruct((r, HALF_D), U32)
    if prepared is None:
        perm = _pair_split_perm()
        w_shape = jax.ShapeDtypeStruct((N_EXP, D, D), BF16)
        in_specs = [rows, ew((D, 2 * D)), ew((1, D)), ew((1, D)), ew((D, D)), ew((1, D)),
                    pl.BlockSpec(perm.shape, lambda i, be, br: (0, 0))]
        args = (wts["w_up"], wts["bg"], wts["bl"], wts["w_down"], wts["bd"], perm)
        out_specs = [rows, ew((D, D)), ew((D, D)), ew((D, D))]
        out_shape = [y_shape, w_shape, w_shape, w_shape]
    else:
        in_specs = [rows, ew((D, D)), ew((D, D)), ew((1, D)), ew((1, D)), ew((D, D)), ew((1, D))]
        args = (prepared[0], prepared[1], wts["bg"], wts["bl"], prepared[2], wts["bd"])
        out_specs = [rows]
        out_shape = [y_shape]
    outs = pl.pallas_call(
        functools.partial(_moe_kernel, prepare=prepared is None),
        grid_spec=pltpu.PrefetchScalarGridSpec(
            num_scalar_prefetch=2, grid=(nblk,), in_specs=in_specs, out_specs=out_specs),
        out_shape=out_shape,
        compiler_params=_cp(("arbitrary",)),
        name="moe_prepare" if prepared is None else "moe",
    )(block_expert, block_rows, xb, *args)
    return outs[0], (tuple(outs[1:]) if prepared is None else prepared)


def _final_kernel(y0_ref, y1_ref, y2_ref, y3_ref, wts_ref, x1_ref, mod_ref, npost_ref, o_ref):
    wts = wts_ref[...]
    y = jnp.zeros(x1_ref.shape, F32)
    for k, y_ref in enumerate((y0_ref, y1_ref, y2_ref, y3_ref)):
        y = y + wts[:, k:k + 1] * jnp.concatenate(_unpack_rows(y_ref[0]), axis=1)
    gt2 = mod_ref[0, 5:6, :]
    o_ref[...] = x1_ref[...] + gt2 * _rms(y, npost_ref[...])


def _final(yg, wts, x1, mod3, npost, seq, tm):
    t = x1.shape[0]
    slot = lambda k: pl.BlockSpec((1, tm, HALF_D), lambda i: (k, i, 0))
    return pl.pallas_call(
        _final_kernel,
        grid=(t // tm,),
        in_specs=[slot(0), slot(1), slot(2), slot(3),
                  pl.BlockSpec((tm, LANES), lambda i: (i, 0)),
                  pl.BlockSpec((tm, D), lambda i: (i, 0)),
                  pl.BlockSpec((1, 6, D), lambda i: ((i * tm) // seq, 0, 0)),
                  _const_spec((1, D))],
        out_specs=pl.BlockSpec((tm, D), lambda i: (i, 0)),
        out_shape=jax.ShapeDtypeStruct((t, D), F32),
        compiler_params=_cp(("parallel",)),
        name="final",
    )(yg, yg, yg, yg, wts, x1, mod3, npost)


def _tiles(seq):
    return dict(tm_in=min(seq, ROW_TILE), tc=min(seq, SEQ_TILE), tm_merge=min(seq, STREAM_TILE),
                bm=MOE_BLOCK, tm_final=min(seq, STREAM_TILE))


def _prep_weights(w_in, w_hg_out, w_att_out, w_out, w_router, b_router, w_up, b_up, w_down, b_down):
    w = w_in[0]
    hw = HG_W
    a0 = 5 * hw
    aw = AT_GW * len(AT_DILS)
    q_at, k_at, v_at = (w[:, a0 + i * aw:a0 + (i + 1) * aw] for i in range(3))
    grp = lambda m, g: m[:, g * AT_GW:(g + 1) * AT_GW]
    w_at = jnp.concatenate(
        [jnp.concatenate([grp(q_at, g), grp(k_at, g), grp(v_at, g)], axis=1)
         for g in range(len(AT_DILS))], axis=1)
    pad = LANES - N_EXP
    return dict(
        w_up=w_up[0],
        w_hga=jnp.concatenate([w[:, 0:hw], w[:, 3 * hw:4 * hw], w[:, 4 * hw:5 * hw]], 1).astype(BF16),
        w_hgf=w[:, hw:3 * hw].astype(BF16),
        w_at=w_at.astype(BF16),
        w_gt=w[:, a0 + 3 * aw:].astype(BF16),
        w_hg_out=w_hg_out[0].astype(BF16),
        w_att_out=w_att_out[0].astype(BF16),
        w_out=w_out[0].astype(BF16),
        w_r=jnp.pad(w_router[0], ((0, 0), (0, pad))).astype(BF16),
        b_r=jnp.pad(b_router[0], (0, pad), constant_values=NEG_BIG).reshape(1, LANES),
        bg=b_up[0][:, 0::2].reshape(N_EXP, 1, D),
        bl=b_up[0][:, 1::2].reshape(N_EXP, 1, D),
        w_down=w_down[0],
        bd=b_down[0].reshape(N_EXP, 1, D),
    )


def _trunk(x, mod, wts, norm_pre, norm_post, lb_logits, hg_gain, prepared=None):
    batch, seq, width = x.shape
    t = batch * seq
    tl = _tiles(seq)
    assert width == D and seq % SEQ_TILE == 0 and seq // max(AT_DILS) >= 2 * AT_HALF
    x2 = x.reshape(t, D)
    mod3 = mod.reshape(batch, 6, D)
    rope = _rope_tables(seq)

    hga, hgf, at0, at1, at2, gates = _inproj(
        x2, mod3, norm_pre[0, 0].reshape(1, D), rope,
        wts["w_hga"], wts["w_hgf"], wts["w_at"], wts["w_gt"], seq, tl["tm_in"])

    o_fw = _hgrn(lb_logits[0], hga, hgf, batch, seq, tl["tc"], reverse=False)
    ohg = _hgrn(lb_logits[1], hga, hgf, batch, seq, tl["tc"], reverse=True,
                o_fw=o_fw, gain=hg_gain[0].reshape(1, HG_W))

    att = [_attn_group(a) for a in (at0, at1, at2)]

    x1, h2, ids, rw, cnt = _merge(
        ohg, [a[0] for a in att], [a[1] for a in att], gates, x2, mod3,
        wts["w_hg_out"], wts["w_att_out"], wts["w_out"],
        norm_post[0, 0].reshape(1, D), norm_pre[0, 1].reshape(1, D),
        wts["w_r"], wts["b_r"], seq, tl["tm_merge"])

    bm = tl["bm"]
    tmr = tl["tm_merge"]
    cnt_tiles = cnt.reshape(t // tmr, LANES).astype(I32)
    total = jnp.sum(cnt_tiles, axis=0)
    min_blocks = (jnp.arange(LANES) < N_EXP).astype(I32)
    padded = jnp.maximum((total + bm - 1) // bm, min_blocks) * bm
    pad_end = jnp.cumsum(padded)
    pad_start = pad_end - padded
    tile_base = pad_start[None, :] + jnp.cumsum(cnt_tiles, axis=0) - cnt_tiles
    dest_kt = _route(ids, tile_base.astype(F32).reshape(-1, 1, LANES), tmr)[:TOP_K]

    n_rows = t * TOP_K + N_EXP * bm
    n_blocks = n_rows // bm
    blk_start = jnp.arange(n_blocks, dtype=I32) * bm
    block_expert = jnp.minimum(
        jnp.sum(pad_end[None, :N_EXP] <= blk_start[:, None], axis=1), N_EXP - 1).astype(I32)
    block_rows = jnp.clip(pad_start[block_expert] + total[block_expert] - blk_start, 0, bm)
    block_rows = jnp.where(blk_start < pad_end[N_EXP - 1], block_rows, 0).astype(I32)

    xb = _sc_scatter_rows(h2, dest_kt, n_rows)
    yb, prepared = _moe(block_expert, block_rows, xb, wts, bm, prepared)
    yg = _sc_gather_rows(yb, dest_kt.reshape(-1)).reshape(TOP_K, t, HALF_D)

    out = _final(yg, rw, x1, mod3, norm_post[0, 1].reshape(1, D), seq, tl["tm_final"])
    return out.reshape(batch, seq, D), prepared


def kernel(x_prompt, x_sample, c_prompt, c_sample, w_ada, b_ada, norm_pre, norm_post, w_in,
           lb_logits, hg_norm_gain, w_hg_out, w_att_out, w_out, w_router, b_router,
           w_up, b_up, w_down, b_down):
    wts = _prep_weights(w_in, w_hg_out, w_att_out, w_out, w_router, b_router,
                        w_up, b_up, w_down, b_down)
    nb = c_prompt.shape[0]
    mod = _ada(jnp.concatenate([c_prompt, c_sample], axis=0), w_ada[0], b_ada[0])
    lb_l = lb_logits.astype(F32)
    y_p, prepared = _trunk(x_prompt, mod[:nb], wts, norm_pre, norm_post, lb_l, hg_norm_gain)
    y_s, _ = _trunk(x_sample, mod[nb:], wts, norm_pre, norm_post, lb_l, hg_norm_gain, prepared)
    return (y_p, y_s)
```

```python
import functools

import numpy as np
import jax
import jax.numpy as jnp
from jax import lax
from jax.experimental import pallas as pl
from jax.experimental.pallas import tpu as pltpu
from jax.experimental.pallas import tpu_sc as plsc

F32 = jnp.float32
BF16 = jnp.bfloat16
I32 = jnp.int32
U32 = jnp.uint32

D = 1024
EPS = 1e-6
HG_H = 4
HG_DK = 128
HG_W = HG_H * HG_DK
HG_C = 64
HG_LEVELS = 6
HG_SAFE_LOGDECAY = -80.0
HG_UNROLL = 8
AT_DILS = (1, 4, 16)
AT_HALF = 64
AT_H = 4
AT_HD = 64
AT_GW = AT_H * AT_HD
ROPE_DIM = 16
ROPE_THETA = 500000.0
N_EXP = 32
TOP_K = 4
SWIGLU_LIMIT = 7.0
SWIGLU_ALPHA = 1.702
MERGE_SUBTILES = 4
ROW_TILE = 512
STREAM_TILE = 1024
SEQ_TILE = 2048
ADA_COL_TILE = 1536
MOE_BLOCK = 1024
ROUTE_ROWS = 8
LANES = 128
NEG_BIG = -1e30
HALF_D = D // 2
SC_CORES = 2
SC_SUBCORES = 16
SC_WORKERS = SC_CORES * SC_SUBCORES
SC_CHUNK = 128

VMEM_LIMIT = 56 * 1024 * 1024


def _cp(sem, vmem=VMEM_LIMIT):
    return pltpu.CompilerParams(dimension_semantics=sem, vmem_limit_bytes=vmem)


def _dot(a, b):
    return jnp.dot(a, b, preferred_element_type=F32)


def _dot_nt(a, b):
    return lax.dot_general(a, b, (((1,), (1,)), ((), ())), preferred_element_type=F32)


def _dot_tn(a, b):
    return lax.dot_general(a, b, (((0,), (0,)), ((), ())), preferred_element_type=F32)


def _split3(x):
    hi = x.astype(BF16)
    r = x - hi.astype(F32)
    mid = r.astype(BF16)
    lo = (r - mid.astype(F32)).astype(BF16)
    return hi, mid, lo


def _rms(x, gain):
    ms = jnp.mean(x * x, axis=-1, keepdims=True)
    return x * lax.rsqrt(ms + EPS) * gain


def _const_spec(shape):
    n = len(shape)
    return pl.BlockSpec(shape, lambda *_: (0,) * n)


def _pack_rows(y):
    bits = lambda a: lax.bitcast_convert_type(a.astype(BF16).astype(F32), U32)
    return (bits(y[:, :HALF_D]) >> 16) | (bits(y[:, HALF_D:]) & jnp.uint32(0xFFFF0000))


def _unpack_rows(w):
    lo = lax.bitcast_convert_type(w << 16, F32)
    hi = lax.bitcast_convert_type(w & jnp.uint32(0xFFFF0000), F32)
    return lo, hi


def _deinterleave_into(w_ref, p_ref, g_ref, l_ref):
    p = p_ref[...]
    n_grp = w_ref.shape[2] // (2 * LANES)
    for b in range(n_grp):
        blk = w_ref[0, :, b * 2 * LANES:(b + 1) * 2 * LANES].astype(BF16)
        r = _dot(blk, p)
        g_ref[:, b * LANES:(b + 1) * LANES] = r[:, :LANES].astype(BF16)
        l_ref[:, b * LANES:(b + 1) * LANES] = r[:, LANES:].astype(BF16)


def _pair_split_perm():
    perm = np.zeros((2 * LANES, 2 * LANES), np.float32)
    perm[2 * np.arange(LANES), np.arange(LANES)] = 1.0
    perm[2 * np.arange(LANES) + 1, LANES + np.arange(LANES)] = 1.0
    return jnp.asarray(perm, BF16)


def _sc_mesh():
    return plsc.VectorSubcoreMesh(core_axis_name="c", subcore_axis_name="s")


def _sc_worker():
    return lax.axis_index("s") * SC_CORES + lax.axis_index("c")


def _sc_scatter_rows(src, dest_kt, n_rows):
    t, w = src.shape
    n_slot = dest_kt.shape[0]
    assert t % (SC_WORKERS * SC_CHUNK) == 0
    nch = t // (SC_WORKERS * SC_CHUNK)
    idx = dest_kt.reshape(n_slot, SC_WORKERS, nch, SC_CHUNK).transpose(1, 0, 2, 3)

    @functools.partial(
        pl.kernel, mesh=_sc_mesh(),
        out_type=jax.ShapeDtypeStruct((n_rows, w), src.dtype),
        scratch_types=[pltpu.VMEM((n_slot, nch, SC_CHUNK), I32),
                       pltpu.VMEM((SC_CHUNK, w), src.dtype)],
        name="sc_scatter")
    def run(src_hbm, idx_hbm, out_hbm, idx_v, rows_v):
        wid = _sc_worker()
        pltpu.sync_copy(idx_hbm.at[wid], idx_v)

        @pl.loop(0, nch)
        def _(j):
            pltpu.sync_copy(src_hbm.at[pl.ds((wid * nch + j) * SC_CHUNK, SC_CHUNK)], rows_v)
            for k in range(n_slot):
                pltpu.sync_copy(rows_v, out_hbm.at[idx_v.at[k, j]])

    return run(src, idx)


def _sc_gather_rows(table, idx):
    n = idx.shape[0]
    w = table.shape[1]
    assert n % (SC_WORKERS * SC_CHUNK) == 0
    nch = n // (SC_WORKERS * SC_CHUNK)
    idx3 = idx.reshape(SC_WORKERS, nch, SC_CHUNK)

    @functools.partial(
        pl.kernel, mesh=_sc_mesh(),
        out_type=jax.ShapeDtypeStruct((n, w), table.dtype),
        scratch_types=[pltpu.VMEM((nch, SC_CHUNK), I32),
                       pltpu.VMEM((SC_CHUNK, w), table.dtype)],
        name="sc_gather")
    def run(table_hbm, idx_hbm, out_hbm, idx_v, rows_v):
        wid = _sc_worker()
        pltpu.sync_copy(idx_hbm.at[wid], idx_v)

        @pl.loop(0, nch)
        def _(j):
            pltpu.sync_copy(table_hbm.at[idx_v.at[j]], rows_v)
            pltpu.sync_copy(rows_v, out_hbm.at[pl.ds((wid * nch + j) * SC_CHUNK, SC_CHUNK)])

    return run(table, idx3)


def _ada_kernel(c_ref, w_ref, b_ref, o_ref):
    c = c_ref[...]
    a = c * jax.nn.sigmoid(c)
    w = w_ref[...]
    a_hi = a.astype(BF16)
    a_lo = (a - a_hi.astype(F32)).astype(BF16)
    w_hi = w.astype(BF16)
    w_lo = (w - w_hi.astype(F32)).astype(BF16)
    o_ref[...] = _dot(a_hi, w_hi) + _dot(a_lo, w_hi) + _dot(a_hi, w_lo) + b_ref[...]


def _ada(c, w, b):
    nb = c.shape[0]
    n = w.shape[1]
    tn = ADA_COL_TILE
    return pl.pallas_call(
        _ada_kernel,
        grid=(n // tn,),
        in_specs=[pl.BlockSpec((nb, D), lambda j: (0, 0)),
                  pl.BlockSpec((D, tn), lambda j: (0, j)),
                  pl.BlockSpec((1, tn), lambda j: (0, j))],
        out_specs=pl.BlockSpec((nb, tn), lambda j: (0, j)),
        out_shape=jax.ShapeDtypeStruct((nb, n), F32),
        compiler_params=_cp(("parallel",)),
        name="ada",
    )(c, w, b.reshape(1, n))


def _inproj_kernel(x_ref, mod_ref, gain_ref, cos_ref, s1_ref, s2_ref,
                   w_hga_ref, w_hgf_ref, w_at_ref, w_gt_ref,
                   hga_ref, hgf_ref, at0_ref, at1_ref, at2_ref, gt_ref, stage_ref):
    x = x_ref[...]
    tm = x.shape[0]
    sh = mod_ref[0, 0:1, :]
    sc = mod_ref[0, 1:2, :]
    h = _rms(x, gain_ref[...]) * (1.0 + sc) + sh
    hb = h.astype(BF16)
    hga_ref[...] = _dot(hb, w_hga_ref[...]).astype(BF16)
    hgf_ref[...] = _dot(hb, w_hgf_ref[...])
    cos = cos_ref[...]
    s1 = s1_ref[...]
    s2 = s2_ref[...]
    for g, o_ref in enumerate((at0_ref, at1_ref, at2_ref)):
        acc = _dot(hb, w_at_ref[:, g * 3 * AT_GW:(g + 1) * 3 * AT_GW])
        parts = []
        for j in range(2 * AT_GW // LANES):
            a = acc[:, j * LANES:(j + 1) * LANES]
            parts.append(a * cos + pltpu.roll(a, ROPE_DIM // 2, 1) * s1
                         + pltpu.roll(a, LANES - ROPE_DIM // 2, 1) * s2)
        parts.append(acc[:, 2 * AT_GW:])
        vals = jnp.concatenate(parts, axis=1)
        dil = AT_DILS[g]
        if dil == 1:
            o_ref[0, 0] = vals.astype(BF16)
        else:
            n_lt = 3 * AT_GW // LANES
            for c in range(n_lt):
                stage_ref[c] = vals[:, c * LANES:(c + 1) * LANES]
            for r in range(dil):
                o_ref[0, r] = jnp.concatenate(
                    [stage_ref[c, pl.ds(r, tm // dil, stride=dil), :] for c in range(n_lt)],
                    axis=1).astype(BF16)
    gt_ref[...] = _dot(hb, w_gt_ref[...]).astype(BF16)


def _inproj(x2, mod3, gain, rope, w_hga, w_hgf, w_at, w_gt, seq, tm):
    t = x2.shape[0]
    n_pos_blk = seq // tm
    row = lambda w: pl.BlockSpec((tm, w), lambda i: (i, 0))
    pos = pl.BlockSpec((tm, LANES), lambda i: (i % n_pos_blk, 0))
    batch = t // seq
    dilated = lambda d: pl.BlockSpec((1, d, tm // d, 3 * AT_GW),
                                     lambda i: ((i * tm) // seq, 0, i % n_pos_blk, 0))
    at_shape = lambda d: jax.ShapeDtypeStruct((batch, d, seq // d, 3 * AT_GW), BF16)
    return pl.pallas_call(
        _inproj_kernel,
        grid=(t // tm,),
        in_specs=[row(D),
                  pl.BlockSpec((1, 6, D), lambda i: ((i * tm) // seq, 0, 0)),
                  _const_spec((1, D)), pos, pos, pos,
                  _const_spec(w_hga.shape), _const_spec(w_hgf.shape),
                  _const_spec(w_at.shape), _const_spec(w_gt.shape)],
        out_specs=[row(3 * HG_W), row(2 * HG_W)] + [dilated(d) for d in AT_DILS] + [row(2 * D)],
        out_shape=[jax.ShapeDtypeStruct((t, 3 * HG_W), BF16),
                   jax.ShapeDtypeStruct((t, 2 * HG_W), F32)]
                  + [at_shape(d) for d in AT_DILS]
                  + [jax.ShapeDtypeStruct((t, 2 * D), BF16)],
        scratch_shapes=[pltpu.VMEM((3 * AT_GW // LANES, tm, LANES), F32)],
        compiler_params=_cp(("parallel",)),
        name="inproj",
    )(x2, mod3, gain, *rope, w_hga, w_hgf, w_at, w_gt)


def _rope_tables(seq):
    half = ROPE_DIM // 2
    inv_freq = ROPE_THETA ** (-np.arange(half, dtype=np.float32) / half)
    chan = np.arange(LANES) % AT_HD
    freq = jnp.asarray(inv_freq[chan % half])[None, :]
    ang = jnp.arange(seq, dtype=F32)[:, None] * freq
    cos, sin = jnp.cos(ang), jnp.sin(ang)
    lo = jnp.asarray(chan < half)[None, :]
    hi = jnp.asarray((chan >= half) & (chan < ROPE_DIM))[None, :]
    c_tab = jnp.where(lo | hi, cos, 1.0)
    s1_tab = jnp.where(hi, sin, 0.0)
    s2_tab = jnp.where(lo, -sin, 0.0)
    return c_tab, s1_tab, s2_tab


def _hgrn_consts(reverse):
    c = HG_C
    t = np.arange(c)[:, None]
    u = np.arange(c)[None, :]
    tri = (u >= t) if reverse else (u <= t)
    sels = []
    for lvl in range(HG_LEVELS):
        blk = 2 << lvl
        start = (np.arange(c) // blk) * blk
        piv = start + (1 << lvl) - (0 if reverse else 1)
        sels.append(np.arange(c)[None, :] == piv[:, None])
    sel = np.concatenate(sels, axis=0)
    rep = lambda m, n: jnp.asarray(np.concatenate([m.astype(np.float32)] * n, axis=1), BF16)
    return rep(tri, 2), rep(sel, 3)


def _hgrn_kernel(*refs, reverse, final, n_chunks):
    if final:
        (lbl_ref, tri_ref, sel_ref, q_ref, f_ref, v_ref, ofw_ref, gh_ref, gain_ref,
         o_ref, st_ref, st0_ref) = refs
    else:
        (lbl_ref, tri_ref, sel_ref, q_ref, f_ref, v_ref,
         o_ref, st_ref, st0_ref) = refs
    c = HG_C

    @pl.when(pl.program_id(1) == 0)
    def _():
        st_ref[...] = jnp.zeros_like(st_ref)

    l0 = lbl_ref[0:1, :]
    l1 = lbl_ref[1:2, :]
    lm = jnp.maximum(l0, l1)
    e0 = jnp.exp(l0 - lm)
    e1 = jnp.exp(l1 - lm)
    lb = e0 / (e0 + e1)
    tri = tri_ref[...]
    piv_row = c // 2 if reverse else c // 2 - 1

    def gate_math(r0):
        ff = f_ref[pl.ds(r0, c), :]
        e = jnp.exp(-jnp.abs(ff))
        r = 1.0 / (1.0 + e)
        sg = jnp.where(ff >= 0.0, r, e * r)
        g = jnp.log(lb + (1.0 - lb) * sg)
        kin = (1.0 - lb) * (1.0 - sg)
        g_hi = g.astype(BF16)
        g_lo = (g - g_hi.astype(F32)).astype(BF16)
        b = _dot(tri, jnp.concatenate([g_hi, g_lo], axis=0))
        btot = b[0:1, :] if reverse else b[c - 1:c, :]
        return kin, b, btot

    ti = lax.broadcasted_iota(I32, (c, c), 0)
    si = lax.broadcasted_iota(I32, (c, c), 1)
    causal = (si >= ti) if reverse else (si <= ti)
    row = lax.broadcasted_iota(I32, (c, 1), 0)

    def att_robust(q, kin, b, piv):
        att = jnp.where(ti == si, _dot_nt(q.astype(BF16), kin.astype(BF16)), 0.0)
        for lvl in range(HG_LEVELS):
            p = piv[lvl * c:(lvl + 1) * c, :]
            bit = ((row >> lvl) & 1) == 1
            q_side = jnp.logical_not(bit) if reverse else bit
            qe = jnp.where(q_side, q * jnp.exp(jnp.minimum(b - p, 0.0)), 0.0).astype(BF16)
            ke = jnp.where(q_side, 0.0, kin * jnp.exp(jnp.minimum(p - b, 0.0))).astype(BF16)
            same = (ti >> (lvl + 1)) == (si >> (lvl + 1))
            att = att + jnp.where(same, _dot_nt(qe, ke), 0.0)
        return att

    heads = [slice(h * HG_DK, (h + 1) * HG_DK) for h in range(HG_H)]


    def emit(r0, outs):
        o = jnp.concatenate(outs, axis=1)
        if final:
            o = o + ofw_ref[pl.ds(r0, c), :]
            normed = []
            for sl in heads:
                ms = jnp.mean(o[:, sl] * o[:, sl], axis=-1, keepdims=True)
                normed.append(o[:, sl] * lax.rsqrt(ms + EPS))
            gh = gh_ref[pl.ds(r0, c), :].astype(F32)
            o = jnp.concatenate(normed, axis=1) * gain_ref[...] * (gh * jax.nn.sigmoid(gh))
        o_ref[pl.ds(r0, c), :] = o.astype(o_ref.dtype)

    def robust_step(i, carry):
        ci = (n_chunks - 1 - i) if reverse else i
        r0 = pl.multiple_of(ci * c, c)
        q = q_ref[pl.ds(r0, c), :].astype(F32)
        v = v_ref[pl.ds(r0, c), :]
        kin, b, btot = gate_math(r0)
        qe_all = (q * jnp.exp(b)).astype(BF16)
        kd_all = (kin * jnp.exp(btot - b)).astype(BF16)
        piv = _dot(sel_ref[...], jnp.concatenate(_split3(b), axis=0))
        outs = []
        for h, sl in enumerate(heads):
            att = att_robust(q[:, sl], kin[:, sl], b[:, sl], piv[:, sl])
            st = st_ref[h]
            outs.append(_dot_nt(qe_all[:, sl], st.astype(BF16)) + _dot(att.astype(BF16), v[:, sl]))
            st_ref[h] = st * jnp.exp(btot[:, sl]) + _dot_tn(v[:, sl], kd_all[:, sl])
        emit(r0, outs)
        return carry

    grp = HG_UNROLL

    def fast_group(i, bmin):
        gi = (n_chunks // grp - 1 - i) if reverse else i
        base = gi * (grp * c)
        order = range(grp - 1, -1, -1) if reverse else range(grp)
        chunks = []
        for j in order:
            r0 = pl.multiple_of(base + j * c, c)
            kin, b, btot = gate_math(r0)
            bp = b[piv_row:piv_row + 1, :]
            bmin = jnp.minimum(bmin, jnp.minimum(bp, btot - bp))
            to_piv = jnp.exp(b - bp)
            k_piv = kin / to_piv
            chunks.append(dict(
                r0=r0, v=v_ref[pl.ds(r0, c), :], dec=jnp.exp(btot), pdec=jnp.exp(bp),
                qe=q_ref[pl.ds(r0, c), :] * to_piv.astype(BF16),
                ke=k_piv.astype(BF16),
                kd=(k_piv * jnp.exp(btot - bp)).astype(BF16)))
        for ch in chunks:
            ch["att"] = [jnp.where(causal, _dot_nt(ch["qe"][:, sl], ch["ke"][:, sl]), 0.0).astype(BF16)
                         for sl in heads]
        for ch in chunks:
            ch["intra"] = [_dot(ch["att"][h], ch["v"][:, sl]) for h, sl in enumerate(heads)]
            ch["upd"] = [_dot_tn(ch["v"][:, sl], ch["kd"][:, sl]) for sl in heads]
        st = [st_ref[h] for h in range(HG_H)]
        for ch in chunks:
            outs = []
            for h, sl in enumerate(heads):
                st_piv = (st[h] * ch["pdec"][:, sl]).astype(BF16)
                outs.append(_dot_nt(ch["qe"][:, sl], st_piv) + ch["intra"][h])
                st[h] = st[h] * ch["dec"][:, sl] + ch["upd"][h]
            emit(ch["r0"], outs)
        for h in range(HG_H):
            st_ref[h] = st[h]
        return bmin

    st0_ref[...] = st_ref[...]
    bmin = lax.fori_loop(0, n_chunks // grp, fast_group, jnp.zeros((1, HG_W), F32))

    @pl.when(jnp.min(bmin) < HG_SAFE_LOGDECAY)
    def _():
        st_ref[...] = st0_ref[...]
        lax.fori_loop(0, n_chunks, robust_step, 0)


def _hgrn(lb_logits_d, hga, hgf, batch, seq, tc, reverse, o_fw=None, gain=None):
    final = o_fw is not None
    t = hga.shape[0]
    nblk = seq // tc
    tri, sel = _hgrn_consts(reverse)

    def rows(col):
        if reverse:
            return lambda b, j: (b * nblk + nblk - 1 - j, col)
        return lambda b, j: (b * nblk + j, col)

    blk = lambda col: pl.BlockSpec((tc, HG_W), rows(col))
    in_specs = [_const_spec((2, HG_W)), _const_spec(tri.shape), _const_spec(sel.shape),
                blk(0), blk(1 if reverse else 0), blk(1)]
    args = [lb_logits_d, tri, sel, hga, hgf, hga]
    if final:
        in_specs += [blk(0), blk(2), _const_spec((1, HG_W))]
        args += [o_fw, hga, gain]
    return pl.pallas_call(
        functools.partial(_hgrn_kernel, reverse=reverse, final=final, n_chunks=tc // HG_C),
        grid=(batch, nblk),
        in_specs=in_specs,
        out_specs=blk(0),
        out_shape=jax.ShapeDtypeStruct((t, HG_W), BF16 if final else F32),
        scratch_shapes=[pltpu.VMEM((HG_H, HG_DK, HG_DK), F32),
                        pltpu.VMEM((HG_H, HG_DK, HG_DK), F32)],
        compiler_params=_cp(("parallel", "arbitrary")),
        name="hgrn_bwd" if reverse else "hgrn_fwd",
    )(*args)


def _attn_kernel(q_ref, kl_ref, km_ref, kr_ref, vl_ref, vm_ref, vr_ref, o_ref, l_ref,
                 k_s, v_s, *, tq, ld):
    hw = AT_HALF
    n_res = q_ref.shape[0]
    for res in range(n_res):
        k_s[res, 0:hw, :] = kl_ref[res]
        k_s[res, hw:hw + tq, :] = km_ref[res]
        k_s[res, hw + tq:, :] = kr_ref[res]
        v_s[res, 0:hw, :] = vl_ref[res]
        v_s[res, hw:hw + tq, :] = vm_ref[res]
        v_s[res, hw + tq:, :] = vr_ref[res]
    qb = 2 * hw
    kb = 4 * hw
    base = pl.program_id(2) * tq
    lane_head = lax.broadcasted_iota(I32, (1, AT_GW), 1) // AT_HD
    q_sel = [jnp.where(lane_head == h, AT_HD ** -0.5, 0.0).astype(BF16) for h in range(AT_H)]
    out_head = lax.broadcasted_iota(I32, (qb, AT_GW), 1) // AT_HD
    rel = (lax.broadcasted_iota(I32, (AT_H * qb, kb), 0) % qb
           - lax.broadcasted_iota(I32, (AT_H * qb, kb), 1) + hw)
    band = jnp.where(jnp.abs(rel) <= hw, 0.0, NEG_BIG)
    key_col = lax.broadcasted_iota(I32, (1, kb), 1)
    for res, qs in [(res, qs) for res in range(n_res) for qs in range(0, tq, qb)]:
        q = q_ref[res, qs:qs + qb, :]
        kk = k_s[res, qs:qs + kb, :]
        vv = v_s[res, qs:qs + kb, :]
        q4 = jnp.concatenate([q * q_sel[h] for h in range(AT_H)], axis=0)
        s = _dot_nt(q4, kk) + band
        if qs == 0 or qs + qb == tq:
            kpos = base + qs - hw + key_col
            s = s + jnp.where((kpos >= 0) & (kpos < ld), 0.0, NEG_BIG)
        m = jnp.max(s, axis=-1, keepdims=True)
        p = jnp.exp(s - m)
        l = jnp.sum(p, axis=-1, keepdims=True)
        pv = _dot(p.astype(BF16), vv) * (1.0 / l)
        lse = m + jnp.log(l)
        o_all = jnp.zeros((qb, AT_GW), F32)
        l_all = jnp.zeros((qb, AT_GW), F32)
        for h in range(AT_H):
            o_all = jnp.where(out_head == h, pv[h * qb:(h + 1) * qb, :], o_all)
            l_all = jnp.where(out_head == h, lse[h * qb:(h + 1) * qb, :], l_all)
        o_ref[res, qs:qs + qb, :] = o_all.astype(o_ref.dtype)
        l_ref[res, qs:qs + qb, :] = l_all


def _attn_group(qkv):
    batch, dil, ld, _ = qkv.shape
    hw = AT_HALF
    tq = min(ld, SEQ_TILE)
    nq = ld // tq
    nh = ld // hw
    per = tq // hw
    n_res = min(dil, max(1, SEQ_TILE // tq))
    main = lambda part: pl.BlockSpec((None, n_res, tq, AT_GW), lambda b, r, j: (b, r, j, part))
    left = lambda part: pl.BlockSpec(
        (None, n_res, hw, AT_GW), lambda b, r, j: (b, r, jnp.maximum(j * per - 1, 0), part))
    right = lambda part: pl.BlockSpec(
        (None, n_res, hw, AT_GW), lambda b, r, j: (b, r, jnp.minimum((j + 1) * per, nh - 1), part))
    out_spec = pl.BlockSpec((None, n_res, tq, AT_GW), lambda b, r, j: (b, r, j, 0))
    return pl.pallas_call(
        functools.partial(_attn_kernel, tq=tq, ld=ld),
        grid=(batch, dil // n_res, nq),
        in_specs=[main(0), left(1), main(1), right(1), left(2), main(2), right(2)],
        out_specs=[out_spec, out_spec],
        out_shape=[jax.ShapeDtypeStruct((batch, dil, ld, AT_GW), BF16),
                   jax.ShapeDtypeStruct((batch, dil, ld, AT_GW), F32)],
        scratch_shapes=[pltpu.VMEM((n_res, tq + 2 * hw, AT_GW), BF16),
                        pltpu.VMEM((n_res, tq + 2 * hw, AT_GW), BF16)],
        compiler_params=_cp(("parallel", "parallel", "parallel")),
        name=f"attn_d{dil}",
    )(qkv, qkv, qkv, qkv, qkv, qkv, qkv)


def _merge_kernel(ohg_ref, o0_ref, o1_ref, o2_ref, l0_ref, l1_ref, l2_ref, gt_ref, x_ref, mod_ref,
                  w_hg_ref, w_at_ref, w_out_ref, npost_ref, npre_ref, w_r_ref, b_r_ref,
                  x1_ref, h2_ref, ids_ref, wts_ref, cnt_ref,
                  so1_ref, sl1_ref, so2_ref, sl2_ref):
    tm = x_ref.shape[0]
    sub = tm // MERGE_SUBTILES
    rows = [slice(i * sub, (i + 1) * sub) for i in range(MERGE_SUBTILES)]
    gt1 = mod_ref[0, 2:3, :]
    sh2 = mod_ref[0, 3:4, :]
    sc2 = mod_ref[0, 4:5, :]

    for src_ref, dst_ref in ((o1_ref, so1_ref), (l1_ref, sl1_ref), (o2_ref, so2_ref), (l2_ref, sl2_ref)):
        dil = src_ref.shape[0]
        for res in range(dil):
            vals = src_ref[res].astype(F32)
            for c in range(AT_GW // LANES):
                dst_ref[c, pl.ds(res, tm // dil, stride=dil), :] = vals[:, c * LANES:(c + 1) * LANES]

    def staged(ref, r):
        return jnp.concatenate([ref[c, r, :] for c in range(AT_GW // LANES)], axis=1)

    def branches(r):
        o0, l0 = o0_ref[0, r, :].astype(F32), l0_ref[0, r, :]
        o1, l1 = staged(so1_ref, r), staged(sl1_ref, r)
        o2, l2 = staged(so2_ref, r), staged(sl2_ref, r)
        m = jnp.maximum(jnp.maximum(l0, l1), l2)
        e0, e1, e2 = jnp.exp(l0 - m), jnp.exp(l1 - m), jnp.exp(l2 - m)
        oa = (e0 * o0 + e1 * o1 + e2 * o2) / (e0 + e1 + e2)
        return _dot(ohg_ref[r, :], w_hg_ref[...]), _dot(oa.astype(BF16), w_at_ref[...])

    def out_proj(r, b_hg, b_at):
        g_hg = jax.nn.sigmoid(gt_ref[r, 0:D])
        g_at = jax.nn.sigmoid(gt_ref[r, D:2 * D])
        merged = g_hg * b_hg.astype(BF16) + g_at * b_at.astype(BF16)
        return _dot(merged, w_out_ref[...])

    def norms(r, y):
        x1 = x_ref[r, :] + gt1 * _rms(y, npost_ref[...])
        x1_ref[r, :] = x1
        h2 = _rms(x1, npre_ref[...]) * (1.0 + sc2) + sh2
        h2_ref[r, :] = _pack_rows(h2)
        return _dot(h2.astype(BF16), w_r_ref[...]) + b_r_ref[...]

    def top_k(r, logits):
        lane = lax.broadcasted_iota(I32, logits.shape, 1)
        lane_f = lane.astype(F32)
        work = logits
        vals, idxs = [], []
        sel = jnp.zeros(logits.shape, F32)
        for _ in range(TOP_K):
            mk = jnp.max(work, axis=-1, keepdims=True)
            ik = jnp.min(jnp.where(work == mk, lane_f, float(LANES)), axis=-1, keepdims=True)
            hit = lane_f == ik
            sel = jnp.where(hit, 1.0, sel)
            work = jnp.where(hit, -jnp.inf, work)
            vals.append(mk)
            idxs.append(ik)
        es = [jnp.exp(v - vals[0]) for v in vals]
        den = es[0] + es[1] + es[2] + es[3]
        ids = jnp.zeros(logits.shape, F32)
        wts = jnp.zeros(logits.shape, F32)
        for k in range(TOP_K):
            ids = jnp.where(lane == k, idxs[k], ids)
            wts = jnp.where(lane == k, es[k] / den, wts)
        ids_ref[r, :] = ids.astype(I32)
        wts_ref[r, :] = wts
        return jnp.sum(sel, axis=0, keepdims=True)

    br = [branches(r) for r in rows]
    ys = [out_proj(r, *b) for r, b in zip(rows, br)]
    logits = [norms(r, y) for r, y in zip(rows, ys)]
    counts = [top_k(r, lg) for r, lg in zip(rows, logits)]
    cnt_ref[0] = functools.reduce(lambda a, b: a + b, counts)


def _merge(ohg, att_o, att_l, gates, x2, mod3, w_hg, w_at, w_out, npost, npre, w_r, b_r, seq, tm):
    t = x2.shape[0]
    nt = t // tm
    row = lambda w: pl.BlockSpec((tm, w), lambda i: (i, 0))
    slab_i = jax.ShapeDtypeStruct((t, LANES), I32)
    slab_f = jax.ShapeDtypeStruct((t, LANES), F32)
    nblk = seq // tm
    grouped = [pl.BlockSpec((None, d, tm // d, AT_GW),
                            lambda i: ((i * tm) // seq, 0, i % nblk, 0)) for d in AT_DILS]
    return pl.pallas_call(
        _merge_kernel,
        grid=(nt,),
        in_specs=[row(HG_W)] + grouped + grouped + [row(2 * D), row(D),
                  pl.BlockSpec((1, 6, D), lambda i: ((i * tm) // seq, 0, 0)),
                  _const_spec(w_hg.shape), _const_spec(w_at.shape), _const_spec(w_out.shape),
                  _const_spec((1, D)), _const_spec((1, D)),
                  _const_spec(w_r.shape), _const_spec(b_r.shape)],
        out_specs=[row(D), row(HALF_D), row(LANES), row(LANES),
                   pl.BlockSpec((1, 1, LANES), lambda i: (i, 0, 0))],
        out_shape=[jax.ShapeDtypeStruct((t, D), F32), jax.ShapeDtypeStruct((t, HALF_D), U32),
                   slab_i, slab_f, jax.ShapeDtypeStruct((nt, 1, LANES), F32)],
        scratch_shapes=[pltpu.VMEM((AT_GW // LANES, tm, LANES), F32)] * 4,
        compiler_params=_cp(("parallel",)),
        name="merge",
    )(ohg, *att_o, *att_l, gates, x2, mod3, w_hg, w_at, w_out, npost, npre, w_r, b_r)


def _route_kernel(ids_ref, base_ref, tril_ref, dest_ref):
    ids = ids_ref[...]
    lane = lax.broadcasted_iota(I32, ids.shape, 1)
    hits = [lane == ids[:, k:k + 1] for k in range(TOP_K)]
    sel = jnp.zeros(ids.shape, F32)
    for hit in hits:
        sel = jnp.where(hit, 1.0, sel)
    before = _dot(tril_ref[...], sel.astype(BF16)) + base_ref[0]
    dest = jnp.zeros(ids.shape, F32)
    for k, hit in enumerate(hits):
        rk = jnp.sum(jnp.where(hit, before, 0.0), axis=-1, keepdims=True)
        dest = jnp.where(lane == k, rk, dest)
    dest_ref[...] = dest.T[:ROUTE_ROWS, :].astype(I32)


def _route(ids, tile_base, tm):
    t = ids.shape[0]
    tril = jnp.asarray(np.tril(np.ones((tm, tm), np.float32), -1), BF16)
    return pl.pallas_call(
        _route_kernel,
        grid=(t // tm,),
        in_specs=[pl.BlockSpec((tm, LANES), lambda i: (i, 0)),
                  pl.BlockSpec((1, 1, LANES), lambda i: (i, 0, 0)),
                  _const_spec((tm, tm))],
        out_specs=pl.BlockSpec((ROUTE_ROWS, tm), lambda i: (0, i)),
        out_shape=jax.ShapeDtypeStruct((ROUTE_ROWS, t), I32),
        compiler_params=_cp(("parallel",)),
        name="route",
    )(ids, tile_base, tril)


def _moe_kernel(be_ref, rows_ref, x_ref, *refs, prepare):
    i = pl.program_id(0)
    if prepare:
        wup_ref, bg_ref, bl_ref, wdn_ref, bd_ref, perm_ref, o_ref, wg_ref, wl_ref, wd_ref = refs

        @pl.when((i == 0) | (be_ref[i] != be_ref[jnp.maximum(i - 1, 0)]))
        def _():
            _deinterleave_into(wup_ref, perm_ref, wg_ref.at[0], wl_ref.at[0])
            wd_ref[0] = wdn_ref[0].astype(BF16)
    else:
        wg_ref, wl_ref, bg_ref, bl_ref, wd_ref, bd_ref, o_ref = refs

    def ffn(x_packed):
        x_lo, x_hi = (a.astype(BF16) for a in _unpack_rows(x_packed))
        gate = _dot(x_lo, wg_ref[0, :HALF_D, :]) + _dot(x_hi, wg_ref[0, HALF_D:, :]) + bg_ref[0]
        up = _dot(x_lo, wl_ref[0, :HALF_D, :]) + _dot(x_hi, wl_ref[0, HALF_D:, :]) + bl_ref[0]
        gate = jnp.minimum(gate, SWIGLU_LIMIT)
        up = jnp.clip(up, -SWIGLU_LIMIT, SWIGLU_LIMIT)
        act = (up + 1.0) * gate * jax.nn.sigmoid(SWIGLU_ALPHA * gate)
        return _pack_rows(_dot(act.astype(BF16), wd_ref[0]) + bd_ref[0])

    valid = rows_ref[i]
    half = x_ref.shape[0] // 2

    @pl.when(valid > half)
    def _():
        o_ref[...] = ffn(x_ref[...])

    @pl.when((valid > 0) & (valid <= half))
    def _():
        o_ref[:half, :] = ffn(x_ref[:half, :])
        o_ref[half:, :] = jnp.zeros((half, HALF_D), U32)

    @pl.when(valid == 0)
    def _():
        o_ref[...] = jnp.zeros_like(o_ref)


def _moe(block_expert, block_rows, xb, wts, bm, prepared):
    r = xb.shape[0]
    nblk = r // bm
    rows = pl.BlockSpec((bm, HALF_D), lambda i, be, br: (i, 0))
    ew = lambda shape: pl.BlockSpec((1,) + shape, lambda i, be, br: (be[i], 0, 0))
    y_shape = jax.ShapeDtypeStruct((r, HALF_D), U32)
    if prepared is None:
        perm = _pair_split_perm()
        w_shape = jax.ShapeDtypeStruct((N_EXP, D, D), BF16)
        in_specs = [rows, ew((D, 2 * D)), ew((1, D)), ew((1, D)), ew((D, D)), ew((1, D)),
                    pl.BlockSpec(perm.shape, lambda i, be, br: (0, 0))]
        args = (wts["w_up"], wts["bg"], wts["bl"], wts["w_down"], wts["bd"], perm)
        out_specs = [rows, ew((D, D)), ew((D, D)), ew((D, D))]
        out_shape = [y_shape, w_shape, w_shape, w_shape]
    else:
        in_specs = [rows, ew((D, D)), ew((D, D)), ew((1, D)), ew((1, D)), ew((D, D)), ew((1, D))]
        args = (prepared[0], prepared[1], wts["bg"], wts["bl"], prepared[2], wts["bd"])
        out_specs = [rows]
        out_shape = [y_shape]
    outs = pl.pallas_call(
        functools.partial(_moe_kernel, prepare=prepared is None),
        grid_spec=pltpu.PrefetchScalarGridSpec(
            num_scalar_prefetch=2, grid=(nblk,), in_specs=in_specs, out_specs=out_specs),
        out_shape=out_shape,
        compiler_params=_cp(("arbitrary",)),
        name="moe_prepare" if prepared is None else "moe",
    )(block_expert, block_rows, xb, *args)
    return outs[0], (tuple(outs[1:]) if prepared is None else prepared)


def _final_kernel(y0_ref, y1_ref, y2_ref, y3_ref, wts_ref, x1_ref, mod_ref, npost_ref, o_ref):
    wts = wts_ref[...]
    y = jnp.zeros(x1_ref.shape, F32)
    for k, y_ref in enumerate((y0_ref, y1_ref, y2_ref, y3_ref)):
        y = y + wts[:, k:k + 1] * jnp.concatenate(_unpack_rows(y_ref[0]), axis=1)
    gt2 = mod_ref[0, 5:6, :]
    o_ref[...] = x1_ref[...] + gt2 * _rms(y, npost_ref[...])


def _final(yg, wts, x1, mod3, npost, seq, tm):
    t = x1.shape[0]
    slot = lambda k: pl.BlockSpec((1, tm, HALF_D), lambda i: (k, i, 0))
    return pl.pallas_call(
        _final_kernel,
        grid=(t // tm,),
        in_specs=[slot(0), slot(1), slot(2), slot(3),
                  pl.BlockSpec((tm, LANES), lambda i: (i, 0)),
                  pl.BlockSpec((tm, D), lambda i: (i, 0)),
                  pl.BlockSpec((1, 6, D), lambda i: ((i * tm) // seq, 0, 0)),
                  _const_spec((1, D))],
        out_specs=pl.BlockSpec((tm, D), lambda i: (i, 0)),
        out_shape=jax.ShapeDtypeStruct((t, D), F32),
        compiler_params=_cp(("parallel",)),
        name="final",
    )(yg, yg, yg, yg, wts, x1, mod3, npost)


def _tiles(seq):
    return dict(tm_in=min(seq, ROW_TILE), tc=min(seq, SEQ_TILE), tm_merge=min(seq, STREAM_TILE),
                bm=MOE_BLOCK, tm_final=min(seq, STREAM_TILE))


def _prep_weights(w_in, w_hg_out, w_att_out, w_out, w_router, b_router, w_up, b_up, w_down, b_down):
    w = w_in[0]
    hw = HG_W
    a0 = 5 * hw
    aw = AT_GW * len(AT_DILS)
    q_at, k_at, v_at = (w[:, a0 + i * aw:a0 + (i + 1) * aw] for i in range(3))
    grp = lambda m, g: m[:, g * AT_GW:(g + 1) * AT_GW]
    w_at = jnp.concatenate(
        [jnp.concatenate([grp(q_at, g), grp(k_at, g), grp(v_at, g)], axis=1)
         for g in range(len(AT_DILS))], axis=1)
    pad = LANES - N_EXP
    return dict(
        w_up=w_up[0],
        w_hga=jnp.concatenate([w[:, 0:hw], w[:, 3 * hw:4 * hw], w[:, 4 * hw:5 * hw]], 1).astype(BF16),
        w_hgf=w[:, hw:3 * hw].astype(BF16),
        w_at=w_at.astype(BF16),
        w_gt=w[:, a0 + 3 * aw:].astype(BF16),
        w_hg_out=w_hg_out[0].astype(BF16),
        w_att_out=w_att_out[0].astype(BF16),
        w_out=w_out[0].astype(BF16),
        w_r=jnp.pad(w_router[0], ((0, 0), (0, pad))).astype(BF16),
        b_r=jnp.pad(b_router[0], (0, pad), constant_values=NEG_BIG).reshape(1, LANES),
        bg=b_up[0][:, 0::2].reshape(N_EXP, 1, D),
        bl=b_up[0][:, 1::2].reshape(N_EXP, 1, D),
        w_down=w_down[0],
        bd=b_down[0].reshape(N_EXP, 1, D),
    )


def _trunk(x, mod, wts, norm_pre, norm_post, lb_logits, hg_gain, prepared=None):
    batch, seq, width = x.shape
    t = batch * seq
    tl = _tiles(seq)
    assert width == D and seq % SEQ_TILE == 0 and seq // max(AT_DILS) >= 2 * AT_HALF
    x2 = x.reshape(t, D)
    mod3 = mod.reshape(batch, 6, D)
    rope = _rope_tables(seq)

    hga, hgf, at0, at1, at2, gates = _inproj(
        x2, mod3, norm_pre[0, 0].reshape(1, D), rope,
        wts["w_hga"], wts["w_hgf"], wts["w_at"], wts["w_gt"], seq, tl["tm_in"])

    o_fw = _hgrn(lb_logits[0], hga, hgf, batch, seq, tl["tc"], reverse=False)
    ohg = _hgrn(lb_logits[1], hga, hgf, batch, seq, tl["tc"], reverse=True,
                o_fw=o_fw, gain=hg_gain[0].reshape(1, HG_W))

    att = [_attn_group(a) for a in (at0, at1, at2)]

    x1, h2, ids, rw, cnt = _merge(
        ohg, [a[0] for a in att], [a[1] for a in att], gates, x2, mod3,
        wts["w_hg_out"], wts["w_att_out"], wts["w_out"],
        norm_post[0, 0].reshape(1, D), norm_pre[0, 1].reshape(1, D),
        wts["w_r"], wts["b_r"], seq, tl["tm_merge"])

    bm = tl["bm"]
    tmr = tl["tm_merge"]
    cnt_tiles = cnt.reshape(t // tmr, LANES).astype(I32)
    total = jnp.sum(cnt_tiles, axis=0)
    min_blocks = (jnp.arange(LANES) < N_EXP).astype(I32)
    padded = jnp.maximum((total + bm - 1) // bm, min_blocks) * bm
    pad_end = jnp.cumsum(padded)
    pad_start = pad_end - padded
    tile_base = pad_start[None, :] + jnp.cumsum(cnt_tiles, axis=0) - cnt_tiles
    dest_kt = _route(ids, tile_base.astype(F32).reshape(-1, 1, LANES), tmr)[:TOP_K]

    n_rows = t * TOP_K + N_EXP * bm
    n_blocks = n_rows // bm
    blk_start = jnp.arange(n_blocks, dtype=I32) * bm
    block_expert = jnp.minimum(
        jnp.sum(pad_end[None, :N_EXP] <= blk_start[:, None], axis=1), N_EXP - 1).astype(I32)
    block_rows = jnp.clip(pad_start[block_expert] + total[block_expert] - blk_start, 0, bm)
    block_rows = jnp.where(blk_start < pad_end[N_EXP - 1], block_rows, 0).astype(I32)

    xb = _sc_scatter_rows(h2, dest_kt, n_rows)
    yb, prepared = _moe(block_expert, block_rows, xb, wts, bm, prepared)
    yg = _sc_gather_rows(yb, dest_kt.reshape(-1)).reshape(TOP_K, t, HALF_D)

    out = _final(yg, rw, x1, mod3, norm_post[0, 1].reshape(1, D), seq, tl["tm_final"])
    return out.reshape(batch, seq, D), prepared


def kernel(x_prompt, x_sample, c_prompt, c_sample, w_ada, b_ada, norm_pre, norm_post, w_in,
           lb_logits, hg_norm_gain, w_hg_out, w_att_out, w_out, w_router, b_router,
           w_up, b_up, w_down, b_down):
    wts = _prep_weights(w_in, w_hg_out, w_att_out, w_out, w_router, b_router,
                        w_up, b_up, w_down, b_down)
    nb = c_prompt.shape[0]
    mod = _ada(jnp.concatenate([c_prompt, c_sample], axis=0), w_ada[0], b_ada[0])
    lb_l = lb_logits.astype(F32)
    y_p, prepared = _trunk(x_prompt, mod[:nb], wts, norm_pre, norm_post, lb_l, hg_norm_gain)
    y_s, _ = _trunk(x_sample, mod[nb:], wts, norm_pre, norm_post, lb_l, hg_norm_gain, prepared)
    return (y_p, y_s)
```
